```python
import math
import jax, jax.numpy as jnp
from jax import lax
import numpy as np

D_MODEL = 1024
BATCH = 8
SEQ = 2048
DEPTH = 1
DEC_BATCH = 128
DEC_SEQ = 4
PAST_LEN = 16384
PAGE_SIZE = 128

H_RET = 4
DK_RET = D_MODEL // H_RET
DV_RET = 2 * DK_RET
D_RET_QK = H_RET * DK_RET
D_RET_V = H_RET * DV_RET
RET_CHUNK = 128
ROPE_BASE = 10000.0
GN_EPS = 1e-5
D_RNN = 1536
RNN_BLOCK = 128
N_RNN_BLOCKS = D_RNN // RNN_BLOCK
CONV_W = 4
LRU_C = 8.0
D_FF = -(-8 * D_MODEL // (3 * 256)) * 256
RMS_EPS = 1e-6
IN_WIDTHS = (D_RET_QK, D_RET_QK, D_RET_V, D_RET_V, D_RNN, D_RNN)
N_IN = sum(IN_WIDTHS)
SPLITS = tuple(int(s) for s in np.cumsum(IN_WIDTHS)[:-1])

kernel_name = "retnet_rglru_parallel_gated_decoder_step"


def rmsnorm(x, w):
    xf = x.astype(jnp.float32)
    y = xf * lax.rsqrt(jnp.mean(xf * xf, axis=-1, keepdims=True) + RMS_EPS)
    return (y * w.astype(jnp.float32)).astype(x.dtype)


def apply_rope(x, pos):
    half = x.shape[-1] // 2
    inv = ROPE_BASE ** (-jnp.arange(half, dtype=jnp.float32) / half)
    ang = pos.astype(jnp.float32)[:, None] * inv[None, :]
    cos = jnp.cos(ang)[None, :, None, :]
    sin = jnp.sin(ang)[None, :, None, :]
    x1, x2 = x[..., :half], x[..., half:]
    return jnp.concatenate([x1 * cos - x2 * sin, x1 * sin + x2 * cos], axis=-1)


def retention_log_gamma():
    return jnp.log(1.0 - 2.0 ** (-5.0 - jnp.arange(H_RET, dtype=jnp.float32)))


def retention_chunk(q, k, v, s, lg):
    C = q.shape[1]
    idx = jnp.arange(C, dtype=jnp.float32)
    diff = idx[:, None] - idx[None, :]
    causal = diff >= 0
    decay = jnp.where(causal[None], jnp.exp(jnp.where(causal, diff, 0.0)[None] * lg[:, None, None]), 0.0)
    scores = jnp.einsum('bihd,bjhd->bhij', q, k) * decay[None]
    intra = jnp.einsum('bhij,bjhe->bihe', scores, v)
    q_dec = jnp.exp((idx + 1.0)[:, None] * lg[None, :])
    cross = jnp.einsum('bihd,bhde->bihe', q, s) * q_dec[None, :, :, None]
    k_dec = jnp.exp((C - 1.0 - idx)[:, None] * lg[None, :])
    s_new = jnp.exp(C * lg)[None, :, None, None] * s + jnp.einsum('bjhd,bjhe->bhde', k * k_dec[None, :, :, None], v)
    return intra + cross, s_new


def retention(q, k, v, s0):
    B, T, H, _ = q.shape
    lg = retention_log_gamma()
    C = RET_CHUNK if T % RET_CHUNK == 0 else T
    n = T // C

    def to_chunks(t):
        return t.reshape(B, n, C, H, t.shape[-1]).swapaxes(0, 1)

    def step(s, qkv):
        qc, kc, vc = qkv
        o, s = retention_chunk(qc, kc, vc, s, lg)
        return s, o

    s_last, o = lax.scan(step, s0, (to_chunks(q), to_chunks(k), to_chunks(v)))
    return o.swapaxes(0, 1).reshape(B, T, H, -1), s_last


def causal_conv(xr, buf, w, b):
    xc = jnp.concatenate([buf, xr], axis=1)
    T = xr.shape[1]
    y = b
    for j in range(CONV_W):
        y = y + xc[:, j:j + T] * w[j]
    return y, xc[:, -(CONV_W - 1):]


def rg_lru(xc, h0, w_a, b_a, w_x, b_x, lru_param):
    B, T, _ = xc.shape
    xb = xc.reshape(B, T, N_RNN_BLOCKS, RNN_BLOCK)
    r = jax.nn.sigmoid(jnp.einsum('btnd,nde->btne', xb, w_a.astype(jnp.float32)).reshape(B, T, D_RNN) + b_a.astype(jnp.float32))
    i = jax.nn.sigmoid(jnp.einsum('btnd,nde->btne', xb, w_x.astype(jnp.float32)).reshape(B, T, D_RNN) + b_x.astype(jnp.float32))
    log_a = -LRU_C * r * jax.nn.softplus(-lru_param.astype(jnp.float32))
    a = jnp.exp(log_a)
    beta = jnp.sqrt(-jnp.expm1(2.0 * log_a))
    bterm = beta * (i * xc)
    bterm = bterm.at[:, 0].add(a[:, 0] * h0)

    def combine(left, right):
        al, bl = left
        ar, br = right
        return al * ar, ar * bl + br

    _, h = lax.associative_scan(combine, (a, bterm), axis=1)
    return h, h[:, -1]


def hybrid_layer(x, c, pos, s_ret, h_rnn, conv_buf, p):
    dt = x.dtype
    f32 = jnp.float32
    B, T, _ = x.shape
    mod = (jax.nn.silu(c) @ p['w_ada'] + p['b_ada'])[:, None, :]
    sh1, sc1, g1, sh2, sc2, g2 = jnp.split(mod, 6, axis=-1)
    h = rmsnorm(x, p['norm_pre_mix']) * (1 + sc1) + sh1
    q, k, v, g, xr, gr = jnp.split(h @ p['w_in'], SPLITS, axis=-1)
    q = apply_rope(q.astype(f32).reshape(B, T, H_RET, DK_RET), pos)
    k = apply_rope(k.astype(f32).reshape(B, T, H_RET, DK_RET), pos) * (DK_RET ** -0.5)
    v = v.astype(f32).reshape(B, T, H_RET, DV_RET)
    o, s_new = retention(q, k, v, s_ret.astype(f32))
    mu = jnp.mean(o, axis=-1, keepdims=True)
    var = jnp.mean(jnp.square(o - mu), axis=-1, keepdims=True)
    o = (o - mu) * lax.rsqrt(var + GN_EPS)
    ret_y = (o.reshape(B, T, D_RET_V) * p['ret_gn_w'].astype(f32) * jax.nn.silu(g.astype(f32))).astype(dt)
    branch_ret = ret_y @ p['w_br_ret']
    xconv, conv_new = causal_conv(xr, conv_buf.astype(dt), p['conv_w'], p['conv_b'])
    hseq, h_last = rg_lru(xconv.astype(f32), h_rnn.astype(f32), p['w_rg_a'], p['b_rg_a'], p['w_rg_x'], p['b_rg_x'], p['lru_param'])
    rnn_y = (hseq * jax.nn.gelu(gr.astype(f32))).astype(dt)
    branch_rnn = rnn_y @ p['w_br_rnn']
    gates = jax.nn.sigmoid((h @ p['w_mgate'] + p['b_mgate']).astype(f32)).astype(dt)
    ga, gb = jnp.split(gates, 2, axis=-1)
    mixed = (ga * branch_ret + gb * branch_rnn) @ p['w_out']
    x = x + g1 * rmsnorm(mixed, p['norm_post_mix'])
    h2 = rmsnorm(x, p['norm_pre_ffn']) * (1 + sc2) + sh2
    fg, fu = jnp.split(h2 @ p['w_ffn_in'], 2, axis=-1)
    ff = (jax.nn.silu(fg) * fu) @ p['w_ffn_out']
    x = x + g2 * rmsnorm(ff, p['norm_post_ffn'])
    return x, s_new, h_last, conv_new


def setup_inputs(seed: int = 0) -> dict:
    key = jax.random.key(seed)
    ks = jax.random.split(key, 32)
    f32 = jnp.float32

    def nrm(k, shape, scale):
        return jax.random.normal(k, shape, f32) * scale

    u = jax.random.uniform(ks[20], (DEPTH, D_RNN), f32, 0.9, 0.999)
    s = u ** (1.0 / LRU_C)
    lru_param = jnp.log(s) - jnp.log1p(-s)
    return {
        'x_prompt': nrm(ks[0], (BATCH, SEQ, D_MODEL), 1.0),
        'x_sample': nrm(ks[1], (DEC_BATCH, DEC_SEQ, D_MODEL), 1.0),
        'state_ret': nrm(ks[2], (DEPTH, DEC_BATCH, H_RET, DK_RET, DV_RET), 0.3),
        'state_rnn_h': nrm(ks[3], (DEPTH, DEC_BATCH, D_RNN), 0.5),
        'state_rnn_conv': nrm(ks[4], (DEPTH, DEC_BATCH, CONV_W - 1, D_RNN), 1.0),
        'c_prompt': nrm(ks[5], (BATCH, D_MODEL), 1.0),
        'c_sample': nrm(ks[6], (DEC_BATCH, D_MODEL), 1.0),
        'w_ada': nrm(ks[7], (DEPTH, D_MODEL, 6 * D_MODEL), D_MODEL ** -0.5),
        'b_ada': nrm(ks[8], (DEPTH, 6 * D_MODEL), 0.02),
        'norm_pre_mix': 1.0 + nrm(ks[9], (DEPTH, D_MODEL), 0.02),
        'norm_post_mix': 1.0 + nrm(ks[10], (DEPTH, D_MODEL), 0.02),
        'norm_pre_ffn': 1.0 + nrm(ks[11], (DEPTH, D_MODEL), 0.02),
        'norm_post_ffn': 1.0 + nrm(ks[12], (DEPTH, D_MODEL), 0.02),
        'w_in': nrm(ks[13], (DEPTH, D_MODEL, N_IN), D_MODEL ** -0.5),
        'ret_gn_w': 1.0 + nrm(ks[14], (DEPTH, D_RET_V), 0.02),
        'w_br_ret': nrm(ks[15], (DEPTH, D_RET_V, D_MODEL), D_RET_V ** -0.5),
        'conv_w': nrm(ks[16], (DEPTH, CONV_W, D_RNN), CONV_W ** -0.5),
        'conv_b': nrm(ks[17], (DEPTH, D_RNN), 0.02),
        'w_rg_a': nrm(ks[18], (DEPTH, N_RNN_BLOCKS, RNN_BLOCK, RNN_BLOCK), RNN_BLOCK ** -0.5),
        'b_rg_a': nrm(ks[19], (DEPTH, D_RNN), 0.02),
        'w_rg_x': nrm(ks[21], (DEPTH, N_RNN_BLOCKS, RNN_BLOCK, RNN_BLOCK), RNN_BLOCK ** -0.5),
        'b_rg_x': nrm(ks[22], (DEPTH, D_RNN), 0.02),
        'lru_param': lru_param,
        'w_br_rnn': nrm(ks[23], (DEPTH, D_RNN, D_MODEL), D_RNN ** -0.5),
        'w_mgate': nrm(ks[24], (DEPTH, D_MODEL, 2 * D_MODEL), D_MODEL ** -0.5),
        'b_mgate': nrm(ks[25], (DEPTH, 2 * D_MODEL), 0.02),
        'w_out': nrm(ks[26], (DEPTH, D_MODEL, D_MODEL), D_MODEL ** -0.5),
        'w_ffn_in': nrm(ks[27], (DEPTH, D_MODEL, 2 * D_FF), D_MODEL ** -0.5),
        'w_ffn_out': nrm(ks[28], (DEPTH, D_FF, D_MODEL), D_FF ** -0.5),
    }


def reference(x_prompt, x_sample, state_ret, state_rnn_h, state_rnn_conv, c_prompt, c_sample,
              w_ada, b_ada, norm_pre_mix, norm_post_mix, norm_pre_ffn, norm_post_ffn,
              w_in, ret_gn_w, w_br_ret, conv_w, conv_b, w_rg_a, b_rg_a, w_rg_x, b_rg_x,
              lru_param, w_br_rnn, w_mgate, b_mgate, w_out, w_ffn_in, w_ffn_out):
    pos_p = jnp.arange(SEQ, dtype=jnp.int32)
    pos_s = PAST_LEN + jnp.arange(DEC_SEQ, dtype=jnp.int32)
    xp, xs = x_prompt, x_sample
    rp, rs, hp, hs, cp, cs = [], [], [], [], [], []
    for l in range(DEPTH):
        p = dict(w_ada=w_ada[l], b_ada=b_ada[l], norm_pre_mix=norm_pre_mix[l], norm_post_mix=norm_post_mix[l],
                 norm_pre_ffn=norm_pre_ffn[l], norm_post_ffn=norm_post_ffn[l], w_in=w_in[l], ret_gn_w=ret_gn_w[l],
                 w_br_ret=w_br_ret[l], conv_w=conv_w[l], conv_b=conv_b[l], w_rg_a=w_rg_a[l], b_rg_a=b_rg_a[l],
                 w_rg_x=w_rg_x[l], b_rg_x=b_rg_x[l], lru_param=lru_param[l], w_br_rnn=w_br_rnn[l],
                 w_mgate=w_mgate[l], b_mgate=b_mgate[l], w_out=w_out[l], w_ffn_in=w_ffn_in[l], w_ffn_out=w_ffn_out[l])
        s0 = jnp.zeros((BATCH, H_RET, DK_RET, DV_RET), jnp.float32)
        h0 = jnp.zeros((BATCH, D_RNN), jnp.float32)
        b0 = jnp.zeros((BATCH, CONV_W - 1, D_RNN), x_prompt.dtype)
        xp, s_p, h_p, c_p = hybrid_layer(xp, c_prompt, pos_p, s0, h0, b0, p)
        xs, s_s, h_s, c_s = hybrid_layer(xs, c_sample, pos_s, state_ret[l], state_rnn_h[l], state_rnn_conv[l], p)
        rp.append(s_p.astype(state_ret.dtype)); rs.append(s_s.astype(state_ret.dtype))
        hp.append(h_p.astype(state_rnn_h.dtype)); hs.append(h_s.astype(state_rnn_h.dtype))
        cp.append(c_p.astype(state_rnn_conv.dtype)); cs.append(c_s.astype(state_rnn_conv.dtype))
    return (xp, xs, jnp.stack(rp), jnp.stack(rs), jnp.stack(hp), jnp.stack(hs), jnp.stack(cp), jnp.stack(cs))
```

```python
import functools

import jax
import jax.numpy as jnp
from jax import lax
from jax.experimental import pallas as pl
from jax.experimental.pallas import tpu as pltpu

F32 = jnp.float32
BF16 = jnp.bfloat16

D_MODEL = 1024
H_RET = 4
DK = D_MODEL // H_RET
DV = 2 * DK
D_QK = H_RET * DK
D_V = H_RET * DV
D_RNN = 1536
RNN_BLOCK = 128
N_RNN_BLOCKS = D_RNN // RNN_BLOCK
CONV_W = 4
LRU_C = 8.0
D_FF = 2816
ROPE_BASE = 10000.0
GN_EPS = 1e-5
RMS_EPS = 1e-6
PAST_LEN = 16384

OFF_Q = 0
OFF_K = OFF_Q + D_QK
OFF_V = OFF_K + D_QK
OFF_G = OFF_V + D_V
OFF_XR = OFF_G + D_V
OFF_GR = OFF_XR + D_RNN
OFF_MG = OFF_GR + D_RNN
N_CAT = OFF_MG + 2 * D_MODEL

SUBLANES = 8
LANES = 128
MXU_DIM = 256
VMEM_BYTES_V7X = 64 * 1024 * 1024

PROMPT_TM = 256
FFN_TM = 256
FF_CHUNK = MXU_DIM
SAMPLE_TM = 128
RET_SAMPLE_BB = 2


def _dot(a, b):
    return jnp.dot(a, b, preferred_element_type=F32)


def _dot_nt(a, b):
    return lax.dot_general(a, b, (((1,), (1,)), ((), ())), preferred_element_type=F32)


def _dot_tn(a, b):
    return lax.dot_general(a, b, (((0,), (0,)), ((), ())), preferred_element_type=F32)


def _rms(x, w):
    ms = jnp.mean(x * x, axis=-1, keepdims=True)
    return x * lax.rsqrt(ms + RMS_EPS) * w


def _silu(x):
    return x * jax.nn.sigmoid(x)


def _rope(x, cos, sin):
    half = DK // 2
    x1, x2 = x[:, :half], x[:, half:]
    return jnp.concatenate([x1 * cos - x2 * sin, x1 * sin + x2 * cos], axis=1)


def _group_norm(o):
    mu = jnp.mean(o, axis=-1, keepdims=True)
    d = o - mu
    var = jnp.mean(d * d, axis=-1, keepdims=True)
    return d * lax.rsqrt(var + GN_EPS)


def _lru_coeffs(xconv, wrg_ref, b_a, b_x, lru):
    xcb = xconv.astype(BF16)
    pre = [_dot(xcb[:, n * RNN_BLOCK:(n + 1) * RNN_BLOCK], wrg_ref[n]) for n in range(N_RNN_BLOCKS)]
    ra = jnp.concatenate([p[:, :RNN_BLOCK] for p in pre], axis=1) + b_a
    ri = jnp.concatenate([p[:, RNN_BLOCK:] for p in pre], axis=1) + b_x
    r = jax.nn.sigmoid(ra)
    i = jax.nn.sigmoid(ri)
    z = -lru
    sp = jnp.maximum(z, 0.0) + jnp.log(1.0 + jnp.exp(-jnp.abs(z)))
    log_a = -LRU_C * r * sp
    a = jnp.exp(log_a)
    beta = jnp.sqrt(-jnp.tanh(log_a) * (jnp.exp(2.0 * log_a) + 1.0))
    return a, beta * (i * xconv)


def _mix_tail(x, g1, br_ret, rnn_y_b, gate_pre, b_mg, wbr_rnn_ref, wout_ref, npost):
    br_rnn = _dot(rnn_y_b, wbr_rnn_ref[...])
    gates = jax.nn.sigmoid(gate_pre + b_mg)
    ga, gb = gates[:, :D_MODEL], gates[:, D_MODEL:]
    mixed = _dot((ga * br_ret + gb * br_rnn).astype(BF16), wout_ref[...])
    return x + g1 * _rms(mixed, npost)


def _mod_kernel(c_ref, w_ref, b_ref, o_ref):
    a = _silu(c_ref[...]).astype(BF16)
    o_ref[...] = _dot(a, w_ref[...]) + b_ref[...]


def _mod_call(c_all, w_ada_b, b_ada):
    rows = c_all.shape[0]
    tn = D_MODEL
    return pl.pallas_call(
        _mod_kernel,
        grid=(6 * D_MODEL // tn,),
        in_specs=[
            pl.BlockSpec((rows, D_MODEL), lambda j: (0, 0)),
            pl.BlockSpec((D_MODEL, tn), lambda j: (0, j)),
            pl.BlockSpec((1, tn), lambda j: (0, j)),
        ],
        out_specs=pl.BlockSpec((rows, tn), lambda j: (0, j)),
        out_shape=jax.ShapeDtypeStruct((rows, 6 * D_MODEL), F32),
        compiler_params=pltpu.CompilerParams(dimension_semantics=("arbitrary",)),
    )(c_all, w_ada_b, b_ada)


def _mix_prompt_kernel(gpow_ref, x_ref, mod_ref, npre_ref, wcat_ref, bmg_ref, rope_ref,
                       dec_ref, qdec_ref, kdec_ref, gnw_ref, wbr_ret_ref, convw_ref, convb_ref,
                       wrg_ref, bra_ref, brx_ref, lru_ref, wbr_rnn_ref, wout_ref, npost_ref,
                       x1_ref, s_ref, hlast_ref, convnew_ref,
                       hb_ref, xr_ref, sa_ref, sb_ref, hc_ref):
    tm = PROMPT_TM
    t = pl.program_id(1)
    nt = pl.num_programs(1)

    @pl.when(t == 0)
    def _():
        s_ref[...] = jnp.zeros_like(s_ref)
        xr_ref[0:SUBLANES, :] = jnp.zeros((SUBLANES, D_RNN), F32)
        sa_ref[0:SUBLANES, :] = jnp.zeros((SUBLANES, D_RNN), F32)
        sb_ref[0:SUBLANES, :] = jnp.zeros((SUBLANES, D_RNN), F32)
        hc_ref[...] = jnp.zeros_like(hc_ref)

    x = x_ref[...]
    m = mod_ref[...]
    sh1, sc1, g1 = m[:, :D_MODEL], m[:, D_MODEL:2 * D_MODEL], m[:, 2 * D_MODEL:]
    hb_ref[...] = (_rms(x, npre_ref[...]) * (1.0 + sc1) + sh1).astype(BF16)

    half = DK // 2
    cos, sin = rope_ref[:, 0:half], rope_ref[:, half:2 * half]
    cosk, sink = rope_ref[:, 2 * half:3 * half], rope_ref[:, 3 * half:4 * half]
    br_ret = jnp.zeros((tm, D_MODEL), F32)
    for hh in range(H_RET):
        hb = hb_ref[...]
        q = _dot(hb, wcat_ref[:, OFF_Q + hh * DK:OFF_Q + (hh + 1) * DK])
        k = _dot(hb, wcat_ref[:, OFF_K + hh * DK:OFF_K + (hh + 1) * DK])
        vb = _dot(hb, wcat_ref[:, OFF_V + hh * DV:OFF_V + (hh + 1) * DV]).astype(BF16)
        g = _dot(hb, wcat_ref[:, OFF_G + hh * DV:OFF_G + (hh + 1) * DV])
        qb = _rope(q, cos, sin).astype(BF16)
        kr = _rope(k, cosk, sink)
        kb = kr.astype(BF16)
        kdec = kdec_ref[hh]
        kdb = (kr * jnp.concatenate([kdec, kdec], axis=1)).astype(BF16)
        scores = _dot_nt(qb, kb) * dec_ref[hh]
        intra = _dot(scores.astype(BF16), vb)
        s_old = s_ref[hh]
        qdec = qdec_ref[hh]
        cross = _dot(qb, s_old.astype(BF16)) * jnp.concatenate([qdec] * (DV // LANES), axis=1)
        s_ref[hh] = gpow_ref[hh] * s_old + _dot_tn(kdb, vb)
        on = _group_norm(intra + cross)
        ry = (on * gnw_ref[:, hh * DV:(hh + 1) * DV] * _silu(g)).astype(BF16)
        br_ret = br_ret + _dot(ry, wbr_ret_ref[hh * DV:(hh + 1) * DV, :])

    hb = hb_ref[...]
    xr = _dot(hb, wcat_ref[:, OFF_XR:OFF_XR + D_RNN])
    xr_ref[SUBLANES:SUBLANES + tm, :] = xr
    cw = convw_ref[...]
    xconv = convb_ref[...]
    for j in range(CONV_W):
        sft = CONV_W - 1 - j
        xconv = xconv + xr_ref[SUBLANES - sft:SUBLANES - sft + tm, :] * cw[j:j + 1, :]

    @pl.when(t == nt - 1)
    def _():
        convnew_ref[...] = xr_ref[SUBLANES + tm - (CONV_W - 1):SUBLANES + tm, :]

    xr_ref[0:SUBLANES, :] = xr_ref[tm:tm + SUBLANES, :]

    a, b = _lru_coeffs(xconv, wrg_ref, bra_ref[...], brx_ref[...], lru_ref[...])
    rid = lax.broadcasted_iota(jnp.int32, (tm, D_RNN), 0) & (SUBLANES - 1)
    for s in (1, 2, 4):
        sa_ref[SUBLANES:SUBLANES + tm, :] = a
        sb_ref[SUBLANES:SUBLANES + tm, :] = b
        keep = rid >= s
        ap = jnp.where(keep, sa_ref[SUBLANES - s:SUBLANES - s + tm, :], 1.0)
        bp = jnp.where(keep, sb_ref[SUBLANES - s:SUBLANES - s + tm, :], 0.0)
        b = a * bp + b
        a = a * ap
    sa_ref[SUBLANES:SUBLANES + tm, :] = a
    sb_ref[SUBLANES:SUBLANES + tm, :] = b

    def group_step(gi, carry):
        r0 = pl.multiple_of(gi * SUBLANES, SUBLANES) + SUBLANES
        hg = sa_ref[pl.ds(r0, SUBLANES), :] * carry + sb_ref[pl.ds(r0, SUBLANES), :]
        sb_ref[pl.ds(r0, SUBLANES), :] = hg
        return hg[SUBLANES - 1:SUBLANES, :]

    carry = lax.fori_loop(0, tm // SUBLANES, group_step, hc_ref[0:1, :])
    hc_ref[0:1, :] = carry

    @pl.when(t == nt - 1)
    def _():
        hlast_ref[...] = carry

    hseq = sb_ref[SUBLANES:SUBLANES + tm, :]
    gr = _dot(hb, wcat_ref[:, OFF_GR:OFF_GR + D_RNN])
    rnn_y_b = (hseq * jax.nn.gelu(gr, approximate=True)).astype(BF16)
    gate_pre = _dot(hb, wcat_ref[:, OFF_MG:OFF_MG + 2 * D_MODEL])
    x1_ref[...] = _mix_tail(x, g1, br_ret, rnn_y_b, gate_pre, bmg_ref[...], wbr_rnn_ref, wout_ref,
                            npost_ref[...])


def _const_spec(shape):
    nd = len(shape)
    return pl.BlockSpec(shape, lambda *_: (0,) * nd, pipeline_mode=pl.Buffered(1))


def _mix_prompt_call(x, mod3, gpow, rope_tab, dec, qdec, kdec, p):
    nb, seq, _ = x.shape
    tm = PROMPT_TM
    nt = seq // tm
    in_specs = [
        pl.BlockSpec(memory_space=pltpu.SMEM),
        pl.BlockSpec((None, tm, D_MODEL), lambda b, t: (b, t, 0)),
        pl.BlockSpec((None, 1, 3 * D_MODEL), lambda b, t: (b, 0, 0)),
        _const_spec((1, D_MODEL)),
        _const_spec((D_MODEL, N_CAT)),
        _const_spec((1, 2 * D_MODEL)),
        pl.BlockSpec((tm, 4 * (DK // 2)), lambda b, t: (t, 0)),
        _const_spec((H_RET, tm, tm)),
        _const_spec((H_RET, tm, LANES)),
        _const_spec((H_RET, tm, LANES)),
        _const_spec((1, D_V)),
        _const_spec((D_V, D_MODEL)),
        _const_spec((CONV_W, D_RNN)),
        _const_spec((1, D_RNN)),
        _const_spec((N_RNN_BLOCKS, RNN_BLOCK, 2 * RNN_BLOCK)),
        _const_spec((1, D_RNN)),
        _const_spec((1, D_RNN)),
        _const_spec((1, D_RNN)),
        _const_spec((D_RNN, D_MODEL)),
        _const_spec((D_MODEL, D_MODEL)),
        _const_spec((1, D_MODEL)),
    ]
    out_specs = [
        pl.BlockSpec((None, tm, D_MODEL), lambda b, t: (b, t, 0)),
        pl.BlockSpec((None, H_RET, DK, DV), lambda b, t: (b, 0, 0, 0)),
        pl.BlockSpec((None, 1, D_RNN), lambda b, t: (b, 0, 0)),
        pl.BlockSpec((None, CONV_W - 1, D_RNN), lambda b, t: (b, 0, 0)),
    ]
    out_shape = [
        jax.ShapeDtypeStruct((nb, seq, D_MODEL), F32),
        jax.ShapeDtypeStruct((nb, H_RET, DK, DV), F32),
        jax.ShapeDtypeStruct((nb, 1, D_RNN), F32),
        jax.ShapeDtypeStruct((nb, CONV_W - 1, D_RNN), F32),
    ]
    scratch = [
        pltpu.VMEM((tm, D_MODEL), BF16),
        pltpu.VMEM((tm + SUBLANES, D_RNN), F32),
        pltpu.VMEM((tm + SUBLANES, D_RNN), F32),
        pltpu.VMEM((tm + SUBLANES, D_RNN), F32),
        pltpu.VMEM((SUBLANES, D_RNN), F32),
    ]
    return pl.pallas_call(
        _mix_prompt_kernel,
        grid=(nb, nt),
        in_specs=in_specs,
        out_specs=out_specs,
        out_shape=out_shape,
        scratch_shapes=scratch,
        compiler_params=pltpu.CompilerParams(
            dimension_semantics=("arbitrary", "arbitrary"),
            vmem_limit_bytes=VMEM_BYTES_V7X - 4 * 1024 * 1024),
    )(gpow, x, mod3, p['npre'], p['wcat'], p['bmg'], rope_tab, dec, qdec, kdec, p['gnw'], p['wbr_ret'],
      p['convw'], p['convb'], p['wrg'], p['bra'], p['brx'], p['lru'], p['wbr_rnn'], p['wout'], p['npost'])


def _ffn_kernel(x_ref, mod_ref, npre_ref, w1_ref, w2_ref, npost_ref, o_ref):
    x = x_ref[...]
    m = mod_ref[...]
    sh2, sc2, g2 = m[:, :D_MODEL], m[:, D_MODEL:2 * D_MODEL], m[:, 2 * D_MODEL:]
    h2 = (_rms(x, npre_ref[...]) * (1.0 + sc2) + sh2).astype(BF16)
    acc = jnp.zeros(x.shape, F32)
    for j in range(D_FF // FF_CHUNK):
        c0 = j * FF_CHUNK
        fg = _dot(h2, w1_ref[:, c0:c0 + FF_CHUNK])
        fu = _dot(h2, w1_ref[:, D_FF + c0:D_FF + c0 + FF_CHUNK])
        acc = acc + _dot((_silu(fg) * fu).astype(BF16), w2_ref[c0:c0 + FF_CHUNK, :])
    o_ref[...] = x + g2 * _rms(acc, npost_ref[...])


def _ffn_call(x2d, mod, mod_rows_per_tile, seq_tiles, p):
    rows = x2d.shape[0]
    tm = FFN_TM
    if mod_rows_per_tile == 1:
        mod_spec = pl.BlockSpec((None, 1, 3 * D_MODEL), lambda i: (i // seq_tiles, 0, 1))
    else:
        mod_spec = pl.BlockSpec((tm, 3 * D_MODEL), lambda i: (i, 1))
    return pl.pallas_call(
        _ffn_kernel,
        grid=(rows // tm,),
        in_specs=[
            pl.BlockSpec((tm, D_MODEL), lambda i: (i, 0)),
            mod_spec,
            _const_spec((1, D_MODEL)),
            _const_spec((D_MODEL, 2 * D_FF)),
            _const_spec((D_FF, D_MODEL)),
            _const_spec((1, D_MODEL)),
        ],
        out_specs=pl.BlockSpec((tm, D_MODEL), lambda i: (i, 0)),
        out_shape=jax.ShapeDtypeStruct((rows, D_MODEL), F32),
        compiler_params=pltpu.CompilerParams(
            dimension_semantics=("arbitrary",),
            vmem_limit_bytes=48 * 1024 * 1024),
    )(x2d, mod, p['npre_ffn'], p['wffn_in'], p['wffn_out'], p['npost_ffn'])


def _proj_sample_kernel(x_ref, mod_ref, npre_ref, w_ref, o_ref):
    m = mod_ref[...]
    sh1, sc1 = m[:, :D_MODEL], m[:, D_MODEL:2 * D_MODEL]
    h = (_rms(x_ref[...], npre_ref[...]) * (1.0 + sc1) + sh1).astype(BF16)
    o_ref[...] = _dot(h, w_ref[...])


def _proj_sample_call(x2d, mod_s, p):
    rows = x2d.shape[0]
    tn = D_MODEL
    return pl.pallas_call(
        _proj_sample_kernel,
        grid=(N_CAT // tn,),
        in_specs=[
            pl.BlockSpec((rows, D_MODEL), lambda j: (0, 0)),
            pl.BlockSpec((rows, 3 * D_MODEL), lambda j: (0, 0)),
            pl.BlockSpec((1, D_MODEL), lambda j: (0, 0)),
            pl.BlockSpec((D_MODEL, tn), lambda j: (0, j)),
        ],
        out_specs=pl.BlockSpec((rows, tn), lambda j: (0, j)),
        out_shape=jax.ShapeDtypeStruct((rows, N_CAT), F32),
        compiler_params=pltpu.CompilerParams(
            dimension_semantics=("arbitrary",),
            vmem_limit_bytes=48 * 1024 * 1024),
    )(x2d, mod_s, p['npre'], p['wcat'])


def _ret_sample_kernel(gpow_ref, qkv_ref, s_ref, rope_ref, dec_ref, qdec_ref, kdec_ref, o_ref, snew_ref):
    half = DK // 2
    cos, sin = rope_ref[:, 0:half], rope_ref[:, half:2 * half]
    cosk, sink = rope_ref[:, 2 * half:3 * half], rope_ref[:, 3 * half:4 * half]
    nseq = RET_SAMPLE_BB
    tlen = SUBLANES // nseq
    row = lax.broadcasted_iota(jnp.int32, (SUBLANES, LANES), 0)
    for hh in range(H_RET):
        q = _rope(qkv_ref[:, OFF_Q + hh * DK:OFF_Q + (hh + 1) * DK], cos, sin)
        k = _rope(qkv_ref[:, OFF_K + hh * DK:OFF_K + (hh + 1) * DK], cosk, sink)
        v = qkv_ref[:, OFF_V + hh * DV:OFF_V + (hh + 1) * DV]
        kdec = kdec_ref[hh]
        kd = k * jnp.concatenate([kdec, kdec], axis=1)
        scores = _dot_nt(q, k) * dec_ref[hh]
        intra = _dot(scores, v)
        qb = q.astype(BF16)
        cross = jnp.zeros((SUBLANES, DV), F32)
        for bi in range(nseq):
            s_old = s_ref[bi, hh]
            in_seq = (row >= bi * tlen) & (row < (bi + 1) * tlen)
            cr = _dot(qb, s_old.astype(BF16))
            cross = jnp.where(jnp.concatenate([in_seq] * (DV // LANES), axis=1), cr, cross)
            kd_b = jnp.where(jnp.concatenate([in_seq] * (DK // LANES), axis=1), kd, 0.0)
            snew_ref[bi, hh] = gpow_ref[hh] * s_old + _dot_tn(kd_b, v)
        qdec = qdec_ref[hh]
        o_ref[:, hh * DV:(hh + 1) * DV] = intra + cross * jnp.concatenate([qdec] * (DV // LANES), axis=1)


def _ret_sample_call(proj_s, state, gpow, rope8, dec8, qdec8, kdec8):
    nb = state.shape[0]
    bb = RET_SAMPLE_BB
    rows = proj_s.shape[0]
    qkv_w = OFF_G
    return pl.pallas_call(
        _ret_sample_kernel,
        grid=(nb // bb,),
        in_specs=[
            pl.BlockSpec(memory_space=pltpu.SMEM),
            pl.BlockSpec((SUBLANES, qkv_w), lambda i: (i, 0)),
            pl.BlockSpec((bb, H_RET, DK, DV), lambda i: (i, 0, 0, 0)),
            pl.BlockSpec((SUBLANES, 4 * (DK // 2)), lambda i: (0, 0)),
            pl.BlockSpec((H_RET, SUBLANES, SUBLANES), lambda i: (0, 0, 0)),
            pl.BlockSpec((H_RET, SUBLANES, LANES), lambda i: (0, 0, 0)),
            pl.BlockSpec((H_RET, SUBLANES, LANES), lambda i: (0, 0, 0)),
        ],
        out_specs=[
            pl.BlockSpec((SUBLANES, D_V), lambda i: (i, 0)),
            pl.BlockSpec((bb, H_RET, DK, DV), lambda i: (i, 0, 0, 0)),
        ],
        out_shape=[
            jax.ShapeDtypeStruct((rows, D_V), F32),
            jax.ShapeDtypeStruct(state.shape, F32),
        ],
        compiler_params=pltpu.CompilerParams(
            dimension_semantics=("arbitrary",),
            vmem_limit_bytes=40 * 1024 * 1024),
    )(gpow, proj_s, state, rope8, dec8, qdec8, kdec8)


def _mix_sample_kernel(x_ref, mod_ref, proj_ref, o_ref, h0_ref, xs1_ref, xs2_ref, xs3_ref,
                       bmg_ref, gnw_ref, wbr_ret_ref, convw_ref, convb_ref, wrg_ref, bra_ref, brx_ref,
                       lru_ref, wbr_rnn_ref, wout_ref, npost_ref,
                       x1_ref, hseq_ref,
                       xr_ref, sa_ref, sb_ref):
    tm = SAMPLE_TM
    tlen = 4
    x = x_ref[...]
    g1 = mod_ref[:, 2 * D_MODEL:]

    br_ret = jnp.zeros((tm, D_MODEL), F32)
    for hh in range(H_RET):
        on = _group_norm(o_ref[:, hh * DV:(hh + 1) * DV])
        g = proj_ref[:, OFF_G + hh * DV:OFF_G + (hh + 1) * DV]
        ry = (on * gnw_ref[:, hh * DV:(hh + 1) * DV] * _silu(g)).astype(BF16)
        br_ret = br_ret + _dot(ry, wbr_ret_ref[hh * DV:(hh + 1) * DV, :])

    tpos = lax.broadcasted_iota(jnp.int32, (tm, D_RNN), 0) & (tlen - 1)
    xr = proj_ref[:, OFF_XR:OFF_XR + D_RNN]
    xr_ref[0:SUBLANES, :] = jnp.zeros((SUBLANES, D_RNN), F32)
    xr_ref[SUBLANES:SUBLANES + tm, :] = xr
    cw = convw_ref[...]
    prev = (xs3_ref, xs2_ref, xs1_ref)
    xconv = convb_ref[...]
    for j in range(CONV_W - 1):
        sft = CONV_W - 1 - j
        shifted = jnp.where(tpos >= sft, xr_ref[SUBLANES - sft:SUBLANES - sft + tm, :], 0.0)
        xconv = xconv + (shifted + prev[j][...]) * cw[j:j + 1, :]
    xconv = xconv + xr * cw[CONV_W - 1:CONV_W, :]

    a, b = _lru_coeffs(xconv, wrg_ref, bra_ref[...], brx_ref[...], lru_ref[...])
    b = jnp.where(tpos == 0, b + a * h0_ref[...], b)
    sa_ref[0:SUBLANES, :] = jnp.zeros((SUBLANES, D_RNN), F32)
    sb_ref[0:SUBLANES, :] = jnp.zeros((SUBLANES, D_RNN), F32)
    for s in (1, 2):
        sa_ref[SUBLANES:SUBLANES + tm, :] = a
        sb_ref[SUBLANES:SUBLANES + tm, :] = b
        keep = tpos >= s
        ap = jnp.where(keep, sa_ref[SUBLANES - s:SUBLANES - s + tm, :], 1.0)
        bp = jnp.where(keep, sb_ref[SUBLANES - s:SUBLANES - s + tm, :], 0.0)
        b = a * bp + b
        a = a * ap
    hseq_ref[...] = b

    gr = proj_ref[:, OFF_GR:OFF_GR + D_RNN]
    rnn_y_b = (b * jax.nn.gelu(gr, approximate=True)).astype(BF16)
    gate_pre = proj_ref[:, OFF_MG:OFF_MG + 2 * D_MODEL]
    x1_ref[...] = _mix_tail(x, g1, br_ret, rnn_y_b, gate_pre, bmg_ref[...], wbr_rnn_ref, wout_ref,
                            npost_ref[...])


def _mix_sample_call(x2d, mod_s, proj_s, o_s, h0rep, xs1, xs2, xs3, p):
    rows = x2d.shape[0]
    tm = SAMPLE_TM
    row_spec = lambda w: pl.BlockSpec((tm, w), lambda i: (i, 0))
    return pl.pallas_call(
        _mix_sample_kernel,
        grid=(rows // tm,),
        in_specs=[
            row_spec(D_MODEL), row_spec(3 * D_MODEL), row_spec(N_CAT), row_spec(D_V),
            row_spec(D_RNN), row_spec(D_RNN), row_spec(D_RNN), row_spec(D_RNN),
            _const_spec((1, 2 * D_MODEL)),
            _const_spec((1, D_V)),
            _const_spec((D_V, D_MODEL)),
            _const_spec((CONV_W, D_RNN)),
            _const_spec((1, D_RNN)),
            _const_spec((N_RNN_BLOCKS, RNN_BLOCK, 2 * RNN_BLOCK)),
            _const_spec((1, D_RNN)),
            _const_spec((1, D_RNN)),
            _const_spec((1, D_RNN)),
            _const_spec((D_RNN, D_MODEL)),
            _const_spec((D_MODEL, D_MODEL)),
            _const_spec((1, D_MODEL)),
        ],
        out_specs=[row_spec(D_MODEL), row_spec(D_RNN)],
        out_shape=[
            jax.ShapeDtypeStruct((rows, D_MODEL), F32),
            jax.ShapeDtypeStruct((rows, D_RNN), F32),
        ],
        scratch_shapes=[
            pltpu.VMEM((tm + SUBLANES, D_RNN), F32),
            pltpu.VMEM((tm + SUBLANES, D_RNN), F32),
            pltpu.VMEM((tm + SUBLANES, D_RNN), F32),
        ],
        compiler_params=pltpu.CompilerParams(
            dimension_semantics=("arbitrary",),
            vmem_limit_bytes=56 * 1024 * 1024),
    )(x2d, mod_s, proj_s, o_s, h0rep, xs1, xs2, xs3, p['bmg'], p['gnw'], p['wbr_ret'], p['convw'],
      p['convb'], p['wrg'], p['bra'], p['brx'], p['lru'], p['wbr_rnn'], p['wout'], p['npost'])


def _rope_table(pos):
    half = DK // 2
    inv = ROPE_BASE ** (-jnp.arange(half, dtype=F32) / half)
    ang = pos.astype(F32)[:, None] * inv[None, :]
    cos, sin = jnp.cos(ang), jnp.sin(ang)
    ks = DK ** -0.5
    return jnp.concatenate([cos, sin, cos * ks, sin * ks], axis=1)


def _decay_tables(tpos, same_seq, chunk):
    lg = jnp.log(1.0 - 2.0 ** (-5.0 - jnp.arange(H_RET, dtype=F32)))
    idx = tpos.astype(F32)
    diff = idx[:, None] - idx[None, :]
    causal = (diff >= 0) & same_seq
    dec = jnp.where(causal[None], jnp.exp(jnp.where(causal, diff, 0.0)[None] * lg[:, None, None]), 0.0)
    qdec = jnp.exp((idx + 1.0)[None, :] * lg[:, None])
    kdec = jnp.exp((chunk - 1.0 - idx)[None, :] * lg[:, None])
    rep = lambda a: jnp.broadcast_to(a[:, :, None], a.shape + (LANES,))
    gpow = jnp.exp(chunk * lg)
    return dec, rep(qdec), rep(kdec), gpow


def kernel(x_prompt, x_sample, state_ret, state_rnn_h, state_rnn_conv, c_prompt, c_sample,
           w_ada, b_ada, norm_pre_mix, norm_post_mix, norm_pre_ffn, norm_post_ffn,
           w_in, ret_gn_w, w_br_ret, conv_w, conv_b, w_rg_a, b_rg_a, w_rg_x, b_rg_x,
           lru_param, w_br_rnn, w_mgate, b_mgate, w_out, w_ffn_in, w_ffn_out):
    depth = w_in.shape[0]
    assert depth == 1, "single layer step"
    nb, seq, _ = x_prompt.shape
    nsb, sseq, _ = x_sample.shape
    assert seq % PROMPT_TM == 0 and sseq * RET_SAMPLE_BB == SUBLANES and sseq == CONV_W
    l = 0
    row = lambda a: a[l][None, :]
    p = dict(
        npre=row(norm_pre_mix), npost=row(norm_post_mix), npre_ffn=row(norm_pre_ffn), npost_ffn=row(norm_post_ffn),
        wcat=jnp.concatenate([w_in[l], w_mgate[l]], axis=1).astype(BF16),
        bmg=row(b_mgate), gnw=row(ret_gn_w), wbr_ret=w_br_ret[l].astype(BF16),
        convw=conv_w[l], convb=row(conv_b),
        wrg=jnp.concatenate([w_rg_a[l], w_rg_x[l]], axis=2).astype(BF16),
        bra=row(b_rg_a), brx=row(b_rg_x), lru=row(lru_param),
        wbr_rnn=w_br_rnn[l].astype(BF16), wout=w_out[l].astype(BF16),
        wffn_in=w_ffn_in[l].astype(BF16), wffn_out=w_ffn_out[l].astype(BF16),
    )

    c_all = jnp.concatenate([c_prompt, jnp.repeat(c_sample, sseq, axis=0)], axis=0)
    mod_all = _mod_call(c_all, w_ada[l].astype(BF16), row(b_ada))
    mod_p = mod_all[:nb].reshape(nb, 1, 6 * D_MODEL)
    mod_s = mod_all[nb:]

    tm = PROMPT_TM
    rope_p = _rope_table(jnp.arange(seq, dtype=jnp.int32))
    tpos = jnp.arange(tm)
    dec, qdec, kdec, gpow = _decay_tables(tpos, jnp.ones((tm, tm), bool), float(tm))
    x1p, ret_p, hlast_p, conv_p = _mix_prompt_call(x_prompt, mod_p, gpow, rope_p, dec, qdec, kdec, p)
    yp = _ffn_call(x1p.reshape(nb * seq, D_MODEL), mod_p, 1, seq // FFN_TM, p).reshape(nb, seq, D_MODEL)

    rows_s = nsb * sseq
    xs2d = x_sample.reshape(rows_s, D_MODEL)
    proj_s = _proj_sample_call(xs2d, mod_s, p)
    r8 = jnp.arange(SUBLANES)
    rope_s = _rope_table(PAST_LEN + (r8 % sseq).astype(jnp.int32))
    same = (r8[:, None] // sseq) == (r8[None, :] // sseq)
    dec8, qdec8, kdec8, gpow_s = _decay_tables(r8 % sseq, same, float(sseq))
    o_s, ret_s = _ret_sample_call(proj_s, state_ret[l], gpow_s, rope_s, dec8, qdec8, kdec8)
    cs = state_rnn_conv[l]
    pad_rows = lambda a: jnp.pad(a, ((0, 0), (0, sseq - a.shape[1]), (0, 0))).reshape(rows_s, D_RNN)
    xs1, xs2, xs3 = pad_rows(cs[:, 2:3]), pad_rows(cs[:, 1:3]), pad_rows(cs[:, 0:3])
    h0rep = jnp.repeat(state_rnn_h[l], sseq, axis=0)
    x1s, hseq_s = _mix_sample_call(xs2d, mod_s, proj_s, o_s, h0rep, xs1, xs2, xs3, p)
    ys = _ffn_call(x1s, mod_s, FFN_TM, 1, p).reshape(nsb, sseq, D_MODEL)
    hlast_s = hseq_s.reshape(nsb, sseq, D_RNN)[:, sseq - 1]
    conv_s = proj_s.reshape(nsb, sseq, N_CAT)[:, sseq - (CONV_W - 1):, OFF_XR:OFF_XR + D_RNN]

    return (yp, ys, ret_p[None], ret_s[None], hlast_p.reshape(nb, D_RNN)[None], hlast_s[None],
            conv_p[None], conv_s[None])
```

```python
import functools

import jax
import jax.numpy as jnp
from jax import lax
from jax.experimental import pallas as pl
from jax.experimental.pallas import tpu as pltpu

F32 = jnp.float32
BF16 = jnp.bfloat16

D_MODEL = 1024
H_RET = 4
DK = D_MODEL // H_RET
DV = 2 * DK
D_QK = H_RET * DK
D_V = H_RET * DV
D_RNN = 1536
RNN_BLOCK = 128
N_RNN_BLOCKS = D_RNN // RNN_BLOCK
CONV_W = 4
LRU_C = 8.0
D_FF = 2816
ROPE_BASE = 10000.0
GN_EPS = 1e-5
RMS_EPS = 1e-6
PAST_LEN = 16384

OFF_Q = 0
OFF_K = OFF_Q + D_QK
OFF_V = OFF_K + D_QK
OFF_G = OFF_V + D_V
OFF_XR = OFF_G + D_V
OFF_GR = OFF_XR + D_RNN
OFF_MG = OFF_GR + D_RNN
N_CAT = OFF_MG + 2 * D_MODEL

SUBLANES = 8
LANES = 128
MXU_DIM = 256
VMEM_BYTES_V7X = 64 * 1024 * 1024

PROMPT_TM = 256
FFN_TM = 256
FF_CHUNK = MXU_DIM
SAMPLE_TM = 128
RET_SAMPLE_BB = 2


def _dot(a, b):
    return jnp.dot(a, b, preferred_element_type=F32)


def _dot_nt(a, b):
    return lax.dot_general(a, b, (((1,), (1,)), ((), ())), preferred_element_type=F32)


def _dot_tn(a, b):
    return lax.dot_general(a, b, (((0,), (0,)), ((), ())), preferred_element_type=F32)


def _wb(ref, k0=None, k1=None, c0=None, c1=None):
    rs = slice(None) if k0 is None else slice(k0 // 2, k1 // 2)
    cs = slice(None) if c0 is None else slice(c0, c1)
    return pltpu.bitcast(ref[rs, cs], BF16)


def _pack_rows(w):
    wb = w.astype(BF16)
    *lead, k, n = wb.shape
    return lax.bitcast_convert_type(jnp.swapaxes(wb.reshape(*lead, k // 2, 2, n), -1, -2), jnp.uint32)


def _rms(x, w):
    ms = jnp.mean(x * x, axis=-1, keepdims=True)
    return x * lax.rsqrt(ms + RMS_EPS) * w


def _silu(x):
    return x * jax.nn.sigmoid(x)


def _rope(x, cos, sin):
    half = DK // 2
    x1, x2 = x[:, :half], x[:, half:]
    return jnp.concatenate([x1 * cos - x2 * sin, x1 * sin + x2 * cos], axis=1)


def _group_norm(o):
    mu = jnp.mean(o, axis=-1, keepdims=True)
    d = o - mu
    var = jnp.mean(d * d, axis=-1, keepdims=True)
    return d * lax.rsqrt(var + GN_EPS)


def _lru_coeffs(xconv, wrg_ref, b_a, b_x, lru):
    xcb = xconv.astype(BF16)
    pre = [_dot(xcb[:, n * RNN_BLOCK:(n + 1) * RNN_BLOCK], pltpu.bitcast(wrg_ref[n], BF16))
           for n in range(N_RNN_BLOCKS)]
    ra = jnp.concatenate([p[:, :RNN_BLOCK] for p in pre], axis=1) + b_a
    ri = jnp.concatenate([p[:, RNN_BLOCK:] for p in pre], axis=1) + b_x
    r = jax.nn.sigmoid(ra)
    i = jax.nn.sigmoid(ri)
    z = -lru
    sp = jnp.maximum(z, 0.0) + jnp.log(1.0 + jnp.exp(-jnp.abs(z)))
    log_a = -LRU_C * r * sp
    a = jnp.exp(log_a)
    beta = jnp.sqrt(-jnp.tanh(log_a) * (jnp.exp(2.0 * log_a) + 1.0))
    return a, beta * (i * xconv)


def _mix_tail(x, g1, br_ret, rnn_y_b, gate_pre, b_mg, wbr_rnn_ref, wout_ref, npost):
    br_rnn = _dot(rnn_y_b, _wb(wbr_rnn_ref))
    gates = jax.nn.sigmoid(gate_pre + b_mg)
    ga, gb = gates[:, :D_MODEL], gates[:, D_MODEL:]
    mixed = _dot((ga * br_ret + gb * br_rnn).astype(BF16), _wb(wout_ref))
    return x + g1 * _rms(mixed, npost)


def _mod_kernel(c_ref, w_ref, b_ref, o_ref):
    a = _silu(c_ref[...]).astype(BF16)
    o_ref[...] = _dot(a, _wb(w_ref)) + b_ref[...]


def _mod_call(c_all, w_ada_b, b_ada):
    rows = c_all.shape[0]
    tn = D_MODEL
    return pl.pallas_call(
        _mod_kernel,
        grid=(6 * D_MODEL // tn,),
        in_specs=[
            pl.BlockSpec((rows, D_MODEL), lambda j: (0, 0)),
            pl.BlockSpec((D_MODEL // 2, tn), lambda j: (0, j)),
            pl.BlockSpec((1, tn), lambda j: (0, j)),
        ],
        out_specs=pl.BlockSpec((rows, tn), lambda j: (0, j)),
        out_shape=jax.ShapeDtypeStruct((rows, 6 * D_MODEL), F32),
        compiler_params=pltpu.CompilerParams(dimension_semantics=("arbitrary",)),
    )(c_all, w_ada_b, b_ada)


def _mix_prompt_kernel(gpow_ref, x_ref, mod_ref, npre_ref, wcat_ref, bmg_ref, rope_ref,
                       dec_ref, qdec_ref, kdec_ref, gnw_ref, wbr_ret_ref, convw_ref, convb_ref,
                       wrg_ref, bra_ref, brx_ref, lru_ref, wbr_rnn_ref, wout_ref, npost_ref,
                       x1_ref, s_ref, hlast_ref, convnew_ref,
                       hb_ref, xr_ref, sa_ref, sb_ref, hc_ref):
    tm = PROMPT_TM
    t = pl.program_id(1)
    nt = pl.num_programs(1)

    @pl.when(t == 0)
    def _():
        s_ref[...] = jnp.zeros_like(s_ref)
        xr_ref[0:SUBLANES, :] = jnp.zeros((SUBLANES, D_RNN), F32)
        sa_ref[0:SUBLANES, :] = jnp.zeros((SUBLANES, D_RNN), F32)
        sb_ref[0:SUBLANES, :] = jnp.zeros((SUBLANES, D_RNN), F32)
        hc_ref[...] = jnp.zeros_like(hc_ref)

    x = x_ref[...]
    m = mod_ref[...]
    sh1, sc1, g1 = m[:, :D_MODEL], m[:, D_MODEL:2 * D_MODEL], m[:, 2 * D_MODEL:]
    hb_ref[...] = (_rms(x, npre_ref[...]) * (1.0 + sc1) + sh1).astype(BF16)

    half = DK // 2
    cos, sin = rope_ref[:, 0:half], rope_ref[:, half:2 * half]
    cosk, sink = rope_ref[:, 2 * half:3 * half], rope_ref[:, 3 * half:4 * half]
    br_ret = jnp.zeros((tm, D_MODEL), F32)
    for hh in range(H_RET):
        hb = hb_ref[...]
        q = _dot(hb, _wb(wcat_ref, c0=OFF_Q + hh * DK, c1=OFF_Q + (hh + 1) * DK))
        k = _dot(hb, _wb(wcat_ref, c0=OFF_K + hh * DK, c1=OFF_K + (hh + 1) * DK))
        vb = _dot(hb, _wb(wcat_ref, c0=OFF_V + hh * DV, c1=OFF_V + (hh + 1) * DV)).astype(BF16)
        g = _dot(hb, _wb(wcat_ref, c0=OFF_G + hh * DV, c1=OFF_G + (hh + 1) * DV))
        qb = _rope(q, cos, sin).astype(BF16)
        kr = _rope(k, cosk, sink)
        kb = kr.astype(BF16)
        kdec = kdec_ref[hh]
        kdb = (kr * jnp.concatenate([kdec, kdec], axis=1)).astype(BF16)
        scores = _dot_nt(qb, kb) * dec_ref[hh]
        intra = _dot(scores.astype(BF16), vb)
        s_old = s_ref[hh]
        qdec = qdec_ref[hh]
        cross = _dot(qb, s_old.astype(BF16)) * jnp.concatenate([qdec] * (DV // LANES), axis=1)
        s_ref[hh] = gpow_ref[hh] * s_old + _dot_tn(kdb, vb)
        on = _group_norm(intra + cross)
        ry = (on * gnw_ref[:, hh * DV:(hh + 1) * DV] * _silu(g)).astype(BF16)
        br_ret = br_ret + _dot(ry, _wb(wbr_ret_ref, hh * DV, (hh + 1) * DV))

    hb = hb_ref[...]
    xr = _dot(hb, _wb(wcat_ref, c0=OFF_XR, c1=OFF_XR + D_RNN))
    xr_ref[SUBLANES:SUBLANES + tm, :] = xr
    cw = convw_ref[...]
    xconv = convb_ref[...]
    for j in range(CONV_W):
        sft = CONV_W - 1 - j
        xconv = xconv + xr_ref[SUBLANES - sft:SUBLANES - sft + tm, :] * cw[j:j + 1, :]

    @pl.when(t == nt - 1)
    def _():
        convnew_ref[...] = xr_ref[SUBLANES + tm - (CONV_W - 1):SUBLANES + tm, :]

    xr_ref[0:SUBLANES, :] = xr_ref[tm:tm + SUBLANES, :]

    a, b = _lru_coeffs(xconv, wrg_ref, bra_ref[...], brx_ref[...], lru_ref[...])
    rid = lax.broadcasted_iota(jnp.int32, (tm, D_RNN), 0) & (SUBLANES - 1)
    for s in (1, 2, 4):
        sa_ref[SUBLANES:SUBLANES + tm, :] = a
        sb_ref[SUBLANES:SUBLANES + tm, :] = b
        keep = rid >= s
        ap = jnp.where(keep, sa_ref[SUBLANES - s:SUBLANES - s + tm, :], 1.0)
        bp = jnp.where(keep, sb_ref[SUBLANES - s:SUBLANES - s + tm, :], 0.0)
        b = a * bp + b
        a = a * ap
    sa_ref[SUBLANES:SUBLANES + tm, :] = a
    sb_ref[SUBLANES:SUBLANES + tm, :] = b

    def group_step(gi, carry):
        r0 = pl.multiple_of(gi * SUBLANES, SUBLANES) + SUBLANES
        hg = sa_ref[pl.ds(r0, SUBLANES), :] * carry + sb_ref[pl.ds(r0, SUBLANES), :]
        sb_ref[pl.ds(r0, SUBLANES), :] = hg
        return hg[SUBLANES - 1:SUBLANES, :]

    carry = lax.fori_loop(0, tm // SUBLANES, group_step, hc_ref[0:1, :])
    hc_ref[0:1, :] = carry

    @pl.when(t == nt - 1)
    def _():
        hlast_ref[...] = carry

    hseq = sb_ref[SUBLANES:SUBLANES + tm, :]
    gr = _dot(hb, _wb(wcat_ref, c0=OFF_GR, c1=OFF_GR + D_RNN))
    rnn_y_b = (hseq * jax.nn.gelu(gr, approximate=True)).astype(BF16)
    gate_pre = _dot(hb, _wb(wcat_ref, c0=OFF_MG, c1=OFF_MG + 2 * D_MODEL))
    x1_ref[...] = _mix_tail(x, g1, br_ret, rnn_y_b, gate_pre, bmg_ref[...], wbr_rnn_ref, wout_ref,
                            npost_ref[...])


def _const_spec(shape):
    nd = len(shape)
    return pl.BlockSpec(shape, lambda *_: (0,) * nd, pipeline_mode=pl.Buffered(1))


def _mix_prompt_call(x, mod3, gpow, rope_tab, dec, qdec, kdec, p):
    nb, seq, _ = x.shape
    tm = PROMPT_TM
    nt = seq // tm
    in_specs = [
        pl.BlockSpec(memory_space=pltpu.SMEM),
        pl.BlockSpec((None, tm, D_MODEL), lambda b, t: (b, t, 0)),
        pl.BlockSpec((None, 1, 3 * D_MODEL), lambda b, t: (b, 0, 0)),
        _const_spec((1, D_MODEL)),
        _const_spec((D_MODEL // 2, N_CAT)),
        _const_spec((1, 2 * D_MODEL)),
        pl.BlockSpec((tm, 4 * (DK // 2)), lambda b, t: (t, 0)),
        _const_spec((H_RET, tm, tm)),
        _const_spec((H_RET, tm, LANES)),
        _const_spec((H_RET, tm, LANES)),
        _const_spec((1, D_V)),
        _const_spec((D_V // 2, D_MODEL)),
        _const_spec((CONV_W, D_RNN)),
        _const_spec((1, D_RNN)),
        _const_spec((N_RNN_BLOCKS, RNN_BLOCK // 2, 2 * RNN_BLOCK)),
        _const_spec((1, D_RNN)),
        _const_spec((1, D_RNN)),
        _const_spec((1, D_RNN)),
        _const_spec((D_RNN // 2, D_MODEL)),
        _const_spec((D_MODEL // 2, D_MODEL)),
        _const_spec((1, D_MODEL)),
    ]
    out_specs = [
        pl.BlockSpec((None, tm, D_MODEL), lambda b, t: (b, t, 0)),
        pl.BlockSpec((None, H_RET, DK, DV), lambda b, t: (b, 0, 0, 0)),
        pl.BlockSpec((None, 1, D_RNN), lambda b, t: (b, 0, 0)),
        pl.BlockSpec((None, CONV_W - 1, D_RNN), lambda b, t: (b, 0, 0)),
    ]
    out_shape = [
        jax.ShapeDtypeStruct((nb, seq, D_MODEL), F32),
        jax.ShapeDtypeStruct((nb, H_RET, DK, DV), F32),
        jax.ShapeDtypeStruct((nb, 1, D_RNN), F32),
        jax.ShapeDtypeStruct((nb, CONV_W - 1, D_RNN), F32),
    ]
    scratch = [
        pltpu.VMEM((tm, D_MODEL), BF16),
        pltpu.VMEM((tm + SUBLANES, D_RNN), F32),
        pltpu.VMEM((tm + SUBLANES, D_RNN), F32),
        pltpu.VMEM((tm + SUBLANES, D_RNN), F32),
        pltpu.VMEM((SUBLANES, D_RNN), F32),
    ]
    return pl.pallas_call(
        _mix_prompt_kernel,
        grid=(nb, nt),
        in_specs=in_specs,
        out_specs=out_specs,
        out_shape=out_shape,
        scratch_shapes=scratch,
        compiler_params=pltpu.CompilerParams(
            dimension_semantics=("arbitrary", "arbitrary"),
            vmem_limit_bytes=VMEM_BYTES_V7X - 4 * 1024 * 1024),
    )(gpow, x, mod3, p['npre'], p['wcat'], p['bmg'], rope_tab, dec, qdec, kdec, p['gnw'], p['wbr_ret'],
      p['convw'], p['convb'], p['wrg'], p['bra'], p['brx'], p['lru'], p['wbr_rnn'], p['wout'], p['npost'])


def _ffn_kernel(x_ref, mod_ref, npre_ref, w1_ref, w2_ref, npost_ref, o_ref):
    x = x_ref[...]
    m = mod_ref[...]
    sh2, sc2, g2 = m[:, :D_MODEL], m[:, D_MODEL:2 * D_MODEL], m[:, 2 * D_MODEL:]
    h2 = (_rms(x, npre_ref[...]) * (1.0 + sc2) + sh2).astype(BF16)
    acc = jnp.zeros(x.shape, F32)
    for j in range(D_FF // FF_CHUNK):
        c0 = j * FF_CHUNK
        fg = _dot(h2, _wb(w1_ref, c0=c0, c1=c0 + FF_CHUNK))
        fu = _dot(h2, _wb(w1_ref, c0=D_FF + c0, c1=D_FF + c0 + FF_CHUNK))
        acc = acc + _dot((_silu(fg) * fu).astype(BF16), _wb(w2_ref, c0, c0 + FF_CHUNK))
    o_ref[...] = x + g2 * _rms(acc, npost_ref[...])


def _ffn_call(x2d, mod, mod_rows_per_tile, seq_tiles, p):
    rows = x2d.shape[0]
    tm = FFN_TM
    if mod_rows_per_tile == 1:
        mod_spec = pl.BlockSpec((None, 1, 3 * D_MODEL), lambda i: (i // seq_tiles, 0, 1))
    else:
        mod_spec = pl.BlockSpec((tm, 3 * D_MODEL), lambda i: (i, 1))
    return pl.pallas_call(
        _ffn_kernel,
        grid=(rows // tm,),
        in_specs=[
            pl.BlockSpec((tm, D_MODEL), lambda i: (i, 0)),
            mod_spec,
            _const_spec((1, D_MODEL)),
            _const_spec((D_MODEL // 2, 2 * D_FF)),
            _const_spec((D_FF // 2, D_MODEL)),
            _const_spec((1, D_MODEL)),
        ],
        out_specs=pl.BlockSpec((tm, D_MODEL), lambda i: (i, 0)),
        out_shape=jax.ShapeDtypeStruct((rows, D_MODEL), F32),
        compiler_params=pltpu.CompilerParams(
            dimension_semantics=("arbitrary",),
            vmem_limit_bytes=48 * 1024 * 1024),
    )(x2d, mod, p['npre_ffn'], p['wffn_in'], p['wffn_out'], p['npost_ffn'])


def _proj_sample_kernel(x_ref, mod_ref, npre_ref, w_ref, o_ref):
    m = mod_ref[...]
    sh1, sc1 = m[:, :D_MODEL], m[:, D_MODEL:2 * D_MODEL]
    h = (_rms(x_ref[...], npre_ref[...]) * (1.0 + sc1) + sh1).astype(BF16)
    o_ref[...] = _dot(h, _wb(w_ref))


def _proj_sample_call(x2d, mod_s, p):
    rows = x2d.shape[0]
    tn = D_MODEL
    return pl.pallas_call(
        _proj_sample_kernel,
        grid=(N_CAT // tn,),
        in_specs=[
            pl.BlockSpec((rows, D_MODEL), lambda j: (0, 0)),
            pl.BlockSpec((rows, 3 * D_MODEL), lambda j: (0, 0)),
            pl.BlockSpec((1, D_MODEL), lambda j: (0, 0)),
            pl.BlockSpec((D_MODEL // 2, tn), lambda j: (0, j)),
        ],
        out_specs=pl.BlockSpec((rows, tn), lambda j: (0, j)),
        out_shape=jax.ShapeDtypeStruct((rows, N_CAT), F32),
        compiler_params=pltpu.CompilerParams(
            dimension_semantics=("arbitrary",),
            vmem_limit_bytes=48 * 1024 * 1024),
    )(x2d, mod_s, p['npre'], p['wcat'])


def _ret_sample_kernel(gpow_ref, qkv_ref, s_ref, rope_ref, dec_ref, qdec_ref, kdec_ref, o_ref, snew_ref):
    half = DK // 2
    cos, sin = rope_ref[:, 0:half], rope_ref[:, half:2 * half]
    cosk, sink = rope_ref[:, 2 * half:3 * half], rope_ref[:, 3 * half:4 * half]
    nseq = RET_SAMPLE_BB
    tlen = SUBLANES // nseq
    row = lax.broadcasted_iota(jnp.int32, (SUBLANES, LANES), 0)
    for hh in range(H_RET):
        q = _rope(qkv_ref[:, OFF_Q + hh * DK:OFF_Q + (hh + 1) * DK], cos, sin)
        k = _rope(qkv_ref[:, OFF_K + hh * DK:OFF_K + (hh + 1) * DK], cosk, sink)
        v = qkv_ref[:, OFF_V + hh * DV:OFF_V + (hh + 1) * DV]
        kdec = kdec_ref[hh]
        kd = k * jnp.concatenate([kdec, kdec], axis=1)
        scores = _dot_nt(q, k) * dec_ref[hh]
        intra = _dot(scores, v)
        qb = q.astype(BF16)
        cross = jnp.zeros((SUBLANES, DV), F32)
        for bi in range(nseq):
            s_old = s_ref[bi, hh]
            in_seq = (row >= bi * tlen) & (row < (bi + 1) * tlen)
            cr = _dot(qb, s_old.astype(BF16))
            cross = jnp.where(jnp.concatenate([in_seq] * (DV // LANES), axis=1), cr, cross)
            kd_b = jnp.where(jnp.concatenate([in_seq] * (DK // LANES), axis=1), kd, 0.0)
            snew_ref[bi, hh] = gpow_ref[hh] * s_old + _dot_tn(kd_b, v)
        qdec = qdec_ref[hh]
        o_ref[:, hh * DV:(hh + 1) * DV] = intra + cross * jnp.concatenate([qdec] * (DV // LANES), axis=1)


def _ret_sample_call(proj_s, state, gpow, rope8, dec8, qdec8, kdec8):
    nb = state.shape[0]
    bb = RET_SAMPLE_BB
    rows = proj_s.shape[0]
    qkv_w = OFF_G
    return pl.pallas_call(
        _ret_sample_kernel,
        grid=(nb // bb,),
        in_specs=[
            pl.BlockSpec(memory_space=pltpu.SMEM),
            pl.BlockSpec((SUBLANES, qkv_w), lambda i: (i, 0)),
            pl.BlockSpec((bb, H_RET, DK, DV), lambda i: (i, 0, 0, 0)),
            pl.BlockSpec((SUBLANES, 4 * (DK // 2)), lambda i: (0, 0)),
            pl.BlockSpec((H_RET, SUBLANES, SUBLANES), lambda i: (0, 0, 0)),
            pl.BlockSpec((H_RET, SUBLANES, LANES), lambda i: (0, 0, 0)),
            pl.BlockSpec((H_RET, SUBLANES, LANES), lambda i: (0, 0, 0)),
        ],
        out_specs=[
            pl.BlockSpec((SUBLANES, D_V), lambda i: (i, 0)),
            pl.BlockSpec((bb, H_RET, DK, DV), lambda i: (i, 0, 0, 0)),
        ],
        out_shape=[
            jax.ShapeDtypeStruct((rows, D_V), F32),
            jax.ShapeDtypeStruct(state.shape, F32),
        ],
        compiler_params=pltpu.CompilerParams(
            dimension_semantics=("arbitrary",),
            vmem_limit_bytes=40 * 1024 * 1024),
    )(gpow, proj_s, state, rope8, dec8, qdec8, kdec8)


def _mix_sample_kernel(x_ref, mod_ref, proj_ref, o_ref, h0_ref, xs1_ref, xs2_ref, xs3_ref,
                       bmg_ref, gnw_ref, wbr_ret_ref, convw_ref, convb_ref, wrg_ref, bra_ref, brx_ref,
                       lru_ref, wbr_rnn_ref, wout_ref, npost_ref,
                       x1_ref, hseq_ref,
                       xr_ref, sa_ref, sb_ref):
    tm = SAMPLE_TM
    tlen = 4
    x = x_ref[...]
    g1 = mod_ref[:, 2 * D_MODEL:]

    br_ret = jnp.zeros((tm, D_MODEL), F32)
    for hh in range(H_RET):
        on = _group_norm(o_ref[:, hh * DV:(hh + 1) * DV])
        g = proj_ref[:, OFF_G + hh * DV:OFF_G + (hh + 1) * DV]
        ry = (on * gnw_ref[:, hh * DV:(hh + 1) * DV] * _silu(g)).astype(BF16)
        br_ret = br_ret + _dot(ry, _wb(wbr_ret_ref, hh * DV, (hh + 1) * DV))

    tpos = lax.broadcasted_iota(jnp.int32, (tm, D_RNN), 0) & (tlen - 1)
    xr = proj_ref[:, OFF_XR:OFF_XR + D_RNN]
    xr_ref[0:SUBLANES, :] = jnp.zeros((SUBLANES, D_RNN), F32)
    xr_ref[SUBLANES:SUBLANES + tm, :] = xr
    cw = convw_ref[...]
    prev = (xs3_ref, xs2_ref, xs1_ref)
    xconv = convb_ref[...]
    for j in range(CONV_W - 1):
        sft = CONV_W - 1 - j
        shifted = jnp.where(tpos >= sft, xr_ref[SUBLANES - sft:SUBLANES - sft + tm, :], 0.0)
        xconv = xconv + (shifted + prev[j][...]) * cw[j:j + 1, :]
    xconv = xconv + xr * cw[CONV_W - 1:CONV_W, :]

    a, b = _lru_coeffs(xconv, wrg_ref, bra_ref[...], brx_ref[...], lru_ref[...])
    b = jnp.where(tpos == 0, b + a * h0_ref[...], b)
    sa_ref[0:SUBLANES, :] = jnp.zeros((SUBLANES, D_RNN), F32)
    sb_ref[0:SUBLANES, :] = jnp.zeros((SUBLANES, D_RNN), F32)
    for s in (1, 2):
        sa_ref[SUBLANES:SUBLANES + tm, :] = a
        sb_ref[SUBLANES:SUBLANES + tm, :] = b
        keep = tpos >= s
        ap = jnp.where(keep, sa_ref[SUBLANES - s:SUBLANES - s + tm, :], 1.0)
        bp = jnp.where(keep, sb_ref[SUBLANES - s:SUBLANES - s + tm, :], 0.0)
        b = a * bp + b
        a = a * ap
    hseq_ref[...] = b

    gr = proj_ref[:, OFF_GR:OFF_GR + D_RNN]
    rnn_y_b = (b * jax.nn.gelu(gr, approximate=True)).astype(BF16)
    gate_pre = proj_ref[:, OFF_MG:OFF_MG + 2 * D_MODEL]
    x1_ref[...] = _mix_tail(x, g1, br_ret, rnn_y_b, gate_pre, bmg_ref[...], wbr_rnn_ref, wout_ref,
                            npost_ref[...])


def _mix_sample_call(x2d, mod_s, proj_s, o_s, h0rep, xs1, xs2, xs3, p):
    rows = x2d.shape[0]
    tm = SAMPLE_TM
    row_spec = lambda w: pl.BlockSpec((tm, w), lambda i: (i, 0))
    return pl.pallas_call(
        _mix_sample_kernel,
        grid=(rows // tm,),
        in_specs=[
            row_spec(D_MODEL), row_spec(3 * D_MODEL), row_spec(N_CAT), row_spec(D_V),
            row_spec(D_RNN), row_spec(D_RNN), row_spec(D_RNN), row_spec(D_RNN),
            _const_spec((1, 2 * D_MODEL)),
            _const_spec((1, D_V)),
            _const_spec((D_V // 2, D_MODEL)),
            _const_spec((CONV_W, D_RNN)),
            _const_spec((1, D_RNN)),
            _const_spec((N_RNN_BLOCKS, RNN_BLOCK // 2, 2 * RNN_BLOCK)),
            _const_spec((1, D_RNN)),
            _const_spec((1, D_RNN)),
            _const_spec((1, D_RNN)),
            _const_spec((D_RNN // 2, D_MODEL)),
            _const_spec((D_MODEL // 2, D_MODEL)),
            _const_spec((1, D_MODEL)),
        ],
        out_specs=[row_spec(D_MODEL), row_spec(D_RNN)],
        out_shape=[
            jax.ShapeDtypeStruct((rows, D_MODEL), F32),
            jax.ShapeDtypeStruct((rows, D_RNN), F32),
        ],
        scratch_shapes=[
            pltpu.VMEM((tm + SUBLANES, D_RNN), F32),
            pltpu.VMEM((tm + SUBLANES, D_RNN), F32),
            pltpu.VMEM((tm + SUBLANES, D_RNN), F32),
        ],
        compiler_params=pltpu.CompilerParams(
            dimension_semantics=("arbitrary",),
            vmem_limit_bytes=56 * 1024 * 1024),
    )(x2d, mod_s, proj_s, o_s, h0rep, xs1, xs2, xs3, p['bmg'], p['gnw'], p['wbr_ret'], p['convw'],
      p['convb'], p['wrg'], p['bra'], p['brx'], p['lru'], p['wbr_rnn'], p['wout'], p['npost'])


def _rope_table(pos):
    half = DK // 2
    inv = ROPE_BASE ** (-jnp.arange(half, dtype=F32) / half)
    ang = pos.astype(F32)[:, None] * inv[None, :]
    cos, sin = jnp.cos(ang), jnp.sin(ang)
    ks = DK ** -0.5
    return jnp.concatenate([cos, sin, cos * ks, sin * ks], axis=1)


def _decay_tables(tpos, same_seq, chunk):
    lg = jnp.log(1.0 - 2.0 ** (-5.0 - jnp.arange(H_RET, dtype=F32)))
    idx = tpos.astype(F32)
    diff = idx[:, None] - idx[None, :]
    causal = (diff >= 0) & same_seq
    dec = jnp.where(causal[None], jnp.exp(jnp.where(causal, diff, 0.0)[None] * lg[:, None, None]), 0.0)
    qdec = jnp.exp((idx + 1.0)[None, :] * lg[:, None])
    kdec = jnp.exp((chunk - 1.0 - idx)[None, :] * lg[:, None])
    rep = lambda a: jnp.broadcast_to(a[:, :, None], a.shape + (LANES,))
    gpow = jnp.exp(chunk * lg)
    return dec, rep(qdec), rep(kdec), gpow


def kernel(x_prompt, x_sample, state_ret, state_rnn_h, state_rnn_conv, c_prompt, c_sample,
           w_ada, b_ada, norm_pre_mix, norm_post_mix, norm_pre_ffn, norm_post_ffn,
           w_in, ret_gn_w, w_br_ret, conv_w, conv_b, w_rg_a, b_rg_a, w_rg_x, b_rg_x,
           lru_param, w_br_rnn, w_mgate, b_mgate, w_out, w_ffn_in, w_ffn_out):
    depth = w_in.shape[0]
    assert depth == 1, "single layer step"
    nb, seq, _ = x_prompt.shape
    nsb, sseq, _ = x_sample.shape
    assert seq % PROMPT_TM == 0 and sseq * RET_SAMPLE_BB == SUBLANES and sseq == CONV_W
    l = 0
    row = lambda a: a[l][None, :]
    p = dict(
        npre=row(norm_pre_mix), npost=row(norm_post_mix), npre_ffn=row(norm_pre_ffn), npost_ffn=row(norm_post_ffn),
        wcat=_pack_rows(jnp.concatenate([w_in[l], w_mgate[l]], axis=1)),
        bmg=row(b_mgate), gnw=row(ret_gn_w), wbr_ret=_pack_rows(w_br_ret[l]),
        convw=conv_w[l], convb=row(conv_b),
        wrg=_pack_rows(jnp.concatenate([w_rg_a[l], w_rg_x[l]], axis=2)),
        bra=row(b_rg_a), brx=row(b_rg_x), lru=row(lru_param),
        wbr_rnn=_pack_rows(w_br_rnn[l]), wout=_pack_rows(w_out[l]),
        wffn_in=_pack_rows(w_ffn_in[l]), wffn_out=_pack_rows(w_ffn_out[l]),
    )

    rows_s = nsb * sseq
    c_all = jnp.concatenate([jnp.repeat(c_sample, sseq, axis=0), c_prompt], axis=0)
    mod_all = _mod_call(c_all, _pack_rows(w_ada[l]), row(b_ada))
    mod_p = mod_all[rows_s:].reshape(nb, 1, 6 * D_MODEL)
    mod_s = mod_all

    tm = PROMPT_TM
    rope_p = _rope_table(jnp.arange(seq, dtype=jnp.int32))
    tpos = jnp.arange(tm)
    dec, qdec, kdec, gpow = _decay_tables(tpos, jnp.ones((tm, tm), bool), float(tm))
    x1p, ret_p, hlast_p, conv_p = _mix_prompt_call(x_prompt, mod_p, gpow, rope_p, dec, qdec, kdec, p)
    yp = _ffn_call(x1p.reshape(nb * seq, D_MODEL), mod_p, 1, seq // FFN_TM, p).reshape(nb, seq, D_MODEL)

    xs2d = x_sample.reshape(rows_s, D_MODEL)
    proj_s = _proj_sample_call(xs2d, mod_s, p)
    r8 = jnp.arange(SUBLANES)
    rope_s = _rope_table(PAST_LEN + (r8 % sseq).astype(jnp.int32))
    same = (r8[:, None] // sseq) == (r8[None, :] // sseq)
    dec8, qdec8, kdec8, gpow_s = _decay_tables(r8 % sseq, same, float(sseq))
    o_s, ret_s = _ret_sample_call(proj_s, state_ret[l], gpow_s, rope_s, dec8, qdec8, kdec8)
    cs = state_rnn_conv[l]
    pad_rows = lambda a: jnp.pad(a, ((0, 0), (0, sseq - a.shape[1]), (0, 0))).reshape(rows_s, D_RNN)
    xs1, xs2, xs3 = pad_rows(cs[:, 2:3]), pad_rows(cs[:, 1:3]), pad_rows(cs[:, 0:3])
    h0rep = jnp.repeat(state_rnn_h[l], sseq, axis=0)
    x1s, hseq_s = _mix_sample_call(xs2d, mod_s, proj_s, o_s, h0rep, xs1, xs2, xs3, p)
    ys = _ffn_call(x1s, mod_s, FFN_TM, 1, p).reshape(nsb, sseq, D_MODEL)
    hlast_s = hseq_s.reshape(nsb, sseq, D_RNN)[:, sseq - 1]
    conv_s = proj_s[:, OFF_XR:OFF_XR + D_RNN].reshape(nsb, sseq, D_RNN)[:, sseq - (CONV_W - 1):]

    return (yp, ys, ret_p[None], ret_s[None], hlast_p.reshape(nb, D_RNN)[None], hlast_s[None],
            conv_p[None], conv_s[None])
```

```python
import functools

import jax
import jax.numpy as jnp
from jax import lax
from jax.experimental import pallas as pl
from jax.experimental.pallas import tpu as pltpu

F32 = jnp.float32
BF16 = jnp.bfloat16

D_MODEL = 1024
H_RET = 4
DK = D_MODEL // H_RET
DV = 2 * DK
D_QK = H_RET * DK
D_V = H_RET * DV
D_RNN = 1536
RNN_BLOCK = 128
N_RNN_BLOCKS = D_RNN // RNN_BLOCK
CONV_W = 4
LRU_C = 8.0
D_FF = 2816
ROPE_BASE = 10000.0
GN_EPS = 1e-5
RMS_EPS = 1e-6
PAST_LEN = 16384

OFF_Q = 0
OFF_K = OFF_Q + D_QK
OFF_V = OFF_K + D_QK
OFF_G = OFF_V + D_V
OFF_XR = OFF_G + D_V
OFF_GR = OFF_XR + D_RNN
OFF_MG = OFF_GR + D_RNN
N_CAT = OFF_MG + 2 * D_MODEL

SUBLANES = 8
LANES = 128
MXU_DIM = 256
VMEM_BYTES_V7X = 64 * 1024 * 1024

PROMPT_TM = 256
FFN_TM = 256
FF_CHUNK = MXU_DIM
SAMPLE_TM = 128
RET_SAMPLE_BB = 2


def _dot(a, b):
    return jnp.dot(a, b, preferred_element_type=F32)


def _dot_nt(a, b):
    return lax.dot_general(a, b, (((1,), (1,)), ((), ())), preferred_element_type=F32)


def _dot_tn(a, b):
    return lax.dot_general(a, b, (((0,), (0,)), ((), ())), preferred_element_type=F32)


def _wb(ref, k0=None, k1=None, c0=None, c1=None):
    rs = slice(None) if k0 is None else slice(k0 // 2, k1 // 2)
    cs = slice(None) if c0 is None else slice(c0, c1)
    return pltpu.bitcast(ref[rs, cs], BF16)


def _pack_rows(w):
    bits = lambda a: lax.bitcast_convert_type(a.astype(BF16), jnp.uint16).astype(jnp.uint32)
    return bits(w[..., 0::2, :]) | (bits(w[..., 1::2, :]) << 16)


def _rms(x, w):
    ms = jnp.mean(x * x, axis=-1, keepdims=True)
    return x * lax.rsqrt(ms + RMS_EPS) * w


def _silu(x):
    return x * jax.nn.sigmoid(x)


def _rope(x, cos, sin):
    half = DK // 2
    x1, x2 = x[:, :half], x[:, half:]
    return jnp.concatenate([x1 * cos - x2 * sin, x1 * sin + x2 * cos], axis=1)


def _group_norm(o):
    mu = jnp.mean(o, axis=-1, keepdims=True)
    d = o - mu
    var = jnp.mean(d * d, axis=-1, keepdims=True)
    return d * lax.rsqrt(var + GN_EPS)


def _lru_coeffs(xconv, wrg_ref, b_a, b_x, lru):
    xcb = xconv.astype(BF16)
    pre = [_dot(xcb[:, n * RNN_BLOCK:(n + 1) * RNN_BLOCK], pltpu.bitcast(wrg_ref[n], BF16))
           for n in range(N_RNN_BLOCKS)]
    ra = jnp.concatenate([p[:, :RNN_BLOCK] for p in pre], axis=1) + b_a
    ri = jnp.concatenate([p[:, RNN_BLOCK:] for p in pre], axis=1) + b_x
    r = jax.nn.sigmoid(ra)
    i = jax.nn.sigmoid(ri)
    z = -lru
    sp = jnp.maximum(z, 0.0) + jnp.log(1.0 + jnp.exp(-jnp.abs(z)))
    log_a = -LRU_C * r * sp
    a = jnp.exp(log_a)
    beta = jnp.sqrt(-jnp.tanh(log_a) * (jnp.exp(2.0 * log_a) + 1.0))
    return a, beta * (i * xconv)


def _mix_tail(x, g1, br_ret, rnn_y_b, gate_pre, b_mg, wbr_rnn_ref, wout_ref, npost):
    br_rnn = _dot(rnn_y_b, _wb(wbr_rnn_ref))
    gates = jax.nn.sigmoid(gate_pre + b_mg)
    ga, gb = gates[:, :D_MODEL], gates[:, D_MODEL:]
    mixed = _dot((ga * br_ret + gb * br_rnn).astype(BF16), _wb(wout_ref))
    return x + g1 * _rms(mixed, npost)


def _mod_kernel(c_ref, w_ref, b_ref, o_ref):
    a = _silu(c_ref[...]).astype(BF16)
    o_ref[...] = _dot(a, _wb(w_ref)) + b_ref[...]


def _mod_call(c_all, w_ada_b, b_ada):
    rows = c_all.shape[0]
    tn = D_MODEL
    return pl.pallas_call(
        _mod_kernel,
        grid=(6 * D_MODEL // tn,),
        in_specs=[
            pl.BlockSpec((rows, D_MODEL), lambda j: (0, 0)),
            pl.BlockSpec((D_MODEL // 2, tn), lambda j: (0, j)),
            pl.BlockSpec((1, tn), lambda j: (0, j)),
        ],
        out_specs=pl.BlockSpec((rows, tn), lambda j: (0, j)),
        out_shape=jax.ShapeDtypeStruct((rows, 6 * D_MODEL), F32),
        compiler_params=pltpu.CompilerParams(dimension_semantics=("arbitrary",)),
    )(c_all, w_ada_b, b_ada)


def _mix_prompt_kernel(gpow_ref, x_ref, mod_ref, npre_ref, wcat_ref, bmg_ref, rope_ref,
                       dec_ref, qdec_ref, kdec_ref, gnw_ref, wbr_ret_ref, convw_ref, convb_ref,
                       wrg_ref, bra_ref, brx_ref, lru_ref, wbr_rnn_ref, wout_ref, npost_ref,
                       x1_ref, s_ref, hlast_ref, convnew_ref,
                       hb_ref, xr_ref, prevg_ref, hc_ref):
    tm = PROMPT_TM
    ng = tm // SUBLANES
    halo = (CONV_W - 1) * SUBLANES
    t = pl.program_id(1)

    @pl.when(t == 0)
    def _():
        s_ref[...] = jnp.zeros_like(s_ref)
        prevg_ref[...] = jnp.zeros_like(prevg_ref)
        hc_ref[...] = jnp.zeros_like(hc_ref)

    x = x_ref[...]
    m = mod_ref[...]
    sh1, sc1, g1 = m[:, :D_MODEL], m[:, D_MODEL:2 * D_MODEL], m[:, 2 * D_MODEL:]
    hb_ref[...] = (_rms(x, npre_ref[...]) * (1.0 + sc1) + sh1).astype(BF16)

    hb = hb_ref[...]
    sub = lax.broadcasted_iota(jnp.int32, (SUBLANES, D_RNN), 0)
    xr = _dot(hb, _wb(wcat_ref, c0=OFF_XR, c1=OFF_XR + D_RNN))
    xr_ref[halo:halo + tm, :] = xr
    for kk in range(1, CONV_W):
        r0 = (CONV_W - 1 - kk) * SUBLANES
        cur = xr[(ng - kk) * SUBLANES:(ng - kk + 1) * SUBLANES, :]
        prv = prevg_ref[r0:r0 + SUBLANES, :]
        xr_ref[r0:r0 + SUBLANES, :] = pltpu.roll(jnp.where(sub == SUBLANES - 1, prv, cur), 1, 0)
    prevg_ref[...] = xr[tm - halo:, :]
    for kk in range(1, CONV_W):
        r1 = (ng - kk) * SUBLANES + SUBLANES - 1
        convnew_ref[CONV_W - 1 - kk:CONV_W - kk, :] = xr[r1:r1 + 1, :]
    cw = convw_ref[...]
    xconv = convb_ref[...]
    for j in range(CONV_W):
        r0 = halo - (CONV_W - 1 - j) * SUBLANES
        xconv = xconv + xr_ref[r0:r0 + tm, :] * cw[j:j + 1, :]
    a, b = _lru_coeffs(xconv, wrg_ref, bra_ref[...], brx_ref[...], lru_ref[...])

    ca, cb = a[0:SUBLANES, :], b[0:SUBLANES, :]
    cas, cbs = [ca], [cb]
    for gi in range(1, ng):
        ag = a[gi * SUBLANES:(gi + 1) * SUBLANES, :]
        cb = ag * cb + b[gi * SUBLANES:(gi + 1) * SUBLANES, :]
        ca = ag * ca
        cas.append(ca)
        cbs.append(cb)
    cin = jnp.where(sub == 0, hc_ref[SUBLANES - 1:SUBLANES, :], 0.0)
    for s in range(SUBLANES - 1):
        cin = jnp.where(sub == s + 1, pltpu.roll(ca * cin + cb, 1, 0), cin)
    seg_end = ca * cin + cb
    hc_ref[...] = seg_end
    hlast_ref[...] = seg_end[SUBLANES - 1:SUBLANES, :]
    hseq = jnp.concatenate([cas[gi] * cin + cbs[gi] for gi in range(ng)], axis=0)
    gr = _dot(hb, _wb(wcat_ref, c0=OFF_GR, c1=OFF_GR + D_RNN))
    rnn_y_b = (hseq * jax.nn.gelu(gr, approximate=True)).astype(BF16)

    half = DK // 2
    cos, sin = rope_ref[:, 0:half], rope_ref[:, half:2 * half]
    cosk, sink = rope_ref[:, 2 * half:3 * half], rope_ref[:, 3 * half:4 * half]
    br_ret = jnp.zeros((tm, D_MODEL), F32)
    for hh in range(H_RET):
        hb = hb_ref[...]
        q = _dot(hb, _wb(wcat_ref, c0=OFF_Q + hh * DK, c1=OFF_Q + (hh + 1) * DK))
        k = _dot(hb, _wb(wcat_ref, c0=OFF_K + hh * DK, c1=OFF_K + (hh + 1) * DK))
        vb = _dot(hb, _wb(wcat_ref, c0=OFF_V + hh * DV, c1=OFF_V + (hh + 1) * DV)).astype(BF16)
        g = _dot(hb, _wb(wcat_ref, c0=OFF_G + hh * DV, c1=OFF_G + (hh + 1) * DV))
        qb = _rope(q, cos, sin).astype(BF16)
        kr = _rope(k, cosk, sink)
        kb = kr.astype(BF16)
        kdec = kdec_ref[hh]
        kdb = (kr * jnp.concatenate([kdec, kdec], axis=1)).astype(BF16)
        scores = _dot_nt(qb, kb) * dec_ref[hh]
        intra = _dot(scores.astype(BF16), vb)
        s_old = s_ref[hh]
        qdec = qdec_ref[hh]
        cross = _dot(qb, s_old.astype(BF16)) * jnp.concatenate([qdec] * (DV // LANES), axis=1)
        s_ref[hh] = gpow_ref[hh] * s_old + _dot_tn(kdb, vb)
        on = _group_norm(intra + cross)
        ry = (on * gnw_ref[:, hh * DV:(hh + 1) * DV] * _silu(g)).astype(BF16)
        br_ret = br_ret + _dot(ry, _wb(wbr_ret_ref, hh * DV, (hh + 1) * DV))

    hb = hb_ref[...]
    gate_pre = _dot(hb, _wb(wcat_ref, c0=OFF_MG, c1=OFF_MG + 2 * D_MODEL))
    x1_ref[...] = _mix_tail(x, g1, br_ret, rnn_y_b, gate_pre, bmg_ref[...], wbr_rnn_ref, wout_ref,
                            npost_ref[...])


def _const_spec(shape):
    nd = len(shape)
    return pl.BlockSpec(shape, lambda *_: (0,) * nd, pipeline_mode=pl.Buffered(1))


def _mix_prompt_call(x, mod3, gpow, rope_tab, dec, qdec, kdec, p):
    nb, seq, _ = x.shape
    tm = PROMPT_TM
    nt = seq // tm
    in_specs = [
        pl.BlockSpec(memory_space=pltpu.SMEM),
        pl.BlockSpec((None, tm, D_MODEL), lambda b, t: (b, t, 0)),
        pl.BlockSpec((None, 1, 3 * D_MODEL), lambda b, t: (b, 0, 0)),
        _const_spec((1, D_MODEL)),
        _const_spec((D_MODEL // 2, N_CAT)),
        _const_spec((1, 2 * D_MODEL)),
        pl.BlockSpec((tm, 4 * (DK // 2)), lambda b, t: (t, 0)),
        _const_spec((H_RET, tm, tm)),
        _const_spec((H_RET, tm, LANES)),
        _const_spec((H_RET, tm, LANES)),
        _const_spec((1, D_V)),
        _const_spec((D_V // 2, D_MODEL)),
        _const_spec((CONV_W, D_RNN)),
        _const_spec((1, D_RNN)),
        _const_spec((N_RNN_BLOCKS, RNN_BLOCK // 2, 2 * RNN_BLOCK)),
        _const_spec((1, D_RNN)),
        _const_spec((1, D_RNN)),
        _const_spec((1, D_RNN)),
        _const_spec((D_RNN // 2, D_MODEL)),
        _const_spec((D_MODEL // 2, D_MODEL)),
        _const_spec((1, D_MODEL)),
    ]
    out_specs = [
        pl.BlockSpec((None, tm, D_MODEL), lambda b, t: (b, t, 0)),
        pl.BlockSpec((None, H_RET, DK, DV), lambda b, t: (b, 0, 0, 0)),
        pl.BlockSpec((None, 1, D_RNN), lambda b, t: (b, 0, 0)),
        pl.BlockSpec((None, CONV_W - 1, D_RNN), lambda b, t: (b, 0, 0)),
    ]
    out_shape = [
        jax.ShapeDtypeStruct((nb, seq, D_MODEL), F32),
        jax.ShapeDtypeStruct((nb, H_RET, DK, DV), F32),
        jax.ShapeDtypeStruct((nb, 1, D_RNN), F32),
        jax.ShapeDtypeStruct((nb, CONV_W - 1, D_RNN), F32),
    ]
    halo = (CONV_W - 1) * SUBLANES
    scratch = [
        pltpu.VMEM((tm, D_MODEL), BF16),
        pltpu.VMEM((halo + tm, D_RNN), F32),
        pltpu.VMEM((halo, D_RNN), F32),
        pltpu.VMEM((SUBLANES, D_RNN), F32),
    ]
    return pl.pallas_call(
        _mix_prompt_kernel,
        grid=(nb, nt),
        in_specs=in_specs,
        out_specs=out_specs,
        out_shape=out_shape,
        scratch_shapes=scratch,
        compiler_params=pltpu.CompilerParams(
            dimension_semantics=("arbitrary", "arbitrary"),
            vmem_limit_bytes=VMEM_BYTES_V7X - 4 * 1024 * 1024),
    )(gpow, x, mod3, p['npre'], p['wcat'], p['bmg'], rope_tab, dec, qdec, kdec, p['gnw'], p['wbr_ret'],
      p['convw'], p['convb'], p['wrg'], p['bra'], p['brx'], p['lru'], p['wbr_rnn'], p['wout'], p['npost'])


def _ffn_kernel(x_ref, mod_ref, npre_ref, w1_ref, w2_ref, npost_ref, o_ref):
    x = x_ref[...]
    m = mod_ref[...]
    sh2, sc2, g2 = m[:, :D_MODEL], m[:, D_MODEL:2 * D_MODEL], m[:, 2 * D_MODEL:]
    h2 = (_rms(x, npre_ref[...]) * (1.0 + sc2) + sh2).astype(BF16)
    acc = jnp.zeros(x.shape, F32)
    for j in range(D_FF // FF_CHUNK):
        c0 = j * FF_CHUNK
        fg = _dot(h2, _wb(w1_ref, c0=c0, c1=c0 + FF_CHUNK))
        fu = _dot(h2, _wb(w1_ref, c0=D_FF + c0, c1=D_FF + c0 + FF_CHUNK))
        acc = acc + _dot((_silu(fg) * fu).astype(BF16), _wb(w2_ref, c0, c0 + FF_CHUNK))
    o_ref[...] = x + g2 * _rms(acc, npost_ref[...])


def _ffn_call(x2d, mod, mod_rows_per_tile, seq_tiles, p):
    rows = x2d.shape[0]
    tm = FFN_TM
    if mod_rows_per_tile == 1:
        mod_spec = pl.BlockSpec((None, 1, 3 * D_MODEL), lambda i: (i // seq_tiles, 0, 1))
    else:
        mod_spec = pl.BlockSpec((tm, 3 * D_MODEL), lambda i: (i, 1))
    return pl.pallas_call(
        _ffn_kernel,
        grid=(rows // tm,),
        in_specs=[
            pl.BlockSpec((tm, D_MODEL), lambda i: (i, 0)),
            mod_spec,
            _const_spec((1, D_MODEL)),
            _const_spec((D_MODEL // 2, 2 * D_FF)),
            _const_spec((D_FF // 2, D_MODEL)),
            _const_spec((1, D_MODEL)),
        ],
        out_specs=pl.BlockSpec((tm, D_MODEL), lambda i: (i, 0)),
        out_shape=jax.ShapeDtypeStruct((rows, D_MODEL), F32),
        compiler_params=pltpu.CompilerParams(
            dimension_semantics=("arbitrary",),
            vmem_limit_bytes=48 * 1024 * 1024),
    )(x2d, mod, p['npre_ffn'], p['wffn_in'], p['wffn_out'], p['npost_ffn'])


def _proj_sample_kernel(x_ref, mod_ref, npre_ref, w_ref, o_ref):
    m = mod_ref[...]
    sh1, sc1 = m[:, :D_MODEL], m[:, D_MODEL:2 * D_MODEL]
    h = (_rms(x_ref[...], npre_ref[...]) * (1.0 + sc1) + sh1).astype(BF16)
    o_ref[...] = _dot(h, _wb(w_ref))


def _proj_sample_call(x2d, mod_s, p):
    rows = x2d.shape[0]
    tn = D_MODEL
    return pl.pallas_call(
        _proj_sample_kernel,
        grid=(N_CAT // tn,),
        in_specs=[
            pl.BlockSpec((rows, D_MODEL), lambda j: (0, 0)),
            pl.BlockSpec((rows, 3 * D_MODEL), lambda j: (0, 0)),
            pl.BlockSpec((1, D_MODEL), lambda j: (0, 0)),
            pl.BlockSpec((D_MODEL // 2, tn), lambda j: (0, j)),
        ],
        out_specs=pl.BlockSpec((rows, tn), lambda j: (0, j)),
        out_shape=jax.ShapeDtypeStruct((rows, N_CAT), F32),
        compiler_params=pltpu.CompilerParams(
            dimension_semantics=("arbitrary",),
            vmem_limit_bytes=48 * 1024 * 1024),
    )(x2d, mod_s, p['npre'], p['wcat'])


def _ret_sample_kernel(gpow_ref, qkv_ref, s_ref, rope_ref, dec_ref, qdec_ref, kdec_ref, o_ref, snew_ref):
    half = DK // 2
    cos, sin = rope_ref[:, 0:half], rope_ref[:, half:2 * half]
    cosk, sink = rope_ref[:, 2 * half:3 * half], rope_ref[:, 3 * half:4 * half]
    nseq = RET_SAMPLE_BB
    tlen = SUBLANES // nseq
    row = lax.broadcasted_iota(jnp.int32, (SUBLANES, LANES), 0)
    for hh in range(H_RET):
        q = _rope(qkv_ref[:, OFF_Q + hh * DK:OFF_Q + (hh + 1) * DK], cos, sin)
        k = _rope(qkv_ref[:, OFF_K + hh * DK:OFF_K + (hh + 1) * DK], cosk, sink)
        v = qkv_ref[:, OFF_V + hh * DV:OFF_V + (hh + 1) * DV]
        kdec = kdec_ref[hh]
        kd = k * jnp.concatenate([kdec, kdec], axis=1)
        scores = _dot_nt(q, k) * dec_ref[hh]
        intra = _dot(scores, v)
        qb = q.astype(BF16)
        cross = jnp.zeros((SUBLANES, DV), F32)
        for bi in range(nseq):
            s_old = s_ref[bi, hh]
            in_seq = (row >= bi * tlen) & (row < (bi + 1) * tlen)
            cr = _dot(qb, s_old.astype(BF16))
            cross = jnp.where(jnp.concatenate([in_seq] * (DV // LANES), axis=1), cr, cross)
            kd_b = jnp.where(jnp.concatenate([in_seq] * (DK // LANES), axis=1), kd, 0.0)
            snew_ref[bi, hh] = gpow_ref[hh] * s_old + _dot_tn(kd_b, v)
        qdec = qdec_ref[hh]
        o_ref[:, hh * DV:(hh + 1) * DV] = intra + cross * jnp.concatenate([qdec] * (DV // LANES), axis=1)


def _ret_sample_call(proj_s, state, gpow, rope8, dec8, qdec8, kdec8):
    nb = state.shape[0]
    bb = RET_SAMPLE_BB
    rows = proj_s.shape[0]
    qkv_w = OFF_G
    return pl.pallas_call(
        _ret_sample_kernel,
        grid=(nb // bb,),
        in_specs=[
            pl.BlockSpec(memory_space=pltpu.SMEM),
            pl.BlockSpec((SUBLANES, qkv_w), lambda i: (i, 0)),
            pl.BlockSpec((bb, H_RET, DK, DV), lambda i: (i, 0, 0, 0)),
            pl.BlockSpec((SUBLANES, 4 * (DK // 2)), lambda i: (0, 0)),
            pl.BlockSpec((H_RET, SUBLANES, SUBLANES), lambda i: (0, 0, 0)),
            pl.BlockSpec((H_RET, SUBLANES, LANES), lambda i: (0, 0, 0)),
            pl.BlockSpec((H_RET, SUBLANES, LANES), lambda i: (0, 0, 0)),
        ],
        out_specs=[
            pl.BlockSpec((SUBLANES, D_V), lambda i: (i, 0)),
            pl.BlockSpec((bb, H_RET, DK, DV), lambda i: (i, 0, 0, 0)),
        ],
        out_shape=[
            jax.ShapeDtypeStruct((rows, D_V), F32),
            jax.ShapeDtypeStruct(state.shape, F32),
        ],
        compiler_params=pltpu.CompilerParams(
            dimension_semantics=("arbitrary",),
            vmem_limit_bytes=40 * 1024 * 1024),
    )(gpow, proj_s, state, rope8, dec8, qdec8, kdec8)


def _mix_sample_kernel(x_ref, mod_ref, proj_ref, o_ref, h0_ref, xs1_ref, xs2_ref, xs3_ref,
                       bmg_ref, gnw_ref, wbr_ret_ref, convw_ref, convb_ref, wrg_ref, bra_ref, brx_ref,
                       lru_ref, wbr_rnn_ref, wout_ref, npost_ref,
                       x1_ref, hseq_ref,
                       xr_ref, sa_ref, sb_ref):
    tm = SAMPLE_TM
    tlen = 4
    x = x_ref[...]
    g1 = mod_ref[:, 2 * D_MODEL:]

    br_ret = jnp.zeros((tm, D_MODEL), F32)
    for hh in range(H_RET):
        on = _group_norm(o_ref[:, hh * DV:(hh + 1) * DV])
        g = proj_ref[:, OFF_G + hh * DV:OFF_G + (hh + 1) * DV]
        ry = (on * gnw_ref[:, hh * DV:(hh + 1) * DV] * _silu(g)).astype(BF16)
        br_ret = br_ret + _dot(ry, _wb(wbr_ret_ref, hh * DV, (hh + 1) * DV))

    tpos = lax.broadcasted_iota(jnp.int32, (tm, D_RNN), 0) & (tlen - 1)
    xr = proj_ref[:, OFF_XR:OFF_XR + D_RNN]
    xr_ref[0:SUBLANES, :] = jnp.zeros((SUBLANES, D_RNN), F32)
    xr_ref[SUBLANES:SUBLANES + tm, :] = xr
    cw = convw_ref[...]
    prev = (xs3_ref, xs2_ref, xs1_ref)
    xconv = convb_ref[...]
    for j in range(CONV_W - 1):
        sft = CONV_W - 1 - j
        shifted = jnp.where(tpos >= sft, xr_ref[SUBLANES - sft:SUBLANES - sft + tm, :], 0.0)
        xconv = xconv + (shifted + prev[j][...]) * cw[j:j + 1, :]
    xconv = xconv + xr * cw[CONV_W - 1:CONV_W, :]

    a, b = _lru_coeffs(xconv, wrg_ref, bra_ref[...], brx_ref[...], lru_ref[...])
    b = jnp.where(tpos == 0, b + a * h0_ref[...], b)
    sa_ref[0:SUBLANES, :] = jnp.zeros((SUBLANES, D_RNN), F32)
    sb_ref[0:SUBLANES, :] = jnp.zeros((SUBLANES, D_RNN), F32)
    for s in (1, 2):
        sa_ref[SUBLANES:SUBLANES + tm, :] = a
        sb_ref[SUBLANES:SUBLANES + tm, :] = b
        keep = tpos >= s
        ap = jnp.where(keep, sa_ref[SUBLANES - s:SUBLANES - s + tm, :], 1.0)
        bp = jnp.where(keep, sb_ref[SUBLANES - s:SUBLANES - s + tm, :], 0.0)
        b = a * bp + b
        a = a * ap
    hseq_ref[...] = b

    gr = proj_ref[:, OFF_GR:OFF_GR + D_RNN]
    rnn_y_b = (b * jax.nn.gelu(gr, approximate=True)).astype(BF16)
    gate_pre = proj_ref[:, OFF_MG:OFF_MG + 2 * D_MODEL]
    x1_ref[...] = _mix_tail(x, g1, br_ret, rnn_y_b, gate_pre, bmg_ref[...], wbr_rnn_ref, wout_ref,
                            npost_ref[...])


def _mix_sample_call(x2d, mod_s, proj_s, o_s, h0rep, xs1, xs2, xs3, p):
    rows = x2d.shape[0]
    tm = SAMPLE_TM
    row_spec = lambda w: pl.BlockSpec((tm, w), lambda i: (i, 0))
    return pl.pallas_call(
        _mix_sample_kernel,
        grid=(rows // tm,),
        in_specs=[
            row_spec(D_MODEL), row_spec(3 * D_MODEL), row_spec(N_CAT), row_spec(D_V),
            row_spec(D_RNN), row_spec(D_RNN), row_spec(D_RNN), row_spec(D_RNN),
            _const_spec((1, 2 * D_MODEL)),
            _const_spec((1, D_V)),
            _const_spec((D_V // 2, D_MODEL)),
            _const_spec((CONV_W, D_RNN)),
            _const_spec((1, D_RNN)),
            _const_spec((N_RNN_BLOCKS, RNN_BLOCK // 2, 2 * RNN_BLOCK)),
            _const_spec((1, D_RNN)),
            _const_spec((1, D_RNN)),
            _const_spec((1, D_RNN)),
            _const_spec((D_RNN // 2, D_MODEL)),
            _const_spec((D_MODEL // 2, D_MODEL)),
            _const_spec((1, D_MODEL)),
        ],
        out_specs=[row_spec(D_MODEL), row_spec(D_RNN)],
        out_shape=[
            jax.ShapeDtypeStruct((rows, D_MODEL), F32),
            jax.ShapeDtypeStruct((rows, D_RNN), F32),
        ],
        scratch_shapes=[
            pltpu.VMEM((tm + SUBLANES, D_RNN), F32),
            pltpu.VMEM((tm + SUBLANES, D_RNN), F32),
            pltpu.VMEM((tm + SUBLANES, D_RNN), F32),
        ],
        compiler_params=pltpu.CompilerParams(
            dimension_semantics=("arbitrary",),
            vmem_limit_bytes=56 * 1024 * 1024),
    )(x2d, mod_s, proj_s, o_s, h0rep, xs1, xs2, xs3, p['bmg'], p['gnw'], p['wbr_ret'], p['convw'],
      p['convb'], p['wrg'], p['bra'], p['brx'], p['lru'], p['wbr_rnn'], p['wout'], p['npost'])


def _rope_table(pos):
    half = DK // 2
    inv = ROPE_BASE ** (-jnp.arange(half, dtype=F32) / half)
    ang = pos.astype(F32)[:, None] * inv[None, :]
    cos, sin = jnp.cos(ang), jnp.sin(ang)
    ks = DK ** -0.5
    return jnp.concatenate([cos, sin, cos * ks, sin * ks], axis=1)


def _decay_tables(tpos, same_seq, chunk):
    lg = jnp.log(1.0 - 2.0 ** (-5.0 - jnp.arange(H_RET, dtype=F32)))
    idx = tpos.astype(F32)
    diff = idx[:, None] - idx[None, :]
    causal = (diff >= 0) & same_seq
    dec = jnp.where(causal[None], jnp.exp(jnp.where(causal, diff, 0.0)[None] * lg[:, None, None]), 0.0)
    qdec = jnp.exp((idx + 1.0)[None, :] * lg[:, None])
    kdec = jnp.exp((chunk - 1.0 - idx)[None, :] * lg[:, None])
    rep = lambda a: jnp.broadcast_to(a[:, :, None], a.shape + (LANES,))
    gpow = jnp.exp(chunk * lg)
    return dec, rep(qdec), rep(kdec), gpow


def kernel(x_prompt, x_sample, state_ret, state_rnn_h, state_rnn_conv, c_prompt, c_sample,
           w_ada, b_ada, norm_pre_mix, norm_post_mix, norm_pre_ffn, norm_post_ffn,
           w_in, ret_gn_w, w_br_ret, conv_w, conv_b, w_rg_a, b_rg_a, w_rg_x, b_rg_x,
           lru_param, w_br_rnn, w_mgate, b_mgate, w_out, w_ffn_in, w_ffn_out):
    depth = w_in.shape[0]
    assert depth == 1, "single layer step"
    nb, seq, _ = x_prompt.shape
    nsb, sseq, _ = x_sample.shape
    assert seq % PROMPT_TM == 0 and sseq * RET_SAMPLE_BB == SUBLANES and sseq == CONV_W
    l = 0
    row = lambda a: a[l][None, :]
    p = dict(
        npre=row(norm_pre_mix), npost=row(norm_post_mix), npre_ffn=row(norm_pre_ffn), npost_ffn=row(norm_post_ffn),
        wcat=_pack_rows(jnp.concatenate([w_in[l], w_mgate[l]], axis=1)),
        bmg=row(b_mgate), gnw=row(ret_gn_w), wbr_ret=_pack_rows(w_br_ret[l]),
        convw=conv_w[l], convb=row(conv_b),
        wrg=_pack_rows(jnp.concatenate([w_rg_a[l], w_rg_x[l]], axis=2)),
        bra=row(b_rg_a), brx=row(b_rg_x), lru=row(lru_param),
        wbr_rnn=_pack_rows(w_br_rnn[l]), wout=_pack_rows(w_out[l]),
        wffn_in=_pack_rows(w_ffn_in[l]), wffn_out=_pack_rows(w_ffn_out[l]),
    )

    rows_s = nsb * sseq
    c_all = jnp.concatenate([jnp.repeat(c_sample, sseq, axis=0), c_prompt], axis=0)
    mod_all = _mod_call(c_all, _pack_rows(w_ada[l]), row(b_ada))
    mod_p = mod_all[rows_s:].reshape(nb, 1, 6 * D_MODEL)
    mod_s = mod_all

    tm = PROMPT_TM
    ng = tm // SUBLANES
    r = jnp.arange(tm)
    tpos = (r % SUBLANES) * ng + r // SUBLANES
    pos_p = (jnp.arange(seq // tm)[:, None] * tm + tpos[None, :]).reshape(seq).astype(jnp.int32)
    rope_p = _rope_table(pos_p)
    dec, qdec, kdec, gpow = _decay_tables(tpos, jnp.ones((tm, tm), bool), float(tm))
    interleave = lambda a: a.reshape(nb, seq // tm, SUBLANES, ng, D_MODEL).swapaxes(2, 3).reshape(nb, seq, D_MODEL)
    restore = lambda a: a.reshape(nb, seq // tm, ng, SUBLANES, D_MODEL).swapaxes(2, 3).reshape(nb, seq, D_MODEL)
    x1p, ret_p, hlast_p, conv_p = _mix_prompt_call(interleave(x_prompt), mod_p, gpow, rope_p, dec, qdec, kdec, p)
    yp = restore(_ffn_call(x1p.reshape(nb * seq, D_MODEL), mod_p, 1, seq // FFN_TM, p).reshape(nb, seq, D_MODEL))

    xs2d = x_sample.reshape(rows_s, D_MODEL)
    proj_s = _proj_sample_call(xs2d, mod_s, p)
    r8 = jnp.arange(SUBLANES)
    rope_s = _rope_table(PAST_LEN + (r8 % sseq).astype(jnp.int32))
    same = (r8[:, None] // sseq) == (r8[None, :] // sseq)
    dec8, qdec8, kdec8, gpow_s = _decay_tables(r8 % sseq, same, float(sseq))
    o_s, ret_s = _ret_sample_call(proj_s, state_ret[l], gpow_s, rope_s, dec8, qdec8, kdec8)
    cs = state_rnn_conv[l]
    pad_rows = lambda a: jnp.pad(a, ((0, 0), (0, sseq - a.shape[1]), (0, 0))).reshape(rows_s, D_RNN)
    xs1, xs2, xs3 = pad_rows(cs[:, 2:3]), pad_rows(cs[:, 1:3]), pad_rows(cs[:, 0:3])
    h0rep = jnp.repeat(state_rnn_h[l], sseq, axis=0)
    x1s, hseq_s = _mix_sample_call(xs2d, mod_s, proj_s, o_s, h0rep, xs1, xs2, xs3, p)
    ys = _ffn_call(x1s, mod_s, FFN_TM, 1, p).reshape(nsb, sseq, D_MODEL)
    hlast_s = hseq_s.reshape(nsb, sseq, D_RNN)[:, sseq - 1]
    conv_s = proj_s[:, OFF_XR:OFF_XR + D_RNN].reshape(nsb, sseq, D_RNN)[:, sseq - (CONV_W - 1):]

    return (yp, ys, ret_p[None], ret_s[None], hlast_p.reshape(nb, D_RNN)[None], hlast_s[None],
            conv_p[None], conv_s[None])
```

```python
import functools

import jax
import jax.numpy as jnp
from jax import lax
from jax.experimental import pallas as pl
from jax.experimental.pallas import tpu as pltpu

F32 = jnp.float32
BF16 = jnp.bfloat16

D_MODEL = 1024
H_RET = 4
DK = D_MODEL // H_RET
DV = 2 * DK
D_QK = H_RET * DK
D_V = H_RET * DV
D_RNN = 1536
RNN_BLOCK = 128
N_RNN_BLOCKS = D_RNN // RNN_BLOCK
CONV_W = 4
LRU_C = 8.0
D_FF = 2816
ROPE_BASE = 10000.0
GN_EPS = 1e-5
RMS_EPS = 1e-6
PAST_LEN = 16384

OFF_Q = 0
OFF_K = OFF_Q + D_QK
OFF_V = OFF_K + D_QK
OFF_G = OFF_V + D_V
OFF_XR = OFF_G + D_V
OFF_GR = OFF_XR + D_RNN
OFF_MG = OFF_GR + D_RNN
N_CAT = OFF_MG + 2 * D_MODEL

SUBLANES = 8
LANES = 128
MXU_DIM = 256
VMEM_BYTES_V7X = 64 * 1024 * 1024

PROMPT_TM = 256
FFN_TM = 256
FF_CHUNK = MXU_DIM
SAMPLE_TM = 128
RET_SAMPLE_BB = 2


def _dot(a, b):
    return jnp.dot(a, b, preferred_element_type=F32)


def _dot_nt(a, b):
    return lax.dot_general(a, b, (((1,), (1,)), ((), ())), preferred_element_type=F32)


def _dot_tn(a, b):
    return lax.dot_general(a, b, (((0,), (0,)), ((), ())), preferred_element_type=F32)


def _wb(ref, k0=None, k1=None, c0=None, c1=None):
    rs = slice(None) if k0 is None else slice(k0, k1)
    cs = slice(None) if c0 is None else slice(c0, c1)
    return ref[rs, cs]


def _pack_rows(w):
    return w.astype(BF16)


def _rms(x, w):
    ms = jnp.mean(x * x, axis=-1, keepdims=True)
    return x * lax.rsqrt(ms + RMS_EPS) * w


def _silu(x):
    return x * jax.nn.sigmoid(x)


def _rope(x, cos, sin):
    half = DK // 2
    x1, x2 = x[:, :half], x[:, half:]
    return jnp.concatenate([x1 * cos - x2 * sin, x1 * sin + x2 * cos], axis=1)


def _group_norm(o):
    mu = jnp.mean(o, axis=-1, keepdims=True)
    d = o - mu
    var = jnp.mean(d * d, axis=-1, keepdims=True)
    return d * lax.rsqrt(var + GN_EPS)


def _lru_gate_pre(xconv, wrg_ref):
    xcb = xconv.astype(BF16)
    return [_dot(xcb[:, n * RNN_BLOCK:(n + 1) * RNN_BLOCK], wrg_ref[n]) for n in range(N_RNN_BLOCKS)]


def _lru_coeffs(xconv, pre, b_a, b_x, lru):
    ra = jnp.concatenate([p[:, :RNN_BLOCK] for p in pre], axis=1) + b_a
    ri = jnp.concatenate([p[:, RNN_BLOCK:] for p in pre], axis=1) + b_x
    r = jax.nn.sigmoid(ra)
    i = jax.nn.sigmoid(ri)
    z = -lru
    sp = jnp.maximum(z, 0.0) + jnp.log(1.0 + jnp.exp(-jnp.abs(z)))
    log_a = -LRU_C * r * sp
    a = jnp.exp(log_a)
    beta = jnp.sqrt(-jnp.tanh(log_a) * (jnp.exp(2.0 * log_a) + 1.0))
    return a, beta * (i * xconv)


def _mix_tail(x, g1, br_ret, rnn_y_b, gate_pre, b_mg, wbr_rnn_ref, wout_ref, npost):
    br_rnn = _dot(rnn_y_b, _wb(wbr_rnn_ref))
    gates = jax.nn.sigmoid(gate_pre + b_mg)
    ga, gb = gates[:, :D_MODEL], gates[:, D_MODEL:]
    mixed = _dot((ga * br_ret + gb * br_rnn).astype(BF16), _wb(wout_ref))
    return x + g1 * _rms(mixed, npost)


def _mod_kernel(c_ref, w_ref, b_ref, o_ref):
    a = _silu(c_ref[...]).astype(BF16)
    o_ref[...] = _dot(a, _wb(w_ref)) + b_ref[...]


def _mod_call(c_all, w_ada_b, b_ada):
    rows = c_all.shape[0]
    tn = D_MODEL
    return pl.pallas_call(
        _mod_kernel,
        grid=(6 * D_MODEL // tn,),
        in_specs=[
            pl.BlockSpec((rows, D_MODEL), lambda j: (0, 0)),
            pl.BlockSpec((D_MODEL, tn), lambda j: (0, j)),
            pl.BlockSpec((1, tn), lambda j: (0, j)),
        ],
        out_specs=pl.BlockSpec((rows, tn), lambda j: (0, j)),
        out_shape=jax.ShapeDtypeStruct((rows, 6 * D_MODEL), F32),
        compiler_params=pltpu.CompilerParams(dimension_semantics=("arbitrary",)),
    )(c_all, w_ada_b, b_ada)


def _mix_prompt_kernel(gpow_ref, x_ref, mod_ref, npre_ref, wcat_ref, bmg_ref, rope_ref,
                       dec_ref, qdec_ref, kdec_ref, gnw_ref, wbr_ret_ref, convw_ref, convb_ref,
                       wrg_ref, bra_ref, brx_ref, lru_ref, wbr_rnn_ref, wout_ref, npost_ref,
                       x1_ref, s_ref, hlast_ref, convnew_ref,
                       hb_ref, xr_ref, prevg_ref, hc_ref):
    tm = PROMPT_TM
    ng = tm // SUBLANES
    halo = (CONV_W - 1) * SUBLANES
    t = pl.program_id(1)

    @pl.when(t == 0)
    def _():
        s_ref[...] = jnp.zeros_like(s_ref)
        prevg_ref[...] = jnp.zeros_like(prevg_ref)
        hc_ref[...] = jnp.zeros_like(hc_ref)

    x = x_ref[...]
    m = mod_ref[...]
    sh1, sc1, g1 = m[:, :D_MODEL], m[:, D_MODEL:2 * D_MODEL], m[:, 2 * D_MODEL:]
    hb_ref[...] = (_rms(x, npre_ref[...]) * (1.0 + sc1) + sh1).astype(BF16)

    hb = hb_ref[...]
    sub = lax.broadcasted_iota(jnp.int32, (SUBLANES, D_RNN), 0)
    half = DK // 2
    cos, sin = rope_ref[:, 0:half], rope_ref[:, half:2 * half]
    cosk, sink = rope_ref[:, 2 * half:3 * half], rope_ref[:, 3 * half:4 * half]
    st = {}

    def xr_proj():
        xr = _dot(hb, _wb(wcat_ref, c0=OFF_XR, c1=OFF_XR + D_RNN))
        xr_ref[halo:halo + tm, :] = xr
        for kk in range(1, CONV_W):
            r0 = (CONV_W - 1 - kk) * SUBLANES
            cur = xr[(ng - kk) * SUBLANES:(ng - kk + 1) * SUBLANES, :]
            prv = prevg_ref[r0:r0 + SUBLANES, :]
            xr_ref[r0:r0 + SUBLANES, :] = pltpu.roll(jnp.where(sub == SUBLANES - 1, prv, cur), 1, 0)
        prevg_ref[...] = xr[tm - halo:, :]
        for kk in range(1, CONV_W):
            r1 = (ng - kk) * SUBLANES + SUBLANES - 1
            convnew_ref[CONV_W - 1 - kk:CONV_W - kk, :] = xr[r1:r1 + 1, :]

    def lru_conv():
        cw = convw_ref[...]
        xconv = convb_ref[...]
        for j in range(CONV_W):
            r0 = halo - (CONV_W - 1 - j) * SUBLANES
            xconv = xconv + xr_ref[r0:r0 + tm, :] * cw[j:j + 1, :]
        st['xconv'] = xconv

    def lru_gate_proj():
        st['gpre'] = _lru_gate_pre(st['xconv'], wrg_ref)

    def lru_coef():
        st['a'], st['b'] = _lru_coeffs(st.pop('xconv'), st.pop('gpre'), bra_ref[...], brx_ref[...], lru_ref[...])

    def lru_scan():
        a, b = st['a'], st['b']
        ca, cb = a[0:SUBLANES, :], b[0:SUBLANES, :]
        cas, cbs = [ca], [cb]
        for gi in range(1, ng):
            ag = a[gi * SUBLANES:(gi + 1) * SUBLANES, :]
            cb = ag * cb + b[gi * SUBLANES:(gi + 1) * SUBLANES, :]
            ca = ag * ca
            cas.append(ca)
            cbs.append(cb)
        cin = jnp.where(sub == 0, hc_ref[SUBLANES - 1:SUBLANES, :], 0.0)
        for s in range(SUBLANES - 1):
            cin = jnp.where(sub == s + 1, pltpu.roll(ca * cin + cb, 1, 0), cin)
        seg_end = ca * cin + cb
        hc_ref[...] = seg_end
        hlast_ref[...] = seg_end[SUBLANES - 1:SUBLANES, :]
        st['hseq'] = jnp.concatenate([cas[gi] * cin + cbs[gi] for gi in range(ng)], axis=0)

    def gr_proj():
        st['gr'] = _dot(hb, _wb(wcat_ref, c0=OFF_GR, c1=OFF_GR + D_RNN))

    def lru_y():
        st['rnn_y'] = (st.pop('hseq') * jax.nn.gelu(st.pop('gr'), approximate=True)).astype(BF16)

    def lru_out():
        st['br_rnn'] = _dot(st.pop('rnn_y'), _wb(wbr_rnn_ref))

    def gate_proj():
        st['gate_pre'] = _dot(hb, _wb(wcat_ref, c0=OFF_MG, c1=OFF_MG + 2 * D_MODEL))

    def gate_act():
        st['gates'] = jax.nn.sigmoid(st.pop('gate_pre') + bmg_ref[...])

    def head_proj(hh):
        q = _dot(hb, _wb(wcat_ref, c0=OFF_Q + hh * DK, c1=OFF_Q + (hh + 1) * DK))
        k = _dot(hb, _wb(wcat_ref, c0=OFF_K + hh * DK, c1=OFF_K + (hh + 1) * DK))
        vb = _dot(hb, _wb(wcat_ref, c0=OFF_V + hh * DV, c1=OFF_V + (hh + 1) * DV)).astype(BF16)
        g = _dot(hb, _wb(wcat_ref, c0=OFF_G + hh * DV, c1=OFF_G + (hh + 1) * DV))
        st['proj', hh] = (q, k, vb, g)

    def head_rope(hh):
        q, k, vb, g = st.pop(('proj', hh))
        kr = _rope(k, cosk, sink)
        kdec = kdec_ref[hh]
        kdb = (kr * jnp.concatenate([kdec, kdec], axis=1)).astype(BF16)
        st['rope', hh] = (_rope(q, cos, sin).astype(BF16), kr.astype(BF16), kdb, vb, g)

    def head_qk(hh):
        qb, kb, kdb, vb, g = st.pop(('rope', hh))
        scores = _dot_nt(qb, kb)
        s_old = s_ref[hh]
        cross = _dot(qb, s_old.astype(BF16))
        s_ref[hh] = gpow_ref[hh] * s_old + _dot_tn(kdb, vb)
        st['qk', hh] = (scores, cross, vb, g)

    def head_decay(hh):
        scores, cross, vb, g = st.pop(('qk', hh))
        st['dec', hh] = ((scores * dec_ref[hh]).astype(BF16), cross, vb, g)

    def head_pv(hh):
        sb, cross, vb, g = st.pop(('dec', hh))
        st['pv', hh] = (_dot(sb, vb), cross, g)

    def head_norm(hh):
        intra, cross, g = st.pop(('pv', hh))
        qdec = qdec_ref[hh]
        on = _group_norm(intra + cross * jnp.concatenate([qdec] * (DV // LANES), axis=1))
        st['ry', hh] = (on * gnw_ref[:, hh * DV:(hh + 1) * DV] * _silu(g)).astype(BF16)

    def head_out(hh):
        part = _dot(st.pop(('ry', hh)), _wb(wbr_ret_ref, hh * DV, (hh + 1) * DV))
        st['br_ret'] = part if hh == 0 else st['br_ret'] + part

    order = [
        xr_proj, (head_proj, 0), lru_conv, lru_gate_proj, (head_rope, 0), (head_qk, 0), lru_coef,
        (head_proj, 1), (head_decay, 0), (head_pv, 0), (head_rope, 1), lru_scan, (head_qk, 1),
        (head_norm, 0), (head_out, 0), (head_proj, 2), (head_decay, 1), (head_pv, 1), gr_proj,
        (head_rope, 2), (head_qk, 2), (head_norm, 1), lru_y, (head_out, 1), lru_out,
        (head_proj, 3), (head_decay, 2), (head_pv, 2), gate_proj, (head_rope, 3), (head_qk, 3),
        (head_norm, 2), gate_act, (head_out, 2), (head_decay, 3), (head_pv, 3), (head_norm, 3),
        (head_out, 3),
    ]
    for stage in order:
        if isinstance(stage, tuple):
            stage[0](stage[1])
        else:
            stage()

    gates = st['gates']
    ga, gb = gates[:, :D_MODEL], gates[:, D_MODEL:]
    mixed = _dot((ga * st['br_ret'] + gb * st['br_rnn']).astype(BF16), _wb(wout_ref))
    x1_ref[...] = x + g1 * _rms(mixed, npost_ref[...])


def _const_spec(shape):
    nd = len(shape)
    return pl.BlockSpec(shape, lambda *_: (0,) * nd, pipeline_mode=pl.Buffered(1))


def _mix_prompt_call(x, mod3, gpow, rope_tab, dec, qdec, kdec, p):
    nb, seq, _ = x.shape
    tm = PROMPT_TM
    nt = seq // tm
    in_specs = [
        pl.BlockSpec(memory_space=pltpu.SMEM),
        pl.BlockSpec((None, tm, D_MODEL), lambda b, t: (b, t, 0)),
        pl.BlockSpec((None, 1, 3 * D_MODEL), lambda b, t: (b, 0, 0)),
        _const_spec((1, D_MODEL)),
        _const_spec((D_MODEL, N_CAT)),
        _const_spec((1, 2 * D_MODEL)),
        pl.BlockSpec((tm, 4 * (DK // 2)), lambda b, t: (t, 0)),
        _const_spec((H_RET, tm, tm)),
        _const_spec((H_RET, tm, LANES)),
        _const_spec((H_RET, tm, LANES)),
        _const_spec((1, D_V)),
        _const_spec((D_V, D_MODEL)),
        _const_spec((CONV_W, D_RNN)),
        _const_spec((1, D_RNN)),
        _const_spec((N_RNN_BLOCKS, RNN_BLOCK, 2 * RNN_BLOCK)),
        _const_spec((1, D_RNN)),
        _const_spec((1, D_RNN)),
        _const_spec((1, D_RNN)),
        _const_spec((D_RNN, D_MODEL)),
        _const_spec((D_MODEL, D_MODEL)),
        _const_spec((1, D_MODEL)),
    ]
    out_specs = [
        pl.BlockSpec((None, tm, D_MODEL), lambda b, t: (b, t, 0)),
        pl.BlockSpec((None, H_RET, DK, DV), lambda b, t: (b, 0, 0, 0)),
        pl.BlockSpec((None, 1, D_RNN), lambda b, t: (b, 0, 0)),
        pl.BlockSpec((None, CONV_W - 1, D_RNN), lambda b, t: (b, 0, 0)),
    ]
    out_shape = [
        jax.ShapeDtypeStruct((nb, seq, D_MODEL), F32),
        jax.ShapeDtypeStruct((nb, H_RET, DK, DV), F32),
        jax.ShapeDtypeStruct((nb, 1, D_RNN), F32),
        jax.ShapeDtypeStruct((nb, CONV_W - 1, D_RNN), F32),
    ]
    halo = (CONV_W - 1) * SUBLANES
    scratch = [
        pltpu.VMEM((tm, D_MODEL), BF16),
        pltpu.VMEM((halo + tm, D_RNN), F32),
        pltpu.VMEM((halo, D_RNN), F32),
        pltpu.VMEM((SUBLANES, D_RNN), F32),
    ]
    return pl.pallas_call(
        _mix_prompt_kernel,
        grid=(nb, nt),
        in_specs=in_specs,
        out_specs=out_specs,
        out_shape=out_shape,
        scratch_shapes=scratch,
        compiler_params=pltpu.CompilerParams(
            dimension_semantics=("arbitrary", "arbitrary"),
            vmem_limit_bytes=VMEM_BYTES_V7X - 4 * 1024 * 1024),
    )(gpow, x, mod3, p['npre'], p['wcat'], p['bmg'], rope_tab, dec, qdec, kdec, p['gnw'], p['wbr_ret'],
      p['convw'], p['convb'], p['wrg'], p['bra'], p['brx'], p['lru'], p['wbr_rnn'], p['wout'], p['npost'])


def _ffn_kernel(x_ref, mod_ref, npre_ref, w1_ref, w2_ref, npost_ref, o_ref):
    x = x_ref[...]
    m = mod_ref[...]
    sh2, sc2, g2 = m[:, :D_MODEL], m[:, D_MODEL:2 * D_MODEL], m[:, 2 * D_MODEL:]
    h2 = (_rms(x, npre_ref[...]) * (1.0 + sc2) + sh2).astype(BF16)
    acc = jnp.zeros(x.shape, F32)
    for j in range(D_FF // FF_CHUNK):
        c0 = j * FF_CHUNK
        fg = _dot(h2, _wb(w1_ref, c0=c0, c1=c0 + FF_CHUNK))
        fu = _dot(h2, _wb(w1_ref, c0=D_FF + c0, c1=D_FF + c0 + FF_CHUNK))
        acc = acc + _dot((_silu(fg) * fu).astype(BF16), _wb(w2_ref, c0, c0 + FF_CHUNK))
    o_ref[...] = x + g2 * _rms(acc, npost_ref[...])


def _ffn_call(x2d, mod, mod_rows_per_tile, seq_tiles, p):
    rows = x2d.shape[0]
    tm = FFN_TM
    if mod_rows_per_tile == 1:
        mod_spec = pl.BlockSpec((None, 1, 3 * D_MODEL), lambda i: (i // seq_tiles, 0, 1))
    else:
        mod_spec = pl.BlockSpec((tm, 3 * D_MODEL), lambda i: (i, 1))
    return pl.pallas_call(
        _ffn_kernel,
        grid=(rows // tm,),
        in_specs=[
            pl.BlockSpec((tm, D_MODEL), lambda i: (i, 0)),
            mod_spec,
            _const_spec((1, D_MODEL)),
            _const_spec((D_MODEL, 2 * D_FF)),
            _const_spec((D_FF, D_MODEL)),
            _const_spec((1, D_MODEL)),
        ],
        out_specs=pl.BlockSpec((tm, D_MODEL), lambda i: (i, 0)),
        out_shape=jax.ShapeDtypeStruct((rows, D_MODEL), F32),
        compiler_params=pltpu.CompilerParams(
            dimension_semantics=("arbitrary",),
            vmem_limit_bytes=48 * 1024 * 1024),
    )(x2d, mod, p['npre_ffn'], p['wffn_in'], p['wffn_out'], p['npost_ffn'])


def _proj_sample_kernel(x_ref, mod_ref, npre_ref, w_ref, o_ref):
    m = mod_ref[...]
    sh1, sc1 = m[:, :D_MODEL], m[:, D_MODEL:2 * D_MODEL]
    h = (_rms(x_ref[...], npre_ref[...]) * (1.0 + sc1) + sh1).astype(BF16)
    o_ref[...] = _dot(h, _wb(w_ref))


def _proj_sample_call(x2d, mod_s, p):
    rows = x2d.shape[0]
    tn = D_MODEL
    return pl.pallas_call(
        _proj_sample_kernel,
        grid=(N_CAT // tn,),
        in_specs=[
            pl.BlockSpec((rows, D_MODEL), lambda j: (0, 0)),
            pl.BlockSpec((rows, 3 * D_MODEL), lambda j: (0, 0)),
            pl.BlockSpec((1, D_MODEL), lambda j: (0, 0)),
            pl.BlockSpec((D_MODEL, tn), lambda j: (0, j)),
        ],
        out_specs=pl.BlockSpec((rows, tn), lambda j: (0, j)),
        out_shape=jax.ShapeDtypeStruct((rows, N_CAT), F32),
        compiler_params=pltpu.CompilerParams(
            dimension_semantics=("arbitrary",),
            vmem_limit_bytes=48 * 1024 * 1024),
    )(x2d, mod_s, p['npre'], p['wcat'])


def _ret_sample_kernel(gpow_ref, qkv_ref, s_ref, rope_ref, dec_ref, qdec_ref, kdec_ref, o_ref, snew_ref):
    half = DK // 2
    cos, sin = rope_ref[:, 0:half], rope_ref[:, half:2 * half]
    cosk, sink = rope_ref[:, 2 * half:3 * half], rope_ref[:, 3 * half:4 * half]
    nseq = RET_SAMPLE_BB
    tlen = SUBLANES // nseq
    row = lax.broadcasted_iota(jnp.int32, (SUBLANES, LANES), 0)
    for hh in range(H_RET):
        q = _rope(qkv_ref[:, OFF_Q + hh * DK:OFF_Q + (hh + 1) * DK], cos, sin)
        k = _rope(qkv_ref[:, OFF_K + hh * DK:OFF_K + (hh + 1) * DK], cosk, sink)
        v = qkv_ref[:, OFF_V + hh * DV:OFF_V + (hh + 1) * DV]
        kdec = kdec_ref[hh]
        kd = k * jnp.concatenate([kdec, kdec], axis=1)
        scores = _dot_nt(q, k) * dec_ref[hh]
        intra = _dot(scores, v)
        qb = q.astype(BF16)
        cross = jnp.zeros((SUBLANES, DV), F32)
        for bi in range(nseq):
            s_old = s_ref[bi, hh]
            in_seq = (row >= bi * tlen) & (row < (bi + 1) * tlen)
            cr = _dot(qb, s_old.astype(BF16))
            cross = jnp.where(jnp.concatenate([in_seq] * (DV // LANES), axis=1), cr, cross)
            kd_b = jnp.where(jnp.concatenate([in_seq] * (DK // LANES), axis=1), kd, 0.0)
            snew_ref[bi, hh] = gpow_ref[hh] * s_old + _dot_tn(kd_b, v)
        qdec = qdec_ref[hh]
        o_ref[:, hh * DV:(hh + 1) * DV] = intra + cross * jnp.concatenate([qdec] * (DV // LANES), axis=1)


def _ret_sample_call(proj_s, state, gpow, rope8, dec8, qdec8, kdec8):
    nb = state.shape[0]
    bb = RET_SAMPLE_BB
    rows = proj_s.shape[0]
    qkv_w = OFF_G
    return pl.pallas_call(
        _ret_sample_kernel,
        grid=(nb // bb,),
        in_specs=[
            pl.BlockSpec(memory_space=pltpu.SMEM),
            pl.BlockSpec((SUBLANES, qkv_w), lambda i: (i, 0)),
            pl.BlockSpec((bb, H_RET, DK, DV), lambda i: (i, 0, 0, 0)),
            pl.BlockSpec((SUBLANES, 4 * (DK // 2)), lambda i: (0, 0)),
            pl.BlockSpec((H_RET, SUBLANES, SUBLANES), lambda i: (0, 0, 0)),
            pl.BlockSpec((H_RET, SUBLANES, LANES), lambda i: (0, 0, 0)),
            pl.BlockSpec((H_RET, SUBLANES, LANES), lambda i: (0, 0, 0)),
        ],
        out_specs=[
            pl.BlockSpec((SUBLANES, D_V), lambda i: (i, 0)),
            pl.BlockSpec((bb, H_RET, DK, DV), lambda i: (i, 0, 0, 0)),
        ],
        out_shape=[
            jax.ShapeDtypeStruct((rows, D_V), F32),
            jax.ShapeDtypeStruct(state.shape, F32),
        ],
        compiler_params=pltpu.CompilerParams(
            dimension_semantics=("arbitrary",),
            vmem_limit_bytes=40 * 1024 * 1024),
    )(gpow, proj_s, state, rope8, dec8, qdec8, kdec8)


def _mix_sample_kernel(x_ref, mod_ref, proj_ref, o_ref, h0_ref, xs1_ref, xs2_ref, xs3_ref,
                       bmg_ref, gnw_ref, wbr_ret_ref, convw_ref, convb_ref, wrg_ref, bra_ref, brx_ref,
                       lru_ref, wbr_rnn_ref, wout_ref, npost_ref,
                       x1_ref, hseq_ref,
                       xr_ref, sa_ref, sb_ref):
    tm = SAMPLE_TM
    tlen = 4
    x = x_ref[...]
    g1 = mod_ref[:, 2 * D_MODEL:]

    br_ret = jnp.zeros((tm, D_MODEL), F32)
    for hh in range(H_RET):
        on = _group_norm(o_ref[:, hh * DV:(hh + 1) * DV])
        g = proj_ref[:, OFF_G + hh * DV:OFF_G + (hh + 1) * DV]
        ry = (on * gnw_ref[:, hh * DV:(hh + 1) * DV] * _silu(g)).astype(BF16)
        br_ret = br_ret + _dot(ry, _wb(wbr_ret_ref, hh * DV, (hh + 1) * DV))

    tpos = lax.broadcasted_iota(jnp.int32, (tm, D_RNN), 0) & (tlen - 1)
    xr = proj_ref[:, OFF_XR:OFF_XR + D_RNN]
    xr_ref[0:SUBLANES, :] = jnp.zeros((SUBLANES, D_RNN), F32)
    xr_ref[SUBLANES:SUBLANES + tm, :] = xr
    cw = convw_ref[...]
    prev = (xs3_ref, xs2_ref, xs1_ref)
    xconv = convb_ref[...]
    for j in range(CONV_W - 1):
        sft = CONV_W - 1 - j
        shifted = jnp.where(tpos >= sft, xr_ref[SUBLANES - sft:SUBLANES - sft + tm, :], 0.0)
        xconv = xconv + (shifted + prev[j][...]) * cw[j:j + 1, :]
    xconv = xconv + xr * cw[CONV_W - 1:CONV_W, :]

    a, b = _lru_coeffs(xconv, _lru_gate_pre(xconv, wrg_ref), bra_ref[...], brx_ref[...], lru_ref[...])
    b = jnp.where(tpos == 0, b + a * h0_ref[...], b)
    sa_ref[0:SUBLANES, :] = jnp.zeros((SUBLANES, D_RNN), F32)
    sb_ref[0:SUBLANES, :] = jnp.zeros((SUBLANES, D_RNN), F32)
    for s in (1, 2):
        sa_ref[SUBLANES:SUBLANES + tm, :] = a
        sb_ref[SUBLANES:SUBLANES + tm, :] = b
        keep = tpos >= s
        ap = jnp.where(keep, sa_ref[SUBLANES - s:SUBLANES - s + tm, :], 1.0)
        bp = jnp.where(keep, sb_ref[SUBLANES - s:SUBLANES - s + tm, :], 0.0)
        b = a * bp + b
        a = a * ap
    hseq_ref[...] = b

    gr = proj_ref[:, OFF_GR:OFF_GR + D_RNN]
    rnn_y_b = (b * jax.nn.gelu(gr, approximate=True)).astype(BF16)
    gate_pre = proj_ref[:, OFF_MG:OFF_MG + 2 * D_MODEL]
    x1_ref[...] = _mix_tail(x, g1, br_ret, rnn_y_b, gate_pre, bmg_ref[...], wbr_rnn_ref, wout_ref,
                            npost_ref[...])


def _mix_sample_call(x2d, mod_s, proj_s, o_s, h0rep, xs1, xs2, xs3, p):
    rows = x2d.shape[0]
    tm = SAMPLE_TM
    row_spec = lambda w: pl.BlockSpec((tm, w), lambda i: (i, 0))
    return pl.pallas_call(
        _mix_sample_kernel,
        grid=(rows // tm,),
        in_specs=[
            row_spec(D_MODEL), row_spec(3 * D_MODEL), row_spec(N_CAT), row_spec(D_V),
            row_spec(D_RNN), row_spec(D_RNN), row_spec(D_RNN), row_spec(D_RNN),
            _const_spec((1, 2 * D_MODEL)),
            _const_spec((1, D_V)),
            _const_spec((D_V, D_MODEL)),
            _const_spec((CONV_W, D_RNN)),
            _const_spec((1, D_RNN)),
            _const_spec((N_RNN_BLOCKS, RNN_BLOCK, 2 * RNN_BLOCK)),
            _const_spec((1, D_RNN)),
            _const_spec((1, D_RNN)),
            _const_spec((1, D_RNN)),
            _const_spec((D_RNN, D_MODEL)),
            _const_spec((D_MODEL, D_MODEL)),
            _const_spec((1, D_MODEL)),
        ],
        out_specs=[row_spec(D_MODEL), row_spec(D_RNN)],
        out_shape=[
            jax.ShapeDtypeStruct((rows, D_MODEL), F32),
            jax.ShapeDtypeStruct((rows, D_RNN), F32),
        ],
        scratch_shapes=[
            pltpu.VMEM((tm + SUBLANES, D_RNN), F32),
            pltpu.VMEM((tm + SUBLANES, D_RNN), F32),
            pltpu.VMEM((tm + SUBLANES, D_RNN), F32),
        ],
        compiler_params=pltpu.CompilerParams(
            dimension_semantics=("arbitrary",),
            vmem_limit_bytes=56 * 1024 * 1024),
    )(x2d, mod_s, proj_s, o_s, h0rep, xs1, xs2, xs3, p['bmg'], p['gnw'], p['wbr_ret'], p['convw'],
      p['convb'], p['wrg'], p['bra'], p['brx'], p['lru'], p['wbr_rnn'], p['wout'], p['npost'])


def _rope_table(pos):
    half = DK // 2
    inv = ROPE_BASE ** (-jnp.arange(half, dtype=F32) / half)
    ang = pos.astype(F32)[:, None] * inv[None, :]
    cos, sin = jnp.cos(ang), jnp.sin(ang)
    ks = DK ** -0.5
    return jnp.concatenate([cos, sin, cos * ks, sin * ks], axis=1)


def _decay_tables(tpos, same_seq, chunk):
    lg = jnp.log(1.0 - 2.0 ** (-5.0 - jnp.arange(H_RET, dtype=F32)))
    idx = tpos.astype(F32)
    diff = idx[:, None] - idx[None, :]
    causal = (diff >= 0) & same_seq
    dec = jnp.where(causal[None], jnp.exp(jnp.where(causal, diff, 0.0)[None] * lg[:, None, None]), 0.0)
    qdec = jnp.exp((idx + 1.0)[None, :] * lg[:, None])
    kdec = jnp.exp((chunk - 1.0 - idx)[None, :] * lg[:, None])
    rep = lambda a: jnp.broadcast_to(a[:, :, None], a.shape + (LANES,))
    gpow = jnp.exp(chunk * lg)
    return dec, rep(qdec), rep(kdec), gpow


def kernel(x_prompt, x_sample, state_ret, state_rnn_h, state_rnn_conv, c_prompt, c_sample,
           w_ada, b_ada, norm_pre_mix, norm_post_mix, norm_pre_ffn, norm_post_ffn,
           w_in, ret_gn_w, w_br_ret, conv_w, conv_b, w_rg_a, b_rg_a, w_rg_x, b_rg_x,
           lru_param, w_br_rnn, w_mgate, b_mgate, w_out, w_ffn_in, w_ffn_out):
    depth = w_in.shape[0]
    assert depth == 1, "single layer step"
    nb, seq, _ = x_prompt.shape
    nsb, sseq, _ = x_sample.shape
    assert seq % PROMPT_TM == 0 and sseq * RET_SAMPLE_BB == SUBLANES and sseq == CONV_W
    l = 0
    row = lambda a: a[l][None, :]
    p = dict(
        npre=row(norm_pre_mix), npost=row(norm_post_mix), npre_ffn=row(norm_pre_ffn), npost_ffn=row(norm_post_ffn),
        wcat=_pack_rows(jnp.concatenate([w_in[l], w_mgate[l]], axis=1)),
        bmg=row(b_mgate), gnw=row(ret_gn_w), wbr_ret=_pack_rows(w_br_ret[l]),
        convw=conv_w[l], convb=row(conv_b),
        wrg=_pack_rows(jnp.concatenate([w_rg_a[l], w_rg_x[l]], axis=2)),
        bra=row(b_rg_a), brx=row(b_rg_x), lru=row(lru_param),
        wbr_rnn=_pack_rows(w_br_rnn[l]), wout=_pack_rows(w_out[l]),
        wffn_in=_pack_rows(w_ffn_in[l]), wffn_out=_pack_rows(w_ffn_out[l]),
    )

    rows_s = nsb * sseq
    c_all = jnp.concatenate([jnp.repeat(c_sample, sseq, axis=0), c_prompt], axis=0)
    mod_all = _mod_call(c_all, _pack_rows(w_ada[l]), row(b_ada))
    mod_p = mod_all[rows_s:].reshape(nb, 1, 6 * D_MODEL)
    mod_s = mod_all

    tm = PROMPT_TM
    ng = tm // SUBLANES
    r = jnp.arange(tm)
    tpos = (r % SUBLANES) * ng + r // SUBLANES
    pos_p = (jnp.arange(seq // tm)[:, None] * tm + tpos[None, :]).reshape(seq).astype(jnp.int32)
    rope_p = _rope_table(pos_p)
    dec, qdec, kdec, gpow = _decay_tables(tpos, jnp.ones((tm, tm), bool), float(tm))
    interleave = lambda a: a.reshape(nb, seq // tm, SUBLANES, ng, D_MODEL).swapaxes(2, 3).reshape(nb, seq, D_MODEL)
    restore = lambda a: a.reshape(nb, seq // tm, ng, SUBLANES, D_MODEL).swapaxes(2, 3).reshape(nb, seq, D_MODEL)
    x1p, ret_p, hlast_p, conv_p = _mix_prompt_call(interleave(x_prompt), mod_p, gpow, rope_p, dec, qdec, kdec, p)
    yp = restore(_ffn_call(x1p.reshape(nb * seq, D_MODEL), mod_p, 1, seq // FFN_TM, p).reshape(nb, seq, D_MODEL))

    xs2d = x_sample.reshape(rows_s, D_MODEL)
    proj_s = _proj_sample_call(xs2d, mod_s, p)
    r8 = jnp.arange(SUBLANES)
    rope_s = _rope_table(PAST_LEN + (r8 % sseq).astype(jnp.int32))
    same = (r8[:, None] // sseq) == (r8[None, :] // sseq)
    dec8, qdec8, kdec8, gpow_s = _decay_tables(r8 % sseq, same, float(sseq))
    o_s, ret_s = _ret_sample_call(proj_s, state_ret[l], gpow_s, rope_s, dec8, qdec8, kdec8)
    cs = state_rnn_conv[l]
    pad_rows = lambda a: jnp.pad(a, ((0, 0), (0, sseq - a.shape[1]), (0, 0))).reshape(rows_s, D_RNN)
    xs1, xs2, xs3 = pad_rows(cs[:, 2:3]), pad_rows(cs[:, 1:3]), pad_rows(cs[:, 0:3])
    h0rep = jnp.repeat(state_rnn_h[l], sseq, axis=0)
    x1s, hseq_s = _mix_sample_call(xs2d, mod_s, proj_s, o_s, h0rep, xs1, xs2, xs3, p)
    ys = _ffn_call(x1s, mod_s, FFN_TM, 1, p).reshape(nsb, sseq, D_MODEL)
    hlast_s = hseq_s.reshape(nsb, sseq, D_RNN)[:, sseq - 1]
    conv_s = proj_s[:, OFF_XR:OFF_XR + D_RNN].reshape(nsb, sseq, D_RNN)[:, sseq - (CONV_W - 1):]

    return (yp, ys, ret_p[None], ret_s[None], hlast_p.reshape(nb, D_RNN)[None], hlast_s[None],
            conv_p[None], conv_s[None])
```

```python
import functools

import jax
import jax.numpy as jnp
from jax import lax
from jax.experimental import pallas as pl
from jax.experimental.pallas import tpu as pltpu

F32 = jnp.float32
BF16 = jnp.bfloat16

D_MODEL = 1024
H_RET = 4
DK = D_MODEL // H_RET
DV = 2 * DK
D_QK = H_RET * DK
D_V = H_RET * DV
D_RNN = 1536
RNN_BLOCK = 128
N_RNN_BLOCKS = D_RNN // RNN_BLOCK
CONV_W = 4
LRU_C = 8.0
D_FF = 2816
ROPE_BASE = 10000.0
GN_EPS = 1e-5
RMS_EPS = 1e-6
PAST_LEN = 16384

OFF_Q = 0
OFF_K = OFF_Q + D_QK
OFF_V = OFF_K + D_QK
OFF_G = OFF_V + D_V
OFF_XR = OFF_G + D_V
OFF_GR = OFF_XR + D_RNN
OFF_MG = OFF_GR + D_RNN
N_CAT = OFF_MG + 2 * D_MODEL

SUBLANES = 8
LANES = 128
MXU_DIM = 256
VMEM_BYTES_V7X = 64 * 1024 * 1024

PROMPT_TM = 256
FFN_TM = 512
FF_CHUNK = MXU_DIM
SAMPLE_TM = 128
RET_SAMPLE_BB = 2


def _dot(a, b):
    return jnp.dot(a, b, preferred_element_type=F32)


def _dot_nt(a, b):
    return lax.dot_general(a, b, (((1,), (1,)), ((), ())), preferred_element_type=F32)


def _dot_tn(a, b):
    return lax.dot_general(a, b, (((0,), (0,)), ((), ())), preferred_element_type=F32)


def _wb(ref, k0=None, k1=None, c0=None, c1=None):
    rs = slice(None) if k0 is None else slice(k0, k1)
    cs = slice(None) if c0 is None else slice(c0, c1)
    return ref[rs, cs]


def _pack_rows(w):
    return w.astype(BF16)


def _rms(x, w):
    ms = jnp.mean(x * x, axis=-1, keepdims=True)
    return x * lax.rsqrt(ms + RMS_EPS) * w


def _silu(x):
    return x * jax.nn.sigmoid(x)


def _rope(x, cos, sin):
    half = DK // 2
    x1, x2 = x[:, :half], x[:, half:]
    return jnp.concatenate([x1 * cos - x2 * sin, x1 * sin + x2 * cos], axis=1)


def _group_norm(o):
    mu = jnp.mean(o, axis=-1, keepdims=True)
    d = o - mu
    var = jnp.mean(d * d, axis=-1, keepdims=True)
    return d * lax.rsqrt(var + GN_EPS)


def _lru_gate_pre(xconv, wrg_ref):
    xcb = xconv.astype(BF16)
    return [_dot(xcb[:, n * RNN_BLOCK:(n + 1) * RNN_BLOCK], wrg_ref[n]) for n in range(N_RNN_BLOCKS)]


def _lru_coeffs(xconv, pre, b_a, b_x, lru):
    ra = jnp.concatenate([p[:, :RNN_BLOCK] for p in pre], axis=1) + b_a
    ri = jnp.concatenate([p[:, RNN_BLOCK:] for p in pre], axis=1) + b_x
    r = jax.nn.sigmoid(ra)
    i = jax.nn.sigmoid(ri)
    z = -lru
    sp = jnp.maximum(z, 0.0) + jnp.log(1.0 + jnp.exp(-jnp.abs(z)))
    log_a = -LRU_C * r * sp
    a = jnp.exp(log_a)
    beta = jnp.sqrt(-jnp.tanh(log_a) * (jnp.exp(2.0 * log_a) + 1.0))
    return a, beta * (i * xconv)


def _mix_tail(x, g1, br_ret, rnn_y_b, gate_pre, b_mg, wbr_rnn_ref, wout_ref, npost):
    br_rnn = _dot(rnn_y_b, _wb(wbr_rnn_ref))
    gates = jax.nn.sigmoid(gate_pre + b_mg)
    ga, gb = gates[:, :D_MODEL], gates[:, D_MODEL:]
    mixed = _dot((ga * br_ret + gb * br_rnn).astype(BF16), _wb(wout_ref))
    return x + g1 * _rms(mixed, npost)


def _mod_kernel(c_ref, w_ref, b_ref, o_ref):
    a = _silu(c_ref[...]).astype(BF16)
    o_ref[...] = _dot(a, _wb(w_ref)) + b_ref[...]


def _mod_call(c_all, w_ada_b, b_ada):
    rows = c_all.shape[0]
    tn = D_MODEL
    return pl.pallas_call(
        _mod_kernel,
        grid=(6 * D_MODEL // tn,),
        in_specs=[
            pl.BlockSpec((rows, D_MODEL), lambda j: (0, 0)),
            pl.BlockSpec((D_MODEL, tn), lambda j: (0, j)),
            pl.BlockSpec((1, tn), lambda j: (0, j)),
        ],
        out_specs=pl.BlockSpec((rows, tn), lambda j: (0, j)),
        out_shape=jax.ShapeDtypeStruct((rows, 6 * D_MODEL), F32),
        compiler_params=pltpu.CompilerParams(dimension_semantics=("arbitrary",)),
    )(c_all, w_ada_b, b_ada)


def _mix_prompt_kernel(gpow_ref, x_ref, mod_ref, npre_ref, wcat_ref, bmg_ref, rope_ref,
                       dec_ref, qdec_ref, kdec_ref, gnw_ref, wbr_ret_ref, convw_ref, convb_ref,
                       wrg_ref, bra_ref, brx_ref, lru_ref, wbr_rnn_ref, wout_ref, npost_ref,
                       x1_ref, s_ref, hlast_ref, convnew_ref,
                       hb_ref, xr_ref, prevg_ref, hc_ref):
    tm = PROMPT_TM
    ng = tm // SUBLANES
    halo = (CONV_W - 1) * SUBLANES
    t = pl.program_id(1)

    @pl.when(t == 0)
    def _():
        s_ref[...] = jnp.zeros_like(s_ref)
        prevg_ref[...] = jnp.zeros_like(prevg_ref)
        hc_ref[...] = jnp.zeros_like(hc_ref)

    x = x_ref[...]
    m = mod_ref[...]
    sh1, sc1, g1 = m[:, :D_MODEL], m[:, D_MODEL:2 * D_MODEL], m[:, 2 * D_MODEL:]
    hb_ref[...] = (_rms(x, npre_ref[...]) * (1.0 + sc1) + sh1).astype(BF16)

    hb = hb_ref[...]
    sub = lax.broadcasted_iota(jnp.int32, (SUBLANES, D_RNN), 0)
    half = DK // 2
    cos, sin = rope_ref[:, 0:half], rope_ref[:, half:2 * half]
    cosk, sink = rope_ref[:, 2 * half:3 * half], rope_ref[:, 3 * half:4 * half]
    st = {}

    def xr_proj():
        xr = _dot(hb, _wb(wcat_ref, c0=OFF_XR, c1=OFF_XR + D_RNN))
        xr_ref[halo:halo + tm, :] = xr
        for kk in range(1, CONV_W):
            r0 = (CONV_W - 1 - kk) * SUBLANES
            cur = xr[(ng - kk) * SUBLANES:(ng - kk + 1) * SUBLANES, :]
            prv = prevg_ref[r0:r0 + SUBLANES, :]
            xr_ref[r0:r0 + SUBLANES, :] = pltpu.roll(jnp.where(sub == SUBLANES - 1, prv, cur), 1, 0)
        prevg_ref[...] = xr[tm - halo:, :]
        for kk in range(1, CONV_W):
            r1 = (ng - kk) * SUBLANES + SUBLANES - 1
            convnew_ref[CONV_W - 1 - kk:CONV_W - kk, :] = xr[r1:r1 + 1, :]

    def lru_conv():
        cw = convw_ref[...]
        xconv = convb_ref[...]
        for j in range(CONV_W):
            r0 = halo - (CONV_W - 1 - j) * SUBLANES
            xconv = xconv + xr_ref[r0:r0 + tm, :] * cw[j:j + 1, :]
        st['xconv'] = xconv

    def lru_gate_proj():
        st['gpre'] = _lru_gate_pre(st['xconv'], wrg_ref)

    def lru_coef():
        st['a'], st['b'] = _lru_coeffs(st.pop('xconv'), st.pop('gpre'), bra_ref[...], brx_ref[...], lru_ref[...])

    def lru_scan():
        a, b = st['a'], st['b']
        ca, cb = a[0:SUBLANES, :], b[0:SUBLANES, :]
        cas, cbs = [ca], [cb]
        for gi in range(1, ng):
            ag = a[gi * SUBLANES:(gi + 1) * SUBLANES, :]
            cb = ag * cb + b[gi * SUBLANES:(gi + 1) * SUBLANES, :]
            ca = ag * ca
            cas.append(ca)
            cbs.append(cb)
        cin = jnp.where(sub == 0, hc_ref[SUBLANES - 1:SUBLANES, :], 0.0)
        for s in range(SUBLANES - 1):
            cin = jnp.where(sub == s + 1, pltpu.roll(ca * cin + cb, 1, 0), cin)
        seg_end = ca * cin + cb
        hc_ref[...] = seg_end
        hlast_ref[...] = seg_end[SUBLANES - 1:SUBLANES, :]
        st['hseq'] = jnp.concatenate([cas[gi] * cin + cbs[gi] for gi in range(ng)], axis=0)

    def gr_proj():
        st['gr'] = _dot(hb, _wb(wcat_ref, c0=OFF_GR, c1=OFF_GR + D_RNN))

    def lru_y():
        st['rnn_y'] = (st.pop('hseq') * jax.nn.gelu(st.pop('gr'), approximate=True)).astype(BF16)

    def lru_out():
        st['br_rnn'] = _dot(st.pop('rnn_y'), _wb(wbr_rnn_ref))

    def gate_proj():
        st['gate_pre'] = _dot(hb, _wb(wcat_ref, c0=OFF_MG, c1=OFF_MG + 2 * D_MODEL))

    def gate_act():
        st['gates'] = jax.nn.sigmoid(st.pop('gate_pre') + bmg_ref[...])

    def head_proj(hh):
        q = _dot(hb, _wb(wcat_ref, c0=OFF_Q + hh * DK, c1=OFF_Q + (hh + 1) * DK))
        k = _dot(hb, _wb(wcat_ref, c0=OFF_K + hh * DK, c1=OFF_K + (hh + 1) * DK))
        vb = _dot(hb, _wb(wcat_ref, c0=OFF_V + hh * DV, c1=OFF_V + (hh + 1) * DV)).astype(BF16)
        g = _dot(hb, _wb(wcat_ref, c0=OFF_G + hh * DV, c1=OFF_G + (hh + 1) * DV))
        st['proj', hh] = (q, k, vb, g)

    def head_rope(hh):
        q, k, vb, g = st.pop(('proj', hh))
        kr = _rope(k, cosk, sink)
        kdec = kdec_ref[hh]
        kdb = (kr * jnp.concatenate([kdec, kdec], axis=1)).astype(BF16)
        st['rope', hh] = (_rope(q, cos, sin).astype(BF16), kr.astype(BF16), kdb, vb, g)

    def head_qk(hh):
        qb, kb, kdb, vb, g = st.pop(('rope', hh))
        scores = _dot_nt(qb, kb)
        cross = _dot(qb, s_ref[hh].astype(BF16))
        st['qk', hh] = (scores, cross, vb, g)
        st['kv', hh] = (kdb, vb)

    def head_state(hh):
        kdb, vb = st.pop(('kv', hh))
        s_ref[hh] = gpow_ref[hh] * s_ref[hh] + _dot_tn(kdb, vb)

    def head_decay(hh):
        scores, cross, vb, g = st.pop(('qk', hh))
        st['dec', hh] = ((scores * dec_ref[hh]).astype(BF16), cross, vb, g)

    def head_pv(hh):
        sb, cross, vb, g = st.pop(('dec', hh))
        st['pv', hh] = (_dot(sb, vb), cross, g)

    def head_norm(hh):
        intra, cross, g = st.pop(('pv', hh))
        qdec = qdec_ref[hh]
        on = _group_norm(intra + cross * jnp.concatenate([qdec] * (DV // LANES), axis=1))
        st['ry', hh] = (on * gnw_ref[:, hh * DV:(hh + 1) * DV] * _silu(g)).astype(BF16)

    def head_out(hh):
        part = _dot(st.pop(('ry', hh)), _wb(wbr_ret_ref, hh * DV, (hh + 1) * DV))
        st['br_ret'] = part if hh == 0 else st['br_ret'] + part

    heads = range(H_RET)
    order = (
        [xr_proj] + [(head_proj, h) for h in heads] + [lru_conv, lru_gate_proj]
        + [(head_rope, h) for h in heads] + [gr_proj, lru_coef] + [(head_qk, h) for h in heads]
        + [gate_proj] + [(head_decay, h) for h in heads] + [lru_scan]
        + [(head_pv, h) for h in heads] + [(head_state, h) for h in heads] + [lru_y, lru_out]
        + [(head_norm, h) for h in heads] + [(head_out, h) for h in heads] + [gate_act]
    )
    for stage in order:
        if isinstance(stage, tuple):
            stage[0](stage[1])
        else:
            stage()

    gates = st['gates']
    ga, gb = gates[:, :D_MODEL], gates[:, D_MODEL:]
    mixed = _dot((ga * st['br_ret'] + gb * st['br_rnn']).astype(BF16), _wb(wout_ref))
    x1_ref[...] = x + g1 * _rms(mixed, npost_ref[...])


def _const_spec(shape):
    nd = len(shape)
    return pl.BlockSpec(shape, lambda *_: (0,) * nd, pipeline_mode=pl.Buffered(1))


def _mix_prompt_call(x, mod3, gpow, rope_tab, dec, qdec, kdec, p):
    nb, seq, _ = x.shape
    tm = PROMPT_TM
    nt = seq // tm
    in_specs = [
        pl.BlockSpec(memory_space=pltpu.SMEM),
        pl.BlockSpec((None, tm, D_MODEL), lambda b, t: (b, t, 0)),
        pl.BlockSpec((None, 1, 3 * D_MODEL), lambda b, t: (b, 0, 0)),
        _const_spec((1, D_MODEL)),
        _const_spec((D_MODEL, N_CAT)),
        _const_spec((1, 2 * D_MODEL)),
        pl.BlockSpec((tm, 4 * (DK // 2)), lambda b, t: (t, 0)),
        _const_spec((H_RET, tm, tm)),
        _const_spec((H_RET, tm, LANES)),
        _const_spec((H_RET, tm, LANES)),
        _const_spec((1, D_V)),
        _const_spec((D_V, D_MODEL)),
        _const_spec((CONV_W, D_RNN)),
        _const_spec((1, D_RNN)),
        _const_spec((N_RNN_BLOCKS, RNN_BLOCK, 2 * RNN_BLOCK)),
        _const_spec((1, D_RNN)),
        _const_spec((1, D_RNN)),
        _const_spec((1, D_RNN)),
        _const_spec((D_RNN, D_MODEL)),
        _const_spec((D_MODEL, D_MODEL)),
        _const_spec((1, D_MODEL)),
    ]
    out_specs = [
        pl.BlockSpec((None, tm, D_MODEL), lambda b, t: (b, t, 0)),
        pl.BlockSpec((None, H_RET, DK, DV), lambda b, t: (b, 0, 0, 0)),
        pl.BlockSpec((None, 1, D_RNN), lambda b, t: (b, 0, 0)),
        pl.BlockSpec((None, CONV_W - 1, D_RNN), lambda b, t: (b, 0, 0)),
    ]
    out_shape = [
        jax.ShapeDtypeStruct((nb, seq, D_MODEL), F32),
        jax.ShapeDtypeStruct((nb, H_RET, DK, DV), F32),
        jax.ShapeDtypeStruct((nb, 1, D_RNN), F32),
        jax.ShapeDtypeStruct((nb, CONV_W - 1, D_RNN), F32),
    ]
    halo = (CONV_W - 1) * SUBLANES
    scratch = [
        pltpu.VMEM((tm, D_MODEL), BF16),
        pltpu.VMEM((halo + tm, D_RNN), F32),
        pltpu.VMEM((halo, D_RNN), F32),
        pltpu.VMEM((SUBLANES, D_RNN), F32),
    ]
    return pl.pallas_call(
        _mix_prompt_kernel,
        grid=(nb, nt),
        in_specs=in_specs,
        out_specs=out_specs,
        out_shape=out_shape,
        scratch_shapes=scratch,
        compiler_params=pltpu.CompilerParams(
            dimension_semantics=("arbitrary", "arbitrary"),
            vmem_limit_bytes=VMEM_BYTES_V7X - 4 * 1024 * 1024),
    )(gpow, x, mod3, p['npre'], p['wcat'], p['bmg'], rope_tab, dec, qdec, kdec, p['gnw'], p['wbr_ret'],
      p['convw'], p['convb'], p['wrg'], p['bra'], p['brx'], p['lru'], p['wbr_rnn'], p['wout'], p['npost'])


def _ffn_kernel(x_ref, mod_ref, npre_ref, w1_ref, w2_ref, npost_ref, o_ref):
    x = x_ref[...]
    m = mod_ref[...]
    sh2, sc2, g2 = m[:, :D_MODEL], m[:, D_MODEL:2 * D_MODEL], m[:, 2 * D_MODEL:]
    h2 = (_rms(x, npre_ref[...]) * (1.0 + sc2) + sh2).astype(BF16)
    acc = jnp.zeros(x.shape, F32)
    for j in range(D_FF // FF_CHUNK):
        c0 = j * FF_CHUNK
        fg = _dot(h2, _wb(w1_ref, c0=c0, c1=c0 + FF_CHUNK))
        fu = _dot(h2, _wb(w1_ref, c0=D_FF + c0, c1=D_FF + c0 + FF_CHUNK))
        acc = acc + _dot((_silu(fg) * fu).astype(BF16), _wb(w2_ref, c0, c0 + FF_CHUNK))
    o_ref[...] = x + g2 * _rms(acc, npost_ref[...])


def _ffn_call(x2d, mod, mod_rows_per_tile, seq_tiles, p):
    rows = x2d.shape[0]
    tm = FFN_TM
    if mod_rows_per_tile == 1:
        mod_spec = pl.BlockSpec((None, 1, 3 * D_MODEL), lambda i: (i // seq_tiles, 0, 1))
    else:
        mod_spec = pl.BlockSpec((tm, 3 * D_MODEL), lambda i: (i, 1))
    return pl.pallas_call(
        _ffn_kernel,
        grid=(rows // tm,),
        in_specs=[
            pl.BlockSpec((tm, D_MODEL), lambda i: (i, 0)),
            mod_spec,
            _const_spec((1, D_MODEL)),
            _const_spec((D_MODEL, 2 * D_FF)),
            _const_spec((D_FF, D_MODEL)),
            _const_spec((1, D_MODEL)),
        ],
        out_specs=pl.BlockSpec((tm, D_MODEL), lambda i: (i, 0)),
        out_shape=jax.ShapeDtypeStruct((rows, D_MODEL), F32),
        compiler_params=pltpu.CompilerParams(
            dimension_semantics=("arbitrary",),
            vmem_limit_bytes=48 * 1024 * 1024),
    )(x2d, mod, p['npre_ffn'], p['wffn_in'], p['wffn_out'], p['npost_ffn'])


def _proj_sample_kernel(x_ref, mod_ref, npre_ref, w_ref, o_ref):
    m = mod_ref[...]
    sh1, sc1 = m[:, :D_MODEL], m[:, D_MODEL:2 * D_MODEL]
    h = (_rms(x_ref[...], npre_ref[...]) * (1.0 + sc1) + sh1).astype(BF16)
    o_ref[...] = _dot(h, _wb(w_ref))


def _proj_sample_call(x2d, mod_s, p):
    rows = x2d.shape[0]
    tn = D_MODEL
    return pl.pallas_call(
        _proj_sample_kernel,
        grid=(N_CAT // tn,),
        in_specs=[
            pl.BlockSpec((rows, D_MODEL), lambda j: (0, 0)),
            pl.BlockSpec((rows, 3 * D_MODEL), lambda j: (0, 0)),
            pl.BlockSpec((1, D_MODEL), lambda j: (0, 0)),
            pl.BlockSpec((D_MODEL, tn), lambda j: (0, j)),
        ],
        out_specs=pl.BlockSpec((rows, tn), lambda j: (0, j)),
        out_shape=jax.ShapeDtypeStruct((rows, N_CAT), F32),
        compiler_params=pltpu.CompilerParams(
            dimension_semantics=("arbitrary",),
            vmem_limit_bytes=48 * 1024 * 1024),
    )(x2d, mod_s, p['npre'], p['wcat'])


def _ret_sample_kernel(gpow_ref, qkv_ref, s_ref, rope_ref, dec_ref, qdec_ref, kdec_ref, o_ref, snew_ref):
    half = DK // 2
    cos, sin = rope_ref[:, 0:half], rope_ref[:, half:2 * half]
    cosk, sink = rope_ref[:, 2 * half:3 * half], rope_ref[:, 3 * half:4 * half]
    nseq = RET_SAMPLE_BB
    tlen = SUBLANES // nseq
    row = lax.broadcasted_iota(jnp.int32, (SUBLANES, LANES), 0)
    for hh in range(H_RET):
        q = _rope(qkv_ref[:, OFF_Q + hh * DK:OFF_Q + (hh + 1) * DK], cos, sin)
        k = _rope(qkv_ref[:, OFF_K + hh * DK:OFF_K + (hh + 1) * DK], cosk, sink)
        v = qkv_ref[:, OFF_V + hh * DV:OFF_V + (hh + 1) * DV]
        kdec = kdec_ref[hh]
        kd = k * jnp.concatenate([kdec, kdec], axis=1)
        scores = _dot_nt(q, k) * dec_ref[hh]
        intra = _dot(scores, v)
        qb = q.astype(BF16)
        cross = jnp.zeros((SUBLANES, DV), F32)
        for bi in range(nseq):
            s_old = s_ref[bi, hh]
            in_seq = (row >= bi * tlen) & (row < (bi + 1) * tlen)
            cr = _dot(qb, s_old.astype(BF16))
            cross = jnp.where(jnp.concatenate([in_seq] * (DV // LANES), axis=1), cr, cross)
            kd_b = jnp.where(jnp.concatenate([in_seq] * (DK // LANES), axis=1), kd, 0.0)
            snew_ref[bi, hh] = gpow_ref[hh] * s_old + _dot_tn(kd_b, v)
        qdec = qdec_ref[hh]
        o_ref[:, hh * DV:(hh + 1) * DV] = intra + cross * jnp.concatenate([qdec] * (DV // LANES), axis=1)


def _ret_sample_call(proj_s, state, gpow, rope8, dec8, qdec8, kdec8):
    nb = state.shape[0]
    bb = RET_SAMPLE_BB
    rows = proj_s.shape[0]
    qkv_w = OFF_G
    return pl.pallas_call(
        _ret_sample_kernel,
        grid=(nb // bb,),
        in_specs=[
            pl.BlockSpec(memory_space=pltpu.SMEM),
            pl.BlockSpec((SUBLANES, qkv_w), lambda i: (i, 0)),
            pl.BlockSpec((bb, H_RET, DK, DV), lambda i: (i, 0, 0, 0)),
            pl.BlockSpec((SUBLANES, 4 * (DK // 2)), lambda i: (0, 0)),
            pl.BlockSpec((H_RET, SUBLANES, SUBLANES), lambda i: (0, 0, 0)),
            pl.BlockSpec((H_RET, SUBLANES, LANES), lambda i: (0, 0, 0)),
            pl.BlockSpec((H_RET, SUBLANES, LANES), lambda i: (0, 0, 0)),
        ],
        out_specs=[
            pl.BlockSpec((SUBLANES, D_V), lambda i: (i, 0)),
            pl.BlockSpec((bb, H_RET, DK, DV), lambda i: (i, 0, 0, 0)),
        ],
        out_shape=[
            jax.ShapeDtypeStruct((rows, D_V), F32),
            jax.ShapeDtypeStruct(state.shape, F32),
        ],
        compiler_params=pltpu.CompilerParams(
            dimension_semantics=("arbitrary",),
            vmem_limit_bytes=40 * 1024 * 1024),
    )(gpow, proj_s, state, rope8, dec8, qdec8, kdec8)


def _mix_sample_kernel(x_ref, mod_ref, proj_ref, o_ref, h0_ref, xs1_ref, xs2_ref, xs3_ref,
                       bmg_ref, gnw_ref, wbr_ret_ref, convw_ref, convb_ref, wrg_ref, bra_ref, brx_ref,
                       lru_ref, wbr_rnn_ref, wout_ref, npost_ref,
                       x1_ref, hseq_ref,
                       xr_ref, sa_ref, sb_ref):
    tm = SAMPLE_TM
    tlen = 4
    x = x_ref[...]
    g1 = mod_ref[:, 2 * D_MODEL:]

    br_ret = jnp.zeros((tm, D_MODEL), F32)
    for hh in range(H_RET):
        on = _group_norm(o_ref[:, hh * DV:(hh + 1) * DV])
        g = proj_ref[:, OFF_G + hh * DV:OFF_G + (hh + 1) * DV]
        ry = (on * gnw_ref[:, hh * DV:(hh + 1) * DV] * _silu(g)).astype(BF16)
        br_ret = br_ret + _dot(ry, _wb(wbr_ret_ref, hh * DV, (hh + 1) * DV))

    tpos = lax.broadcasted_iota(jnp.int32, (tm, D_RNN), 0) & (tlen - 1)
    xr = proj_ref[:, OFF_XR:OFF_XR + D_RNN]
    xr_ref[0:SUBLANES, :] = jnp.zeros((SUBLANES, D_RNN), F32)
    xr_ref[SUBLANES:SUBLANES + tm, :] = xr
    cw = convw_ref[...]
    prev = (xs3_ref, xs2_ref, xs1_ref)
    xconv = convb_ref[...]
    for j in range(CONV_W - 1):
        sft = CONV_W - 1 - j
        shifted = jnp.where(tpos >= sft, xr_ref[SUBLANES - sft:SUBLANES - sft + tm, :], 0.0)
        xconv = xconv + (shifted + prev[j][...]) * cw[j:j + 1, :]
    xconv = xconv + xr * cw[CONV_W - 1:CONV_W, :]

    a, b = _lru_coeffs(xconv, _lru_gate_pre(xconv, wrg_ref), bra_ref[...], brx_ref[...], lru_ref[...])
    b = jnp.where(tpos == 0, b + a * h0_ref[...], b)
    sa_ref[0:SUBLANES, :] = jnp.zeros((SUBLANES, D_RNN), F32)
    sb_ref[0:SUBLANES, :] = jnp.zeros((SUBLANES, D_RNN), F32)
    for s in (1, 2):
        sa_ref[SUBLANES:SUBLANES + tm, :] = a
        sb_ref[SUBLANES:SUBLANES + tm, :] = b
        keep = tpos >= s
        ap = jnp.where(keep, sa_ref[SUBLANES - s:SUBLANES - s + tm, :], 1.0)
        bp = jnp.where(keep, sb_ref[SUBLANES - s:SUBLANES - s + tm, :], 0.0)
        b = a * bp + b
        a = a * ap
    hseq_ref[...] = b

    gr = proj_ref[:, OFF_GR:OFF_GR + D_RNN]
    rnn_y_b = (b * jax.nn.gelu(gr, approximate=True)).astype(BF16)
    gate_pre = proj_ref[:, OFF_MG:OFF_MG + 2 * D_MODEL]
    x1_ref[...] = _mix_tail(x, g1, br_ret, rnn_y_b, gate_pre, bmg_ref[...], wbr_rnn_ref, wout_ref,
                            npost_ref[...])


def _mix_sample_call(x2d, mod_s, proj_s, o_s, h0rep, xs1, xs2, xs3, p):
    rows = x2d.shape[0]
    tm = SAMPLE_TM
    row_spec = lambda w: pl.BlockSpec((tm, w), lambda i: (i, 0))
    return pl.pallas_call(
        _mix_sample_kernel,
        grid=(rows // tm,),
        in_specs=[
            row_spec(D_MODEL), row_spec(3 * D_MODEL), row_spec(N_CAT), row_spec(D_V),
            row_spec(D_RNN), row_spec(D_RNN), row_spec(D_RNN), row_spec(D_RNN),
            _const_spec((1, 2 * D_MODEL)),
            _const_spec((1, D_V)),
            _const_spec((D_V, D_MODEL)),
            _const_spec((CONV_W, D_RNN)),
            _const_spec((1, D_RNN)),
            _const_spec((N_RNN_BLOCKS, RNN_BLOCK, 2 * RNN_BLOCK)),
            _const_spec((1, D_RNN)),
            _const_spec((1, D_RNN)),
            _const_spec((1, D_RNN)),
            _const_spec((D_RNN, D_MODEL)),
            _const_spec((D_MODEL, D_MODEL)),
            _const_spec((1, D_MODEL)),
        ],
        out_specs=[row_spec(D_MODEL), row_spec(D_RNN)],
        out_shape=[
            jax.ShapeDtypeStruct((rows, D_MODEL), F32),
            jax.ShapeDtypeStruct((rows, D_RNN), F32),
        ],
        scratch_shapes=[
            pltpu.VMEM((tm + SUBLANES, D_RNN), F32),
            pltpu.VMEM((tm + SUBLANES, D_RNN), F32),
            pltpu.VMEM((tm + SUBLANES, D_RNN), F32),
        ],
        compiler_params=pltpu.CompilerParams(
            dimension_semantics=("arbitrary",),
            vmem_limit_bytes=56 * 1024 * 1024),
    )(x2d, mod_s, proj_s, o_s, h0rep, xs1, xs2, xs3, p['bmg'], p['gnw'], p['wbr_ret'], p['convw'],
      p['convb'], p['wrg'], p['bra'], p['brx'], p['lru'], p['wbr_rnn'], p['wout'], p['npost'])


def _rope_table(pos):
    half = DK // 2
    inv = ROPE_BASE ** (-jnp.arange(half, dtype=F32) / half)
    ang = pos.astype(F32)[:, None] * inv[None, :]
    cos, sin = jnp.cos(ang), jnp.sin(ang)
    ks = DK ** -0.5
    return jnp.concatenate([cos, sin, cos * ks, sin * ks], axis=1)


def _decay_tables(tpos, same_seq, chunk):
    lg = jnp.log(1.0 - 2.0 ** (-5.0 - jnp.arange(H_RET, dtype=F32)))
    idx = tpos.astype(F32)
    diff = idx[:, None] - idx[None, :]
    causal = (diff >= 0) & same_seq
    dec = jnp.where(causal[None], jnp.exp(jnp.where(causal, diff, 0.0)[None] * lg[:, None, None]), 0.0)
    qdec = jnp.exp((idx + 1.0)[None, :] * lg[:, None])
    kdec = jnp.exp((chunk - 1.0 - idx)[None, :] * lg[:, None])
    rep = lambda a: jnp.broadcast_to(a[:, :, None], a.shape + (LANES,))
    gpow = jnp.exp(chunk * lg)
    return dec, rep(qdec), rep(kdec), gpow


def kernel(x_prompt, x_sample, state_ret, state_rnn_h, state_rnn_conv, c_prompt, c_sample,
           w_ada, b_ada, norm_pre_mix, norm_post_mix, norm_pre_ffn, norm_post_ffn,
           w_in, ret_gn_w, w_br_ret, conv_w, conv_b, w_rg_a, b_rg_a, w_rg_x, b_rg_x,
           lru_param, w_br_rnn, w_mgate, b_mgate, w_out, w_ffn_in, w_ffn_out):
    depth = w_in.shape[0]
    assert depth == 1, "single layer step"
    nb, seq, _ = x_prompt.shape
    nsb, sseq, _ = x_sample.shape
    assert seq % PROMPT_TM == 0 and sseq * RET_SAMPLE_BB == SUBLANES and sseq == CONV_W
    l = 0
    row = lambda a: a[l][None, :]
    p = dict(
        npre=row(norm_pre_mix), npost=row(norm_post_mix), npre_ffn=row(norm_pre_ffn), npost_ffn=row(norm_post_ffn),
        wcat=_pack_rows(jnp.concatenate([w_in[l], w_mgate[l]], axis=1)),
        bmg=row(b_mgate), gnw=row(ret_gn_w), wbr_ret=_pack_rows(w_br_ret[l]),
        convw=conv_w[l], convb=row(conv_b),
        wrg=_pack_rows(jnp.concatenate([w_rg_a[l], w_rg_x[l]], axis=2)),
        bra=row(b_rg_a), brx=row(b_rg_x), lru=row(lru_param),
        wbr_rnn=_pack_rows(w_br_rnn[l]), wout=_pack_rows(w_out[l]),
        wffn_in=_pack_rows(w_ffn_in[l]), wffn_out=_pack_rows(w_ffn_out[l]),
    )

    rows_s = nsb * sseq
    c_all = jnp.concatenate([jnp.repeat(c_sample, sseq, axis=0), c_prompt], axis=0)
    mod_all = _mod_call(c_all, _pack_rows(w_ada[l]), row(b_ada))
    mod_p = mod_all[rows_s:].reshape(nb, 1, 6 * D_MODEL)
    mod_s = mod_all

    tm = PROMPT_TM
    ng = tm // SUBLANES
    r = jnp.arange(tm)
    tpos = (r % SUBLANES) * ng + r // SUBLANES
    pos_p = (jnp.arange(seq // tm)[:, None] * tm + tpos[None, :]).reshape(seq).astype(jnp.int32)
    rope_p = _rope_table(pos_p)
    dec, qdec, kdec, gpow = _decay_tables(tpos, jnp.ones((tm, tm), bool), float(tm))
    interleave = lambda a: a.reshape(nb, seq // tm, SUBLANES, ng, D_MODEL).swapaxes(2, 3).reshape(nb, seq, D_MODEL)
    restore = lambda a: a.reshape(nb, seq // tm, ng, SUBLANES, D_MODEL).swapaxes(2, 3).reshape(nb, seq, D_MODEL)
    x1p, ret_p, hlast_p, conv_p = _mix_prompt_call(interleave(x_prompt), mod_p, gpow, rope_p, dec, qdec, kdec, p)
    yp = restore(_ffn_call(x1p.reshape(nb * seq, D_MODEL), mod_p, 1, seq // FFN_TM, p).reshape(nb, seq, D_MODEL))

    xs2d = x_sample.reshape(rows_s, D_MODEL)
    proj_s = _proj_sample_call(xs2d, mod_s, p)
    r8 = jnp.arange(SUBLANES)
    rope_s = _rope_table(PAST_LEN + (r8 % sseq).astype(jnp.int32))
    same = (r8[:, None] // sseq) == (r8[None, :] // sseq)
    dec8, qdec8, kdec8, gpow_s = _decay_tables(r8 % sseq, same, float(sseq))
    o_s, ret_s = _ret_sample_call(proj_s, state_ret[l], gpow_s, rope_s, dec8, qdec8, kdec8)
    cs = state_rnn_conv[l]
    pad_rows = lambda a: jnp.pad(a, ((0, 0), (0, sseq - a.shape[1]), (0, 0))).reshape(rows_s, D_RNN)
    xs1, xs2, xs3 = pad_rows(cs[:, 2:3]), pad_rows(cs[:, 1:3]), pad_rows(cs[:, 0:3])
    h0rep = jnp.repeat(state_rnn_h[l], sseq, axis=0)
    x1s, hseq_s = _mix_sample_call(xs2d, mod_s, proj_s, o_s, h0rep, xs1, xs2, xs3, p)
    ys = _ffn_call(x1s, mod_s, FFN_TM, 1, p).reshape(nsb, sseq, D_MODEL)
    hlast_s = hseq_s.reshape(nsb, sseq, D_RNN)[:, sseq - 1]
    conv_s = proj_s[:, OFF_XR:OFF_XR + D_RNN].reshape(nsb, sseq, D_RNN)[:, sseq - (CONV_W - 1):]

    return (yp, ys, ret_p[None], ret_s[None], hlast_p.reshape(nb, D_RNN)[None], hlast_s[None],
            conv_p[None], conv_s[None])
```

```python
import functools

import jax
import jax.numpy as jnp
from jax import lax
from jax.experimental import pallas as pl
from jax.experimental.pallas import tpu as pltpu

F32 = jnp.float32
BF16 = jnp.bfloat16

D_MODEL = 1024
H_RET = 4
DK = D_MODEL // H_RET
DV = 2 * DK
D_QK = H_RET * DK
D_V = H_RET * DV
D_RNN = 1536
RNN_BLOCK = 128
N_RNN_BLOCKS = D_RNN // RNN_BLOCK
CONV_W = 4
LRU_C = 8.0
D_FF = 2816
ROPE_BASE = 10000.0
GN_EPS = 1e-5
RMS_EPS = 1e-6
PAST_LEN = 16384

OFF_Q = 0
OFF_K = OFF_Q + D_QK
OFF_V = OFF_K + D_QK
OFF_G = OFF_V + D_V
OFF_XR = OFF_G + D_V
OFF_GR = OFF_XR + D_RNN
OFF_MG = OFF_GR + D_RNN
N_CAT = OFF_MG + 2 * D_MODEL

SUBLANES = 8
LANES = 128
MXU_DIM = 256
VMEM_BYTES_V7X = 64 * 1024 * 1024

PROMPT_TM = 256
FFN_TM = 512
FF_CHUNK = MXU_DIM
SAMPLE_TM = 128
RET_SAMPLE_BB = 2


def _dot(a, b):
    return jnp.dot(a, b, preferred_element_type=F32)


def _dot_nt(a, b):
    return lax.dot_general(a, b, (((1,), (1,)), ((), ())), preferred_element_type=F32)


def _dot_tn(a, b):
    return lax.dot_general(a, b, (((0,), (0,)), ((), ())), preferred_element_type=F32)


def _wb(ref, k0=None, k1=None, c0=None, c1=None):
    rs = slice(None) if k0 is None else slice(k0, k1)
    cs = slice(None) if c0 is None else slice(c0, c1)
    return ref[rs, cs]


def _pack_rows(w):
    return w.astype(BF16)


def _rms(x, w):
    ms = jnp.mean(x * x, axis=-1, keepdims=True)
    return x * lax.rsqrt(ms + RMS_EPS) * w


def _silu(x):
    return x * jax.nn.sigmoid(x)


def _rope(x, cos, sin):
    half = DK // 2
    x1, x2 = x[:, :half], x[:, half:]
    return jnp.concatenate([x1 * cos - x2 * sin, x1 * sin + x2 * cos], axis=1)


def _group_norm(o):
    mu = jnp.mean(o, axis=-1, keepdims=True)
    d = o - mu
    var = jnp.mean(d * d, axis=-1, keepdims=True)
    return d * lax.rsqrt(var + GN_EPS)


def _lru_gate_pre(xconv, wrg_ref):
    xcb = xconv.astype(BF16)
    return [_dot(xcb[:, n * RNN_BLOCK:(n + 1) * RNN_BLOCK], wrg_ref[n]) for n in range(N_RNN_BLOCKS)]


def _lru_coeffs(xconv, pre, b_a, b_x, lru):
    ra = jnp.concatenate([p[:, :RNN_BLOCK] for p in pre], axis=1) + b_a
    ri = jnp.concatenate([p[:, RNN_BLOCK:] for p in pre], axis=1) + b_x
    r = jax.nn.sigmoid(ra)
    i = jax.nn.sigmoid(ri)
    z = -lru
    sp = jnp.maximum(z, 0.0) + jnp.log(1.0 + jnp.exp(-jnp.abs(z)))
    log_a = -LRU_C * r * sp
    a = jnp.exp(log_a)
    beta = jnp.sqrt(-jnp.tanh(log_a) * (jnp.exp(2.0 * log_a) + 1.0))
    return a, beta * (i * xconv)


def _mix_tail(x, g1, br_ret, rnn_y_b, gate_pre, b_mg, wbr_rnn_ref, wout_ref, npost):
    br_rnn = _dot(rnn_y_b, _wb(wbr_rnn_ref))
    gates = jax.nn.sigmoid(gate_pre + b_mg)
    ga, gb = gates[:, :D_MODEL], gates[:, D_MODEL:]
    mixed = _dot((ga * br_ret + gb * br_rnn).astype(BF16), _wb(wout_ref))
    return x + g1 * _rms(mixed, npost)


def _mod_kernel(c_ref, w_ref, b_ref, o_ref):
    a = _silu(c_ref[...]).astype(BF16)
    o_ref[...] = _dot(a, _wb(w_ref)) + b_ref[...]


def _mod_call(c_all, w_ada_b, b_ada):
    rows = c_all.shape[0]
    tn = D_MODEL
    return pl.pallas_call(
        _mod_kernel,
        grid=(6 * D_MODEL // tn,),
        in_specs=[
            pl.BlockSpec((rows, D_MODEL), lambda j: (0, 0)),
            pl.BlockSpec((D_MODEL, tn), lambda j: (0, j)),
            pl.BlockSpec((1, tn), lambda j: (0, j)),
        ],
        out_specs=pl.BlockSpec((rows, tn), lambda j: (0, j)),
        out_shape=jax.ShapeDtypeStruct((rows, 6 * D_MODEL), F32),
        compiler_params=pltpu.CompilerParams(dimension_semantics=("arbitrary",)),
    )(c_all, w_ada_b, b_ada)


def _mix_prompt_kernel(gpow_ref, x_ref, mod_ref, npre_ref, wcat_ref, bmg_ref, rope_ref,
                       dec_ref, qdec_ref, kdec_ref, gnw_ref, wbr_ret_ref, convw_ref, convb_ref,
                       wrg_ref, bra_ref, brx_ref, lru_ref, wbr_rnn_ref, wout_ref, npost_ref,
                       x1_ref, s_ref, hlast_ref, convnew_ref,
                       hb_ref, xr_ref, prevg_ref, hc_ref):
    tm = PROMPT_TM
    ng = tm // SUBLANES
    halo = (CONV_W - 1) * SUBLANES
    t = pl.program_id(1)

    @pl.when(t == 0)
    def _():
        s_ref[...] = jnp.zeros_like(s_ref)
        prevg_ref[...] = jnp.zeros_like(prevg_ref)
        hc_ref[...] = jnp.zeros_like(hc_ref)

    x = x_ref[...]
    m = mod_ref[...]
    sh1, sc1, g1 = m[:, :D_MODEL], m[:, D_MODEL:2 * D_MODEL], m[:, 2 * D_MODEL:]
    hb_ref[...] = (_rms(x, npre_ref[...]) * (1.0 + sc1) + sh1).astype(BF16)

    hb = hb_ref[...]
    sub = lax.broadcasted_iota(jnp.int32, (SUBLANES, D_RNN), 0)
    half = DK // 2
    cos, sin = rope_ref[:, 0:half], rope_ref[:, half:2 * half]
    cosk, sink = rope_ref[:, 2 * half:3 * half], rope_ref[:, 3 * half:4 * half]
    st = {}

    def xr_proj():
        xr = _dot(hb, _wb(wcat_ref, c0=OFF_XR, c1=OFF_XR + D_RNN))
        xr_ref[halo:halo + tm, :] = xr
        for kk in range(1, CONV_W):
            r0 = (CONV_W - 1 - kk) * SUBLANES
            cur = xr[(ng - kk) * SUBLANES:(ng - kk + 1) * SUBLANES, :]
            prv = prevg_ref[r0:r0 + SUBLANES, :]
            xr_ref[r0:r0 + SUBLANES, :] = pltpu.roll(jnp.where(sub == SUBLANES - 1, prv, cur), 1, 0)
        prevg_ref[...] = xr[tm - halo:, :]
        for kk in range(1, CONV_W):
            r1 = (ng - kk) * SUBLANES + SUBLANES - 1
            convnew_ref[CONV_W - 1 - kk:CONV_W - kk, :] = xr[r1:r1 + 1, :]

    def lru_conv():
        cw = convw_ref[...]
        xconv = convb_ref[...]
        for j in range(CONV_W):
            r0 = halo - (CONV_W - 1 - j) * SUBLANES
            xconv = xconv + xr_ref[r0:r0 + tm, :] * cw[j:j + 1, :]
        st['xconv'] = xconv

    def lru_gate_proj():
        st['gpre'] = _lru_gate_pre(st['xconv'], wrg_ref)

    def lru_coef():
        st['a'], st['b'] = _lru_coeffs(st.pop('xconv'), st.pop('gpre'), bra_ref[...], brx_ref[...], lru_ref[...])

    def lru_scan():
        a, b = st['a'], st['b']
        ca, cb = a[0:SUBLANES, :], b[0:SUBLANES, :]
        cas, cbs = [ca], [cb]
        for gi in range(1, ng):
            ag = a[gi * SUBLANES:(gi + 1) * SUBLANES, :]
            cb = ag * cb + b[gi * SUBLANES:(gi + 1) * SUBLANES, :]
            ca = ag * ca
            cas.append(ca)
            cbs.append(cb)
        cin = jnp.where(sub == 0, hc_ref[SUBLANES - 1:SUBLANES, :], 0.0)
        for s in range(SUBLANES - 1):
            cin = jnp.where(sub == s + 1, pltpu.roll(ca * cin + cb, 1, 0), cin)
        seg_end = ca * cin + cb
        hc_ref[...] = seg_end
        hlast_ref[...] = seg_end[SUBLANES - 1:SUBLANES, :]
        st['hseq'] = jnp.concatenate([cas[gi] * cin + cbs[gi] for gi in range(ng)], axis=0)

    def gr_proj():
        st['gr'] = _dot(hb, _wb(wcat_ref, c0=OFF_GR, c1=OFF_GR + D_RNN))

    def lru_y():
        st['rnn_y'] = (st.pop('hseq') * jax.nn.gelu(st.pop('gr'), approximate=True)).astype(BF16)

    def lru_out():
        st['br_rnn'] = _dot(st.pop('rnn_y'), _wb(wbr_rnn_ref))

    def gate_proj():
        st['gate_pre'] = _dot(hb, _wb(wcat_ref, c0=OFF_MG, c1=OFF_MG + 2 * D_MODEL))

    def gate_act():
        st['gates'] = jax.nn.sigmoid(st.pop('gate_pre') + bmg_ref[...])

    def head_proj(hh):
        q = _dot(hb, _wb(wcat_ref, c0=OFF_Q + hh * DK, c1=OFF_Q + (hh + 1) * DK))
        k = _dot(hb, _wb(wcat_ref, c0=OFF_K + hh * DK, c1=OFF_K + (hh + 1) * DK))
        vb = _dot(hb, _wb(wcat_ref, c0=OFF_V + hh * DV, c1=OFF_V + (hh + 1) * DV)).astype(BF16)
        g = _dot(hb, _wb(wcat_ref, c0=OFF_G + hh * DV, c1=OFF_G + (hh + 1) * DV))
        st['proj', hh] = (q, k, vb, g)

    def head_rope(hh):
        q, k, vb, g = st.pop(('proj', hh))
        kr = _rope(k, cosk, sink)
        kdec = kdec_ref[hh]
        kdb = (kr * jnp.concatenate([kdec, kdec], axis=1)).astype(BF16)
        st['rope', hh] = (_rope(q, cos, sin).astype(BF16), kr.astype(BF16), kdb, vb, g)

    def head_qk(hh):
        qb, kb, kdb, vb, g = st.pop(('rope', hh))
        scores = _dot_nt(qb, kb)
        cross = _dot(qb, s_ref[hh].astype(BF16))
        st['qk', hh] = (scores, cross, vb, g)
        st['kv', hh] = (kdb, vb)

    def head_state(hh):
        kdb, vb = st.pop(('kv', hh))
        s_ref[hh] = gpow_ref[hh] * s_ref[hh] + _dot_tn(kdb, vb)

    def head_decay(hh):
        scores, cross, vb, g = st.pop(('qk', hh))
        st['dec', hh] = ((scores * dec_ref[hh]).astype(BF16), cross, vb, g)

    def head_pv(hh):
        sb, cross, vb, g = st.pop(('dec', hh))
        st['pv', hh] = (_dot(sb, vb), cross, g)

    def head_norm(hh):
        intra, cross, g = st.pop(('pv', hh))
        qdec = qdec_ref[hh]
        on = _group_norm(intra + cross * jnp.concatenate([qdec] * (DV // LANES), axis=1))
        st['ry', hh] = (on * gnw_ref[:, hh * DV:(hh + 1) * DV] * _silu(g)).astype(BF16)

    def head_out(hh):
        part = _dot(st.pop(('ry', hh)), _wb(wbr_ret_ref, hh * DV, (hh + 1) * DV))
        st['br_ret'] = part if hh == 0 else st['br_ret'] + part

    heads = range(H_RET)
    order = (
        [xr_proj] + [(head_proj, h) for h in heads] + [lru_conv, lru_gate_proj]
        + [(head_rope, h) for h in heads] + [gr_proj, lru_coef] + [(head_qk, h) for h in heads]
        + [gate_proj] + [(head_decay, h) for h in heads] + [lru_scan]
        + [(head_pv, h) for h in heads] + [(head_state, h) for h in heads] + [lru_y, lru_out]
        + [(head_norm, h) for h in heads] + [(head_out, h) for h in heads] + [gate_act]
    )
    for stage in order:
        if isinstance(stage, tuple):
            stage[0](stage[1])
        else:
            stage()

    gates = st['gates']
    ga, gb = gates[:, :D_MODEL], gates[:, D_MODEL:]
    mixed = _dot((ga * st['br_ret'] + gb * st['br_rnn']).astype(BF16), _wb(wout_ref))
    x1_ref[...] = x + g1 * _rms(mixed, npost_ref[...])


def _const_spec(shape):
    nd = len(shape)
    return pl.BlockSpec(shape, lambda *_: (0,) * nd, pipeline_mode=pl.Buffered(1))


def _mix_prompt_call(x, mod3, gpow, rope_tab, dec, qdec, kdec, p):
    nb, seq, _ = x.shape
    tm = PROMPT_TM
    nt = seq // tm
    in_specs = [
        pl.BlockSpec(memory_space=pltpu.SMEM),
        pl.BlockSpec((None, tm, D_MODEL), lambda b, t: (b, t, 0)),
        pl.BlockSpec((None, 1, 3 * D_MODEL), lambda b, t: (b, 0, 0)),
        _const_spec((1, D_MODEL)),
        _const_spec((D_MODEL, N_CAT)),
        _const_spec((1, 2 * D_MODEL)),
        pl.BlockSpec((tm, 4 * (DK // 2)), lambda b, t: (t, 0)),
        _const_spec((H_RET, tm, tm)),
        _const_spec((H_RET, tm, LANES)),
        _const_spec((H_RET, tm, LANES)),
        _const_spec((1, D_V)),
        _const_spec((D_V, D_MODEL)),
        _const_spec((CONV_W, D_RNN)),
        _const_spec((1, D_RNN)),
        _const_spec((N_RNN_BLOCKS, RNN_BLOCK, 2 * RNN_BLOCK)),
        _const_spec((1, D_RNN)),
        _const_spec((1, D_RNN)),
        _const_spec((1, D_RNN)),
        _const_spec((D_RNN, D_MODEL)),
        _const_spec((D_MODEL, D_MODEL)),
        _const_spec((1, D_MODEL)),
    ]
    out_specs = [
        pl.BlockSpec((None, tm, D_MODEL), lambda b, t: (b, t, 0)),
        pl.BlockSpec((None, H_RET, DK, DV), lambda b, t: (b, 0, 0, 0)),
        pl.BlockSpec((None, 1, D_RNN), lambda b, t: (b, 0, 0)),
        pl.BlockSpec((None, CONV_W - 1, D_RNN), lambda b, t: (b, 0, 0)),
    ]
    out_shape = [
        jax.ShapeDtypeStruct((nb, seq, D_MODEL), F32),
        jax.ShapeDtypeStruct((nb, H_RET, DK, DV), F32),
        jax.ShapeDtypeStruct((nb, 1, D_RNN), F32),
        jax.ShapeDtypeStruct((nb, CONV_W - 1, D_RNN), F32),
    ]
    halo = (CONV_W - 1) * SUBLANES
    scratch = [
        pltpu.VMEM((tm, D_MODEL), BF16),
        pltpu.VMEM((halo + tm, D_RNN), F32),
        pltpu.VMEM((halo, D_RNN), F32),
        pltpu.VMEM((SUBLANES, D_RNN), F32),
    ]
    return pl.pallas_call(
        _mix_prompt_kernel,
        grid=(nb, nt),
        in_specs=in_specs,
        out_specs=out_specs,
        out_shape=out_shape,
        scratch_shapes=scratch,
        compiler_params=pltpu.CompilerParams(
            dimension_semantics=("arbitrary", "arbitrary"),
            vmem_limit_bytes=VMEM_BYTES_V7X - 4 * 1024 * 1024),
    )(gpow, x, mod3, p['npre'], p['wcat'], p['bmg'], rope_tab, dec, qdec, kdec, p['gnw'], p['wbr_ret'],
      p['convw'], p['convb'], p['wrg'], p['bra'], p['brx'], p['lru'], p['wbr_rnn'], p['wout'], p['npost'])


def _ffn_kernel(x_ref, mod_ref, npre_ref, w1_ref, w2_ref, npost_ref, o_ref):
    x = x_ref[...]
    m = mod_ref[...]
    sh2, sc2, g2 = m[:, :D_MODEL], m[:, D_MODEL:2 * D_MODEL], m[:, 2 * D_MODEL:]
    h2 = (_rms(x, npre_ref[...]) * (1.0 + sc2) + sh2).astype(BF16)
    acc = jnp.zeros(x.shape, F32)
    for j in range(D_FF // FF_CHUNK):
        c0 = j * FF_CHUNK
        fg = _dot(h2, _wb(w1_ref, c0=c0, c1=c0 + FF_CHUNK))
        fu = _dot(h2, _wb(w1_ref, c0=D_FF + c0, c1=D_FF + c0 + FF_CHUNK))
        acc = acc + _dot((_silu(fg) * fu).astype(BF16), _wb(w2_ref, c0, c0 + FF_CHUNK))
    o_ref[...] = x + g2 * _rms(acc, npost_ref[...])


def _ffn_call(x2d, mod, mod_rows_per_tile, seq_tiles, p):
    rows = x2d.shape[0]
    tm = FFN_TM
    if mod_rows_per_tile == 1:
        mod_spec = pl.BlockSpec((None, 1, 3 * D_MODEL), lambda i: (i // seq_tiles, 0, 1))
    else:
        mod_spec = pl.BlockSpec((tm, 3 * D_MODEL), lambda i: (i, 1))
    return pl.pallas_call(
        _ffn_kernel,
        grid=(rows // tm,),
        in_specs=[
            pl.BlockSpec((tm, D_MODEL), lambda i: (i, 0)),
            mod_spec,
            _const_spec((1, D_MODEL)),
            _const_spec((D_MODEL, 2 * D_FF)),
            _const_spec((D_FF, D_MODEL)),
            _const_spec((1, D_MODEL)),
        ],
        out_specs=pl.BlockSpec((tm, D_MODEL), lambda i: (i, 0)),
        out_shape=jax.ShapeDtypeStruct((rows, D_MODEL), F32),
        compiler_params=pltpu.CompilerParams(
            dimension_semantics=("arbitrary",),
            vmem_limit_bytes=48 * 1024 * 1024),
    )(x2d, mod, p['npre_ffn'], p['wffn_in'], p['wffn_out'], p['npost_ffn'])


def _proj_sample_kernel(x_ref, mod_ref, npre_ref, w_ref, o_ref):
    m = mod_ref[...]
    sh1, sc1 = m[:, :D_MODEL], m[:, D_MODEL:2 * D_MODEL]
    h = (_rms(x_ref[...], npre_ref[...]) * (1.0 + sc1) + sh1).astype(BF16)
    o_ref[...] = _dot(h, _wb(w_ref))


def _proj_sample_call(x2d, mod_s, p):
    rows = x2d.shape[0]
    tn = D_MODEL
    return pl.pallas_call(
        _proj_sample_kernel,
        grid=(N_CAT // tn,),
        in_specs=[
            pl.BlockSpec((rows, D_MODEL), lambda j: (0, 0)),
            pl.BlockSpec((rows, 3 * D_MODEL), lambda j: (0, 0)),
            pl.BlockSpec((1, D_MODEL), lambda j: (0, 0)),
            pl.BlockSpec((D_MODEL, tn), lambda j: (0, j)),
        ],
        out_specs=pl.BlockSpec((rows, tn), lambda j: (0, j)),
        out_shape=jax.ShapeDtypeStruct((rows, N_CAT), F32),
        compiler_params=pltpu.CompilerParams(
            dimension_semantics=("arbitrary",),
            vmem_limit_bytes=48 * 1024 * 1024),
    )(x2d, mod_s, p['npre'], p['wcat'])


def _ret_sample_kernel(gpow_ref, qkv_ref, s_ref, rope_ref, dec_ref, qdec_ref, kdec_ref, o_ref, snew_ref):
    half = DK // 2
    cos, sin = rope_ref[:, 0:half], rope_ref[:, half:2 * half]
    cosk, sink = rope_ref[:, 2 * half:3 * half], rope_ref[:, 3 * half:4 * half]
    nseq = RET_SAMPLE_BB
    tlen = SUBLANES // nseq
    row = lax.broadcasted_iota(jnp.int32, (SUBLANES, LANES), 0)
    for hh in range(H_RET):
        q = _rope(qkv_ref[:, OFF_Q + hh * DK:OFF_Q + (hh + 1) * DK], cos, sin)
        k = _rope(qkv_ref[:, OFF_K + hh * DK:OFF_K + (hh + 1) * DK], cosk, sink)
        v = qkv_ref[:, OFF_V + hh * DV:OFF_V + (hh + 1) * DV]
        kdec = kdec_ref[hh]
        kd = k * jnp.concatenate([kdec, kdec], axis=1)
        scores = _dot_nt(q, k) * dec_ref[hh]
        intra = _dot(scores, v)
        qb = q.astype(BF16)
        cross = jnp.zeros((SUBLANES, DV), F32)
        for bi in range(nseq):
            s_old = s_ref[bi, hh]
            in_seq = (row >= bi * tlen) & (row < (bi + 1) * tlen)
            cr = _dot(qb, s_old.astype(BF16))
            cross = jnp.where(jnp.concatenate([in_seq] * (DV // LANES), axis=1), cr, cross)
            kd_b = jnp.where(jnp.concatenate([in_seq] * (DK // LANES), axis=1), kd, 0.0)
            snew_ref[bi, hh] = gpow_ref[hh] * s_old + _dot_tn(kd_b, v)
        qdec = qdec_ref[hh]
        o_ref[:, hh * DV:(hh + 1) * DV] = intra + cross * jnp.concatenate([qdec] * (DV // LANES), axis=1)


def _ffn_ret_kernel(gpow_ref, x_ref, mod_ref, npre_ref, w1_ref, w2_ref, npost_ref,
                    qkv_ref, s_ref, rope_ref, dec_ref, qdec_ref, kdec_ref,
                    y_ref, o_ref, snew_ref):
    @pl.when(pl.program_id(1) == 0)
    def _():
        _ffn_kernel(x_ref, mod_ref, npre_ref, w1_ref, w2_ref, npost_ref, y_ref)

    _ret_sample_kernel(gpow_ref, qkv_ref, s_ref, rope_ref, dec_ref, qdec_ref, kdec_ref, o_ref, snew_ref)


def _ffn_ret_call(x2d, mod_p, seq_tiles, proj_s, state, gpow, rope8, dec8, qdec8, kdec8, p):
    rows = x2d.shape[0]
    tm = FFN_TM
    nb = state.shape[0]
    bb = RET_SAMPLE_BB
    sub_steps = nb // bb // (rows // tm)
    assert sub_steps * (rows // tm) * bb == nb
    srow = lambda i, j: i * sub_steps + j
    return pl.pallas_call(
        _ffn_ret_kernel,
        grid=(rows // tm, sub_steps),
        in_specs=[
            pl.BlockSpec(memory_space=pltpu.SMEM),
            pl.BlockSpec((tm, D_MODEL), lambda i, j: (i, 0)),
            pl.BlockSpec((None, 1, 3 * D_MODEL), lambda i, j: (i // seq_tiles, 0, 1)),
            _const_spec((1, D_MODEL)),
            _const_spec((D_MODEL, 2 * D_FF)),
            _const_spec((D_FF, D_MODEL)),
            _const_spec((1, D_MODEL)),
            pl.BlockSpec((SUBLANES, OFF_G), lambda i, j: (srow(i, j), 0)),
            pl.BlockSpec((bb, H_RET, DK, DV), lambda i, j: (srow(i, j), 0, 0, 0)),
            _const_spec((SUBLANES, 4 * (DK // 2))),
            _const_spec((H_RET, SUBLANES, SUBLANES)),
            _const_spec((H_RET, SUBLANES, LANES)),
            _const_spec((H_RET, SUBLANES, LANES)),
        ],
        out_specs=[
            pl.BlockSpec((tm, D_MODEL), lambda i, j: (i, 0)),
            pl.BlockSpec((SUBLANES, D_V), lambda i, j: (srow(i, j), 0)),
            pl.BlockSpec((bb, H_RET, DK, DV), lambda i, j: (srow(i, j), 0, 0, 0)),
        ],
        out_shape=[
            jax.ShapeDtypeStruct((rows, D_MODEL), F32),
            jax.ShapeDtypeStruct((proj_s.shape[0], D_V), F32),
            jax.ShapeDtypeStruct(state.shape, F32),
        ],
        compiler_params=pltpu.CompilerParams(
            dimension_semantics=("arbitrary", "arbitrary"),
            vmem_limit_bytes=56 * 1024 * 1024),
    )(gpow, x2d, mod_p, p['npre_ffn'], p['wffn_in'], p['wffn_out'], p['npost_ffn'],
      proj_s, state, rope8, dec8, qdec8, kdec8)


def _mix_sample_kernel(x_ref, mod_ref, proj_ref, o_ref, h0_ref, xs1_ref, xs2_ref, xs3_ref,
                       bmg_ref, gnw_ref, wbr_ret_ref, convw_ref, convb_ref, wrg_ref, bra_ref, brx_ref,
                       lru_ref, wbr_rnn_ref, wout_ref, npost_ref,
                       x1_ref, hseq_ref,
                       xr_ref, sa_ref, sb_ref):
    tm = SAMPLE_TM
    tlen = 4
    x = x_ref[...]
    g1 = mod_ref[:, 2 * D_MODEL:]

    br_ret = jnp.zeros((tm, D_MODEL), F32)
    for hh in range(H_RET):
        on = _group_norm(o_ref[:, hh * DV:(hh + 1) * DV])
        g = proj_ref[:, OFF_G + hh * DV:OFF_G + (hh + 1) * DV]
        ry = (on * gnw_ref[:, hh * DV:(hh + 1) * DV] * _silu(g)).astype(BF16)
        br_ret = br_ret + _dot(ry, _wb(wbr_ret_ref, hh * DV, (hh + 1) * DV))

    tpos = lax.broadcasted_iota(jnp.int32, (tm, D_RNN), 0) & (tlen - 1)
    xr = proj_ref[:, OFF_XR:OFF_XR + D_RNN]
    xr_ref[0:SUBLANES, :] = jnp.zeros((SUBLANES, D_RNN), F32)
    xr_ref[SUBLANES:SUBLANES + tm, :] = xr
    cw = convw_ref[...]
    prev = (xs3_ref, xs2_ref, xs1_ref)
    xconv = convb_ref[...]
    for j in range(CONV_W - 1):
        sft = CONV_W - 1 - j
        shifted = jnp.where(tpos >= sft, xr_ref[SUBLANES - sft:SUBLANES - sft + tm, :], 0.0)
        xconv = xconv + (shifted + prev[j][...]) * cw[j:j + 1, :]
    xconv = xconv + xr * cw[CONV_W - 1:CONV_W, :]

    a, b = _lru_coeffs(xconv, _lru_gate_pre(xconv, wrg_ref), bra_ref[...], brx_ref[...], lru_ref[...])
    b = jnp.where(tpos == 0, b + a * h0_ref[...], b)
    sa_ref[0:SUBLANES, :] = jnp.zeros((SUBLANES, D_RNN), F32)
    sb_ref[0:SUBLANES, :] = jnp.zeros((SUBLANES, D_RNN), F32)
    for s in (1, 2):
        sa_ref[SUBLANES:SUBLANES + tm, :] = a
        sb_ref[SUBLANES:SUBLANES + tm, :] = b
        keep = tpos >= s
        ap = jnp.where(keep, sa_ref[SUBLANES - s:SUBLANES - s + tm, :], 1.0)
        bp = jnp.where(keep, sb_ref[SUBLANES - s:SUBLANES - s + tm, :], 0.0)
        b = a * bp + b
        a = a * ap
    hseq_ref[...] = b

    gr = proj_ref[:, OFF_GR:OFF_GR + D_RNN]
    rnn_y_b = (b * jax.nn.gelu(gr, approximate=True)).astype(BF16)
    gate_pre = proj_ref[:, OFF_MG:OFF_MG + 2 * D_MODEL]
    x1_ref[...] = _mix_tail(x, g1, br_ret, rnn_y_b, gate_pre, bmg_ref[...], wbr_rnn_ref, wout_ref,
                            npost_ref[...])


def _mix_sample_call(x2d, mod_s, proj_s, o_s, h0rep, xs1, xs2, xs3, p):
    rows = x2d.shape[0]
    tm = SAMPLE_TM
    row_spec = lambda w: pl.BlockSpec((tm, w), lambda i: (i, 0))
    return pl.pallas_call(
        _mix_sample_kernel,
        grid=(rows // tm,),
        in_specs=[
            row_spec(D_MODEL), row_spec(3 * D_MODEL), row_spec(N_CAT), row_spec(D_V),
            row_spec(D_RNN), row_spec(D_RNN), row_spec(D_RNN), row_spec(D_RNN),
            _const_spec((1, 2 * D_MODEL)),
            _const_spec((1, D_V)),
            _const_spec((D_V, D_MODEL)),
            _const_spec((CONV_W, D_RNN)),
            _const_spec((1, D_RNN)),
            _const_spec((N_RNN_BLOCKS, RNN_BLOCK, 2 * RNN_BLOCK)),
            _const_spec((1, D_RNN)),
            _const_spec((1, D_RNN)),
            _const_spec((1, D_RNN)),
            _const_spec((D_RNN, D_MODEL)),
            _const_spec((D_MODEL, D_MODEL)),
            _const_spec((1, D_MODEL)),
        ],
        out_specs=[row_spec(D_MODEL), row_spec(D_RNN)],
        out_shape=[
            jax.ShapeDtypeStruct((rows, D_MODEL), F32),
            jax.ShapeDtypeStruct((rows, D_RNN), F32),
        ],
        scratch_shapes=[
            pltpu.VMEM((tm + SUBLANES, D_RNN), F32),
            pltpu.VMEM((tm + SUBLANES, D_RNN), F32),
            pltpu.VMEM((tm + SUBLANES, D_RNN), F32),
        ],
        compiler_params=pltpu.CompilerParams(
            dimension_semantics=("arbitrary",),
            vmem_limit_bytes=56 * 1024 * 1024),
    )(x2d, mod_s, proj_s, o_s, h0rep, xs1, xs2, xs3, p['bmg'], p['gnw'], p['wbr_ret'], p['convw'],
      p['convb'], p['wrg'], p['bra'], p['brx'], p['lru'], p['wbr_rnn'], p['wout'], p['npost'])


def _rope_table(pos):
    half = DK // 2
    inv = ROPE_BASE ** (-jnp.arange(half, dtype=F32) / half)
    ang = pos.astype(F32)[:, None] * inv[None, :]
    cos, sin = jnp.cos(ang), jnp.sin(ang)
    ks = DK ** -0.5
    return jnp.concatenate([cos, sin, cos * ks, sin * ks], axis=1)


def _decay_tables(tpos, same_seq, chunk):
    lg = jnp.log(1.0 - 2.0 ** (-5.0 - jnp.arange(H_RET, dtype=F32)))
    idx = tpos.astype(F32)
    diff = idx[:, None] - idx[None, :]
    causal = (diff >= 0) & same_seq
    dec = jnp.where(causal[None], jnp.exp(jnp.where(causal, diff, 0.0)[None] * lg[:, None, None]), 0.0)
    qdec = jnp.exp((idx + 1.0)[None, :] * lg[:, None])
    kdec = jnp.exp((chunk - 1.0 - idx)[None, :] * lg[:, None])
    rep = lambda a: jnp.broadcast_to(a[:, :, None], a.shape + (LANES,))
    gpow = jnp.exp(chunk * lg)
    return dec, rep(qdec), rep(kdec), gpow


def kernel(x_prompt, x_sample, state_ret, state_rnn_h, state_rnn_conv, c_prompt, c_sample,
           w_ada, b_ada, norm_pre_mix, norm_post_mix, norm_pre_ffn, norm_post_ffn,
           w_in, ret_gn_w, w_br_ret, conv_w, conv_b, w_rg_a, b_rg_a, w_rg_x, b_rg_x,
           lru_param, w_br_rnn, w_mgate, b_mgate, w_out, w_ffn_in, w_ffn_out):
    depth = w_in.shape[0]
    assert depth == 1, "single layer step"
    nb, seq, _ = x_prompt.shape
    nsb, sseq, _ = x_sample.shape
    assert seq % PROMPT_TM == 0 and sseq * RET_SAMPLE_BB == SUBLANES and sseq == CONV_W
    l = 0
    row = lambda a: a[l][None, :]
    p = dict(
        npre=row(norm_pre_mix), npost=row(norm_post_mix), npre_ffn=row(norm_pre_ffn), npost_ffn=row(norm_post_ffn),
        wcat=_pack_rows(jnp.concatenate([w_in[l], w_mgate[l]], axis=1)),
        bmg=row(b_mgate), gnw=row(ret_gn_w), wbr_ret=_pack_rows(w_br_ret[l]),
        convw=conv_w[l], convb=row(conv_b),
        wrg=_pack_rows(jnp.concatenate([w_rg_a[l], w_rg_x[l]], axis=2)),
        bra=row(b_rg_a), brx=row(b_rg_x), lru=row(lru_param),
        wbr_rnn=_pack_rows(w_br_rnn[l]), wout=_pack_rows(w_out[l]),
        wffn_in=_pack_rows(w_ffn_in[l]), wffn_out=_pack_rows(w_ffn_out[l]),
    )

    rows_s = nsb * sseq
    c_all = jnp.concatenate([jnp.repeat(c_sample, sseq, axis=0), c_prompt], axis=0)
    mod_all = _mod_call(c_all, _pack_rows(w_ada[l]), row(b_ada))
    mod_p = mod_all[rows_s:].reshape(nb, 1, 6 * D_MODEL)
    mod_s = mod_all

    tm = PROMPT_TM
    ng = tm // SUBLANES
    r = jnp.arange(tm)
    tpos = (r % SUBLANES) * ng + r // SUBLANES
    pos_p = (jnp.arange(seq // tm)[:, None] * tm + tpos[None, :]).reshape(seq).astype(jnp.int32)
    rope_p = _rope_table(pos_p)
    dec, qdec, kdec, gpow = _decay_tables(tpos, jnp.ones((tm, tm), bool), float(tm))
    interleave = lambda a: a.reshape(nb, seq // tm, SUBLANES, ng, D_MODEL).swapaxes(2, 3).reshape(nb, seq, D_MODEL)
    restore = lambda a: a.reshape(nb, seq // tm, ng, SUBLANES, D_MODEL).swapaxes(2, 3).reshape(nb, seq, D_MODEL)
    x1p, ret_p, hlast_p, conv_p = _mix_prompt_call(interleave(x_prompt), mod_p, gpow, rope_p, dec, qdec, kdec, p)

    xs2d = x_sample.reshape(rows_s, D_MODEL)
    proj_s = _proj_sample_call(xs2d, mod_s, p)
    r8 = jnp.arange(SUBLANES)
    rope_s = _rope_table(PAST_LEN + (r8 % sseq).astype(jnp.int32))
    same = (r8[:, None] // sseq) == (r8[None, :] // sseq)
    dec8, qdec8, kdec8, gpow_s = _decay_tables(r8 % sseq, same, float(sseq))
    yp2d, o_s, ret_s = _ffn_ret_call(x1p.reshape(nb * seq, D_MODEL), mod_p, seq // FFN_TM, proj_s, state_ret[l],
                                     gpow_s, rope_s, dec8, qdec8, kdec8, p)
    yp = restore(yp2d.reshape(nb, seq, D_MODEL))
    cs = state_rnn_conv[l]
    pad_rows = lambda a: jnp.pad(a, ((0, 0), (0, sseq - a.shape[1]), (0, 0))).reshape(rows_s, D_RNN)
    xs1, xs2, xs3 = pad_rows(cs[:, 2:3]), pad_rows(cs[:, 1:3]), pad_rows(cs[:, 0:3])
    h0rep = jnp.repeat(state_rnn_h[l], sseq, axis=0)
    x1s, hseq_s = _mix_sample_call(xs2d, mod_s, proj_s, o_s, h0rep, xs1, xs2, xs3, p)
    ys = _ffn_call(x1s, mod_s, FFN_TM, 1, p).reshape(nsb, sseq, D_MODEL)
    hlast_s = hseq_s.reshape(nsb, sseq, D_RNN)[:, sseq - 1]
    conv_s = proj_s[:, OFF_XR:OFF_XR + D_RNN].reshape(nsb, sseq, D_RNN)[:, sseq - (CONV_W - 1):]

    return (yp, ys, ret_p[None], ret_s[None], hlast_p.reshape(nb, D_RNN)[None], hlast_s[None],
            conv_p[None], conv_s[None])
```

```python
import functools

import jax
import jax.numpy as jnp
from jax import lax
from jax.experimental import pallas as pl
from jax.experimental.pallas import tpu as pltpu

F32 = jnp.float32
BF16 = jnp.bfloat16

D_MODEL = 1024
H_RET = 4
DK = D_MODEL // H_RET
DV = 2 * DK
D_QK = H_RET * DK
D_V = H_RET * DV
D_RNN = 1536
RNN_BLOCK = 128
N_RNN_BLOCKS = D_RNN // RNN_BLOCK
CONV_W = 4
LRU_C = 8.0
D_FF = 2816
ROPE_BASE = 10000.0
GN_EPS = 1e-5
RMS_EPS = 1e-6
PAST_LEN = 16384

OFF_Q = 0
OFF_K = OFF_Q + D_QK
OFF_V = OFF_K + D_QK
OFF_G = OFF_V + D_V
OFF_XR = OFF_G + D_V
OFF_GR = OFF_XR + D_RNN
OFF_MG = OFF_GR + D_RNN
N_CAT = OFF_MG + 2 * D_MODEL

SUBLANES = 8
LANES = 128
MXU_DIM = 256
VMEM_BYTES_V7X = 64 * 1024 * 1024

PROMPT_TM = 256
FFN_TM = 512
FF_CHUNK = MXU_DIM
SAMPLE_TM = 128
RET_SAMPLE_BB = 2


def _dot(a, b):
    return jnp.dot(a, b, preferred_element_type=F32)


def _dot_nt(a, b):
    return lax.dot_general(a, b, (((1,), (1,)), ((), ())), preferred_element_type=F32)


def _dot_tn(a, b):
    return lax.dot_general(a, b, (((0,), (0,)), ((), ())), preferred_element_type=F32)


def _wb(ref, k0=None, k1=None, c0=None, c1=None):
    rs = slice(None) if k0 is None else slice(k0, k1)
    cs = slice(None) if c0 is None else slice(c0, c1)
    return ref[rs, cs]


def _pack_rows(w):
    return w.astype(BF16)


def _rms(x, w):
    ms = jnp.mean(x * x, axis=-1, keepdims=True)
    return x * lax.rsqrt(ms + RMS_EPS) * w


def _silu(x):
    return x * jax.nn.sigmoid(x)


def _rope(x, cos, sin):
    half = DK // 2
    x1, x2 = x[:, :half], x[:, half:]
    return jnp.concatenate([x1 * cos - x2 * sin, x1 * sin + x2 * cos], axis=1)


def _group_norm(o):
    mu = jnp.mean(o, axis=-1, keepdims=True)
    d = o - mu
    var = jnp.mean(d * d, axis=-1, keepdims=True)
    return d * lax.rsqrt(var + GN_EPS)


def _lru_gate_pre(xconv, wrg_ref):
    xcb = xconv.astype(BF16)
    return [_dot(xcb[:, n * RNN_BLOCK:(n + 1) * RNN_BLOCK], wrg_ref[n]) for n in range(N_RNN_BLOCKS)]


def _lru_coeffs(xconv, pre, b_a, b_x, lru):
    ra = jnp.concatenate([p[:, :RNN_BLOCK] for p in pre], axis=1) + b_a
    ri = jnp.concatenate([p[:, RNN_BLOCK:] for p in pre], axis=1) + b_x
    r = jax.nn.sigmoid(ra)
    i = jax.nn.sigmoid(ri)
    z = -lru
    sp = jnp.maximum(z, 0.0) + jnp.log(1.0 + jnp.exp(-jnp.abs(z)))
    log_a = -LRU_C * r * sp
    a = jnp.exp(log_a)
    beta = jnp.sqrt(-jnp.tanh(log_a) * (jnp.exp(2.0 * log_a) + 1.0))
    return a, beta * (i * xconv)


def _mix_tail(x, g1, br_ret, rnn_y_b, gate_pre, b_mg, wbr_rnn_ref, wout_ref, npost):
    br_rnn = _dot(rnn_y_b, _wb(wbr_rnn_ref))
    gates = jax.nn.sigmoid(gate_pre + b_mg)
    ga, gb = gates[:, :D_MODEL], gates[:, D_MODEL:]
    mixed = _dot((ga * br_ret + gb * br_rnn).astype(BF16), _wb(wout_ref))
    return x + g1 * _rms(mixed, npost)


def _mod_kernel(c_ref, w_ref, b_ref, o_ref):
    a = _silu(c_ref[...]).astype(BF16)
    o_ref[...] = _dot(a, _wb(w_ref)) + b_ref[...]


def _mod_call(c_all, w_ada_b, b_ada):
    rows = c_all.shape[0]
    tn = D_MODEL
    return pl.pallas_call(
        _mod_kernel,
        grid=(6 * D_MODEL // tn,),
        in_specs=[
            pl.BlockSpec((rows, D_MODEL), lambda j: (0, 0)),
            pl.BlockSpec((D_MODEL, tn), lambda j: (0, j)),
            pl.BlockSpec((1, tn), lambda j: (0, j)),
        ],
        out_specs=pl.BlockSpec((rows, tn), lambda j: (0, j)),
        out_shape=jax.ShapeDtypeStruct((rows, 6 * D_MODEL), F32),
        compiler_params=pltpu.CompilerParams(dimension_semantics=("arbitrary",)),
    )(c_all, w_ada_b, b_ada)


def _mix_prompt_kernel(gpow_ref, x_ref, mod_ref, npre_ref, wcat_ref, bmg_ref, rope_ref,
                       dec_ref, qdec_ref, kdec_ref, gnw_ref, wbr_ret_ref, convw_ref, convb_ref,
                       wrg_ref, bra_ref, brx_ref, lru_ref, wbr_rnn_ref, wout_ref, npost_ref,
                       x1_ref, s_ref, hlast_ref, convnew_ref,
                       hb_ref, xr_ref, prevg_ref, hc_ref):
    tm = PROMPT_TM
    ng = tm // SUBLANES
    halo = (CONV_W - 1) * SUBLANES
    t = pl.program_id(1)

    @pl.when(t == 0)
    def _():
        s_ref[...] = jnp.zeros_like(s_ref)
        prevg_ref[...] = jnp.zeros_like(prevg_ref)
        hc_ref[...] = jnp.zeros_like(hc_ref)

    x = x_ref[...]
    m = mod_ref[...]
    sh1, sc1, g1 = m[:, :D_MODEL], m[:, D_MODEL:2 * D_MODEL], m[:, 2 * D_MODEL:]
    hb_ref[...] = (_rms(x, npre_ref[...]) * (1.0 + sc1) + sh1).astype(BF16)

    hb = hb_ref[...]
    sub = lax.broadcasted_iota(jnp.int32, (SUBLANES, D_RNN), 0)
    half = DK // 2
    cos, sin = rope_ref[:, 0:half], rope_ref[:, half:2 * half]
    cosk, sink = rope_ref[:, 2 * half:3 * half], rope_ref[:, 3 * half:4 * half]
    st = {}

    def xr_proj():
        xr = _dot(hb, _wb(wcat_ref, c0=OFF_XR, c1=OFF_XR + D_RNN))
        xr_ref[halo:halo + tm, :] = xr
        for kk in range(1, CONV_W):
            r0 = (CONV_W - 1 - kk) * SUBLANES
            cur = xr[(ng - kk) * SUBLANES:(ng - kk + 1) * SUBLANES, :]
            prv = prevg_ref[r0:r0 + SUBLANES, :]
            xr_ref[r0:r0 + SUBLANES, :] = pltpu.roll(jnp.where(sub == SUBLANES - 1, prv, cur), 1, 0)
        prevg_ref[...] = xr[tm - halo:, :]
        for kk in range(1, CONV_W):
            r1 = (ng - kk) * SUBLANES + SUBLANES - 1
            convnew_ref[CONV_W - 1 - kk:CONV_W - kk, :] = xr[r1:r1 + 1, :]

    def lru_conv():
        cw = convw_ref[...]
        xconv = convb_ref[...]
        for j in range(CONV_W):
            r0 = halo - (CONV_W - 1 - j) * SUBLANES
            xconv = xconv + xr_ref[r0:r0 + tm, :] * cw[j:j + 1, :]
        st['xconv'] = xconv

    def lru_gate_proj():
        st['gpre'] = _lru_gate_pre(st['xconv'], wrg_ref)

    def lru_coef():
        st['a'], st['b'] = _lru_coeffs(st.pop('xconv'), st.pop('gpre'), bra_ref[...], brx_ref[...], lru_ref[...])

    def lru_scan():
        a, b = st['a'], st['b']
        ca, cb = a[0:SUBLANES, :], b[0:SUBLANES, :]
        cas, cbs = [ca], [cb]
        for gi in range(1, ng):
            ag = a[gi * SUBLANES:(gi + 1) * SUBLANES, :]
            cb = ag * cb + b[gi * SUBLANES:(gi + 1) * SUBLANES, :]
            ca = ag * ca
            cas.append(ca)
            cbs.append(cb)
        cin = jnp.where(sub == 0, hc_ref[SUBLANES - 1:SUBLANES, :], 0.0)
        for s in range(SUBLANES - 1):
            cin = jnp.where(sub == s + 1, pltpu.roll(ca * cin + cb, 1, 0), cin)
        seg_end = ca * cin + cb
        hc_ref[...] = seg_end
        hlast_ref[...] = seg_end[SUBLANES - 1:SUBLANES, :]
        st['hseq'] = jnp.concatenate([cas[gi] * cin + cbs[gi] for gi in range(ng)], axis=0)

    def gr_proj():
        st['gr'] = _dot(hb, _wb(wcat_ref, c0=OFF_GR, c1=OFF_GR + D_RNN))

    def lru_y():
        st['rnn_y'] = (st.pop('hseq') * jax.nn.gelu(st.pop('gr'), approximate=True)).astype(BF16)

    def lru_out():
        st['br_rnn'] = _dot(st.pop('rnn_y'), _wb(wbr_rnn_ref))

    def gate_proj():
        st['gate_pre'] = _dot(hb, _wb(wcat_ref, c0=OFF_MG, c1=OFF_MG + 2 * D_MODEL))

    def gate_act():
        st['gates'] = jax.nn.sigmoid(st.pop('gate_pre') + bmg_ref[...])

    def head_proj(hh):
        q = _dot(hb, _wb(wcat_ref, c0=OFF_Q + hh * DK, c1=OFF_Q + (hh + 1) * DK))
        k = _dot(hb, _wb(wcat_ref, c0=OFF_K + hh * DK, c1=OFF_K + (hh + 1) * DK))
        vb = _dot(hb, _wb(wcat_ref, c0=OFF_V + hh * DV, c1=OFF_V + (hh + 1) * DV)).astype(BF16)
        g = _dot(hb, _wb(wcat_ref, c0=OFF_G + hh * DV, c1=OFF_G + (hh + 1) * DV))
        st['proj', hh] = (q, k, vb, g)

    def head_rope(hh):
        q, k, vb, g = st.pop(('proj', hh))
        kr = _rope(k, cosk, sink)
        kdec = kdec_ref[hh]
        kdb = (kr * jnp.concatenate([kdec, kdec], axis=1)).astype(BF16)
        st['rope', hh] = (_rope(q, cos, sin).astype(BF16), kr.astype(BF16), kdb, vb, g)

    def head_qk(hh):
        qb, kb, kdb, vb, g = st.pop(('rope', hh))
        scores = _dot_nt(qb, kb)
        cross = _dot(qb, s_ref[hh].astype(BF16))
        st['qk', hh] = (scores, cross, vb, g)
        st['kv', hh] = (kdb, vb)

    def head_state(hh):
        kdb, vb = st.pop(('kv', hh))
        s_ref[hh] = gpow_ref[hh] * s_ref[hh] + _dot_tn(kdb, vb)

    def head_decay(hh):
        scores, cross, vb, g = st.pop(('qk', hh))
        st['dec', hh] = ((scores * dec_ref[hh]).astype(BF16), cross, vb, g)

    def head_pv(hh):
        sb, cross, vb, g = st.pop(('dec', hh))
        st['pv', hh] = (_dot(sb, vb), cross, g)

    def head_norm(hh):
        intra, cross, g = st.pop(('pv', hh))
        qdec = qdec_ref[hh]
        on = _group_norm(intra + cross * jnp.concatenate([qdec] * (DV // LANES), axis=1))
        st['ry', hh] = (on * gnw_ref[:, hh * DV:(hh + 1) * DV] * _silu(g)).astype(BF16)

    def head_out(hh):
        part = _dot(st.pop(('ry', hh)), _wb(wbr_ret_ref, hh * DV, (hh + 1) * DV))
        st['br_ret'] = part if hh == 0 else st['br_ret'] + part

    heads = range(H_RET)
    order = (
        [xr_proj] + [(head_proj, h) for h in heads] + [lru_conv, lru_gate_proj]
        + [(head_rope, h) for h in heads] + [gr_proj, lru_coef] + [(head_qk, h) for h in heads]
        + [gate_proj] + [(head_decay, h) for h in heads] + [lru_scan]
        + [(head_pv, h) for h in heads] + [(head_state, h) for h in heads] + [lru_y, lru_out]
        + [(head_norm, h) for h in heads] + [(head_out, h) for h in heads] + [gate_act]
    )
    for stage in order:
        if isinstance(stage, tuple):
            stage[0](stage[1])
        else:
            stage()

    gates = st['gates']
    ga, gb = gates[:, :D_MODEL], gates[:, D_MODEL:]
    mixed = _dot((ga * st['br_ret'] + gb * st['br_rnn']).astype(BF16), _wb(wout_ref))
    x1_ref[...] = x + g1 * _rms(mixed, npost_ref[...])


def _const_spec(shape):
    nd = len(shape)
    return pl.BlockSpec(shape, lambda *_: (0,) * nd, pipeline_mode=pl.Buffered(1))


def _mix_prompt_call(x, mod3, gpow, rope_tab, dec, qdec, kdec, p):
    nb, seq, _ = x.shape
    tm = PROMPT_TM
    nt = seq // tm
    in_specs = [
        pl.BlockSpec(memory_space=pltpu.SMEM),
        pl.BlockSpec((None, tm, D_MODEL), lambda b, t: (b, t, 0)),
        pl.BlockSpec((None, 1, 3 * D_MODEL), lambda b, t: (b, 0, 0)),
        _const_spec((1, D_MODEL)),
        _const_spec((D_MODEL, N_CAT)),
        _const_spec((1, 2 * D_MODEL)),
        pl.BlockSpec((tm, 4 * (DK // 2)), lambda b, t: (t, 0)),
        _const_spec((H_RET, tm, tm)),
        _const_spec((H_RET, tm, LANES)),
        _const_spec((H_RET, tm, LANES)),
        _const_spec((1, D_V)),
        _const_spec((D_V, D_MODEL)),
        _const_spec((CONV_W, D_RNN)),
        _const_spec((1, D_RNN)),
        _const_spec((N_RNN_BLOCKS, RNN_BLOCK, 2 * RNN_BLOCK)),
        _const_spec((1, D_RNN)),
        _const_spec((1, D_RNN)),
        _const_spec((1, D_RNN)),
        _const_spec((D_RNN, D_MODEL)),
        _const_spec((D_MODEL, D_MODEL)),
        _const_spec((1, D_MODEL)),
    ]
    out_specs = [
        pl.BlockSpec((None, tm, D_MODEL), lambda b, t: (b, t, 0)),
        pl.BlockSpec((None, H_RET, DK, DV), lambda b, t: (b, 0, 0, 0)),
        pl.BlockSpec((None, 1, D_RNN), lambda b, t: (b, 0, 0)),
        pl.BlockSpec((None, CONV_W - 1, D_RNN), lambda b, t: (b, 0, 0)),
    ]
    out_shape = [
        jax.ShapeDtypeStruct((nb, seq, D_MODEL), F32),
        jax.ShapeDtypeStruct((nb, H_RET, DK, DV), F32),
        jax.ShapeDtypeStruct((nb, 1, D_RNN), F32),
        jax.ShapeDtypeStruct((nb, CONV_W - 1, D_RNN), F32),
    ]
    halo = (CONV_W - 1) * SUBLANES
    scratch = [
        pltpu.VMEM((tm, D_MODEL), BF16),
        pltpu.VMEM((halo + tm, D_RNN), F32),
        pltpu.VMEM((halo, D_RNN), F32),
        pltpu.VMEM((SUBLANES, D_RNN), F32),
    ]
    return pl.pallas_call(
        _mix_prompt_kernel,
        grid=(nb, nt),
        in_specs=in_specs,
        out_specs=out_specs,
        out_shape=out_shape,
        scratch_shapes=scratch,
        compiler_params=pltpu.CompilerParams(
            dimension_semantics=("arbitrary", "arbitrary"),
            vmem_limit_bytes=VMEM_BYTES_V7X - 4 * 1024 * 1024),
    )(gpow, x, mod3, p['npre'], p['wcat'], p['bmg'], rope_tab, dec, qdec, kdec, p['gnw'], p['wbr_ret'],
      p['convw'], p['convb'], p['wrg'], p['bra'], p['brx'], p['lru'], p['wbr_rnn'], p['wout'], p['npost'])


def _ffn_pre(x_ref, mod_ref, npre_ref):
    m = mod_ref[...]
    sh2, sc2 = m[:, :D_MODEL], m[:, D_MODEL:2 * D_MODEL]
    return (_rms(x_ref[...], npre_ref[...]) * (1.0 + sc2) + sh2).astype(BF16)


def _ffn_hidden(h2, acc, w1_ref, w2_ref, chunk_ids):
    for j in chunk_ids:
        c0 = j * FF_CHUNK
        fg = _dot(h2, _wb(w1_ref, c0=c0, c1=c0 + FF_CHUNK))
        fu = _dot(h2, _wb(w1_ref, c0=D_FF + c0, c1=D_FF + c0 + FF_CHUNK))
        part = _dot((_silu(fg) * fu).astype(BF16), _wb(w2_ref, c0, c0 + FF_CHUNK))
        acc = part if acc is None else acc + part
    return acc


def _ffn_post(x_ref, mod_ref, acc, npost_ref, o_ref):
    o_ref[...] = x_ref[...] + mod_ref[:, 2 * D_MODEL:] * _rms(acc, npost_ref[...])


def _ffn_kernel(x_ref, mod_ref, npre_ref, w1_ref, w2_ref, npost_ref, o_ref):
    h2 = _ffn_pre(x_ref, mod_ref, npre_ref)
    acc = _ffn_hidden(h2, None, w1_ref, w2_ref, range(D_FF // FF_CHUNK))
    _ffn_post(x_ref, mod_ref, acc, npost_ref, o_ref)


def _ffn_call(x2d, mod, mod_rows_per_tile, seq_tiles, p):
    rows = x2d.shape[0]
    tm = FFN_TM
    if mod_rows_per_tile == 1:
        mod_spec = pl.BlockSpec((None, 1, 3 * D_MODEL), lambda i: (i // seq_tiles, 0, 1))
    else:
        mod_spec = pl.BlockSpec((tm, 3 * D_MODEL), lambda i: (i, 1))
    return pl.pallas_call(
        _ffn_kernel,
        grid=(rows // tm,),
        in_specs=[
            pl.BlockSpec((tm, D_MODEL), lambda i: (i, 0)),
            mod_spec,
            _const_spec((1, D_MODEL)),
            _const_spec((D_MODEL, 2 * D_FF)),
            _const_spec((D_FF, D_MODEL)),
            _const_spec((1, D_MODEL)),
        ],
        out_specs=pl.BlockSpec((tm, D_MODEL), lambda i: (i, 0)),
        out_shape=jax.ShapeDtypeStruct((rows, D_MODEL), F32),
        compiler_params=pltpu.CompilerParams(
            dimension_semantics=("arbitrary",),
            vmem_limit_bytes=48 * 1024 * 1024),
    )(x2d, mod, p['npre_ffn'], p['wffn_in'], p['wffn_out'], p['npost_ffn'])


def _proj_sample_kernel(x_ref, mod_ref, npre_ref, w_ref, o_ref):
    m = mod_ref[...]
    sh1, sc1 = m[:, :D_MODEL], m[:, D_MODEL:2 * D_MODEL]
    h = (_rms(x_ref[...], npre_ref[...]) * (1.0 + sc1) + sh1).astype(BF16)
    o_ref[...] = _dot(h, _wb(w_ref))


def _proj_sample_call(x2d, mod_s, p):
    rows = x2d.shape[0]
    tn = D_MODEL
    return pl.pallas_call(
        _proj_sample_kernel,
        grid=(N_CAT // tn,),
        in_specs=[
            pl.BlockSpec((rows, D_MODEL), lambda j: (0, 0)),
            pl.BlockSpec((rows, 3 * D_MODEL), lambda j: (0, 0)),
            pl.BlockSpec((1, D_MODEL), lambda j: (0, 0)),
            pl.BlockSpec((D_MODEL, tn), lambda j: (0, j)),
        ],
        out_specs=pl.BlockSpec((rows, tn), lambda j: (0, j)),
        out_shape=jax.ShapeDtypeStruct((rows, N_CAT), F32),
        compiler_params=pltpu.CompilerParams(
            dimension_semantics=("arbitrary",),
            vmem_limit_bytes=48 * 1024 * 1024),
    )(x2d, mod_s, p['npre'], p['wcat'])


def _ret_sample_kernel(gpow_ref, qkv_ref, s_ref, rope_ref, dec_ref, qdec_ref, kdec_ref, o_ref, snew_ref):
    half = DK // 2
    cos, sin = rope_ref[:, 0:half], rope_ref[:, half:2 * half]
    cosk, sink = rope_ref[:, 2 * half:3 * half], rope_ref[:, 3 * half:4 * half]
    nseq = RET_SAMPLE_BB
    tlen = SUBLANES // nseq
    row = lax.broadcasted_iota(jnp.int32, (SUBLANES, LANES), 0)
    for hh in range(H_RET):
        q = _rope(qkv_ref[:, OFF_Q + hh * DK:OFF_Q + (hh + 1) * DK], cos, sin)
        k = _rope(qkv_ref[:, OFF_K + hh * DK:OFF_K + (hh + 1) * DK], cosk, sink)
        v = qkv_ref[:, OFF_V + hh * DV:OFF_V + (hh + 1) * DV]
        kdec = kdec_ref[hh]
        kd = k * jnp.concatenate([kdec, kdec], axis=1)
        scores = _dot_nt(q, k) * dec_ref[hh]
        intra = _dot(scores, v)
        qb = q.astype(BF16)
        cross = jnp.zeros((SUBLANES, DV), F32)
        for bi in range(nseq):
            s_old = s_ref[bi, hh]
            in_seq = (row >= bi * tlen) & (row < (bi + 1) * tlen)
            cr = _dot(qb, s_old.astype(BF16))
            cross = jnp.where(jnp.concatenate([in_seq] * (DV // LANES), axis=1), cr, cross)
            kd_b = jnp.where(jnp.concatenate([in_seq] * (DK // LANES), axis=1), kd, 0.0)
            snew_ref[bi, hh] = gpow_ref[hh] * s_old + _dot_tn(kd_b, v)
        qdec = qdec_ref[hh]
        o_ref[:, hh * DV:(hh + 1) * DV] = intra + cross * jnp.concatenate([qdec] * (DV // LANES), axis=1)


def _ffn_ret_kernel(sub_steps, gpow_ref, x_ref, mod_ref, npre_ref, w1_ref, w2_ref, npost_ref,
                    qkv_ref, s_ref, rope_ref, dec_ref, qdec_ref, kdec_ref,
                    y_ref, o_ref, snew_ref, h2_ref, acc_ref):
    n_chunks = D_FF // FF_CHUNK
    bounds = [n_chunks * s // sub_steps for s in range(sub_steps + 1)]
    for s in range(sub_steps):
        @pl.when(pl.program_id(1) == s)
        def _(s=s):
            if s == 0:
                h2_ref[...] = _ffn_pre(x_ref, mod_ref, npre_ref)
            acc = _ffn_hidden(h2_ref[...], None if s == 0 else acc_ref[...], w1_ref, w2_ref,
                              range(bounds[s], bounds[s + 1]))
            if s == sub_steps - 1:
                _ffn_post(x_ref, mod_ref, acc, npost_ref, y_ref)
            else:
                acc_ref[...] = acc

    _ret_sample_kernel(gpow_ref, qkv_ref, s_ref, rope_ref, dec_ref, qdec_ref, kdec_ref, o_ref, snew_ref)


def _ffn_ret_call(x2d, mod_p, seq_tiles, proj_s, state, gpow, rope8, dec8, qdec8, kdec8, p):
    rows = x2d.shape[0]
    tm = FFN_TM
    nb = state.shape[0]
    bb = RET_SAMPLE_BB
    sub_steps = nb // bb // (rows // tm)
    assert sub_steps * (rows // tm) * bb == nb
    srow = lambda i, j: i * sub_steps + j
    return pl.pallas_call(
        functools.partial(_ffn_ret_kernel, sub_steps),
        grid=(rows // tm, sub_steps),
        in_specs=[
            pl.BlockSpec(memory_space=pltpu.SMEM),
            pl.BlockSpec((tm, D_MODEL), lambda i, j: (i, 0)),
            pl.BlockSpec((None, 1, 3 * D_MODEL), lambda i, j: (i // seq_tiles, 0, 1)),
            _const_spec((1, D_MODEL)),
            _const_spec((D_MODEL, 2 * D_FF)),
            _const_spec((D_FF, D_MODEL)),
            _const_spec((1, D_MODEL)),
            pl.BlockSpec((SUBLANES, OFF_G), lambda i, j: (srow(i, j), 0)),
            pl.BlockSpec((bb, H_RET, DK, DV), lambda i, j: (srow(i, j), 0, 0, 0)),
            _const_spec((SUBLANES, 4 * (DK // 2))),
            _const_spec((H_RET, SUBLANES, SUBLANES)),
            _const_spec((H_RET, SUBLANES, LANES)),
            _const_spec((H_RET, SUBLANES, LANES)),
        ],
        out_specs=[
            pl.BlockSpec((tm, D_MODEL), lambda i, j: (i, 0)),
            pl.BlockSpec((SUBLANES, D_V), lambda i, j: (srow(i, j), 0)),
            pl.BlockSpec((bb, H_RET, DK, DV), lambda i, j: (srow(i, j), 0, 0, 0)),
        ],
        out_shape=[
            jax.ShapeDtypeStruct((rows, D_MODEL), F32),
            jax.ShapeDtypeStruct((proj_s.shape[0], D_V), F32),
            jax.ShapeDtypeStruct(state.shape, F32),
        ],
        scratch_shapes=[pltpu.VMEM((tm, D_MODEL), BF16), pltpu.VMEM((tm, D_MODEL), F32)],
        compiler_params=pltpu.CompilerParams(
            dimension_semantics=("arbitrary", "arbitrary"),
            vmem_limit_bytes=56 * 1024 * 1024),
    )(gpow, x2d, mod_p, p['npre_ffn'], p['wffn_in'], p['wffn_out'], p['npost_ffn'],
      proj_s, state, rope8, dec8, qdec8, kdec8)


def _mix_sample_kernel(x_ref, mod_ref, proj_ref, o_ref, h0_ref, xs1_ref, xs2_ref, xs3_ref,
                       bmg_ref, gnw_ref, wbr_ret_ref, convw_ref, convb_ref, wrg_ref, bra_ref, brx_ref,
                       lru_ref, wbr_rnn_ref, wout_ref, npost_ref,
                       x1_ref, hseq_ref,
                       xr_ref, sa_ref, sb_ref):
    tm = SAMPLE_TM
    tlen = 4
    x = x_ref[...]
    g1 = mod_ref[:, 2 * D_MODEL:]

    br_ret = jnp.zeros((tm, D_MODEL), F32)
    for hh in range(H_RET):
        on = _group_norm(o_ref[:, hh * DV:(hh + 1) * DV])
        g = proj_ref[:, OFF_G + hh * DV:OFF_G + (hh + 1) * DV]
        ry = (on * gnw_ref[:, hh * DV:(hh + 1) * DV] * _silu(g)).astype(BF16)
        br_ret = br_ret + _dot(ry, _wb(wbr_ret_ref, hh * DV, (hh + 1) * DV))

    tpos = lax.broadcasted_iota(jnp.int32, (tm, D_RNN), 0) & (tlen - 1)
    xr = proj_ref[:, OFF_XR:OFF_XR + D_RNN]
    xr_ref[0:SUBLANES, :] = jnp.zeros((SUBLANES, D_RNN), F32)
    xr_ref[SUBLANES:SUBLANES + tm, :] = xr
    cw = convw_ref[...]
    prev = (xs3_ref, xs2_ref, xs1_ref)
    xconv = convb_ref[...]
    for j in range(CONV_W - 1):
        sft = CONV_W - 1 - j
        shifted = jnp.where(tpos >= sft, xr_ref[SUBLANES - sft:SUBLANES - sft + tm, :], 0.0)
        xconv = xconv + (shifted + prev[j][...]) * cw[j:j + 1, :]
    xconv = xconv + xr * cw[CONV_W - 1:CONV_W, :]

    a, b = _lru_coeffs(xconv, _lru_gate_pre(xconv, wrg_ref), bra_ref[...], brx_ref[...], lru_ref[...])
    b = jnp.where(tpos == 0, b + a * h0_ref[...], b)
    sa_ref[0:SUBLANES, :] = jnp.zeros((SUBLANES, D_RNN), F32)
    sb_ref[0:SUBLANES, :] = jnp.zeros((SUBLANES, D_RNN), F32)
    for s in (1, 2):
        sa_ref[SUBLANES:SUBLANES + tm, :] = a
        sb_ref[SUBLANES:SUBLANES + tm, :] = b
        keep = tpos >= s
        ap = jnp.where(keep, sa_ref[SUBLANES - s:SUBLANES - s + tm, :], 1.0)
        bp = jnp.where(keep, sb_ref[SUBLANES - s:SUBLANES - s + tm, :], 0.0)
        b = a * bp + b
        a = a * ap
    hseq_ref[...] = b

    gr = proj_ref[:, OFF_GR:OFF_GR + D_RNN]
    rnn_y_b = (b * jax.nn.gelu(gr, approximate=True)).astype(BF16)
    gate_pre = proj_ref[:, OFF_MG:OFF_MG + 2 * D_MODEL]
    x1_ref[...] = _mix_tail(x, g1, br_ret, rnn_y_b, gate_pre, bmg_ref[...], wbr_rnn_ref, wout_ref,
                            npost_ref[...])


def _mix_sample_call(x2d, mod_s, proj_s, o_s, h0rep, xs1, xs2, xs3, p):
    rows = x2d.shape[0]
    tm = SAMPLE_TM
    row_spec = lambda w: pl.BlockSpec((tm, w), lambda i: (i, 0))
    return pl.pallas_call(
        _mix_sample_kernel,
        grid=(rows // tm,),
        in_specs=[
            row_spec(D_MODEL), row_spec(3 * D_MODEL), row_spec(N_CAT), row_spec(D_V),
            row_spec(D_RNN), row_spec(D_RNN), row_spec(D_RNN), row_spec(D_RNN),
            _const_spec((1, 2 * D_MODEL)),
            _const_spec((1, D_V)),
            _const_spec((D_V, D_MODEL)),
            _const_spec((CONV_W, D_RNN)),
            _const_spec((1, D_RNN)),
            _const_spec((N_RNN_BLOCKS, RNN_BLOCK, 2 * RNN_BLOCK)),
            _const_spec((1, D_RNN)),
            _const_spec((1, D_RNN)),
            _const_spec((1, D_RNN)),
            _const_spec((D_RNN, D_MODEL)),
            _const_spec((D_MODEL, D_MODEL)),
            _const_spec((1, D_MODEL)),
        ],
        out_specs=[row_spec(D_MODEL), row_spec(D_RNN)],
        out_shape=[
            jax.ShapeDtypeStruct((rows, D_MODEL), F32),
            jax.ShapeDtypeStruct((rows, D_RNN), F32),
        ],
        scratch_shapes=[
            pltpu.VMEM((tm + SUBLANES, D_RNN), F32),
            pltpu.VMEM((tm + SUBLANES, D_RNN), F32),
            pltpu.VMEM((tm + SUBLANES, D_RNN), F32),
        ],
        compiler_params=pltpu.CompilerParams(
            dimension_semantics=("arbitrary",),
            vmem_limit_bytes=56 * 1024 * 1024),
    )(x2d, mod_s, proj_s, o_s, h0rep, xs1, xs2, xs3, p['bmg'], p['gnw'], p['wbr_ret'], p['convw'],
      p['convb'], p['wrg'], p['bra'], p['brx'], p['lru'], p['wbr_rnn'], p['wout'], p['npost'])


def _rope_table(pos):
    half = DK // 2
    inv = ROPE_BASE ** (-jnp.arange(half, dtype=F32) / half)
    ang = pos.astype(F32)[:, None] * inv[None, :]
    cos, sin = jnp.cos(ang), jnp.sin(ang)
    ks = DK ** -0.5
    return jnp.concatenate([cos, sin, cos * ks, sin * ks], axis=1)


def _decay_tables(tpos, same_seq, chunk):
    lg = jnp.log(1.0 - 2.0 ** (-5.0 - jnp.arange(H_RET, dtype=F32)))
    idx = tpos.astype(F32)
    diff = idx[:, None] - idx[None, :]
    causal = (diff >= 0) & same_seq
    dec = jnp.where(causal[None], jnp.exp(jnp.where(causal, diff, 0.0)[None] * lg[:, None, None]), 0.0)
    qdec = jnp.exp((idx + 1.0)[None, :] * lg[:, None])
    kdec = jnp.exp((chunk - 1.0 - idx)[None, :] * lg[:, None])
    rep = lambda a: jnp.broadcast_to(a[:, :, None], a.shape + (LANES,))
    gpow = jnp.exp(chunk * lg)
    return dec, rep(qdec), rep(kdec), gpow


def kernel(x_prompt, x_sample, state_ret, state_rnn_h, state_rnn_conv, c_prompt, c_sample,
           w_ada, b_ada, norm_pre_mix, norm_post_mix, norm_pre_ffn, norm_post_ffn,
           w_in, ret_gn_w, w_br_ret, conv_w, conv_b, w_rg_a, b_rg_a, w_rg_x, b_rg_x,
           lru_param, w_br_rnn, w_mgate, b_mgate, w_out, w_ffn_in, w_ffn_out):
    depth = w_in.shape[0]
    assert depth == 1, "single layer step"
    nb, seq, _ = x_prompt.shape
    nsb, sseq, _ = x_sample.shape
    assert seq % PROMPT_TM == 0 and sseq * RET_SAMPLE_BB == SUBLANES and sseq == CONV_W
    l = 0
    row = lambda a: a[l][None, :]
    p = dict(
        npre=row(norm_pre_mix), npost=row(norm_post_mix), npre_ffn=row(norm_pre_ffn), npost_ffn=row(norm_post_ffn),
        wcat=_pack_rows(jnp.concatenate([w_in[l], w_mgate[l]], axis=1)),
        bmg=row(b_mgate), gnw=row(ret_gn_w), wbr_ret=_pack_rows(w_br_ret[l]),
        convw=conv_w[l], convb=row(conv_b),
        wrg=_pack_rows(jnp.concatenate([w_rg_a[l], w_rg_x[l]], axis=2)),
        bra=row(b_rg_a), brx=row(b_rg_x), lru=row(lru_param),
        wbr_rnn=_pack_rows(w_br_rnn[l]), wout=_pack_rows(w_out[l]),
        wffn_in=_pack_rows(w_ffn_in[l]), wffn_out=_pack_rows(w_ffn_out[l]),
    )

    rows_s = nsb * sseq
    c_all = jnp.concatenate([jnp.repeat(c_sample, sseq, axis=0), c_prompt], axis=0)
    mod_all = _mod_call(c_all, _pack_rows(w_ada[l]), row(b_ada))
    mod_p = mod_all[rows_s:].reshape(nb, 1, 6 * D_MODEL)
    mod_s = mod_all

    tm = PROMPT_TM
    ng = tm // SUBLANES
    r = jnp.arange(tm)
    tpos = (r % SUBLANES) * ng + r // SUBLANES
    pos_p = (jnp.arange(seq // tm)[:, None] * tm + tpos[None, :]).reshape(seq).astype(jnp.int32)
    rope_p = _rope_table(pos_p)
    dec, qdec, kdec, gpow = _decay_tables(tpos, jnp.ones((tm, tm), bool), float(tm))
    interleave = lambda a: a.reshape(nb, seq // tm, SUBLANES, ng, D_MODEL).swapaxes(2, 3).reshape(nb, seq, D_MODEL)
    restore = lambda a: a.reshape(nb, seq // tm, ng, SUBLANES, D_MODEL).swapaxes(2, 3).reshape(nb, seq, D_MODEL)
    x1p, ret_p, hlast_p, conv_p = _mix_prompt_call(interleave(x_prompt), mod_p, gpow, rope_p, dec, qdec, kdec, p)

    xs2d = x_sample.reshape(rows_s, D_MODEL)
    proj_s = _proj_sample_call(xs2d, mod_s, p)
    r8 = jnp.arange(SUBLANES)
    rope_s = _rope_table(PAST_LEN + (r8 % sseq).astype(jnp.int32))
    same = (r8[:, None] // sseq) == (r8[None, :] // sseq)
    dec8, qdec8, kdec8, gpow_s = _decay_tables(r8 % sseq, same, float(sseq))
    yp2d, o_s, ret_s = _ffn_ret_call(x1p.reshape(nb * seq, D_MODEL), mod_p, seq // FFN_TM, proj_s, state_ret[l],
                                     gpow_s, rope_s, dec8, qdec8, kdec8, p)
    yp = restore(yp2d.reshape(nb, seq, D_MODEL))
    cs = state_rnn_conv[l]
    pad_rows = lambda a: jnp.pad(a, ((0, 0), (0, sseq - a.shape[1]), (0, 0))).reshape(rows_s, D_RNN)
    xs1, xs2, xs3 = pad_rows(cs[:, 2:3]), pad_rows(cs[:, 1:3]), pad_rows(cs[:, 0:3])
    h0rep = jnp.repeat(state_rnn_h[l], sseq, axis=0)
    x1s, hseq_s = _mix_sample_call(xs2d, mod_s, proj_s, o_s, h0rep, xs1, xs2, xs3, p)
    ys = _ffn_call(x1s, mod_s, FFN_TM, 1, p).reshape(nsb, sseq, D_MODEL)
    hlast_s = hseq_s.reshape(nsb, sseq, D_RNN)[:, sseq - 1]
    conv_s = proj_s[:, OFF_XR:OFF_XR + D_RNN].reshape(nsb, sseq, D_RNN)[:, sseq - (CONV_W - 1):]

    return (yp, ys, ret_p[None], ret_s[None], hlast_p.reshape(nb, D_RNN)[None], hlast_s[None],
            conv_p[None], conv_s[None])
```

```python
import functools

import jax
import jax.numpy as jnp
from jax import lax
from jax.experimental import pallas as pl
from jax.experimental.pallas import tpu as pltpu

F32 = jnp.float32
BF16 = jnp.bfloat16

D_MODEL = 1024
H_RET = 4
DK = D_MODEL // H_RET
DV = 2 * DK
D_QK = H_RET * DK
D_V = H_RET * DV
D_RNN = 1536
RNN_BLOCK = 128
N_RNN_BLOCKS = D_RNN // RNN_BLOCK
CONV_W = 4
LRU_C = 8.0
D_FF = 2816
ROPE_BASE = 10000.0
GN_EPS = 1e-5
RMS_EPS = 1e-6
PAST_LEN = 16384

OFF_Q = 0
OFF_K = OFF_Q + D_QK
OFF_V = OFF_K + D_QK
OFF_G = OFF_V + D_V
OFF_XR = OFF_G + D_V
OFF_GR = OFF_XR + D_RNN
OFF_MG = OFF_GR + D_RNN
N_CAT = OFF_MG + 2 * D_MODEL

SUBLANES = 8
LANES = 128
MXU_DIM = 256
VMEM_BYTES_V7X = 64 * 1024 * 1024

PROMPT_TM = 256
FFN_TM = 512
FF_CHUNK = MXU_DIM
SAMPLE_TM = 128
RET_SAMPLE_BB = 2


def _dot(a, b):
    return jnp.dot(a, b, preferred_element_type=F32)


def _dot_nt(a, b):
    return lax.dot_general(a, b, (((1,), (1,)), ((), ())), preferred_element_type=F32)


def _dot_tn(a, b):
    return lax.dot_general(a, b, (((0,), (0,)), ((), ())), preferred_element_type=F32)


def _wb(ref, k0=None, k1=None, c0=None, c1=None):
    rs = slice(None) if k0 is None else slice(k0, k1)
    cs = slice(None) if c0 is None else slice(c0, c1)
    return ref[rs, cs]


def _pack_rows(w):
    return w.astype(BF16)


def _rms(x, w):
    ms = jnp.mean(x * x, axis=-1, keepdims=True)
    return x * lax.rsqrt(ms + RMS_EPS) * w


def _sigmoid(x):
    return 0.5 * jnp.tanh(0.5 * x) + 0.5


def _silu(x):
    return x * _sigmoid(x)


def _rope(x, cos, sin):
    half = DK // 2
    x1, x2 = x[:, :half], x[:, half:]
    return jnp.concatenate([x1 * cos - x2 * sin, x1 * sin + x2 * cos], axis=1)


def _group_norm(o):
    mu = jnp.mean(o, axis=-1, keepdims=True)
    d = o - mu
    var = jnp.mean(d * d, axis=-1, keepdims=True)
    return d * lax.rsqrt(var + GN_EPS)


def _lru_gate_pre(xconv, wrg_ref):
    xcb = xconv.astype(BF16)
    return [_dot(xcb[:, n * RNN_BLOCK:(n + 1) * RNN_BLOCK], wrg_ref[n]) for n in range(N_RNN_BLOCKS)]


def _lru_coeffs(xconv, pre, b_a, b_x, lru):
    ra = jnp.concatenate([p[:, :RNN_BLOCK] for p in pre], axis=1) + b_a
    ri = jnp.concatenate([p[:, RNN_BLOCK:] for p in pre], axis=1) + b_x
    r = _sigmoid(ra)
    i = _sigmoid(ri)
    z = -lru
    sp = jnp.maximum(z, 0.0) + jnp.log(1.0 + jnp.exp(-jnp.abs(z)))
    log_a = -LRU_C * r * sp
    a = jnp.exp(log_a)
    beta = jnp.sqrt(-jnp.tanh(log_a) * (a * a + 1.0))
    return a, beta * (i * xconv)


def _mix_tail(x, g1, br_ret, rnn_y_b, gate_pre, b_mg, wbr_rnn_ref, wout_ref, npost):
    br_rnn = _dot(rnn_y_b, _wb(wbr_rnn_ref))
    gates = _sigmoid(gate_pre + b_mg)
    ga, gb = gates[:, :D_MODEL], gates[:, D_MODEL:]
    mixed = _dot((ga * br_ret + gb * br_rnn).astype(BF16), _wb(wout_ref))
    return x + g1 * _rms(mixed, npost)


def _mod_kernel(c_ref, w_ref, b_ref, o_ref):
    a = _silu(c_ref[...]).astype(BF16)
    o_ref[...] = _dot(a, w_ref[...].astype(BF16)) + b_ref[...]


def _mod_call(c_all, w_ada_f32, b_ada):
    rows = c_all.shape[0]
    tn = D_MODEL
    return pl.pallas_call(
        _mod_kernel,
        grid=(6 * D_MODEL // tn,),
        in_specs=[
            pl.BlockSpec((rows, D_MODEL), lambda j: (0, 0)),
            pl.BlockSpec((D_MODEL, tn), lambda j: (0, j)),
            pl.BlockSpec((1, tn), lambda j: (0, j)),
        ],
        out_specs=pl.BlockSpec((rows, tn), lambda j: (0, j)),
        out_shape=jax.ShapeDtypeStruct((rows, 6 * D_MODEL), F32),
        compiler_params=pltpu.CompilerParams(dimension_semantics=("arbitrary",)),
    )(c_all, w_ada_f32, b_ada)


def _mix_prompt_kernel(gpow_ref, x_ref, mod_ref, npre_ref, wcat_ref, wmg_ref, bmg_ref, rope_ref,
                       dec_ref, qdec_ref, kdec_ref, gnw_ref, wbr_ret_ref, convw_ref, convb_ref,
                       wrg_ref, bra_ref, brx_ref, lru_ref, wbr_rnn_ref, wout_ref, npost_ref,
                       x1_ref, s_ref, hlast_ref, convnew_ref,
                       hb_ref, xr_ref, prevg_ref, hc_ref):
    tm = PROMPT_TM
    ng = tm // SUBLANES
    halo = (CONV_W - 1) * SUBLANES
    t = pl.program_id(1)

    @pl.when(t == 0)
    def _():
        s_ref[...] = jnp.zeros_like(s_ref)
        prevg_ref[...] = jnp.zeros_like(prevg_ref)
        hc_ref[...] = jnp.zeros_like(hc_ref)

    sh1, sc1 = mod_ref[:, :D_MODEL], mod_ref[:, D_MODEL:2 * D_MODEL]
    hb_ref[...] = (_rms(x_ref[...], npre_ref[...]) * (1.0 + sc1) + sh1).astype(BF16)

    hb = hb_ref[...]
    sub = lax.broadcasted_iota(jnp.int32, (SUBLANES, D_RNN), 0)
    half = DK // 2
    cos, sin = rope_ref[:, 0:half], rope_ref[:, half:2 * half]
    cosk, sink = rope_ref[:, 2 * half:3 * half], rope_ref[:, 3 * half:4 * half]
    st = {}

    def xr_proj():
        xr = _dot(hb, _wb(wcat_ref, c0=OFF_XR, c1=OFF_XR + D_RNN))
        xr_ref[halo:halo + tm, :] = xr
        for kk in range(1, CONV_W):
            r0 = (CONV_W - 1 - kk) * SUBLANES
            cur = xr[(ng - kk) * SUBLANES:(ng - kk + 1) * SUBLANES, :]
            prv = prevg_ref[r0:r0 + SUBLANES, :]
            xr_ref[r0:r0 + SUBLANES, :] = pltpu.roll(jnp.where(sub == SUBLANES - 1, prv, cur), 1, 0)
        prevg_ref[...] = xr[tm - halo:, :]
        for kk in range(1, CONV_W):
            r1 = (ng - kk) * SUBLANES + SUBLANES - 1
            convnew_ref[CONV_W - 1 - kk:CONV_W - kk, :] = xr[r1:r1 + 1, :]

    def lru_conv():
        cw = convw_ref[...]
        xconv = convb_ref[...]
        for j in range(CONV_W):
            r0 = halo - (CONV_W - 1 - j) * SUBLANES
            xconv = xconv + xr_ref[r0:r0 + tm, :] * cw[j:j + 1, :]
        st['xconv'] = xconv

    def lru_gate_proj():
        st['gpre'] = _lru_gate_pre(st['xconv'], wrg_ref)

    def lru_coef():
        st['a'], st['b'] = _lru_coeffs(st.pop('xconv'), st.pop('gpre'), bra_ref[...], brx_ref[...], lru_ref[...])

    def lru_scan():
        a, b = st['a'], st['b']
        ca, cb = a[0:SUBLANES, :], b[0:SUBLANES, :]
        cas, cbs = [ca], [cb]
        for gi in range(1, ng):
            ag = a[gi * SUBLANES:(gi + 1) * SUBLANES, :]
            cb = ag * cb + b[gi * SUBLANES:(gi + 1) * SUBLANES, :]
            ca = ag * ca
            cas.append(ca)
            cbs.append(cb)
        cin = jnp.where(sub == 0, hc_ref[SUBLANES - 1:SUBLANES, :], 0.0)
        for s in range(SUBLANES - 1):
            cin = jnp.where(sub == s + 1, pltpu.roll(ca * cin + cb, 1, 0), cin)
        seg_end = ca * cin + cb
        hc_ref[...] = seg_end
        hlast_ref[...] = seg_end[SUBLANES - 1:SUBLANES, :]
        st['hseq'] = jnp.concatenate([cas[gi] * cin + cbs[gi] for gi in range(ng)], axis=0)

    def gr_proj():
        st['gr'] = _dot(hb, _wb(wcat_ref, c0=OFF_GR, c1=OFF_GR + D_RNN))

    def lru_y():
        st['rnn_y'] = (st.pop('hseq') * jax.nn.gelu(st.pop('gr'), approximate=True)).astype(BF16)

    def lru_out():
        st['br_rnn'] = _dot(st.pop('rnn_y'), _wb(wbr_rnn_ref))

    def gate_proj():
        st['gate_pre'] = _dot(hb, _wb(wmg_ref))

    def gate_act():
        st['gates'] = _sigmoid(st.pop('gate_pre') + bmg_ref[...])

    def head_proj(hh):
        q = _dot(hb, _wb(wcat_ref, c0=OFF_Q + hh * DK, c1=OFF_Q + (hh + 1) * DK))
        k = _dot(hb, _wb(wcat_ref, c0=OFF_K + hh * DK, c1=OFF_K + (hh + 1) * DK))
        vb = _dot(hb, _wb(wcat_ref, c0=OFF_V + hh * DV, c1=OFF_V + (hh + 1) * DV)).astype(BF16)
        g = _dot(hb, _wb(wcat_ref, c0=OFF_G + hh * DV, c1=OFF_G + (hh + 1) * DV))
        st['proj', hh] = (q, k, vb, g)

    def head_rope(hh):
        q, k, vb, g = st.pop(('proj', hh))
        kr = _rope(k, cosk, sink)
        kdec = kdec_ref[hh]
        kdb = (kr * jnp.concatenate([kdec, kdec], axis=1)).astype(BF16)
        st['rope', hh] = (_rope(q, cos, sin).astype(BF16), kr.astype(BF16), kdb, vb, g)

    def head_qk(hh):
        qb, kb, kdb, vb, g = st.pop(('rope', hh))
        scores = _dot_nt(qb, kb)
        cross = _dot(qb, s_ref[hh].astype(BF16))
        st['qk', hh] = (scores, cross, vb, g)
        st['kv', hh] = (kdb, vb)

    def head_state(hh):
        kdb, vb = st.pop(('kv', hh))
        s_ref[hh] = gpow_ref[hh] * s_ref[hh] + _dot_tn(kdb, vb)

    def head_decay(hh):
        scores, cross, vb, g = st.pop(('qk', hh))
        st['dec', hh] = ((scores * dec_ref[hh]).astype(BF16), cross, vb, g)

    def head_pv(hh):
        sb, cross, vb, g = st.pop(('dec', hh))
        st['pv', hh] = (_dot(sb, vb), cross, g)

    def head_norm(hh):
        intra, cross, g = st.pop(('pv', hh))
        qdec = qdec_ref[hh]
        on = _group_norm(intra + cross * jnp.concatenate([qdec] * (DV // LANES), axis=1))
        st['ry', hh] = (on * gnw_ref[:, hh * DV:(hh + 1) * DV] * _silu(g)).astype(BF16)

    def head_out(hh):
        part = _dot(st.pop(('ry', hh)), _wb(wbr_ret_ref, hh * DV, (hh + 1) * DV))
        st['br_ret'] = part if hh == 0 else st['br_ret'] + part

    heads = range(H_RET)
    order = (
        [xr_proj, (head_proj, 0), lru_conv, lru_gate_proj, (head_proj, 1), (head_proj, 2), lru_coef, (head_proj, 3)]
        + [(head_rope, h) for h in heads] + [gr_proj] + [(head_qk, h) for h in heads]
        + [gate_proj] + [(head_decay, h) for h in heads] + [lru_scan]
        + [(head_pv, h) for h in heads] + [(head_state, h) for h in heads] + [lru_y, lru_out]
        + [(head_norm, h) for h in heads] + [(head_out, h) for h in heads] + [gate_act]
    )
    for stage in order:
        if isinstance(stage, tuple):
            stage[0](stage[1])
        else:
            stage()

    gates = st['gates']
    ga, gb = gates[:, :D_MODEL], gates[:, D_MODEL:]
    mixed = _dot((ga * st['br_ret'] + gb * st['br_rnn']).astype(BF16), _wb(wout_ref))
    x1_ref[...] = x_ref[...] + mod_ref[:, 2 * D_MODEL:] * _rms(mixed, npost_ref[...])


def _const_spec(shape):
    nd = len(shape)
    return pl.BlockSpec(shape, lambda *_: (0,) * nd, pipeline_mode=pl.Buffered(1))


def _mix_prompt_call(x, mod3, gpow, rope_tab, dec, qdec, kdec, p):
    nb, seq, _ = x.shape
    tm = PROMPT_TM
    nt = seq // tm
    in_specs = [
        pl.BlockSpec(memory_space=pltpu.SMEM),
        pl.BlockSpec((None, tm, D_MODEL), lambda b, t: (b, t, 0)),
        pl.BlockSpec((None, 1, 3 * D_MODEL), lambda b, t: (b, 0, 0)),
        _const_spec((1, D_MODEL)),
        _const_spec((D_MODEL, OFF_MG)),
        _const_spec((D_MODEL, 2 * D_MODEL)),
        _const_spec((1, 2 * D_MODEL)),
        pl.BlockSpec((tm, 4 * (DK // 2)), lambda b, t: (t, 0)),
        _const_spec((H_RET, tm, tm)),
        _const_spec((H_RET, tm, LANES)),
        _const_spec((H_RET, tm, LANES)),
        _const_spec((1, D_V)),
        _const_spec((D_V, D_MODEL)),
        _const_spec((CONV_W, D_RNN)),
        _const_spec((1, D_RNN)),
        _const_spec((N_RNN_BLOCKS, RNN_BLOCK, 2 * RNN_BLOCK)),
        _const_spec((1, D_RNN)),
        _const_spec((1, D_RNN)),
        _const_spec((1, D_RNN)),
        _const_spec((D_RNN, D_MODEL)),
        _const_spec((D_MODEL, D_MODEL)),
        _const_spec((1, D_MODEL)),
    ]
    out_specs = [
        pl.BlockSpec((None, tm, D_MODEL), lambda b, t: (b, t, 0)),
        pl.BlockSpec((None, H_RET, DK, DV), lambda b, t: (b, 0, 0, 0)),
        pl.BlockSpec((None, 1, D_RNN), lambda b, t: (b, 0, 0)),
        pl.BlockSpec((None, CONV_W - 1, D_RNN), lambda b, t: (b, 0, 0)),
    ]
    out_shape = [
        jax.ShapeDtypeStruct((nb, seq, D_MODEL), F32),
        jax.ShapeDtypeStruct((nb, H_RET, DK, DV), F32),
        jax.ShapeDtypeStruct((nb, 1, D_RNN), F32),
        jax.ShapeDtypeStruct((nb, CONV_W - 1, D_RNN), F32),
    ]
    halo = (CONV_W - 1) * SUBLANES
    scratch = [
        pltpu.VMEM((tm, D_MODEL), BF16),
        pltpu.VMEM((halo + tm, D_RNN), F32),
        pltpu.VMEM((halo, D_RNN), F32),
        pltpu.VMEM((SUBLANES, D_RNN), F32),
    ]
    return pl.pallas_call(
        _mix_prompt_kernel,
        grid=(nb, nt),
        in_specs=in_specs,
        out_specs=out_specs,
        out_shape=out_shape,
        scratch_shapes=scratch,
        compiler_params=pltpu.CompilerParams(
            dimension_semantics=("arbitrary", "arbitrary"),
            vmem_limit_bytes=VMEM_BYTES_V7X - 4 * 1024 * 1024),
    )(gpow, x, mod3, p['npre'], p['win'], p['wmg'], p['bmg'], rope_tab, dec, qdec, kdec, p['gnw'], p['wbr_ret'],
      p['convw'], p['convb'], p['wrg'], p['bra'], p['brx'], p['lru'], p['wbr_rnn'], p['wout'], p['npost'])


def _ffn_pre(x_ref, mod_ref, npre_ref):
    m = mod_ref[...]
    sh2, sc2 = m[:, :D_MODEL], m[:, D_MODEL:2 * D_MODEL]
    return (_rms(x_ref[...], npre_ref[...]) * (1.0 + sc2) + sh2).astype(BF16)


def _ffn_hidden(h2, acc, w1_ref, w2_ref, chunk_ids):
    for j in chunk_ids:
        c0 = j * FF_CHUNK
        fg = _dot(h2, _wb(w1_ref, c0=c0, c1=c0 + FF_CHUNK))
        fu = _dot(h2, _wb(w1_ref, c0=D_FF + c0, c1=D_FF + c0 + FF_CHUNK))
        part = _dot((_silu(fg) * fu).astype(BF16), _wb(w2_ref, c0, c0 + FF_CHUNK))
        acc = part if acc is None else acc + part
    return acc


def _ffn_post(x_ref, mod_ref, acc, npost_ref, o_ref):
    o_ref[...] = x_ref[...] + mod_ref[:, 2 * D_MODEL:] * _rms(acc, npost_ref[...])


def _ffn_kernel(x_ref, mod_ref, npre_ref, w1_ref, w2_ref, npost_ref, o_ref):
    h2 = _ffn_pre(x_ref, mod_ref, npre_ref)
    acc = _ffn_hidden(h2, None, w1_ref, w2_ref, range(D_FF // FF_CHUNK))
    _ffn_post(x_ref, mod_ref, acc, npost_ref, o_ref)


def _ffn_call(x2d, mod, mod_rows_per_tile, seq_tiles, p):
    rows = x2d.shape[0]
    tm = FFN_TM
    if mod_rows_per_tile == 1:
        mod_spec = pl.BlockSpec((None, 1, 3 * D_MODEL), lambda i: (i // seq_tiles, 0, 1))
    else:
        mod_spec = pl.BlockSpec((tm, 3 * D_MODEL), lambda i: (i, 1))
    return pl.pallas_call(
        _ffn_kernel,
        grid=(rows // tm,),
        in_specs=[
            pl.BlockSpec((tm, D_MODEL), lambda i: (i, 0)),
            mod_spec,
            _const_spec((1, D_MODEL)),
            _const_spec((D_MODEL, 2 * D_FF)),
            _const_spec((D_FF, D_MODEL)),
            _const_spec((1, D_MODEL)),
        ],
        out_specs=pl.BlockSpec((tm, D_MODEL), lambda i: (i, 0)),
        out_shape=jax.ShapeDtypeStruct((rows, D_MODEL), F32),
        compiler_params=pltpu.CompilerParams(
            dimension_semantics=("arbitrary",),
            vmem_limit_bytes=48 * 1024 * 1024),
    )(x2d, mod, p['npre_ffn'], p['wffn_in'], p['wffn_out'], p['npost_ffn'])


def _proj_sample_kernel(n_in_tiles, x_ref, mod_ref, npre_ref, win_ref, wmg_ref, o_ref):
    m = mod_ref[...]
    sh1, sc1 = m[:, :D_MODEL], m[:, D_MODEL:2 * D_MODEL]
    h = (_rms(x_ref[...], npre_ref[...]) * (1.0 + sc1) + sh1).astype(BF16)
    j = pl.program_id(0)

    @pl.when(j < n_in_tiles)
    def _():
        o_ref[...] = _dot(h, _wb(win_ref))

    @pl.when(j >= n_in_tiles)
    def _():
        o_ref[...] = _dot(h, _wb(wmg_ref))


def _proj_sample_call(x2d, mod_s, p):
    rows = x2d.shape[0]
    tn = D_MODEL
    n_in = OFF_MG // tn
    return pl.pallas_call(
        functools.partial(_proj_sample_kernel, n_in),
        grid=(N_CAT // tn,),
        in_specs=[
            pl.BlockSpec((rows, D_MODEL), lambda j: (0, 0)),
            pl.BlockSpec((rows, 3 * D_MODEL), lambda j: (0, 0)),
            pl.BlockSpec((1, D_MODEL), lambda j: (0, 0)),
            pl.BlockSpec((D_MODEL, tn), lambda j: (0, jnp.minimum(j, n_in - 1))),
            pl.BlockSpec((D_MODEL, tn), lambda j: (0, jnp.maximum(j - n_in, 0))),
        ],
        out_specs=pl.BlockSpec((rows, tn), lambda j: (0, j)),
        out_shape=jax.ShapeDtypeStruct((rows, N_CAT), F32),
        compiler_params=pltpu.CompilerParams(
            dimension_semantics=("arbitrary",),
            vmem_limit_bytes=48 * 1024 * 1024),
    )(x2d, mod_s, p['npre'], p['win'], p['wmg'])


def _ret_sample_kernel(gpow_ref, qkv_ref, s_ref, rope_ref, dec_ref, qdec_ref, kdec_ref, o_ref, snew_ref):
    half = DK // 2
    cos, sin = rope_ref[:, 0:half], rope_ref[:, half:2 * half]
    cosk, sink = rope_ref[:, 2 * half:3 * half], rope_ref[:, 3 * half:4 * half]
    nseq = RET_SAMPLE_BB
    tlen = SUBLANES // nseq
    row = lax.broadcasted_iota(jnp.int32, (SUBLANES, LANES), 0)
    for hh in range(H_RET):
        q = _rope(qkv_ref[:, OFF_Q + hh * DK:OFF_Q + (hh + 1) * DK], cos, sin)
        k = _rope(qkv_ref[:, OFF_K + hh * DK:OFF_K + (hh + 1) * DK], cosk, sink)
        v = qkv_ref[:, OFF_V + hh * DV:OFF_V + (hh + 1) * DV]
        kdec = kdec_ref[hh]
        kd = k * jnp.concatenate([kdec, kdec], axis=1)
        scores = _dot_nt(q, k) * dec_ref[hh]
        intra = _dot(scores, v)
        qb = q.astype(BF16)
        cross = jnp.zeros((SUBLANES, DV), F32)
        for bi in range(nseq):
            s_old = s_ref[bi, hh]
            in_seq = (row >= bi * tlen) & (row < (bi + 1) * tlen)
            cr = _dot(qb, s_old.astype(BF16))
            cross = jnp.where(jnp.concatenate([in_seq] * (DV // LANES), axis=1), cr, cross)
            kd_b = jnp.where(jnp.concatenate([in_seq] * (DK // LANES), axis=1), kd, 0.0)
            snew_ref[bi, hh] = gpow_ref[hh] * s_old + _dot_tn(kd_b, v)
        qdec = qdec_ref[hh]
        o_ref[:, hh * DV:(hh + 1) * DV] = intra + cross * jnp.concatenate([qdec] * (DV // LANES), axis=1)


def _ffn_ret_kernel(sub_steps, gpow_ref, x_ref, mod_ref, npre_ref, w1_ref, w2_ref, npost_ref,
                    qkv_ref, s_ref, rope_ref, dec_ref, qdec_ref, kdec_ref,
                    y_ref, o_ref, snew_ref, h2_ref, acc_ref):
    n_chunks = D_FF // FF_CHUNK
    bounds = [n_chunks * s // sub_steps for s in range(sub_steps + 1)]
    for s in range(sub_steps):
        @pl.when(pl.program_id(1) == s)
        def _(s=s):
            if s == 0:
                h2_ref[...] = _ffn_pre(x_ref, mod_ref, npre_ref)
            acc = _ffn_hidden(h2_ref[...], None if s == 0 else acc_ref[...], w1_ref, w2_ref,
                              range(bounds[s], bounds[s + 1]))
            if s == sub_steps - 1:
                _ffn_post(x_ref, mod_ref, acc, npost_ref, y_ref)
            else:
                acc_ref[...] = acc

    _ret_sample_kernel(gpow_ref, qkv_ref, s_ref, rope_ref, dec_ref, qdec_ref, kdec_ref, o_ref, snew_ref)


def _ffn_ret_call(x2d, mod_p, seq_tiles, proj_s, state, gpow, rope8, dec8, qdec8, kdec8, p):
    rows = x2d.shape[0]
    tm = FFN_TM
    nb = state.shape[0]
    bb = RET_SAMPLE_BB
    sub_steps = nb // bb // (rows // tm)
    assert sub_steps * (rows // tm) * bb == nb
    srow = lambda i, j: i * sub_steps + j
    return pl.pallas_call(
        functools.partial(_ffn_ret_kernel, sub_steps),
        grid=(rows // tm, sub_steps),
        in_specs=[
            pl.BlockSpec(memory_space=pltpu.SMEM),
            pl.BlockSpec((tm, D_MODEL), lambda i, j: (i, 0)),
            pl.BlockSpec((None, 1, 3 * D_MODEL), lambda i, j: (i // seq_tiles, 0, 1)),
            _const_spec((1, D_MODEL)),
            _const_spec((D_MODEL, 2 * D_FF)),
            _const_spec((D_FF, D_MODEL)),
            _const_spec((1, D_MODEL)),
            pl.BlockSpec((SUBLANES, OFF_G), lambda i, j: (srow(i, j), 0)),
            pl.BlockSpec((bb, H_RET, DK, DV), lambda i, j: (srow(i, j), 0, 0, 0)),
            _const_spec((SUBLANES, 4 * (DK // 2))),
            _const_spec((H_RET, SUBLANES, SUBLANES)),
            _const_spec((H_RET, SUBLANES, LANES)),
            _const_spec((H_RET, SUBLANES, LANES)),
        ],
        out_specs=[
            pl.BlockSpec((tm, D_MODEL), lambda i, j: (i, 0)),
            pl.BlockSpec((SUBLANES, D_V), lambda i, j: (srow(i, j), 0)),
            pl.BlockSpec((bb, H_RET, DK, DV), lambda i, j: (srow(i, j), 0, 0, 0)),
        ],
        out_shape=[
            jax.ShapeDtypeStruct((rows, D_MODEL), F32),
            jax.ShapeDtypeStruct((proj_s.shape[0], D_V), F32),
            jax.ShapeDtypeStruct(state.shape, F32),
        ],
        scratch_shapes=[pltpu.VMEM((tm, D_MODEL), BF16), pltpu.VMEM((tm, D_MODEL), F32)],
        compiler_params=pltpu.CompilerParams(
            dimension_semantics=("arbitrary", "arbitrary"),
            vmem_limit_bytes=56 * 1024 * 1024),
    )(gpow, x2d, mod_p, p['npre_ffn'], p['wffn_in'], p['wffn_out'], p['npost_ffn'],
      proj_s, state, rope8, dec8, qdec8, kdec8)


def _mix_sample_kernel(x_ref, mod_ref, proj_ref, o_ref, zs_ref,
                       bmg_ref, gnw_ref, wbr_ret_ref, convw_ref, convb_ref, wrg_ref, bra_ref, brx_ref,
                       lru_ref, wbr_rnn_ref, wout_ref, npost_ref,
                       x1_ref, hseq_ref,
                       xr_ref, sa_ref, sb_ref, zsc_ref):
    tm = SAMPLE_TM
    tlen = 4
    x = x_ref[...]
    g1 = mod_ref[:, 2 * D_MODEL:]

    br_ret = jnp.zeros((tm, D_MODEL), F32)
    for hh in range(H_RET):
        on = _group_norm(o_ref[:, hh * DV:(hh + 1) * DV])
        g = proj_ref[:, OFF_G + hh * DV:OFF_G + (hh + 1) * DV]
        ry = (on * gnw_ref[:, hh * DV:(hh + 1) * DV] * _silu(g)).astype(BF16)
        br_ret = br_ret + _dot(ry, _wb(wbr_ret_ref, hh * DV, (hh + 1) * DV))

    zsc_ref[0:tm, :] = zs_ref[...]
    zsc_ref[tm:tm + SUBLANES, :] = jnp.zeros((SUBLANES, D_RNN), F32)
    tpos = lax.broadcasted_iota(jnp.int32, (tm, D_RNN), 0) & (tlen - 1)
    xr = proj_ref[:, OFF_XR:OFF_XR + D_RNN]
    xr_ref[0:SUBLANES, :] = jnp.zeros((SUBLANES, D_RNN), F32)
    xr_ref[SUBLANES:SUBLANES + tm, :] = xr
    cw = convw_ref[...]
    xconv = convb_ref[...]
    for j in range(CONV_W - 1):
        sft = CONV_W - 1 - j
        shifted = jnp.where(tpos >= sft, xr_ref[SUBLANES - sft:SUBLANES - sft + tm, :], 0.0)
        carried = jnp.where(tpos < sft, zsc_ref[CONV_W - 1 - sft:CONV_W - 1 - sft + tm, :], 0.0)
        xconv = xconv + (shifted + carried) * cw[j:j + 1, :]
    xconv = xconv + xr * cw[CONV_W - 1:CONV_W, :]

    a, b = _lru_coeffs(xconv, _lru_gate_pre(xconv, wrg_ref), bra_ref[...], brx_ref[...], lru_ref[...])
    b = jnp.where(tpos == 0, b + a * zsc_ref[CONV_W - 1:CONV_W - 1 + tm, :], b)
    sa_ref[0:SUBLANES, :] = jnp.zeros((SUBLANES, D_RNN), F32)
    sb_ref[0:SUBLANES, :] = jnp.zeros((SUBLANES, D_RNN), F32)
    for s in (1, 2):
        sa_ref[SUBLANES:SUBLANES + tm, :] = a
        sb_ref[SUBLANES:SUBLANES + tm, :] = b
        keep = tpos >= s
        ap = jnp.where(keep, sa_ref[SUBLANES - s:SUBLANES - s + tm, :], 1.0)
        bp = jnp.where(keep, sb_ref[SUBLANES - s:SUBLANES - s + tm, :], 0.0)
        b = a * bp + b
        a = a * ap
    hseq_ref[...] = b

    gr = proj_ref[:, OFF_GR:OFF_GR + D_RNN]
    rnn_y_b = (b * jax.nn.gelu(gr, approximate=True)).astype(BF16)
    gate_pre = proj_ref[:, OFF_MG:OFF_MG + 2 * D_MODEL]
    x1_ref[...] = _mix_tail(x, g1, br_ret, rnn_y_b, gate_pre, bmg_ref[...], wbr_rnn_ref, wout_ref,
                            npost_ref[...])


def _mix_sample_call(x2d, mod_s, proj_s, o_s, zs, p):
    rows = x2d.shape[0]
    tm = SAMPLE_TM
    row_spec = lambda w: pl.BlockSpec((tm, w), lambda i: (i, 0))
    return pl.pallas_call(
        _mix_sample_kernel,
        grid=(rows // tm,),
        in_specs=[
            row_spec(D_MODEL), row_spec(3 * D_MODEL), row_spec(N_CAT), row_spec(D_V),
            row_spec(D_RNN),
            _const_spec((1, 2 * D_MODEL)),
            _const_spec((1, D_V)),
            _const_spec((D_V, D_MODEL)),
            _const_spec((CONV_W, D_RNN)),
            _const_spec((1, D_RNN)),
            _const_spec((N_RNN_BLOCKS, RNN_BLOCK, 2 * RNN_BLOCK)),
            _const_spec((1, D_RNN)),
            _const_spec((1, D_RNN)),
            _const_spec((1, D_RNN)),
            _const_spec((D_RNN, D_MODEL)),
            _const_spec((D_MODEL, D_MODEL)),
            _const_spec((1, D_MODEL)),
        ],
        out_specs=[row_spec(D_MODEL), row_spec(D_RNN)],
        out_shape=[
            jax.ShapeDtypeStruct((rows, D_MODEL), F32),
            jax.ShapeDtypeStruct((rows, D_RNN), F32),
        ],
        scratch_shapes=[pltpu.VMEM((tm + SUBLANES, D_RNN), F32)] * 4,
        compiler_params=pltpu.CompilerParams(
            dimension_semantics=("arbitrary",),
            vmem_limit_bytes=56 * 1024 * 1024),
    )(x2d, mod_s, proj_s, o_s, zs, p['bmg'], p['gnw'], p['wbr_ret'], p['convw'],
      p['convb'], p['wrg'], p['bra'], p['brx'], p['lru'], p['wbr_rnn'], p['wout'], p['npost'])


def _rope_table(pos):
    half = DK // 2
    inv = ROPE_BASE ** (-jnp.arange(half, dtype=F32) / half)
    ang = pos.astype(F32)[:, None] * inv[None, :]
    cos, sin = jnp.cos(ang), jnp.sin(ang)
    ks = DK ** -0.5
    return jnp.concatenate([cos, sin, cos * ks, sin * ks], axis=1)


def _decay_tables(tpos, same_seq, chunk):
    lg = jnp.log(1.0 - 2.0 ** (-5.0 - jnp.arange(H_RET, dtype=F32)))
    idx = tpos.astype(F32)
    diff = idx[:, None] - idx[None, :]
    causal = (diff >= 0) & same_seq
    dec = jnp.where(causal[None], jnp.exp(jnp.where(causal, diff, 0.0)[None] * lg[:, None, None]), 0.0)
    qdec = jnp.exp((idx + 1.0)[None, :] * lg[:, None])
    kdec = jnp.exp((chunk - 1.0 - idx)[None, :] * lg[:, None])
    rep = lambda a: jnp.broadcast_to(a[:, :, None], a.shape + (LANES,))
    gpow = jnp.exp(chunk * lg)
    return dec, rep(qdec), rep(kdec), gpow


def kernel(x_prompt, x_sample, state_ret, state_rnn_h, state_rnn_conv, c_prompt, c_sample,
           w_ada, b_ada, norm_pre_mix, norm_post_mix, norm_pre_ffn, norm_post_ffn,
           w_in, ret_gn_w, w_br_ret, conv_w, conv_b, w_rg_a, b_rg_a, w_rg_x, b_rg_x,
           lru_param, w_br_rnn, w_mgate, b_mgate, w_out, w_ffn_in, w_ffn_out):
    depth = w_in.shape[0]
    assert depth == 1, "single layer step"
    nb, seq, _ = x_prompt.shape
    nsb, sseq, _ = x_sample.shape
    assert seq % PROMPT_TM == 0 and sseq * RET_SAMPLE_BB == SUBLANES and sseq == CONV_W
    l = 0
    row = lambda a: a[l][None, :]
    p = dict(
        npre=row(norm_pre_mix), npost=row(norm_post_mix), npre_ffn=row(norm_pre_ffn), npost_ffn=row(norm_post_ffn),
        win=_pack_rows(w_in[l]), wmg=_pack_rows(w_mgate[l]),
        bmg=row(b_mgate), gnw=row(ret_gn_w), wbr_ret=_pack_rows(w_br_ret[l]),
        convw=conv_w[l], convb=row(conv_b),
        wrg=_pack_rows(jnp.concatenate([w_rg_a[l], w_rg_x[l]], axis=2)),
        bra=row(b_rg_a), brx=row(b_rg_x), lru=row(lru_param),
        wbr_rnn=_pack_rows(w_br_rnn[l]), wout=_pack_rows(w_out[l]),
        wffn_in=_pack_rows(w_ffn_in[l]), wffn_out=_pack_rows(w_ffn_out[l]),
    )

    rows_s = nsb * sseq
    c_all = jnp.concatenate([jnp.repeat(c_sample, sseq, axis=0), c_prompt], axis=0)
    mod_all = _mod_call(c_all, w_ada[l], row(b_ada))
    mod_p = mod_all[rows_s:].reshape(nb, 1, 6 * D_MODEL)
    mod_s = mod_all

    tm = PROMPT_TM
    ng = tm // SUBLANES
    r = jnp.arange(tm)
    tpos = (r % SUBLANES) * ng + r // SUBLANES
    pos_p = (jnp.arange(seq // tm)[:, None] * tm + tpos[None, :]).reshape(seq).astype(jnp.int32)
    rope_p = _rope_table(pos_p)
    dec, qdec, kdec, gpow = _decay_tables(tpos, jnp.ones((tm, tm), bool), float(tm))
    interleave = lambda a: a.reshape(nb, seq // tm, SUBLANES, ng, D_MODEL).swapaxes(2, 3).reshape(nb, seq, D_MODEL)
    restore = lambda a: a.reshape(nb, seq // tm, ng, SUBLANES, D_MODEL).swapaxes(2, 3).reshape(nb, seq, D_MODEL)
    x1p, ret_p, hlast_p, conv_p = _mix_prompt_call(interleave(x_prompt), mod_p, gpow, rope_p, dec, qdec, kdec, p)

    xs2d = x_sample.reshape(rows_s, D_MODEL)
    proj_s = _proj_sample_call(xs2d, mod_s, p)
    r8 = jnp.arange(SUBLANES)
    rope_s = _rope_table(PAST_LEN + (r8 % sseq).astype(jnp.int32))
    same = (r8[:, None] // sseq) == (r8[None, :] // sseq)
    dec8, qdec8, kdec8, gpow_s = _decay_tables(r8 % sseq, same, float(sseq))
    yp2d, o_s, ret_s = _ffn_ret_call(x1p.reshape(nb * seq, D_MODEL), mod_p, seq // FFN_TM, proj_s, state_ret[l],
                                     gpow_s, rope_s, dec8, qdec8, kdec8, p)
    yp = restore(yp2d.reshape(nb, seq, D_MODEL))
    zs = jnp.concatenate([state_rnn_conv[l], state_rnn_h[l][:, None, :]], axis=1).reshape(rows_s, D_RNN)
    x1s, hseq_s = _mix_sample_call(xs2d, mod_s, proj_s, o_s, zs, p)
    ys = _ffn_call(x1s, mod_s, FFN_TM, 1, p).reshape(nsb, sseq, D_MODEL)
    hlast_s = hseq_s.reshape(nsb, sseq, D_RNN)[:, sseq - 1]
    conv_s = proj_s[:, OFF_XR:OFF_XR + D_RNN].reshape(nsb, sseq, D_RNN)[:, sseq - (CONV_W - 1):]

    return (yp, ys, ret_p[None], ret_s[None], hlast_p.reshape(nb, D_RNN)[None], hlast_s[None],
            conv_p[None], conv_s[None])
```

```python
import functools

import jax
import jax.numpy as jnp
from jax import lax
from jax.experimental import pallas as pl
from jax.experimental.pallas import tpu as pltpu

F32 = jnp.float32
BF16 = jnp.bfloat16

D_MODEL = 1024
H_RET = 4
DK = D_MODEL // H_RET
DV = 2 * DK
D_QK = H_RET * DK
D_V = H_RET * DV
D_RNN = 1536
RNN_BLOCK = 128
N_RNN_BLOCKS = D_RNN // RNN_BLOCK
CONV_W = 4
LRU_C = 8.0
D_FF = 2816
ROPE_BASE = 10000.0
GN_EPS = 1e-5
RMS_EPS = 1e-6
PAST_LEN = 16384

OFF_Q = 0
OFF_K = OFF_Q + D_QK
OFF_V = OFF_K + D_QK
OFF_G = OFF_V + D_V
OFF_XR = OFF_G + D_V
OFF_GR = OFF_XR + D_RNN
OFF_MG = OFF_GR + D_RNN
N_CAT = OFF_MG + 2 * D_MODEL

SUBLANES = 8
LANES = 128
MXU_DIM = 256
VMEM_BYTES_V7X = 64 * 1024 * 1024

PROMPT_TM = 256
FFN_TM = 512
FF_CHUNK = MXU_DIM
SAMPLE_TM = 128
RET_SAMPLE_BB = 2


def _dot(a, b):
    return jnp.dot(a, b, preferred_element_type=F32)


def _dot_nt(a, b):
    return lax.dot_general(a, b, (((1,), (1,)), ((), ())), preferred_element_type=F32)


def _dot_tn(a, b):
    return lax.dot_general(a, b, (((0,), (0,)), ((), ())), preferred_element_type=F32)


def _wb(ref, k0=None, k1=None, c0=None, c1=None):
    rs = slice(None) if k0 is None else slice(k0, k1)
    cs = slice(None) if c0 is None else slice(c0, c1)
    return ref[rs, cs]


def _pack_rows(w):
    return w.astype(BF16)


def _rms(x, w):
    ms = jnp.mean(x * x, axis=-1, keepdims=True)
    return x * lax.rsqrt(ms + RMS_EPS) * w


def _sigmoid(x):
    return 0.5 * jnp.tanh(0.5 * x) + 0.5


def _silu(x):
    return x * _sigmoid(x)


def _rope(x, cos, sin):
    half = DK // 2
    x1, x2 = x[:, :half], x[:, half:]
    return jnp.concatenate([x1 * cos - x2 * sin, x1 * sin + x2 * cos], axis=1)


def _group_norm(o):
    mu = jnp.mean(o, axis=-1, keepdims=True)
    d = o - mu
    var = jnp.mean(d * d, axis=-1, keepdims=True)
    return d * lax.rsqrt(var + GN_EPS)


def _lru_gate_pre(xconv, wrg_ref):
    xcb = xconv.astype(BF16)
    return [_dot(xcb[:, n * RNN_BLOCK:(n + 1) * RNN_BLOCK], wrg_ref[n]) for n in range(N_RNN_BLOCKS)]


def _lru_coeffs(xconv, pre, b_a, b_x, lru):
    ra = jnp.concatenate([p[:, :RNN_BLOCK] for p in pre], axis=1) + b_a
    ri = jnp.concatenate([p[:, RNN_BLOCK:] for p in pre], axis=1) + b_x
    r = _sigmoid(ra)
    i = _sigmoid(ri)
    z = -lru
    sp = jnp.maximum(z, 0.0) + jnp.log(1.0 + jnp.exp(-jnp.abs(z)))
    log_a = -LRU_C * r * sp
    a = jnp.exp(log_a)
    beta = jnp.sqrt(-jnp.tanh(log_a) * (a * a + 1.0))
    return a, beta * (i * xconv)


def _mix_tail(x, g1, br_ret, rnn_y_b, gate_pre, b_mg, wbr_rnn_ref, wout_ref, npost):
    br_rnn = _dot(rnn_y_b, _wb(wbr_rnn_ref))
    gates = _sigmoid(gate_pre + b_mg)
    ga, gb = gates[:, :D_MODEL], gates[:, D_MODEL:]
    mixed = _dot((ga * br_ret + gb * br_rnn).astype(BF16), _wb(wout_ref))
    return x + g1 * _rms(mixed, npost)


def _mod_kernel(c_ref, w_ref, b_ref, o_ref):
    a = _silu(c_ref[...]).astype(BF16)
    o_ref[...] = _dot(a, w_ref[...].astype(BF16)) + b_ref[...]


def _mod_call(c_all, w_ada_f32, b_ada):
    rows = c_all.shape[0]
    tn = D_MODEL
    return pl.pallas_call(
        _mod_kernel,
        grid=(6 * D_MODEL // tn,),
        in_specs=[
            pl.BlockSpec((rows, D_MODEL), lambda j: (0, 0)),
            pl.BlockSpec((D_MODEL, tn), lambda j: (0, j)),
            pl.BlockSpec((1, tn), lambda j: (0, j)),
        ],
        out_specs=pl.BlockSpec((rows, tn), lambda j: (0, j)),
        out_shape=jax.ShapeDtypeStruct((rows, 6 * D_MODEL), F32),
        compiler_params=pltpu.CompilerParams(dimension_semantics=("arbitrary",)),
    )(c_all, w_ada_f32, b_ada)


def _mix_prompt_kernel(gpow_ref, x_ref, mod_ref, npre_ref, wcat_ref, wmg_ref, bmg_ref, rope_ref,
                       dec_ref, qdec_ref, kdec_ref, gnw_ref, wbr_ret_ref, convw_ref, convb_ref,
                       wrg_ref, bra_ref, brx_ref, lru_ref, wbr_rnn_ref, wout_ref, npost_ref,
                       wf1_ref, wf2_ref,
                       x1_ref, s_ref, hlast_ref, convnew_ref, wf1b_ref, wf2b_ref,
                       xr_ref, prevg_ref, hc_ref):
    tm = PROMPT_TM
    ng = tm // SUBLANES
    halo = (CONV_W - 1) * SUBLANES
    t = pl.program_id(1)

    @pl.when(t == 0)
    def _():
        s_ref[...] = jnp.zeros_like(s_ref)
        prevg_ref[...] = jnp.zeros_like(prevg_ref)
        hc_ref[...] = jnp.zeros_like(hc_ref)

    sh1, sc1 = mod_ref[:, :D_MODEL], mod_ref[:, D_MODEL:2 * D_MODEL]
    hb = (_rms(x_ref[...], npre_ref[...]) * (1.0 + sc1) + sh1).astype(BF16)
    sub = lax.broadcasted_iota(jnp.int32, (SUBLANES, D_RNN), 0)
    half = DK // 2
    cos, sin = rope_ref[:, 0:half], rope_ref[:, half:2 * half]
    cosk, sink = rope_ref[:, 2 * half:3 * half], rope_ref[:, 3 * half:4 * half]
    st = {}

    def xr_proj():
        xr = _dot(hb, _wb(wcat_ref, c0=OFF_XR, c1=OFF_XR + D_RNN))
        xr_ref[halo:halo + tm, :] = xr
        for kk in range(1, CONV_W):
            r0 = (CONV_W - 1 - kk) * SUBLANES
            cur = xr[(ng - kk) * SUBLANES:(ng - kk + 1) * SUBLANES, :]
            prv = prevg_ref[r0:r0 + SUBLANES, :]
            xr_ref[r0:r0 + SUBLANES, :] = pltpu.roll(jnp.where(sub == SUBLANES - 1, prv, cur), 1, 0)
        prevg_ref[...] = xr[tm - halo:, :]
        for kk in range(1, CONV_W):
            r1 = (ng - kk) * SUBLANES + SUBLANES - 1
            convnew_ref[CONV_W - 1 - kk:CONV_W - kk, :] = xr[r1:r1 + 1, :]

    def lru_conv():
        cw = convw_ref[...]
        xconv = convb_ref[...]
        for j in range(CONV_W):
            r0 = halo - (CONV_W - 1 - j) * SUBLANES
            xconv = xconv + xr_ref[r0:r0 + tm, :] * cw[j:j + 1, :]
        st['xconv'] = xconv

    def lru_gate_proj():
        st['gpre'] = _lru_gate_pre(st['xconv'], wrg_ref)

    def lru_coef():
        st['a'], st['b'] = _lru_coeffs(st.pop('xconv'), st.pop('gpre'), bra_ref[...], brx_ref[...], lru_ref[...])

    def lru_scan():
        a, b = st['a'], st['b']
        ca, cb = a[0:SUBLANES, :], b[0:SUBLANES, :]
        cas, cbs = [ca], [cb]
        for gi in range(1, ng):
            ag = a[gi * SUBLANES:(gi + 1) * SUBLANES, :]
            cb = ag * cb + b[gi * SUBLANES:(gi + 1) * SUBLANES, :]
            ca = ag * ca
            cas.append(ca)
            cbs.append(cb)
        cin = jnp.where(sub == 0, hc_ref[SUBLANES - 1:SUBLANES, :], 0.0)
        for s in range(SUBLANES - 1):
            cin = jnp.where(sub == s + 1, pltpu.roll(ca * cin + cb, 1, 0), cin)
        seg_end = ca * cin + cb
        hc_ref[...] = seg_end
        hlast_ref[...] = seg_end[SUBLANES - 1:SUBLANES, :]
        st['hseq'] = jnp.concatenate([cas[gi] * cin + cbs[gi] for gi in range(ng)], axis=0)

    def gr_proj():
        st['gr'] = _dot(hb, _wb(wcat_ref, c0=OFF_GR, c1=OFF_GR + D_RNN))

    def lru_y():
        st['rnn_y'] = (st.pop('hseq') * jax.nn.gelu(st.pop('gr'), approximate=True)).astype(BF16)

    def lru_out():
        st['br_rnn'] = _dot(st.pop('rnn_y'), _wb(wbr_rnn_ref))

    def gate_proj():
        st['gate_pre'] = _dot(hb, _wb(wmg_ref))

    def gate_act():
        st['gates'] = _sigmoid(st.pop('gate_pre') + bmg_ref[...])

    def head_proj(hh):
        q = _dot(hb, _wb(wcat_ref, c0=OFF_Q + hh * DK, c1=OFF_Q + (hh + 1) * DK))
        k = _dot(hb, _wb(wcat_ref, c0=OFF_K + hh * DK, c1=OFF_K + (hh + 1) * DK))
        vb = _dot(hb, _wb(wcat_ref, c0=OFF_V + hh * DV, c1=OFF_V + (hh + 1) * DV)).astype(BF16)
        g = _dot(hb, _wb(wcat_ref, c0=OFF_G + hh * DV, c1=OFF_G + (hh + 1) * DV))
        st['proj', hh] = (q, k, vb, g)

    def head_rope(hh):
        q, k, vb, g = st.pop(('proj', hh))
        kr = _rope(k, cosk, sink)
        kdec = kdec_ref[hh]
        kdb = (kr * jnp.concatenate([kdec, kdec], axis=1)).astype(BF16)
        st['rope', hh] = (_rope(q, cos, sin).astype(BF16), kr.astype(BF16), kdb, vb, g)

    def head_qk(hh):
        qb, kb, kdb, vb, g = st.pop(('rope', hh))
        scores = _dot_nt(qb, kb)
        cross = _dot(qb, s_ref[hh].astype(BF16))
        st['qk', hh] = (scores, cross, vb, g)
        st['kv', hh] = (kdb, vb)

    def head_state(hh):
        kdb, vb = st.pop(('kv', hh))
        s_ref[hh] = gpow_ref[hh] * s_ref[hh] + _dot_tn(kdb, vb)

    def head_decay(hh):
        scores, cross, vb, g = st.pop(('qk', hh))
        st['dec', hh] = ((scores * dec_ref[hh]).astype(BF16), cross, vb, g)

    def head_pv(hh):
        sb, cross, vb, g = st.pop(('dec', hh))
        st['pv', hh] = (_dot(sb, vb), cross, g)

    def head_norm(hh):
        intra, cross, g = st.pop(('pv', hh))
        qdec = qdec_ref[hh]
        on = _group_norm(intra + cross * jnp.concatenate([qdec] * (DV // LANES), axis=1))
        st['ry', hh] = (on * gnw_ref[:, hh * DV:(hh + 1) * DV] * _silu(g)).astype(BF16)

    def head_out(hh):
        part = _dot(st.pop(('ry', hh)), _wb(wbr_ret_ref, hh * DV, (hh + 1) * DV))
        st['br_ret'] = part if hh == 0 else st['br_ret'] + part

    heads = range(H_RET)
    order = (
        [xr_proj, (head_proj, 0), lru_conv, lru_gate_proj, (head_proj, 1), (head_proj, 2), lru_coef, (head_proj, 3)]
        + [(head_rope, h) for h in heads] + [gr_proj] + [(head_qk, h) for h in heads]
        + [gate_proj] + [(head_decay, h) for h in heads] + [lru_scan]
        + [(head_pv, h) for h in heads] + [(head_state, h) for h in heads] + [lru_y, lru_out]
        + [(head_norm, h) for h in heads] + [(head_out, h) for h in heads] + [gate_act]
    )
    for stage in order:
        if isinstance(stage, tuple):
            stage[0](stage[1])
        else:
            stage()

    gates = st['gates']
    ga, gb = gates[:, :D_MODEL], gates[:, D_MODEL:]
    mixed = _dot((ga * st['br_ret'] + gb * st['br_rnn']).astype(BF16), _wb(wout_ref))
    x1_ref[...] = x_ref[...] + mod_ref[:, 2 * D_MODEL:] * _rms(mixed, npost_ref[...])

    wf1b_ref[...] = wf1_ref[...].astype(BF16)
    wf2b_ref[...] = wf2_ref[...].astype(BF16)


def _const_spec(shape):
    nd = len(shape)
    return pl.BlockSpec(shape, lambda *_: (0,) * nd, pipeline_mode=pl.Buffered(1))


def _mix_prompt_call(x, mod3, gpow, rope_tab, dec, qdec, kdec, p, wf1, wf2):
    nb, seq, _ = x.shape
    tm = PROMPT_TM
    nt = seq // tm
    steps = nb * nt
    bf16_rows = 2 * SUBLANES
    r1 = wf1.shape[0] // steps
    assert r1 * steps == wf1.shape[0] and r1 % bf16_rows == 0
    rep2 = 1
    while wf2.shape[0] % (steps // rep2) or (wf2.shape[0] // (steps // rep2)) % bf16_rows:
        rep2 *= 2
    r2 = wf2.shape[0] // (steps // rep2)
    wf1_spec = pl.BlockSpec((r1, wf1.shape[1]), lambda b, t: (b * nt + t, 0))
    wf2_spec = pl.BlockSpec((r2, wf2.shape[1]), lambda b, t: ((b * nt + t) // rep2, 0))
    in_specs = [
        pl.BlockSpec(memory_space=pltpu.SMEM),
        pl.BlockSpec((None, tm, D_MODEL), lambda b, t: (b, t, 0)),
        pl.BlockSpec((None, 1, 3 * D_MODEL), lambda b, t: (b, 0, 0)),
        _const_spec((1, D_MODEL)),
        _const_spec((D_MODEL, OFF_MG)),
        _const_spec((D_MODEL, 2 * D_MODEL)),
        _const_spec((1, 2 * D_MODEL)),
        pl.BlockSpec((tm, 4 * (DK // 2)), lambda b, t: (t, 0)),
        _const_spec((H_RET, tm, tm)),
        _const_spec((H_RET, tm, LANES)),
        _const_spec((H_RET, tm, LANES)),
        _const_spec((1, D_V)),
        _const_spec((D_V, D_MODEL)),
        _const_spec((CONV_W, D_RNN)),
        _const_spec((1, D_RNN)),
        _const_spec((N_RNN_BLOCKS, RNN_BLOCK, 2 * RNN_BLOCK)),
        _const_spec((1, D_RNN)),
        _const_spec((1, D_RNN)),
        _const_spec((1, D_RNN)),
        _const_spec((D_RNN, D_MODEL)),
        _const_spec((D_MODEL, D_MODEL)),
        _const_spec((1, D_MODEL)),
        wf1_spec,
        wf2_spec,
    ]
    out_specs = [
        pl.BlockSpec((None, tm, D_MODEL), lambda b, t: (b, t, 0)),
        pl.BlockSpec((None, H_RET, DK, DV), lambda b, t: (b, 0, 0, 0)),
        pl.BlockSpec((None, 1, D_RNN), lambda b, t: (b, 0, 0)),
        pl.BlockSpec((None, CONV_W - 1, D_RNN), lambda b, t: (b, 0, 0)),
        wf1_spec,
        wf2_spec,
    ]
    out_shape = [
        jax.ShapeDtypeStruct((nb, seq, D_MODEL), F32),
        jax.ShapeDtypeStruct((nb, H_RET, DK, DV), F32),
        jax.ShapeDtypeStruct((nb, 1, D_RNN), F32),
        jax.ShapeDtypeStruct((nb, CONV_W - 1, D_RNN), F32),
        jax.ShapeDtypeStruct(wf1.shape, BF16),
        jax.ShapeDtypeStruct(wf2.shape, BF16),
    ]
    halo = (CONV_W - 1) * SUBLANES
    scratch = [
        pltpu.VMEM((halo + tm, D_RNN), F32),
        pltpu.VMEM((halo, D_RNN), F32),
        pltpu.VMEM((SUBLANES, D_RNN), F32),
    ]
    return pl.pallas_call(
        _mix_prompt_kernel,
        grid=(nb, nt),
        in_specs=in_specs,
        out_specs=out_specs,
        out_shape=out_shape,
        scratch_shapes=scratch,
        compiler_params=pltpu.CompilerParams(
            dimension_semantics=("arbitrary", "arbitrary"),
            vmem_limit_bytes=VMEM_BYTES_V7X - 4 * 1024 * 1024),
    )(gpow, x, mod3, p['npre'], p['win'], p['wmg'], p['bmg'], rope_tab, dec, qdec, kdec, p['gnw'], p['wbr_ret'],
      p['convw'], p['convb'], p['wrg'], p['bra'], p['brx'], p['lru'], p['wbr_rnn'], p['wout'], p['npost'],
      wf1, wf2)


def _ffn_pre(x_ref, mod_ref, npre_ref):
    m = mod_ref[...]
    sh2, sc2 = m[:, :D_MODEL], m[:, D_MODEL:2 * D_MODEL]
    return (_rms(x_ref[...], npre_ref[...]) * (1.0 + sc2) + sh2).astype(BF16)


def _ffn_hidden(h2, acc, w1_ref, w2_ref, chunk_ids):
    for j in chunk_ids:
        c0 = j * FF_CHUNK
        fg = _dot(h2, _wb(w1_ref, c0=c0, c1=c0 + FF_CHUNK))
        fu = _dot(h2, _wb(w1_ref, c0=D_FF + c0, c1=D_FF + c0 + FF_CHUNK))
        part = _dot((_silu(fg) * fu).astype(BF16), _wb(w2_ref, c0, c0 + FF_CHUNK))
        acc = part if acc is None else acc + part
    return acc


def _ffn_post(x_ref, mod_ref, acc, npost_ref, o_ref):
    o_ref[...] = x_ref[...] + mod_ref[:, 2 * D_MODEL:] * _rms(acc, npost_ref[...])


def _ffn_kernel(x_ref, mod_ref, npre_ref, w1_ref, w2_ref, npost_ref, o_ref):
    h2 = _ffn_pre(x_ref, mod_ref, npre_ref)
    acc = _ffn_hidden(h2, None, w1_ref, w2_ref, range(D_FF // FF_CHUNK))
    _ffn_post(x_ref, mod_ref, acc, npost_ref, o_ref)


def _ffn_call(x2d, mod, mod_rows_per_tile, seq_tiles, p):
    rows = x2d.shape[0]
    tm = FFN_TM
    if mod_rows_per_tile == 1:
        mod_spec = pl.BlockSpec((None, 1, 3 * D_MODEL), lambda i: (i // seq_tiles, 0, 1))
    else:
        mod_spec = pl.BlockSpec((tm, 3 * D_MODEL), lambda i: (i, 1))
    return pl.pallas_call(
        _ffn_kernel,
        grid=(rows // tm,),
        in_specs=[
            pl.BlockSpec((tm, D_MODEL), lambda i: (i, 0)),
            mod_spec,
            _const_spec((1, D_MODEL)),
            _const_spec((D_MODEL, 2 * D_FF)),
            _const_spec((D_FF, D_MODEL)),
            _const_spec((1, D_MODEL)),
        ],
        out_specs=pl.BlockSpec((tm, D_MODEL), lambda i: (i, 0)),
        out_shape=jax.ShapeDtypeStruct((rows, D_MODEL), F32),
        compiler_params=pltpu.CompilerParams(
            dimension_semantics=("arbitrary",),
            vmem_limit_bytes=48 * 1024 * 1024),
    )(x2d, mod, p['npre_ffn'], p['wffn_in'], p['wffn_out'], p['npost_ffn'])


def _proj_sample_kernel(n_in_tiles, x_ref, mod_ref, npre_ref, win_ref, wmg_ref, o_ref):
    m = mod_ref[...]
    sh1, sc1 = m[:, :D_MODEL], m[:, D_MODEL:2 * D_MODEL]
    h = (_rms(x_ref[...], npre_ref[...]) * (1.0 + sc1) + sh1).astype(BF16)
    j = pl.program_id(0)

    @pl.when(j < n_in_tiles)
    def _():
        o_ref[...] = _dot(h, _wb(win_ref))

    @pl.when(j >= n_in_tiles)
    def _():
        o_ref[...] = _dot(h, _wb(wmg_ref))


def _proj_sample_call(x2d, mod_s, p):
    rows = x2d.shape[0]
    tn = D_MODEL
    n_in = OFF_MG // tn
    return pl.pallas_call(
        functools.partial(_proj_sample_kernel, n_in),
        grid=(N_CAT // tn,),
        in_specs=[
            pl.BlockSpec((rows, D_MODEL), lambda j: (0, 0)),
            pl.BlockSpec((rows, 3 * D_MODEL), lambda j: (0, 0)),
            pl.BlockSpec((1, D_MODEL), lambda j: (0, 0)),
            pl.BlockSpec((D_MODEL, tn), lambda j: (0, jnp.minimum(j, n_in - 1))),
            pl.BlockSpec((D_MODEL, tn), lambda j: (0, jnp.maximum(j - n_in, 0))),
        ],
        out_specs=pl.BlockSpec((rows, tn), lambda j: (0, j)),
        out_shape=jax.ShapeDtypeStruct((rows, N_CAT), F32),
        compiler_params=pltpu.CompilerParams(
            dimension_semantics=("arbitrary",),
            vmem_limit_bytes=48 * 1024 * 1024),
    )(x2d, mod_s, p['npre'], p['win'], p['wmg'])


def _ret_sample_kernel(gpow_ref, qkv_ref, s_ref, rope_ref, dec_ref, qdec_ref, kdec_ref, o_ref, snew_ref):
    half = DK // 2
    cos, sin = rope_ref[:, 0:half], rope_ref[:, half:2 * half]
    cosk, sink = rope_ref[:, 2 * half:3 * half], rope_ref[:, 3 * half:4 * half]
    nseq = RET_SAMPLE_BB
    tlen = SUBLANES // nseq
    row = lax.broadcasted_iota(jnp.int32, (SUBLANES, LANES), 0)
    for hh in range(H_RET):
        q = _rope(qkv_ref[:, OFF_Q + hh * DK:OFF_Q + (hh + 1) * DK], cos, sin)
        k = _rope(qkv_ref[:, OFF_K + hh * DK:OFF_K + (hh + 1) * DK], cosk, sink)
        v = qkv_ref[:, OFF_V + hh * DV:OFF_V + (hh + 1) * DV]
        kdec = kdec_ref[hh]
        kd = k * jnp.concatenate([kdec, kdec], axis=1)
        scores = _dot_nt(q, k) * dec_ref[hh]
        intra = _dot(scores, v)
        qb = q.astype(BF16)
        cross = jnp.zeros((SUBLANES, DV), F32)
        for bi in range(nseq):
            s_old = s_ref[bi, hh]
            in_seq = (row >= bi * tlen) & (row < (bi + 1) * tlen)
            cr = _dot(qb, s_old.astype(BF16))
            cross = jnp.where(jnp.concatenate([in_seq] * (DV // LANES), axis=1), cr, cross)
            kd_b = jnp.where(jnp.concatenate([in_seq] * (DK // LANES), axis=1), kd, 0.0)
            snew_ref[bi, hh] = gpow_ref[hh] * s_old + _dot_tn(kd_b, v)
        qdec = qdec_ref[hh]
        o_ref[:, hh * DV:(hh + 1) * DV] = intra + cross * jnp.concatenate([qdec] * (DV // LANES), axis=1)


def _ffn_ret_kernel(sub_steps, gpow_ref, x_ref, mod_ref, npre_ref, w1_ref, w2_ref, npost_ref,
                    qkv_ref, s_ref, rope_ref, dec_ref, qdec_ref, kdec_ref,
                    y_ref, o_ref, snew_ref, h2_ref, acc_ref):
    n_chunks = D_FF // FF_CHUNK
    bounds = [n_chunks * s // sub_steps for s in range(sub_steps + 1)]
    for s in range(sub_steps):
        @pl.when(pl.program_id(1) == s)
        def _(s=s):
            if s == 0:
                h2_ref[...] = _ffn_pre(x_ref, mod_ref, npre_ref)
            acc = _ffn_hidden(h2_ref[...], None if s == 0 else acc_ref[...], w1_ref, w2_ref,
                              range(bounds[s], bounds[s + 1]))
            if s == sub_steps - 1:
                _ffn_post(x_ref, mod_ref, acc, npost_ref, y_ref)
            else:
                acc_ref[...] = acc

    _ret_sample_kernel(gpow_ref, qkv_ref, s_ref, rope_ref, dec_ref, qdec_ref, kdec_ref, o_ref, snew_ref)


def _ffn_ret_call(x2d, mod_p, seq_tiles, proj_s, state, gpow, rope8, dec8, qdec8, kdec8, p):
    rows = x2d.shape[0]
    tm = FFN_TM
    nb = state.shape[0]
    bb = RET_SAMPLE_BB
    sub_steps = nb // bb // (rows // tm)
    assert sub_steps * (rows // tm) * bb == nb
    srow = lambda i, j: i * sub_steps + j
    return pl.pallas_call(
        functools.partial(_ffn_ret_kernel, sub_steps),
        grid=(rows // tm, sub_steps),
        in_specs=[
            pl.BlockSpec(memory_space=pltpu.SMEM),
            pl.BlockSpec((tm, D_MODEL), lambda i, j: (i, 0)),
            pl.BlockSpec((None, 1, 3 * D_MODEL), lambda i, j: (i // seq_tiles, 0, 1)),
            _const_spec((1, D_MODEL)),
            _const_spec((D_MODEL, 2 * D_FF)),
            _const_spec((D_FF, D_MODEL)),
            _const_spec((1, D_MODEL)),
            pl.BlockSpec((SUBLANES, OFF_G), lambda i, j: (srow(i, j), 0)),
            pl.BlockSpec((bb, H_RET, DK, DV), lambda i, j: (srow(i, j), 0, 0, 0)),
            _const_spec((SUBLANES, 4 * (DK // 2))),
            _const_spec((H_RET, SUBLANES, SUBLANES)),
            _const_spec((H_RET, SUBLANES, LANES)),
            _const_spec((H_RET, SUBLANES, LANES)),
        ],
        out_specs=[
            pl.BlockSpec((tm, D_MODEL), lambda i, j: (i, 0)),
            pl.BlockSpec((SUBLANES, D_V), lambda i, j: (srow(i, j), 0)),
            pl.BlockSpec((bb, H_RET, DK, DV), lambda i, j: (srow(i, j), 0, 0, 0)),
        ],
        out_shape=[
            jax.ShapeDtypeStruct((rows, D_MODEL), F32),
            jax.ShapeDtypeStruct((proj_s.shape[0], D_V), F32),
            jax.ShapeDtypeStruct(state.shape, F32),
        ],
        scratch_shapes=[pltpu.VMEM((tm, D_MODEL), BF16), pltpu.VMEM((tm, D_MODEL), F32)],
        compiler_params=pltpu.CompilerParams(
            dimension_semantics=("arbitrary", "arbitrary"),
            vmem_limit_bytes=56 * 1024 * 1024),
    )(gpow, x2d, mod_p, p['npre_ffn'], p['wffn_in'], p['wffn_out'], p['npost_ffn'],
      proj_s, state, rope8, dec8, qdec8, kdec8)


def _mix_sample_kernel(x_ref, mod_ref, proj_ref, o_ref, zs_ref,
                       bmg_ref, gnw_ref, wbr_ret_ref, convw_ref, convb_ref, wrg_ref, bra_ref, brx_ref,
                       lru_ref, wbr_rnn_ref, wout_ref, npost_ref,
                       x1_ref, hseq_ref,
                       xr_ref, sa_ref, sb_ref, zsc_ref):
    tm = SAMPLE_TM
    tlen = 4
    x = x_ref[...]
    g1 = mod_ref[:, 2 * D_MODEL:]

    br_ret = jnp.zeros((tm, D_MODEL), F32)
    for hh in range(H_RET):
        on = _group_norm(o_ref[:, hh * DV:(hh + 1) * DV])
        g = proj_ref[:, OFF_G + hh * DV:OFF_G + (hh + 1) * DV]
        ry = (on * gnw_ref[:, hh * DV:(hh + 1) * DV] * _silu(g)).astype(BF16)
        br_ret = br_ret + _dot(ry, _wb(wbr_ret_ref, hh * DV, (hh + 1) * DV))

    zsc_ref[0:tm, :] = zs_ref[...]
    zsc_ref[tm:tm + SUBLANES, :] = jnp.zeros((SUBLANES, D_RNN), F32)
    tpos = lax.broadcasted_iota(jnp.int32, (tm, D_RNN), 0) & (tlen - 1)
    xr = proj_ref[:, OFF_XR:OFF_XR + D_RNN]
    xr_ref[0:SUBLANES, :] = jnp.zeros((SUBLANES, D_RNN), F32)
    xr_ref[SUBLANES:SUBLANES + tm, :] = xr
    cw = convw_ref[...]
    xconv = convb_ref[...]
    for j in range(CONV_W - 1):
        sft = CONV_W - 1 - j
        shifted = jnp.where(tpos >= sft, xr_ref[SUBLANES - sft:SUBLANES - sft + tm, :], 0.0)
        carried = jnp.where(tpos < sft, zsc_ref[CONV_W - 1 - sft:CONV_W - 1 - sft + tm, :], 0.0)
        xconv = xconv + (shifted + carried) * cw[j:j + 1, :]
    xconv = xconv + xr * cw[CONV_W - 1:CONV_W, :]

    a, b = _lru_coeffs(xconv, _lru_gate_pre(xconv, wrg_ref), bra_ref[...], brx_ref[...], lru_ref[...])
    b = jnp.where(tpos == 0, b + a * zsc_ref[CONV_W - 1:CONV_W - 1 + tm, :], b)
    sa_ref[0:SUBLANES, :] = jnp.zeros((SUBLANES, D_RNN), F32)
    sb_ref[0:SUBLANES, :] = jnp.zeros((SUBLANES, D_RNN), F32)
    for s in (1, 2):
        sa_ref[SUBLANES:SUBLANES + tm, :] = a
        sb_ref[SUBLANES:SUBLANES + tm, :] = b
        keep = tpos >= s
        ap = jnp.where(keep, sa_ref[SUBLANES - s:SUBLANES - s + tm, :], 1.0)
        bp = jnp.where(keep, sb_ref[SUBLANES - s:SUBLANES - s + tm, :], 0.0)
        b = a * bp + b
        a = a * ap
    hseq_ref[...] = b

    gr = proj_ref[:, OFF_GR:OFF_GR + D_RNN]
    rnn_y_b = (b * jax.nn.gelu(gr, approximate=True)).astype(BF16)
    gate_pre = proj_ref[:, OFF_MG:OFF_MG + 2 * D_MODEL]
    x1_ref[...] = _mix_tail(x, g1, br_ret, rnn_y_b, gate_pre, bmg_ref[...], wbr_rnn_ref, wout_ref,
                            npost_ref[...])


def _mix_sample_call(x2d, mod_s, proj_s, o_s, zs, p):
    rows = x2d.shape[0]
    tm = SAMPLE_TM
    row_spec = lambda w: pl.BlockSpec((tm, w), lambda i: (i, 0))
    return pl.pallas_call(
        _mix_sample_kernel,
        grid=(rows // tm,),
        in_specs=[
            row_spec(D_MODEL), row_spec(3 * D_MODEL), row_spec(N_CAT), row_spec(D_V),
            row_spec(D_RNN),
            _const_spec((1, 2 * D_MODEL)),
            _const_spec((1, D_V)),
            _const_spec((D_V, D_MODEL)),
            _const_spec((CONV_W, D_RNN)),
            _const_spec((1, D_RNN)),
            _const_spec((N_RNN_BLOCKS, RNN_BLOCK, 2 * RNN_BLOCK)),
            _const_spec((1, D_RNN)),
            _const_spec((1, D_RNN)),
            _const_spec((1, D_RNN)),
            _const_spec((D_RNN, D_MODEL)),
            _const_spec((D_MODEL, D_MODEL)),
            _const_spec((1, D_MODEL)),
        ],
        out_specs=[row_spec(D_MODEL), row_spec(D_RNN)],
        out_shape=[
            jax.ShapeDtypeStruct((rows, D_MODEL), F32),
            jax.ShapeDtypeStruct((rows, D_RNN), F32),
        ],
        scratch_shapes=[pltpu.VMEM((tm + SUBLANES, D_RNN), F32)] * 4,
        compiler_params=pltpu.CompilerParams(
            dimension_semantics=("arbitrary",),
            vmem_limit_bytes=56 * 1024 * 1024),
    )(x2d, mod_s, proj_s, o_s, zs, p['bmg'], p['gnw'], p['wbr_ret'], p['convw'],
      p['convb'], p['wrg'], p['bra'], p['brx'], p['lru'], p['wbr_rnn'], p['wout'], p['npost'])


def _rope_table(pos):
    half = DK // 2
    inv = ROPE_BASE ** (-jnp.arange(half, dtype=F32) / half)
    ang = pos.astype(F32)[:, None] * inv[None, :]
    cos, sin = jnp.cos(ang), jnp.sin(ang)
    ks = DK ** -0.5
    return jnp.concatenate([cos, sin, cos * ks, sin * ks], axis=1)


def _decay_tables(tpos, same_seq, chunk):
    lg = jnp.log(1.0 - 2.0 ** (-5.0 - jnp.arange(H_RET, dtype=F32)))
    idx = tpos.astype(F32)
    diff = idx[:, None] - idx[None, :]
    causal = (diff >= 0) & same_seq
    dec = jnp.where(causal[None], jnp.exp(jnp.where(causal, diff, 0.0)[None] * lg[:, None, None]), 0.0)
    qdec = jnp.exp((idx + 1.0)[None, :] * lg[:, None])
    kdec = jnp.exp((chunk - 1.0 - idx)[None, :] * lg[:, None])
    rep = lambda a: jnp.broadcast_to(a[:, :, None], a.shape + (LANES,))
    gpow = jnp.exp(chunk * lg)
    return dec, rep(qdec), rep(kdec), gpow


def kernel(x_prompt, x_sample, state_ret, state_rnn_h, state_rnn_conv, c_prompt, c_sample,
           w_ada, b_ada, norm_pre_mix, norm_post_mix, norm_pre_ffn, norm_post_ffn,
           w_in, ret_gn_w, w_br_ret, conv_w, conv_b, w_rg_a, b_rg_a, w_rg_x, b_rg_x,
           lru_param, w_br_rnn, w_mgate, b_mgate, w_out, w_ffn_in, w_ffn_out):
    depth = w_in.shape[0]
    assert depth == 1, "single layer step"
    nb, seq, _ = x_prompt.shape
    nsb, sseq, _ = x_sample.shape
    assert seq % PROMPT_TM == 0 and sseq * RET_SAMPLE_BB == SUBLANES and sseq == CONV_W
    l = 0
    row = lambda a: a[l][None, :]
    p = dict(
        npre=row(norm_pre_mix), npost=row(norm_post_mix), npre_ffn=row(norm_pre_ffn), npost_ffn=row(norm_post_ffn),
        win=_pack_rows(w_in[l]), wmg=_pack_rows(w_mgate[l]),
        bmg=row(b_mgate), gnw=row(ret_gn_w), wbr_ret=_pack_rows(w_br_ret[l]),
        convw=conv_w[l], convb=row(conv_b),
        wrg=_pack_rows(jnp.concatenate([w_rg_a[l], w_rg_x[l]], axis=2)),
        bra=row(b_rg_a), brx=row(b_rg_x), lru=row(lru_param),
        wbr_rnn=_pack_rows(w_br_rnn[l]), wout=_pack_rows(w_out[l]),
    )

    rows_s = nsb * sseq
    c_all = jnp.concatenate([jnp.repeat(c_sample, sseq, axis=0), c_prompt], axis=0)
    mod_all = _mod_call(c_all, w_ada[l], row(b_ada))
    mod_p = mod_all[rows_s:].reshape(nb, 1, 6 * D_MODEL)
    mod_s = mod_all

    tm = PROMPT_TM
    ng = tm // SUBLANES
    r = jnp.arange(tm)
    tpos = (r % SUBLANES) * ng + r // SUBLANES
    pos_p = (jnp.arange(seq // tm)[:, None] * tm + tpos[None, :]).reshape(seq).astype(jnp.int32)
    rope_p = _rope_table(pos_p)
    dec, qdec, kdec, gpow = _decay_tables(tpos, jnp.ones((tm, tm), bool), float(tm))
    interleave = lambda a: a.reshape(nb, seq // tm, SUBLANES, ng, D_MODEL).swapaxes(2, 3).reshape(nb, seq, D_MODEL)
    restore = lambda a: a.reshape(nb, seq // tm, ng, SUBLANES, D_MODEL).swapaxes(2, 3).reshape(nb, seq, D_MODEL)
    x1p, ret_p, hlast_p, conv_p, p['wffn_in'], p['wffn_out'] = _mix_prompt_call(
        interleave(x_prompt), mod_p, gpow, rope_p, dec, qdec, kdec, p, w_ffn_in[l], w_ffn_out[l])

    xs2d = x_sample.reshape(rows_s, D_MODEL)
    proj_s = _proj_sample_call(xs2d, mod_s, p)
    r8 = jnp.arange(SUBLANES)
    rope_s = _rope_table(PAST_LEN + (r8 % sseq).astype(jnp.int32))
    same = (r8[:, None] // sseq) == (r8[None, :] // sseq)
    dec8, qdec8, kdec8, gpow_s = _decay_tables(r8 % sseq, same, float(sseq))
    yp2d, o_s, ret_s = _ffn_ret_call(x1p.reshape(nb * seq, D_MODEL), mod_p, seq // FFN_TM, proj_s, state_ret[l],
                                     gpow_s, rope_s, dec8, qdec8, kdec8, p)
    yp = restore(yp2d.reshape(nb, seq, D_MODEL))
    zs = jnp.concatenate([state_rnn_conv[l], state_rnn_h[l][:, None, :]], axis=1).reshape(rows_s, D_RNN)
    x1s, hseq_s = _mix_sample_call(xs2d, mod_s, proj_s, o_s, zs, p)
    ys = _ffn_call(x1s, mod_s, FFN_TM, 1, p).reshape(nsb, sseq, D_MODEL)
    hlast_s = hseq_s.reshape(nsb, sseq, D_RNN)[:, sseq - 1]
    conv_s = proj_s[:, OFF_XR:OFF_XR + D_RNN].reshape(nsb, sseq, D_RNN)[:, sseq - (CONV_W - 1):]

    return (yp, ys, ret_p[None], ret_s[None], hlast_p.reshape(nb, D_RNN)[None], hlast_s[None],
            conv_p[None], conv_s[None])
```

```python
import functools

import jax
import jax.numpy as jnp
from jax import lax
from jax.experimental import pallas as pl
from jax.experimental.pallas import tpu as pltpu

F32 = jnp.float32
BF16 = jnp.bfloat16

D_MODEL = 1024
H_RET = 4
DK = D_MODEL // H_RET
DV = 2 * DK
D_QK = H_RET * DK
D_V = H_RET * DV
D_RNN = 1536
RNN_BLOCK = 128
N_RNN_BLOCKS = D_RNN // RNN_BLOCK
CONV_W = 4
LRU_C = 8.0
D_FF = 2816
ROPE_BASE = 10000.0
GN_EPS = 1e-5
RMS_EPS = 1e-6
PAST_LEN = 16384

OFF_Q = 0
OFF_K = OFF_Q + D_QK
OFF_V = OFF_K + D_QK
OFF_G = OFF_V + D_V
OFF_XR = OFF_G + D_V
OFF_GR = OFF_XR + D_RNN
OFF_MG = OFF_GR + D_RNN
N_CAT = OFF_MG + 2 * D_MODEL

SUBLANES = 8
LANES = 128
MXU_DIM = 256
VMEM_BYTES_V7X = 64 * 1024 * 1024

PROMPT_TM = 256
FFN_TM = 512
FF_CHUNK = MXU_DIM
SAMPLE_TM = 128
RET_SAMPLE_BB = 2


def _dot(a, b):
    return jnp.dot(a, b, preferred_element_type=F32)


def _dot_nt(a, b):
    return lax.dot_general(a, b, (((1,), (1,)), ((), ())), preferred_element_type=F32)


def _dot_tn(a, b):
    return lax.dot_general(a, b, (((0,), (0,)), ((), ())), preferred_element_type=F32)


def _wb(ref, k0=None, k1=None, c0=None, c1=None):
    rs = slice(None) if k0 is None else slice(k0 // 2, k1 // 2)
    cs = slice(None) if c0 is None else slice(c0, c1)
    return pltpu.bitcast(ref[rs, cs], BF16)


def _to_words(w):
    return pltpu.bitcast(w.astype(BF16), jnp.uint32)


def _pack_kernel(w_ref, o_ref):
    o_ref[...] = _to_words(w_ref[...])


def _pack_rows(w, nb=None):
    k, n = w.shape
    nb = n if nb is None else nb
    return pl.pallas_call(
        _pack_kernel,
        grid=(n // nb,),
        in_specs=[pl.BlockSpec((k, nb), lambda j: (0, j))],
        out_specs=pl.BlockSpec((k // 2, nb), lambda j: (0, j)),
        out_shape=jax.ShapeDtypeStruct((k // 2, n), jnp.uint32),
        compiler_params=pltpu.CompilerParams(
            dimension_semantics=("arbitrary",),
            vmem_limit_bytes=40 * 1024 * 1024),
    )(w)


def _rms(x, w):
    ms = jnp.mean(x * x, axis=-1, keepdims=True)
    return x * lax.rsqrt(ms + RMS_EPS) * w


def _sigmoid(x):
    return 0.5 * jnp.tanh(0.5 * x) + 0.5


def _silu(x):
    return x * _sigmoid(x)


def _rope(x, cos, sin):
    half = DK // 2
    x1, x2 = x[:, :half], x[:, half:]
    return jnp.concatenate([x1 * cos - x2 * sin, x1 * sin + x2 * cos], axis=1)


def _group_norm(o):
    mu = jnp.mean(o, axis=-1, keepdims=True)
    d = o - mu
    var = jnp.mean(d * d, axis=-1, keepdims=True)
    return d * lax.rsqrt(var + GN_EPS)


def _lru_gate_pre(xcb, wrg_ref):
    return [_dot(xcb[:, n * RNN_BLOCK:(n + 1) * RNN_BLOCK], _wb(wrg_ref, n * RNN_BLOCK, (n + 1) * RNN_BLOCK))
            for n in range(N_RNN_BLOCKS)]


def _lru_coeffs(xconv, pre, b_a, b_x, lru):
    ra = jnp.concatenate([p[:, :RNN_BLOCK] for p in pre], axis=1) + b_a
    ri = jnp.concatenate([p[:, RNN_BLOCK:] for p in pre], axis=1) + b_x
    r = _sigmoid(ra)
    i = _sigmoid(ri)
    z = -lru
    sp = jnp.maximum(z, 0.0) + jnp.log(1.0 + jnp.exp(-jnp.abs(z)))
    log_a = -LRU_C * r * sp
    a = jnp.exp(log_a)
    beta = jnp.sqrt(-jnp.tanh(log_a) * (a * a + 1.0))
    return a, beta * (i * xconv)


def _mix_tail(x, g1, br_ret, rnn_y_b, gate_pre, b_mg, wbr_rnn_ref, wout_ref, npost):
    br_rnn = _dot(rnn_y_b, _wb(wbr_rnn_ref))
    gates = _sigmoid(gate_pre + b_mg)
    ga, gb = gates[:, :D_MODEL], gates[:, D_MODEL:]
    mixed = _dot((ga * br_ret + gb * br_rnn).astype(BF16), _wb(wout_ref))
    return x + g1 * _rms(mixed, npost)


def _mod_kernel(c_ref, w_ref, b_ref, o_ref):
    a = _silu(c_ref[...]).astype(BF16)
    o_ref[...] = _dot(a, w_ref[...].astype(BF16)) + b_ref[...]


def _mod_call(c_all, w_ada_f32, b_ada):
    rows = c_all.shape[0]
    tn = D_MODEL
    return pl.pallas_call(
        _mod_kernel,
        grid=(6 * D_MODEL // tn,),
        in_specs=[
            pl.BlockSpec((rows, D_MODEL), lambda j: (0, 0)),
            pl.BlockSpec((D_MODEL, tn), lambda j: (0, j)),
            pl.BlockSpec((1, tn), lambda j: (0, j)),
        ],
        out_specs=pl.BlockSpec((rows, tn), lambda j: (0, j)),
        out_shape=jax.ShapeDtypeStruct((rows, 6 * D_MODEL), F32),
        compiler_params=pltpu.CompilerParams(dimension_semantics=("arbitrary",)),
    )(c_all, w_ada_f32, b_ada)


def _mix_prompt_kernel(gpow_ref, x_ref, mod_ref, npre_ref, wcat_ref, wmg_ref, bmg_ref, rope_ref,
                       dec_ref, qdec_ref, kdec_ref, gnw_ref, wbr_ret_ref, convw_ref, convb_ref,
                       wrg_ref, bra_ref, brx_ref, lru_ref, wbr_rnn_ref, wout_ref, npost_ref,
                       wf1_ref, wf2_ref,
                       x1_ref, s_ref, hlast_ref, convnew_ref, wf1b_ref, wf2b_ref,
                       xr_ref, prevg_ref, hc_ref):
    tm = PROMPT_TM
    ng = tm // SUBLANES
    halo = (CONV_W - 1) * SUBLANES
    t = pl.program_id(1)

    @pl.when(t == 0)
    def _():
        s_ref[...] = jnp.zeros_like(s_ref)
        prevg_ref[...] = jnp.zeros_like(prevg_ref)
        hc_ref[...] = jnp.zeros_like(hc_ref)

    sh1, sc1 = mod_ref[:, :D_MODEL], mod_ref[:, D_MODEL:2 * D_MODEL]
    hb = (_rms(x_ref[...], npre_ref[...]) * (1.0 + sc1) + sh1).astype(BF16)
    sub = lax.broadcasted_iota(jnp.int32, (SUBLANES, D_RNN), 0)
    half = DK // 2
    cos, sin = rope_ref[:, 0:half], rope_ref[:, half:2 * half]
    cosk, sink = rope_ref[:, 2 * half:3 * half], rope_ref[:, 3 * half:4 * half]
    st = {}

    def xr_proj():
        xr = _dot(hb, _wb(wcat_ref, c0=OFF_XR, c1=OFF_XR + D_RNN))
        xr_ref[halo:halo + tm, :] = xr
        for kk in range(1, CONV_W):
            r0 = (CONV_W - 1 - kk) * SUBLANES
            cur = xr[(ng - kk) * SUBLANES:(ng - kk + 1) * SUBLANES, :]
            prv = prevg_ref[r0:r0 + SUBLANES, :]
            xr_ref[r0:r0 + SUBLANES, :] = pltpu.roll(jnp.where(sub == SUBLANES - 1, prv, cur), 1, 0)
        prevg_ref[...] = xr[tm - halo:, :]
        for kk in range(1, CONV_W):
            r1 = (ng - kk) * SUBLANES + SUBLANES - 1
            convnew_ref[CONV_W - 1 - kk:CONV_W - kk, :] = xr[r1:r1 + 1, :]

    def lru_conv():
        cw = convw_ref[...]
        xconv = convb_ref[...]
        for j in range(CONV_W):
            r0 = halo - (CONV_W - 1 - j) * SUBLANES
            xconv = xconv + xr_ref[r0:r0 + tm, :] * cw[j:j + 1, :]
        st['xconv'] = xconv

    def lru_gate_proj():
        st['gpre'] = _lru_gate_pre(st['xconv'].astype(BF16), wrg_ref)

    def lru_coef():
        st['a'], st['b'] = _lru_coeffs(st.pop('xconv'), st.pop('gpre'), bra_ref[...], brx_ref[...], lru_ref[...])

    def lru_scan():
        a, b = st['a'], st['b']
        ca, cb = a[0:SUBLANES, :], b[0:SUBLANES, :]
        cas, cbs = [ca], [cb]
        for gi in range(1, ng):
            ag = a[gi * SUBLANES:(gi + 1) * SUBLANES, :]
            cb = ag * cb + b[gi * SUBLANES:(gi + 1) * SUBLANES, :]
            ca = ag * ca
            cas.append(ca)
            cbs.append(cb)
        cin = jnp.where(sub == 0, hc_ref[SUBLANES - 1:SUBLANES, :], 0.0)
        for s in range(SUBLANES - 1):
            cin = jnp.where(sub == s + 1, pltpu.roll(ca * cin + cb, 1, 0), cin)
        seg_end = ca * cin + cb
        hc_ref[...] = seg_end
        hlast_ref[...] = seg_end[SUBLANES - 1:SUBLANES, :]
        st['hseq'] = jnp.concatenate([cas[gi] * cin + cbs[gi] for gi in range(ng)], axis=0)

    def gr_proj():
        st['gr'] = _dot(hb, _wb(wcat_ref, c0=OFF_GR, c1=OFF_GR + D_RNN))

    def lru_y():
        st['rnn_y'] = (st.pop('hseq') * jax.nn.gelu(st.pop('gr'), approximate=True)).astype(BF16)

    def lru_out():
        st['br_rnn'] = _dot(st.pop('rnn_y'), _wb(wbr_rnn_ref))

    def gate_proj():
        st['gate_pre'] = _dot(hb, _wb(wmg_ref))

    def gate_act():
        st['gates'] = _sigmoid(st.pop('gate_pre') + bmg_ref[...])

    def head_proj(hh):
        q = _dot(hb, _wb(wcat_ref, c0=OFF_Q + hh * DK, c1=OFF_Q + (hh + 1) * DK))
        k = _dot(hb, _wb(wcat_ref, c0=OFF_K + hh * DK, c1=OFF_K + (hh + 1) * DK))
        vb = _dot(hb, _wb(wcat_ref, c0=OFF_V + hh * DV, c1=OFF_V + (hh + 1) * DV)).astype(BF16)
        g = _dot(hb, _wb(wcat_ref, c0=OFF_G + hh * DV, c1=OFF_G + (hh + 1) * DV))
        st['proj', hh] = (q, k, vb, g)

    def head_rope(hh):
        q, k, vb, g = st.pop(('proj', hh))
        kr = _rope(k, cosk, sink)
        kdec = kdec_ref[hh]
        kdb = (kr * jnp.concatenate([kdec, kdec], axis=1)).astype(BF16)
        st['rope', hh] = (_rope(q, cos, sin).astype(BF16), kr.astype(BF16), kdb, vb, g)

    def head_qk(hh):
        qb, kb, kdb, vb, g = st.pop(('rope', hh))
        scores = _dot_nt(qb, kb)
        cross = _dot(qb, s_ref[hh].astype(BF16))
        st['qk', hh] = (scores, cross, vb, g)
        st['kv', hh] = (kdb, vb)

    def head_state(hh):
        kdb, vb = st.pop(('kv', hh))
        s_ref[hh] = gpow_ref[hh] * s_ref[hh] + _dot_tn(kdb, vb)

    def head_decay(hh):
        scores, cross, vb, g = st.pop(('qk', hh))
        st['dec', hh] = ((scores * dec_ref[hh]).astype(BF16), cross, vb, g)

    def head_pv(hh):
        sb, cross, vb, g = st.pop(('dec', hh))
        st['pv', hh] = (_dot(sb, vb), cross, g)

    def head_norm(hh):
        intra, cross, g = st.pop(('pv', hh))
        qdec = qdec_ref[hh]
        on = _group_norm(intra + cross * jnp.concatenate([qdec] * (DV // LANES), axis=1))
        st['ry', hh] = (on * gnw_ref[:, hh * DV:(hh + 1) * DV] * _silu(g)).astype(BF16)

    def head_out(hh):
        part = _dot(st.pop(('ry', hh)), _wb(wbr_ret_ref, hh * DV, (hh + 1) * DV))
        st['br_ret'] = part if hh == 0 else st['br_ret'] + part

    heads = range(H_RET)
    order = (
        [xr_proj, (head_proj, 0), lru_conv, lru_gate_proj, (head_proj, 1), (head_proj, 2), lru_coef, (head_proj, 3)]
        + [(head_rope, h) for h in heads] + [gr_proj] + [(head_qk, h) for h in heads]
        + [gate_proj] + [(head_decay, h) for h in heads] + [lru_scan]
        + [(head_pv, h) for h in heads] + [(head_state, h) for h in heads] + [lru_y, lru_out]
        + [(head_norm, h) for h in heads] + [(head_out, h) for h in heads] + [gate_act]
    )
    for stage in order:
        if isinstance(stage, tuple):
            stage[0](stage[1])
        else:
            stage()

    gates = st['gates']
    ga, gb = gates[:, :D_MODEL], gates[:, D_MODEL:]
    mixed = _dot((ga * st['br_ret'] + gb * st['br_rnn']).astype(BF16), _wb(wout_ref))
    x1_ref[...] = x_ref[...] + mod_ref[:, 2 * D_MODEL:] * _rms(mixed, npost_ref[...])

    wf1b_ref[...] = _to_words(wf1_ref[...])
    wf2b_ref[...] = _to_words(wf2_ref[...])


def _const_spec(shape):
    nd = len(shape)
    return pl.BlockSpec(shape, lambda *_: (0,) * nd, pipeline_mode=pl.Buffered(1))


def _mix_prompt_call(x, mod3, gpow, rope_tab, dec, qdec, kdec, p, wf1, wf2):
    nb, seq, _ = x.shape
    tm = PROMPT_TM
    nt = seq // tm
    steps = nb * nt
    bf16_rows = 2 * SUBLANES
    r1 = wf1.shape[0] // steps
    assert r1 * steps == wf1.shape[0] and r1 % bf16_rows == 0
    rep2 = 1
    while wf2.shape[0] % (steps // rep2) or (wf2.shape[0] // (steps // rep2)) % bf16_rows:
        rep2 *= 2
    r2 = wf2.shape[0] // (steps // rep2)
    wf1_spec = pl.BlockSpec((r1, wf1.shape[1]), lambda b, t: (b * nt + t, 0))
    wf2_spec = pl.BlockSpec((r2, wf2.shape[1]), lambda b, t: ((b * nt + t) // rep2, 0))
    in_specs = [
        pl.BlockSpec(memory_space=pltpu.SMEM),
        pl.BlockSpec((None, tm, D_MODEL), lambda b, t: (b, t, 0)),
        pl.BlockSpec((None, 1, 3 * D_MODEL), lambda b, t: (b, 0, 0)),
        _const_spec((1, D_MODEL)),
        _const_spec((D_MODEL // 2, OFF_MG)),
        _const_spec((D_MODEL // 2, 2 * D_MODEL)),
        _const_spec((1, 2 * D_MODEL)),
        pl.BlockSpec((tm, 4 * (DK // 2)), lambda b, t: (t, 0)),
        _const_spec((H_RET, tm, tm)),
        _const_spec((H_RET, tm, LANES)),
        _const_spec((H_RET, tm, LANES)),
        _const_spec((1, D_V)),
        _const_spec((D_V // 2, D_MODEL)),
        _const_spec((CONV_W, D_RNN)),
        _const_spec((1, D_RNN)),
        _const_spec((D_RNN // 2, 2 * RNN_BLOCK)),
        _const_spec((1, D_RNN)),
        _const_spec((1, D_RNN)),
        _const_spec((1, D_RNN)),
        _const_spec((D_RNN // 2, D_MODEL)),
        _const_spec((D_MODEL // 2, D_MODEL)),
        _const_spec((1, D_MODEL)),
        wf1_spec,
        wf2_spec,
    ]
    out_specs = [
        pl.BlockSpec((None, tm, D_MODEL), lambda b, t: (b, t, 0)),
        pl.BlockSpec((None, H_RET, DK, DV), lambda b, t: (b, 0, 0, 0)),
        pl.BlockSpec((None, 1, D_RNN), lambda b, t: (b, 0, 0)),
        pl.BlockSpec((None, CONV_W - 1, D_RNN), lambda b, t: (b, 0, 0)),
        pl.BlockSpec((r1 // 2, wf1.shape[1]), wf1_spec.index_map),
        pl.BlockSpec((r2 // 2, wf2.shape[1]), wf2_spec.index_map),
    ]
    out_shape = [
        jax.ShapeDtypeStruct((nb, seq, D_MODEL), F32),
        jax.ShapeDtypeStruct((nb, H_RET, DK, DV), F32),
        jax.ShapeDtypeStruct((nb, 1, D_RNN), F32),
        jax.ShapeDtypeStruct((nb, CONV_W - 1, D_RNN), F32),
        jax.ShapeDtypeStruct((wf1.shape[0] // 2, wf1.shape[1]), jnp.uint32),
        jax.ShapeDtypeStruct((wf2.shape[0] // 2, wf2.shape[1]), jnp.uint32),
    ]
    halo = (CONV_W - 1) * SUBLANES
    scratch = [
        pltpu.VMEM((halo + tm, D_RNN), F32),
        pltpu.VMEM((halo, D_RNN), F32),
        pltpu.VMEM((SUBLANES, D_RNN), F32),
    ]
    return pl.pallas_call(
        _mix_prompt_kernel,
        grid=(nb, nt),
        in_specs=in_specs,
        out_specs=out_specs,
        out_shape=out_shape,
        scratch_shapes=scratch,
        compiler_params=pltpu.CompilerParams(
            dimension_semantics=("arbitrary", "arbitrary"),
            vmem_limit_bytes=VMEM_BYTES_V7X - 4 * 1024 * 1024),
    )(gpow, x, mod3, p['npre'], p['win'], p['wmg'], p['bmg'], rope_tab, dec, qdec, kdec, p['gnw'], p['wbr_ret'],
      p['convw'], p['convb'], p['wrg'], p['bra'], p['brx'], p['lru'], p['wbr_rnn'], p['wout'], p['npost'],
      wf1, wf2)


def _ffn_pre(x_ref, mod_ref, npre_ref):
    m = mod_ref[...]
    sh2, sc2 = m[:, :D_MODEL], m[:, D_MODEL:2 * D_MODEL]
    return (_rms(x_ref[...], npre_ref[...]) * (1.0 + sc2) + sh2).astype(BF16)


def _ffn_hidden(h2, acc, w1_ref, w2_ref, chunk_ids):
    for j in chunk_ids:
        c0 = j * FF_CHUNK
        fg = _dot(h2, _wb(w1_ref, c0=c0, c1=c0 + FF_CHUNK))
        fu = _dot(h2, _wb(w1_ref, c0=D_FF + c0, c1=D_FF + c0 + FF_CHUNK))
        part = _dot((_silu(fg) * fu).astype(BF16), _wb(w2_ref, c0, c0 + FF_CHUNK))
        acc = part if acc is None else acc + part
    return acc


def _ffn_post(x_ref, mod_ref, acc, npost_ref, o_ref):
    o_ref[...] = x_ref[...] + mod_ref[:, 2 * D_MODEL:] * _rms(acc, npost_ref[...])


def _ffn_kernel(x_ref, mod_ref, npre_ref, w1_ref, w2_ref, npost_ref, o_ref):
    h2 = _ffn_pre(x_ref, mod_ref, npre_ref)
    acc = _ffn_hidden(h2, None, w1_ref, w2_ref, range(D_FF // FF_CHUNK))
    _ffn_post(x_ref, mod_ref, acc, npost_ref, o_ref)


def _ffn_call(x2d, mod, mod_rows_per_tile, seq_tiles, p):
    rows = x2d.shape[0]
    tm = FFN_TM
    if mod_rows_per_tile == 1:
        mod_spec = pl.BlockSpec((None, 1, 3 * D_MODEL), lambda i: (i // seq_tiles, 0, 1))
    else:
        mod_spec = pl.BlockSpec((tm, 3 * D_MODEL), lambda i: (i, 1))
    return pl.pallas_call(
        _ffn_kernel,
        grid=(rows // tm,),
        in_specs=[
            pl.BlockSpec((tm, D_MODEL), lambda i: (i, 0)),
            mod_spec,
            _const_spec((1, D_MODEL)),
            _const_spec((D_MODEL // 2, 2 * D_FF)),
            _const_spec((D_FF // 2, D_MODEL)),
            _const_spec((1, D_MODEL)),
        ],
        out_specs=pl.BlockSpec((tm, D_MODEL), lambda i: (i, 0)),
        out_shape=jax.ShapeDtypeStruct((rows, D_MODEL), F32),
        compiler_params=pltpu.CompilerParams(
            dimension_semantics=("arbitrary",),
            vmem_limit_bytes=48 * 1024 * 1024),
    )(x2d, mod, p['npre_ffn'], p['wffn_in'], p['wffn_out'], p['npost_ffn'])


def _proj_sample_kernel(n_in_tiles, x_ref, mod_ref, npre_ref, win_ref, wmg_ref, o_ref):
    m = mod_ref[...]
    sh1, sc1 = m[:, :D_MODEL], m[:, D_MODEL:2 * D_MODEL]
    h = (_rms(x_ref[...], npre_ref[...]) * (1.0 + sc1) + sh1).astype(BF16)
    j = pl.program_id(0)

    @pl.when(j < n_in_tiles)
    def _():
        o_ref[...] = _dot(h, _wb(win_ref))

    @pl.when(j >= n_in_tiles)
    def _():
        o_ref[...] = _dot(h, _wb(wmg_ref))


def _proj_sample_call(x2d, mod_s, p):
    rows = x2d.shape[0]
    tn = D_MODEL
    n_in = OFF_MG // tn
    return pl.pallas_call(
        functools.partial(_proj_sample_kernel, n_in),
        grid=(N_CAT // tn,),
        in_specs=[
            pl.BlockSpec((rows, D_MODEL), lambda j: (0, 0)),
            pl.BlockSpec((rows, 3 * D_MODEL), lambda j: (0, 0)),
            pl.BlockSpec((1, D_MODEL), lambda j: (0, 0)),
            pl.BlockSpec((D_MODEL // 2, tn), lambda j: (0, jnp.minimum(j, n_in - 1))),
            pl.BlockSpec((D_MODEL // 2, tn), lambda j: (0, jnp.maximum(j - n_in, 0))),
        ],
        out_specs=pl.BlockSpec((rows, tn), lambda j: (0, j)),
        out_shape=jax.ShapeDtypeStruct((rows, N_CAT), F32),
        compiler_params=pltpu.CompilerParams(
            dimension_semantics=("arbitrary",),
            vmem_limit_bytes=48 * 1024 * 1024),
    )(x2d, mod_s, p['npre'], p['win'], p['wmg'])


def _ret_sample_kernel(gpow_ref, qkv_ref, s_ref, rope_ref, dec_ref, qdec_ref, kdec_ref, o_ref, snew_ref):
    half = DK // 2
    cos, sin = rope_ref[:, 0:half], rope_ref[:, half:2 * half]
    cosk, sink = rope_ref[:, 2 * half:3 * half], rope_ref[:, 3 * half:4 * half]
    nseq = RET_SAMPLE_BB
    tlen = SUBLANES // nseq
    row = lax.broadcasted_iota(jnp.int32, (SUBLANES, LANES), 0)
    for hh in range(H_RET):
        q = _rope(qkv_ref[:, OFF_Q + hh * DK:OFF_Q + (hh + 1) * DK], cos, sin)
        k = _rope(qkv_ref[:, OFF_K + hh * DK:OFF_K + (hh + 1) * DK], cosk, sink)
        v = qkv_ref[:, OFF_V + hh * DV:OFF_V + (hh + 1) * DV]
        kdec = kdec_ref[hh]
        kd = k * jnp.concatenate([kdec, kdec], axis=1)
        scores = _dot_nt(q, k) * dec_ref[hh]
        intra = _dot(scores, v)
        qb = q.astype(BF16)
        cross = jnp.zeros((SUBLANES, DV), F32)
        for bi in range(nseq):
            s_old = s_ref[bi, hh]
            in_seq = (row >= bi * tlen) & (row < (bi + 1) * tlen)
            cr = _dot(qb, s_old.astype(BF16))
            cross = jnp.where(jnp.concatenate([in_seq] * (DV // LANES), axis=1), cr, cross)
            kd_b = jnp.where(jnp.concatenate([in_seq] * (DK // LANES), axis=1), kd, 0.0)
            snew_ref[bi, hh] = gpow_ref[hh] * s_old + _dot_tn(kd_b, v)
        qdec = qdec_ref[hh]
        o_ref[:, hh * DV:(hh + 1) * DV] = intra + cross * jnp.concatenate([qdec] * (DV // LANES), axis=1)


def _ffn_ret_kernel(sub_steps, gpow_ref, x_ref, mod_ref, npre_ref, w1_ref, w2_ref, npost_ref,
                    qkv_ref, s_ref, rope_ref, dec_ref, qdec_ref, kdec_ref,
                    y_ref, o_ref, snew_ref, h2_ref, acc_ref):
    n_chunks = D_FF // FF_CHUNK
    bounds = [n_chunks * s // sub_steps for s in range(sub_steps + 1)]
    for s in range(sub_steps):
        @pl.when(pl.program_id(1) == s)
        def _(s=s):
            if s == 0:
                h2_ref[...] = _ffn_pre(x_ref, mod_ref, npre_ref)
            acc = _ffn_hidden(h2_ref[...], None if s == 0 else acc_ref[...], w1_ref, w2_ref,
                              range(bounds[s], bounds[s + 1]))
            if s == sub_steps - 1:
                _ffn_post(x_ref, mod_ref, acc, npost_ref, y_ref)
            else:
                acc_ref[...] = acc

    _ret_sample_kernel(gpow_ref, qkv_ref, s_ref, rope_ref, dec_ref, qdec_ref, kdec_ref, o_ref, snew_ref)


def _ffn_ret_call(x2d, mod_p, seq_tiles, proj_s, state, gpow, rope8, dec8, qdec8, kdec8, p):
    rows = x2d.shape[0]
    tm = FFN_TM
    nb = state.shape[0]
    bb = RET_SAMPLE_BB
    sub_steps = nb // bb // (rows // tm)
    assert sub_steps * (rows // tm) * bb == nb
    srow = lambda i, j: i * sub_steps + j
    return pl.pallas_call(
        functools.partial(_ffn_ret_kernel, sub_steps),
        grid=(rows // tm, sub_steps),
        in_specs=[
            pl.BlockSpec(memory_space=pltpu.SMEM),
            pl.BlockSpec((tm, D_MODEL), lambda i, j: (i, 0)),
            pl.BlockSpec((None, 1, 3 * D_MODEL), lambda i, j: (i // seq_tiles, 0, 1)),
            _const_spec((1, D_MODEL)),
            _const_spec((D_MODEL // 2, 2 * D_FF)),
            _const_spec((D_FF // 2, D_MODEL)),
            _const_spec((1, D_MODEL)),
            pl.BlockSpec((SUBLANES, OFF_G), lambda i, j: (srow(i, j), 0)),
            pl.BlockSpec((bb, H_RET, DK, DV), lambda i, j: (srow(i, j), 0, 0, 0)),
            _const_spec((SUBLANES, 4 * (DK // 2))),
            _const_spec((H_RET, SUBLANES, SUBLANES)),
            _const_spec((H_RET, SUBLANES, LANES)),
            _const_spec((H_RET, SUBLANES, LANES)),
        ],
        out_specs=[
            pl.BlockSpec((tm, D_MODEL), lambda i, j: (i, 0)),
            pl.BlockSpec((SUBLANES, D_V), lambda i, j: (srow(i, j), 0)),
            pl.BlockSpec((bb, H_RET, DK, DV), lambda i, j: (srow(i, j), 0, 0, 0)),
        ],
        out_shape=[
            jax.ShapeDtypeStruct((rows, D_MODEL), F32),
            jax.ShapeDtypeStruct((proj_s.shape[0], D_V), F32),
            jax.ShapeDtypeStruct(state.shape, F32),
        ],
        scratch_shapes=[pltpu.VMEM((tm, D_MODEL), BF16), pltpu.VMEM((tm, D_MODEL), F32)],
        compiler_params=pltpu.CompilerParams(
            dimension_semantics=("arbitrary", "arbitrary"),
            vmem_limit_bytes=56 * 1024 * 1024),
    )(gpow, x2d, mod_p, p['npre_ffn'], p['wffn_in'], p['wffn_out'], p['npost_ffn'],
      proj_s, state, rope8, dec8, qdec8, kdec8)


def _mix_sample_kernel(x_ref, mod_ref, proj_ref, o_ref, zs_ref,
                       bmg_ref, gnw_ref, wbr_ret_ref, convw_ref, convb_ref, wrg_ref, bra_ref, brx_ref,
                       lru_ref, wbr_rnn_ref, wout_ref, npost_ref,
                       x1_ref, hseq_ref,
                       xr_ref, sa_ref, sb_ref, zsc_ref):
    tm = SAMPLE_TM
    tlen = 4
    x = x_ref[...]
    g1 = mod_ref[:, 2 * D_MODEL:]

    br_ret = jnp.zeros((tm, D_MODEL), F32)
    for hh in range(H_RET):
        on = _group_norm(o_ref[:, hh * DV:(hh + 1) * DV])
        g = proj_ref[:, OFF_G + hh * DV:OFF_G + (hh + 1) * DV]
        ry = (on * gnw_ref[:, hh * DV:(hh + 1) * DV] * _silu(g)).astype(BF16)
        br_ret = br_ret + _dot(ry, _wb(wbr_ret_ref, hh * DV, (hh + 1) * DV))

    zsc_ref[0:tm, :] = zs_ref[...]
    zsc_ref[tm:tm + SUBLANES, :] = jnp.zeros((SUBLANES, D_RNN), F32)
    tpos = lax.broadcasted_iota(jnp.int32, (tm, D_RNN), 0) & (tlen - 1)
    xr = proj_ref[:, OFF_XR:OFF_XR + D_RNN]
    xr_ref[0:SUBLANES, :] = jnp.zeros((SUBLANES, D_RNN), F32)
    xr_ref[SUBLANES:SUBLANES + tm, :] = xr
    cw = convw_ref[...]
    xconv = convb_ref[...]
    for j in range(CONV_W - 1):
        sft = CONV_W - 1 - j
        shifted = jnp.where(tpos >= sft, xr_ref[SUBLANES - sft:SUBLANES - sft + tm, :], 0.0)
        carried = jnp.where(tpos < sft, zsc_ref[CONV_W - 1 - sft:CONV_W - 1 - sft + tm, :], 0.0)
        xconv = xconv + (shifted + carried) * cw[j:j + 1, :]
    xconv = xconv + xr * cw[CONV_W - 1:CONV_W, :]

    a, b = _lru_coeffs(xconv, _lru_gate_pre(xconv.astype(BF16), wrg_ref), bra_ref[...], brx_ref[...], lru_ref[...])
    b = jnp.where(tpos == 0, b + a * zsc_ref[CONV_W - 1:CONV_W - 1 + tm, :], b)
    sa_ref[0:SUBLANES, :] = jnp.zeros((SUBLANES, D_RNN), F32)
    sb_ref[0:SUBLANES, :] = jnp.zeros((SUBLANES, D_RNN), F32)
    for s in (1, 2):
        sa_ref[SUBLANES:SUBLANES + tm, :] = a
        sb_ref[SUBLANES:SUBLANES + tm, :] = b
        keep = tpos >= s
        ap = jnp.where(keep, sa_ref[SUBLANES - s:SUBLANES - s + tm, :], 1.0)
        bp = jnp.where(keep, sb_ref[SUBLANES - s:SUBLANES - s + tm, :], 0.0)
        b = a * bp + b
        a = a * ap
    hseq_ref[...] = b

    gr = proj_ref[:, OFF_GR:OFF_GR + D_RNN]
    rnn_y_b = (b * jax.nn.gelu(gr, approximate=True)).astype(BF16)
    gate_pre = proj_ref[:, OFF_MG:OFF_MG + 2 * D_MODEL]
    x1_ref[...] = _mix_tail(x, g1, br_ret, rnn_y_b, gate_pre, bmg_ref[...], wbr_rnn_ref, wout_ref,
                            npost_ref[...])


def _mix_sample_call(x2d, mod_s, proj_s, o_s, zs, p):
    rows = x2d.shape[0]
    tm = SAMPLE_TM
    row_spec = lambda w: pl.BlockSpec((tm, w), lambda i: (i, 0))
    return pl.pallas_call(
        _mix_sample_kernel,
        grid=(rows // tm,),
        in_specs=[
            row_spec(D_MODEL), row_spec(3 * D_MODEL), row_spec(N_CAT), row_spec(D_V),
            row_spec(D_RNN),
            _const_spec((1, 2 * D_MODEL)),
            _const_spec((1, D_V)),
            _const_spec((D_V // 2, D_MODEL)),
            _const_spec((CONV_W, D_RNN)),
            _const_spec((1, D_RNN)),
            _const_spec((D_RNN // 2, 2 * RNN_BLOCK)),
            _const_spec((1, D_RNN)),
            _const_spec((1, D_RNN)),
            _const_spec((1, D_RNN)),
            _const_spec((D_RNN // 2, D_MODEL)),
            _const_spec((D_MODEL // 2, D_MODEL)),
            _const_spec((1, D_MODEL)),
        ],
        out_specs=[row_spec(D_MODEL), row_spec(D_RNN)],
        out_shape=[
            jax.ShapeDtypeStruct((rows, D_MODEL), F32),
            jax.ShapeDtypeStruct((rows, D_RNN), F32),
        ],
        scratch_shapes=[pltpu.VMEM((tm + SUBLANES, D_RNN), F32)] * 4,
        compiler_params=pltpu.CompilerParams(
            dimension_semantics=("arbitrary",),
            vmem_limit_bytes=56 * 1024 * 1024),
    )(x2d, mod_s, proj_s, o_s, zs, p['bmg'], p['gnw'], p['wbr_ret'], p['convw'],
      p['convb'], p['wrg'], p['bra'], p['brx'], p['lru'], p['wbr_rnn'], p['wout'], p['npost'])


def _rope_table(pos):
    half = DK // 2
    inv = ROPE_BASE ** (-jnp.arange(half, dtype=F32) / half)
    ang = pos.astype(F32)[:, None] * inv[None, :]
    cos, sin = jnp.cos(ang), jnp.sin(ang)
    ks = DK ** -0.5
    return jnp.concatenate([cos, sin, cos * ks, sin * ks], axis=1)


def _decay_tables(tpos, same_seq, chunk):
    lg = jnp.log(1.0 - 2.0 ** (-5.0 - jnp.arange(H_RET, dtype=F32)))
    idx = tpos.astype(F32)
    diff = idx[:, None] - idx[None, :]
    causal = (diff >= 0) & same_seq
    dec = jnp.where(causal[None], jnp.exp(jnp.where(causal, diff, 0.0)[None] * lg[:, None, None]), 0.0)
    qdec = jnp.exp((idx + 1.0)[None, :] * lg[:, None])
    kdec = jnp.exp((chunk - 1.0 - idx)[None, :] * lg[:, None])
    rep = lambda a: jnp.broadcast_to(a[:, :, None], a.shape + (LANES,))
    gpow = jnp.exp(chunk * lg)
    return dec, rep(qdec), rep(kdec), gpow


def kernel(x_prompt, x_sample, state_ret, state_rnn_h, state_rnn_conv, c_prompt, c_sample,
           w_ada, b_ada, norm_pre_mix, norm_post_mix, norm_pre_ffn, norm_post_ffn,
           w_in, ret_gn_w, w_br_ret, conv_w, conv_b, w_rg_a, b_rg_a, w_rg_x, b_rg_x,
           lru_param, w_br_rnn, w_mgate, b_mgate, w_out, w_ffn_in, w_ffn_out):
    depth = w_in.shape[0]
    assert depth == 1, "single layer step"
    nb, seq, _ = x_prompt.shape
    nsb, sseq, _ = x_sample.shape
    assert seq % PROMPT_TM == 0 and sseq * RET_SAMPLE_BB == SUBLANES and sseq == CONV_W
    l = 0
    row = lambda a: a[l][None, :]
    p = dict(
        npre=row(norm_pre_mix), npost=row(norm_post_mix), npre_ffn=row(norm_pre_ffn), npost_ffn=row(norm_post_ffn),
        win=_pack_rows(w_in[l], nb=OFF_MG // 6), wmg=_pack_rows(w_mgate[l]),
        bmg=row(b_mgate), gnw=row(ret_gn_w), wbr_ret=_pack_rows(w_br_ret[l]),
        convw=conv_w[l], convb=row(conv_b),
        wrg=_pack_rows(jnp.concatenate([w_rg_a[l], w_rg_x[l]], axis=2).reshape(D_RNN, 2 * RNN_BLOCK)),
        bra=row(b_rg_a), brx=row(b_rg_x), lru=row(lru_param),
        wbr_rnn=_pack_rows(w_br_rnn[l]), wout=_pack_rows(w_out[l]),
    )

    rows_s = nsb * sseq
    c_all = jnp.concatenate([jnp.repeat(c_sample, sseq, axis=0), c_prompt], axis=0)
    mod_all = _mod_call(c_all, w_ada[l], row(b_ada))
    mod_p = mod_all[rows_s:].reshape(nb, 1, 6 * D_MODEL)
    mod_s = mod_all

    tm = PROMPT_TM
    ng = tm // SUBLANES
    r = jnp.arange(tm)
    tpos = (r % SUBLANES) * ng + r // SUBLANES
    pos_p = (jnp.arange(seq // tm)[:, None] * tm + tpos[None, :]).reshape(seq).astype(jnp.int32)
    rope_p = _rope_table(pos_p)
    dec, qdec, kdec, gpow = _decay_tables(tpos, jnp.ones((tm, tm), bool), float(tm))
    interleave = lambda a: a.reshape(nb, seq // tm, SUBLANES, ng, D_MODEL).swapaxes(2, 3).reshape(nb, seq, D_MODEL)
    restore = lambda a: a.reshape(nb, seq // tm, ng, SUBLANES, D_MODEL).swapaxes(2, 3).reshape(nb, seq, D_MODEL)
    x1p, ret_p, hlast_p, conv_p, p['wffn_in'], p['wffn_out'] = _mix_prompt_call(
        interleave(x_prompt), mod_p, gpow, rope_p, dec, qdec, kdec, p, w_ffn_in[l], w_ffn_out[l])

    xs2d = x_sample.reshape(rows_s, D_MODEL)
    proj_s = _proj_sample_call(xs2d, mod_s, p)
    r8 = jnp.arange(SUBLANES)
    rope_s = _rope_table(PAST_LEN + (r8 % sseq).astype(jnp.int32))
    same = (r8[:, None] // sseq) == (r8[None, :] // sseq)
    dec8, qdec8, kdec8, gpow_s = _decay_tables(r8 % sseq, same, float(sseq))
    yp2d, o_s, ret_s = _ffn_ret_call(x1p.reshape(nb * seq, D_MODEL), mod_p, seq // FFN_TM, proj_s, state_ret[l],
                                     gpow_s, rope_s, dec8, qdec8, kdec8, p)
    yp = restore(yp2d.reshape(nb, seq, D_MODEL))
    zs = jnp.concatenate([state_rnn_conv[l], state_rnn_h[l][:, None, :]], axis=1).reshape(rows_s, D_RNN)
    x1s, hseq_s = _mix_sample_call(xs2d, mod_s, proj_s, o_s, zs, p)
    ys = _ffn_call(x1s, mod_s, FFN_TM, 1, p).reshape(nsb, sseq, D_MODEL)
    hlast_s = hseq_s.reshape(nsb, sseq, D_RNN)[:, sseq - 1]
    conv_s = proj_s[:, OFF_XR:OFF_XR + D_RNN].reshape(nsb, sseq, D_RNN)[:, sseq - (CONV_W - 1):]

    return (yp, ys, ret_p[None], ret_s[None], hlast_p.reshape(nb, D_RNN)[None], hlast_s[None],
            conv_p[None], conv_s[None])
```

```python
import functools

import jax
import jax.numpy as jnp
from jax import lax
from jax.experimental import pallas as pl
from jax.experimental.pallas import tpu as pltpu

F32 = jnp.float32
BF16 = jnp.bfloat16

D_MODEL = 1024
H_RET = 4
DK = D_MODEL // H_RET
DV = 2 * DK
D_QK = H_RET * DK
D_V = H_RET * DV
D_RNN = 1536
RNN_BLOCK = 128
N_RNN_BLOCKS = D_RNN // RNN_BLOCK
CONV_W = 4
LRU_C = 8.0
D_FF = 2816
ROPE_BASE = 10000.0
GN_EPS = 1e-5
RMS_EPS = 1e-6
PAST_LEN = 16384

OFF_Q = 0
OFF_K = OFF_Q + D_QK
OFF_V = OFF_K + D_QK
OFF_G = OFF_V + D_V
OFF_XR = OFF_G + D_V
OFF_GR = OFF_XR + D_RNN
OFF_MG = OFF_GR + D_RNN
N_CAT = OFF_MG + 2 * D_MODEL

SUBLANES = 8
LANES = 128
MXU_DIM = 256
VMEM_BYTES_V7X = 64 * 1024 * 1024

PROMPT_TM = 256
FFN_TM = 512
FF_CHUNK = MXU_DIM
SAMPLE_TM = 128
RET_SAMPLE_BB = 2


def _dot(a, b):
    return jnp.dot(a, b, preferred_element_type=F32)


def _dot_nt(a, b):
    return lax.dot_general(a, b, (((1,), (1,)), ((), ())), preferred_element_type=F32)


def _dot_tn(a, b):
    return lax.dot_general(a, b, (((0,), (0,)), ((), ())), preferred_element_type=F32)


def _wb(ref, k0=None, k1=None, c0=None, c1=None):
    rs = slice(None) if k0 is None else slice(k0 // 2, k1 // 2)
    cs = slice(None) if c0 is None else slice(c0, c1)
    return pltpu.bitcast(ref[rs, cs], BF16)


def _to_words(w):
    return pltpu.bitcast(w.astype(BF16), jnp.uint32)


def _pack_kernel(w_ref, o_ref):
    o_ref[...] = _to_words(w_ref[...])


def _pack_rows(w, kb=None):
    k, n = w.shape
    kb = k if kb is None else kb
    return pl.pallas_call(
        _pack_kernel,
        grid=(k // kb,),
        in_specs=[pl.BlockSpec((kb, n), lambda i: (i, 0))],
        out_specs=pl.BlockSpec((kb // 2, n), lambda i: (i, 0)),
        out_shape=jax.ShapeDtypeStruct((k // 2, n), jnp.uint32),
        compiler_params=pltpu.CompilerParams(
            dimension_semantics=("arbitrary",),
            vmem_limit_bytes=40 * 1024 * 1024),
    )(w)


def _rms(x, w):
    ms = jnp.mean(x * x, axis=-1, keepdims=True)
    return x * lax.rsqrt(ms + RMS_EPS) * w


def _sigmoid(x):
    return 0.5 * jnp.tanh(0.5 * x) + 0.5


def _silu(x):
    return x * _sigmoid(x)


def _rope(x, cos, sin):
    half = DK // 2
    x1, x2 = x[:, :half], x[:, half:]
    return jnp.concatenate([x1 * cos - x2 * sin, x1 * sin + x2 * cos], axis=1)


def _group_norm(o):
    mu = jnp.mean(o, axis=-1, keepdims=True)
    d = o - mu
    var = jnp.mean(d * d, axis=-1, keepdims=True)
    return d * lax.rsqrt(var + GN_EPS)


def _lru_gate_pre(xcb, wrg_ref):
    return [_dot(xcb[:, n * RNN_BLOCK:(n + 1) * RNN_BLOCK], _wb(wrg_ref, n * RNN_BLOCK, (n + 1) * RNN_BLOCK))
            for n in range(N_RNN_BLOCKS)]


def _lru_coeffs(xconv, pre, b_a, b_x, lru):
    ra = jnp.concatenate([p[:, :RNN_BLOCK] for p in pre], axis=1) + b_a
    ri = jnp.concatenate([p[:, RNN_BLOCK:] for p in pre], axis=1) + b_x
    r = _sigmoid(ra)
    i = _sigmoid(ri)
    z = -lru
    sp = jnp.maximum(z, 0.0) + jnp.log(1.0 + jnp.exp(-jnp.abs(z)))
    log_a = -LRU_C * r * sp
    a = jnp.exp(log_a)
    beta = jnp.sqrt(-jnp.tanh(log_a) * (a * a + 1.0))
    return a, beta * (i * xconv)


def _mix_tail(x, g1, br_ret, rnn_y_b, gate_pre, b_mg, wbr_rnn_ref, wout_ref, npost):
    br_rnn = _dot(rnn_y_b, _wb(wbr_rnn_ref))
    gates = _sigmoid(gate_pre + b_mg)
    ga, gb = gates[:, :D_MODEL], gates[:, D_MODEL:]
    mixed = _dot((ga * br_ret + gb * br_rnn).astype(BF16), _wb(wout_ref))
    return x + g1 * _rms(mixed, npost)


def _mod_kernel(c_ref, w_ref, b_ref, o_ref):
    a = _silu(c_ref[...]).astype(BF16)
    o_ref[...] = _dot(a, w_ref[...].astype(BF16)) + b_ref[...]


def _mod_call(c_all, w_ada_f32, b_ada):
    rows = c_all.shape[0]
    tn = 2 * D_MODEL
    return pl.pallas_call(
        _mod_kernel,
        grid=(6 * D_MODEL // tn,),
        in_specs=[
            pl.BlockSpec((rows, D_MODEL), lambda j: (0, 0)),
            pl.BlockSpec((D_MODEL, tn), lambda j: (0, j)),
            pl.BlockSpec((1, tn), lambda j: (0, j)),
        ],
        out_specs=pl.BlockSpec((rows, tn), lambda j: (0, j)),
        out_shape=jax.ShapeDtypeStruct((rows, 6 * D_MODEL), F32),
        compiler_params=pltpu.CompilerParams(
            dimension_semantics=("arbitrary",),
            vmem_limit_bytes=48 * 1024 * 1024),
    )(c_all, w_ada_f32, b_ada)


def _mix_prompt_kernel(gpow_ref, x_ref, mod_ref, npre_ref, wcat_ref, wmg_ref, bmg_ref, rope_ref,
                       dec_ref, qdec_ref, kdec_ref, gnw_ref, wbr_ret_ref, convw_ref, convb_ref,
                       wrg_ref, bra_ref, brx_ref, lru_ref, wbr_rnn_ref, wout_ref, npost_ref,
                       wf1_ref, wf2_ref,
                       x1_ref, s_ref, hlast_ref, convnew_ref, wf1b_ref, wf2b_ref,
                       xr_ref, prevg_ref, hc_ref):
    tm = PROMPT_TM
    ng = tm // SUBLANES
    halo = (CONV_W - 1) * SUBLANES
    t = pl.program_id(1)

    @pl.when(t == 0)
    def _():
        s_ref[...] = jnp.zeros_like(s_ref)
        prevg_ref[...] = jnp.zeros_like(prevg_ref)
        hc_ref[...] = jnp.zeros_like(hc_ref)

    sh1, sc1 = mod_ref[:, :D_MODEL], mod_ref[:, D_MODEL:2 * D_MODEL]
    hb = (_rms(x_ref[...], npre_ref[...]) * (1.0 + sc1) + sh1).astype(BF16)
    sub = lax.broadcasted_iota(jnp.int32, (SUBLANES, D_RNN), 0)
    half = DK // 2
    cos, sin = rope_ref[:, 0:half], rope_ref[:, half:2 * half]
    cosk, sink = rope_ref[:, 2 * half:3 * half], rope_ref[:, 3 * half:4 * half]
    st = {}

    def xr_proj():
        xr = _dot(hb, _wb(wcat_ref, c0=OFF_XR, c1=OFF_XR + D_RNN))
        xr_ref[halo:halo + tm, :] = xr
        for kk in range(1, CONV_W):
            r0 = (CONV_W - 1 - kk) * SUBLANES
            cur = xr[(ng - kk) * SUBLANES:(ng - kk + 1) * SUBLANES, :]
            prv = prevg_ref[r0:r0 + SUBLANES, :]
            xr_ref[r0:r0 + SUBLANES, :] = pltpu.roll(jnp.where(sub == SUBLANES - 1, prv, cur), 1, 0)
        prevg_ref[...] = xr[tm - halo:, :]
        for kk in range(1, CONV_W):
            r1 = (ng - kk) * SUBLANES + SUBLANES - 1
            convnew_ref[CONV_W - 1 - kk:CONV_W - kk, :] = xr[r1:r1 + 1, :]

    def lru_conv():
        cw = convw_ref[...]
        xconv = convb_ref[...]
        for j in range(CONV_W):
            r0 = halo - (CONV_W - 1 - j) * SUBLANES
            xconv = xconv + xr_ref[r0:r0 + tm, :] * cw[j:j + 1, :]
        st['xconv'] = xconv

    def lru_gate_proj():
        st['gpre'] = _lru_gate_pre(st['xconv'].astype(BF16), wrg_ref)

    def lru_coef():
        st['a'], st['b'] = _lru_coeffs(st.pop('xconv'), st.pop('gpre'), bra_ref[...], brx_ref[...], lru_ref[...])

    def lru_scan():
        a, b = st['a'], st['b']
        ca, cb = a[0:SUBLANES, :], b[0:SUBLANES, :]
        cas, cbs = [ca], [cb]
        for gi in range(1, ng):
            ag = a[gi * SUBLANES:(gi + 1) * SUBLANES, :]
            cb = ag * cb + b[gi * SUBLANES:(gi + 1) * SUBLANES, :]
            ca = ag * ca
            cas.append(ca)
            cbs.append(cb)
        cin = jnp.where(sub == 0, hc_ref[SUBLANES - 1:SUBLANES, :], 0.0)
        for s in range(SUBLANES - 1):
            cin = jnp.where(sub == s + 1, pltpu.roll(ca * cin + cb, 1, 0), cin)
        seg_end = ca * cin + cb
        hc_ref[...] = seg_end
        hlast_ref[...] = seg_end[SUBLANES - 1:SUBLANES, :]
        st['hseq'] = jnp.concatenate([cas[gi] * cin + cbs[gi] for gi in range(ng)], axis=0)

    def gr_proj():
        st['gr'] = _dot(hb, _wb(wcat_ref, c0=OFF_GR, c1=OFF_GR + D_RNN))

    def lru_y():
        st['rnn_y'] = (st.pop('hseq') * jax.nn.gelu(st.pop('gr'), approximate=True)).astype(BF16)

    def lru_out():
        st['br_rnn'] = _dot(st.pop('rnn_y'), _wb(wbr_rnn_ref))

    def gate_proj():
        st['gate_pre'] = _dot(hb, _wb(wmg_ref))

    def gate_act():
        st['gates'] = _sigmoid(st.pop('gate_pre') + bmg_ref[...])

    def head_proj(hh):
        q = _dot(hb, _wb(wcat_ref, c0=OFF_Q + hh * DK, c1=OFF_Q + (hh + 1) * DK))
        k = _dot(hb, _wb(wcat_ref, c0=OFF_K + hh * DK, c1=OFF_K + (hh + 1) * DK))
        vb = _dot(hb, _wb(wcat_ref, c0=OFF_V + hh * DV, c1=OFF_V + (hh + 1) * DV)).astype(BF16)
        g = _dot(hb, _wb(wcat_ref, c0=OFF_G + hh * DV, c1=OFF_G + (hh + 1) * DV))
        st['proj', hh] = (q, k, vb, g)

    def head_rope(hh):
        q, k, vb, g = st.pop(('proj', hh))
        kr = _rope(k, cosk, sink)
        kdec = kdec_ref[hh]
        kdb = (kr * jnp.concatenate([kdec, kdec], axis=1)).astype(BF16)
        st['rope', hh] = (_rope(q, cos, sin).astype(BF16), kr.astype(BF16), kdb, vb, g)

    def head_qk(hh):
        qb, kb, kdb, vb, g = st.pop(('rope', hh))
        scores = _dot_nt(qb, kb)
        cross = _dot(qb, s_ref[hh].astype(BF16))
        st['qk', hh] = (scores, cross, vb, g)
        st['kv', hh] = (kdb, vb)

    def head_state(hh):
        kdb, vb = st.pop(('kv', hh))
        s_ref[hh] = gpow_ref[hh] * s_ref[hh] + _dot_tn(kdb, vb)

    def head_decay(hh):
        scores, cross, vb, g = st.pop(('qk', hh))
        st['dec', hh] = ((scores * dec_ref[hh]).astype(BF16), cross, vb, g)

    def head_pv(hh):
        sb, cross, vb, g = st.pop(('dec', hh))
        st['pv', hh] = (_dot(sb, vb), cross, g)

    def head_norm(hh):
        intra, cross, g = st.pop(('pv', hh))
        qdec = qdec_ref[hh]
        on = _group_norm(intra + cross * jnp.concatenate([qdec] * (DV // LANES), axis=1))
        st['ry', hh] = (on * gnw_ref[:, hh * DV:(hh + 1) * DV] * _silu(g)).astype(BF16)

    def head_out(hh):
        part = _dot(st.pop(('ry', hh)), _wb(wbr_ret_ref, hh * DV, (hh + 1) * DV))
        st['br_ret'] = part if hh == 0 else st['br_ret'] + part

    heads = range(H_RET)
    order = (
        [xr_proj, (head_proj, 0), lru_conv, lru_gate_proj, (head_proj, 1), (head_proj, 2), lru_coef, (head_proj, 3)]
        + [(head_rope, h) for h in heads] + [gr_proj] + [(head_qk, h) for h in heads]
        + [gate_proj] + [(head_decay, h) for h in heads] + [lru_scan]
        + [(head_pv, h) for h in heads] + [(head_state, h) for h in heads] + [lru_y, lru_out]
        + [(head_norm, h) for h in heads] + [(head_out, h) for h in heads] + [gate_act]
    )
    for stage in order:
        if isinstance(stage, tuple):
            stage[0](stage[1])
        else:
            stage()

    gates = st['gates']
    ga, gb = gates[:, :D_MODEL], gates[:, D_MODEL:]
    mixed = _dot((ga * st['br_ret'] + gb * st['br_rnn']).astype(BF16), _wb(wout_ref))
    x1_ref[...] = x_ref[...] + mod_ref[:, 2 * D_MODEL:] * _rms(mixed, npost_ref[...])

    wf1b_ref[...] = _to_words(wf1_ref[...])
    wf2b_ref[...] = _to_words(wf2_ref[...])


def _const_spec(shape):
    nd = len(shape)
    return pl.BlockSpec(shape, lambda *_: (0,) * nd, pipeline_mode=pl.Buffered(1))


def _mix_prompt_call(x, mod3, gpow, rope_tab, dec, qdec, kdec, p, wf1, wf2):
    nb, seq, _ = x.shape
    tm = PROMPT_TM
    nt = seq // tm
    steps = nb * nt
    bf16_rows = 2 * SUBLANES
    r1 = wf1.shape[0] // steps
    assert r1 * steps == wf1.shape[0] and r1 % bf16_rows == 0
    rep2 = 1
    while wf2.shape[0] % (steps // rep2) or (wf2.shape[0] // (steps // rep2)) % bf16_rows:
        rep2 *= 2
    r2 = wf2.shape[0] // (steps // rep2)
    wf1_spec = pl.BlockSpec((r1, wf1.shape[1]), lambda b, t: (b * nt + t, 0))
    wf2_spec = pl.BlockSpec((r2, wf2.shape[1]), lambda b, t: ((b * nt + t) // rep2, 0))
    in_specs = [
        pl.BlockSpec(memory_space=pltpu.SMEM),
        pl.BlockSpec((None, tm, D_MODEL), lambda b, t: (b, t, 0)),
        pl.BlockSpec((None, 1, 3 * D_MODEL), lambda b, t: (b, 0, 0)),
        _const_spec((1, D_MODEL)),
        _const_spec((D_MODEL // 2, OFF_MG)),
        _const_spec((D_MODEL // 2, 2 * D_MODEL)),
        _const_spec((1, 2 * D_MODEL)),
        pl.BlockSpec((tm, 4 * (DK // 2)), lambda b, t: (t, 0)),
        _const_spec((H_RET, tm, tm)),
        _const_spec((H_RET, tm, LANES)),
        _const_spec((H_RET, tm, LANES)),
        _const_spec((1, D_V)),
        _const_spec((D_V // 2, D_MODEL)),
        _const_spec((CONV_W, D_RNN)),
        _const_spec((1, D_RNN)),
        _const_spec((D_RNN // 2, 2 * RNN_BLOCK)),
        _const_spec((1, D_RNN)),
        _const_spec((1, D_RNN)),
        _const_spec((1, D_RNN)),
        _const_spec((D_RNN // 2, D_MODEL)),
        _const_spec((D_MODEL // 2, D_MODEL)),
        _const_spec((1, D_MODEL)),
        wf1_spec,
        wf2_spec,
    ]
    out_specs = [
        pl.BlockSpec((None, tm, D_MODEL), lambda b, t: (b, t, 0)),
        pl.BlockSpec((None, H_RET, DK, DV), lambda b, t: (b, 0, 0, 0)),
        pl.BlockSpec((None, 1, D_RNN), lambda b, t: (b, 0, 0)),
        pl.BlockSpec((None, CONV_W - 1, D_RNN), lambda b, t: (b, 0, 0)),
        pl.BlockSpec((r1 // 2, wf1.shape[1]), wf1_spec.index_map),
        pl.BlockSpec((r2 // 2, wf2.shape[1]), wf2_spec.index_map),
    ]
    out_shape = [
        jax.ShapeDtypeStruct((nb, seq, D_MODEL), F32),
        jax.ShapeDtypeStruct((nb, H_RET, DK, DV), F32),
        jax.ShapeDtypeStruct((nb, 1, D_RNN), F32),
        jax.ShapeDtypeStruct((nb, CONV_W - 1, D_RNN), F32),
        jax.ShapeDtypeStruct((wf1.shape[0] // 2, wf1.shape[1]), jnp.uint32),
        jax.ShapeDtypeStruct((wf2.shape[0] // 2, wf2.shape[1]), jnp.uint32),
    ]
    halo = (CONV_W - 1) * SUBLANES
    scratch = [
        pltpu.VMEM((halo + tm, D_RNN), F32),
        pltpu.VMEM((halo, D_RNN), F32),
        pltpu.VMEM((SUBLANES, D_RNN), F32),
    ]
    return pl.pallas_call(
        _mix_prompt_kernel,
        grid=(nb, nt),
        in_specs=in_specs,
        out_specs=out_specs,
        out_shape=out_shape,
        scratch_shapes=scratch,
        compiler_params=pltpu.CompilerParams(
            dimension_semantics=("arbitrary", "arbitrary"),
            vmem_limit_bytes=VMEM_BYTES_V7X - 4 * 1024 * 1024),
    )(gpow, x, mod3, p['npre'], p['win'], p['wmg'], p['bmg'], rope_tab, dec, qdec, kdec, p['gnw'], p['wbr_ret'],
      p['convw'], p['convb'], p['wrg'], p['bra'], p['brx'], p['lru'], p['wbr_rnn'], p['wout'], p['npost'],
      wf1, wf2)


def _ffn_pre(x_ref, mod_ref, npre_ref):
    m = mod_ref[...]
    sh2, sc2 = m[:, :D_MODEL], m[:, D_MODEL:2 * D_MODEL]
    return (_rms(x_ref[...], npre_ref[...]) * (1.0 + sc2) + sh2).astype(BF16)


def _ffn_hidden(h2, acc, w1_ref, w2_ref, chunk_ids):
    for j in chunk_ids:
        c0 = j * FF_CHUNK
        fg = _dot(h2, _wb(w1_ref, c0=c0, c1=c0 + FF_CHUNK))
        fu = _dot(h2, _wb(w1_ref, c0=D_FF + c0, c1=D_FF + c0 + FF_CHUNK))
        part = _dot((_silu(fg) * fu).astype(BF16), _wb(w2_ref, c0, c0 + FF_CHUNK))
        acc = part if acc is None else acc + part
    return acc


def _ffn_post(x_ref, mod_ref, acc, npost_ref, o_ref):
    o_ref[...] = x_ref[...] + mod_ref[:, 2 * D_MODEL:] * _rms(acc, npost_ref[...])


def _ffn_kernel(x_ref, mod_ref, npre_ref, w1_ref, w2_ref, npost_ref, o_ref):
    h2 = _ffn_pre(x_ref, mod_ref, npre_ref)
    acc = _ffn_hidden(h2, None, w1_ref, w2_ref, range(D_FF // FF_CHUNK))
    _ffn_post(x_ref, mod_ref, acc, npost_ref, o_ref)


def _ffn_call(x2d, mod, mod_rows_per_tile, seq_tiles, p):
    rows = x2d.shape[0]
    tm = FFN_TM
    if mod_rows_per_tile == 1:
        mod_spec = pl.BlockSpec((None, 1, 3 * D_MODEL), lambda i: (i // seq_tiles, 0, 1))
    else:
        mod_spec = pl.BlockSpec((tm, 3 * D_MODEL), lambda i: (i, 1))
    return pl.pallas_call(
        _ffn_kernel,
        grid=(rows // tm,),
        in_specs=[
            pl.BlockSpec((tm, D_MODEL), lambda i: (i, 0)),
            mod_spec,
            _const_spec((1, D_MODEL)),
            _const_spec((D_MODEL // 2, 2 * D_FF)),
            _const_spec((D_FF // 2, D_MODEL)),
            _const_spec((1, D_MODEL)),
        ],
        out_specs=pl.BlockSpec((tm, D_MODEL), lambda i: (i, 0)),
        out_shape=jax.ShapeDtypeStruct((rows, D_MODEL), F32),
        compiler_params=pltpu.CompilerParams(
            dimension_semantics=("arbitrary",),
            vmem_limit_bytes=48 * 1024 * 1024),
    )(x2d, mod, p['npre_ffn'], p['wffn_in'], p['wffn_out'], p['npost_ffn'])


def _proj_sample_kernel(n_in_tiles, x_ref, mod_ref, npre_ref, win_ref, wmg_ref, o_ref):
    m = mod_ref[...]
    sh1, sc1 = m[:, :D_MODEL], m[:, D_MODEL:2 * D_MODEL]
    h = (_rms(x_ref[...], npre_ref[...]) * (1.0 + sc1) + sh1).astype(BF16)
    j = pl.program_id(0)

    @pl.when(j < n_in_tiles)
    def _():
        o_ref[...] = _dot(h, _wb(win_ref))

    @pl.when(j >= n_in_tiles)
    def _():
        o_ref[...] = _dot(h, _wb(wmg_ref))


def _proj_sample_call(x2d, mod_s, p):
    rows = x2d.shape[0]
    tn = D_MODEL
    n_in = OFF_MG // tn
    return pl.pallas_call(
        functools.partial(_proj_sample_kernel, n_in),
        grid=(N_CAT // tn,),
        in_specs=[
            pl.BlockSpec((rows, D_MODEL), lambda j: (0, 0)),
            pl.BlockSpec((rows, 3 * D_MODEL), lambda j: (0, 0)),
            pl.BlockSpec((1, D_MODEL), lambda j: (0, 0)),
            pl.BlockSpec((D_MODEL // 2, tn), lambda j: (0, jnp.minimum(j, n_in - 1))),
            pl.BlockSpec((D_MODEL // 2, tn), lambda j: (0, jnp.maximum(j - n_in, 0))),
        ],
        out_specs=pl.BlockSpec((rows, tn), lambda j: (0, j)),
        out_shape=jax.ShapeDtypeStruct((rows, N_CAT), F32),
        compiler_params=pltpu.CompilerParams(
            dimension_semantics=("arbitrary",),
            vmem_limit_bytes=48 * 1024 * 1024),
    )(x2d, mod_s, p['npre'], p['win'], p['wmg'])


def _ret_sample_kernel(gpow_ref, qkv_ref, s_ref, rope_ref, dec_ref, qdec_ref, kdec_ref, o_ref, snew_ref):
    half = DK // 2
    cos, sin = rope_ref[:, 0:half], rope_ref[:, half:2 * half]
    cosk, sink = rope_ref[:, 2 * half:3 * half], rope_ref[:, 3 * half:4 * half]
    nseq = RET_SAMPLE_BB
    tlen = SUBLANES // nseq
    row = lax.broadcasted_iota(jnp.int32, (SUBLANES, LANES), 0)
    for hh in range(H_RET):
        q = _rope(qkv_ref[:, OFF_Q + hh * DK:OFF_Q + (hh + 1) * DK], cos, sin)
        k = _rope(qkv_ref[:, OFF_K + hh * DK:OFF_K + (hh + 1) * DK], cosk, sink)
        v = qkv_ref[:, OFF_V + hh * DV:OFF_V + (hh + 1) * DV]
        kdec = kdec_ref[hh]
        kd = k * jnp.concatenate([kdec, kdec], axis=1)
        scores = _dot_nt(q, k) * dec_ref[hh]
        intra = _dot(scores, v)
        qb = q.astype(BF16)
        cross = jnp.zeros((SUBLANES, DV), F32)
        for bi in range(nseq):
            s_old = s_ref[bi, hh]
            in_seq = (row >= bi * tlen) & (row < (bi + 1) * tlen)
            cr = _dot(qb, s_old.astype(BF16))
            cross = jnp.where(jnp.concatenate([in_seq] * (DV // LANES), axis=1), cr, cross)
            kd_b = jnp.where(jnp.concatenate([in_seq] * (DK // LANES), axis=1), kd, 0.0)
            snew_ref[bi, hh] = gpow_ref[hh] * s_old + _dot_tn(kd_b, v)
        qdec = qdec_ref[hh]
        o_ref[:, hh * DV:(hh + 1) * DV] = intra + cross * jnp.concatenate([qdec] * (DV // LANES), axis=1)


def _ffn_ret_kernel(sub_steps, gpow_ref, x_ref, mod_ref, npre_ref, w1_ref, w2_ref, npost_ref,
                    qkv_ref, s_ref, rope_ref, dec_ref, qdec_ref, kdec_ref,
                    y_ref, o_ref, snew_ref, h2_ref, acc_ref):
    n_chunks = D_FF // FF_CHUNK
    bounds = [n_chunks * s // sub_steps for s in range(sub_steps + 1)]
    for s in range(sub_steps):
        @pl.when(pl.program_id(1) == s)
        def _(s=s):
            if s == 0:
                h2_ref[...] = _ffn_pre(x_ref, mod_ref, npre_ref)
            acc = _ffn_hidden(h2_ref[...], None if s == 0 else acc_ref[...], w1_ref, w2_ref,
                              range(bounds[s], bounds[s + 1]))
            if s == sub_steps - 1:
                _ffn_post(x_ref, mod_ref, acc, npost_ref, y_ref)
            else:
                acc_ref[...] = acc

    _ret_sample_kernel(gpow_ref, qkv_ref, s_ref, rope_ref, dec_ref, qdec_ref, kdec_ref, o_ref, snew_ref)


def _ffn_ret_call(x2d, mod_p, seq_tiles, proj_s, state, gpow, rope8, dec8, qdec8, kdec8, p):
    rows = x2d.shape[0]
    tm = FFN_TM
    nb = state.shape[0]
    bb = RET_SAMPLE_BB
    sub_steps = nb // bb // (rows // tm)
    assert sub_steps * (rows // tm) * bb == nb
    srow = lambda i, j: i * sub_steps + j
    return pl.pallas_call(
        functools.partial(_ffn_ret_kernel, sub_steps),
        grid=(rows // tm, sub_steps),
        in_specs=[
            pl.BlockSpec(memory_space=pltpu.SMEM),
            pl.BlockSpec((tm, D_MODEL), lambda i, j: (i, 0)),
            pl.BlockSpec((None, 1, 3 * D_MODEL), lambda i, j: (i // seq_tiles, 0, 1)),
            _const_spec((1, D_MODEL)),
            _const_spec((D_MODEL // 2, 2 * D_FF)),
            _const_spec((D_FF // 2, D_MODEL)),
            _const_spec((1, D_MODEL)),
            pl.BlockSpec((SUBLANES, OFF_G), lambda i, j: (srow(i, j), 0)),
            pl.BlockSpec((bb, H_RET, DK, DV), lambda i, j: (srow(i, j), 0, 0, 0)),
            _const_spec((SUBLANES, 4 * (DK // 2))),
            _const_spec((H_RET, SUBLANES, SUBLANES)),
            _const_spec((H_RET, SUBLANES, LANES)),
            _const_spec((H_RET, SUBLANES, LANES)),
        ],
        out_specs=[
            pl.BlockSpec((tm, D_MODEL), lambda i, j: (i, 0)),
            pl.BlockSpec((SUBLANES, D_V), lambda i, j: (srow(i, j), 0)),
            pl.BlockSpec((bb, H_RET, DK, DV), lambda i, j: (srow(i, j), 0, 0, 0)),
        ],
        out_shape=[
            jax.ShapeDtypeStruct((rows, D_MODEL), F32),
            jax.ShapeDtypeStruct((proj_s.shape[0], D_V), F32),
            jax.ShapeDtypeStruct(state.shape, F32),
        ],
        scratch_shapes=[pltpu.VMEM((tm, D_MODEL), BF16), pltpu.VMEM((tm, D_MODEL), F32)],
        compiler_params=pltpu.CompilerParams(
            dimension_semantics=("arbitrary", "arbitrary"),
            vmem_limit_bytes=56 * 1024 * 1024),
    )(gpow, x2d, mod_p, p['npre_ffn'], p['wffn_in'], p['wffn_out'], p['npost_ffn'],
      proj_s, state, rope8, dec8, qdec8, kdec8)


def _mix_sample_kernel(x_ref, mod_ref, proj_ref, o_ref, zs_ref,
                       bmg_ref, gnw_ref, wbr_ret_ref, convw_ref, convb_ref, wrg_ref, bra_ref, brx_ref,
                       lru_ref, wbr_rnn_ref, wout_ref, npost_ref,
                       x1_ref, hseq_ref,
                       xr_ref, sa_ref, sb_ref, zsc_ref):
    tm = SAMPLE_TM
    tlen = 4
    x = x_ref[...]
    g1 = mod_ref[:, 2 * D_MODEL:]

    br_ret = jnp.zeros((tm, D_MODEL), F32)
    for hh in range(H_RET):
        on = _group_norm(o_ref[:, hh * DV:(hh + 1) * DV])
        g = proj_ref[:, OFF_G + hh * DV:OFF_G + (hh + 1) * DV]
        ry = (on * gnw_ref[:, hh * DV:(hh + 1) * DV] * _silu(g)).astype(BF16)
        br_ret = br_ret + _dot(ry, _wb(wbr_ret_ref, hh * DV, (hh + 1) * DV))

    zsc_ref[0:tm, :] = zs_ref[...]
    zsc_ref[tm:tm + SUBLANES, :] = jnp.zeros((SUBLANES, D_RNN), F32)
    tpos = lax.broadcasted_iota(jnp.int32, (tm, D_RNN), 0) & (tlen - 1)
    xr = proj_ref[:, OFF_XR:OFF_XR + D_RNN]
    xr_ref[0:SUBLANES, :] = jnp.zeros((SUBLANES, D_RNN), F32)
    xr_ref[SUBLANES:SUBLANES + tm, :] = xr
    cw = convw_ref[...]
    xconv = convb_ref[...]
    for j in range(CONV_W - 1):
        sft = CONV_W - 1 - j
        shifted = jnp.where(tpos >= sft, xr_ref[SUBLANES - sft:SUBLANES - sft + tm, :], 0.0)
        carried = jnp.where(tpos < sft, zsc_ref[CONV_W - 1 - sft:CONV_W - 1 - sft + tm, :], 0.0)
        xconv = xconv + (shifted + carried) * cw[j:j + 1, :]
    xconv = xconv + xr * cw[CONV_W - 1:CONV_W, :]

    a, b = _lru_coeffs(xconv, _lru_gate_pre(xconv.astype(BF16), wrg_ref), bra_ref[...], brx_ref[...], lru_ref[...])
    b = jnp.where(tpos == 0, b + a * zsc_ref[CONV_W - 1:CONV_W - 1 + tm, :], b)
    sa_ref[0:SUBLANES, :] = jnp.zeros((SUBLANES, D_RNN), F32)
    sb_ref[0:SUBLANES, :] = jnp.zeros((SUBLANES, D_RNN), F32)
    for s in (1, 2):
        sa_ref[SUBLANES:SUBLANES + tm, :] = a
        sb_ref[SUBLANES:SUBLANES + tm, :] = b
        keep = tpos >= s
        ap = jnp.where(keep, sa_ref[SUBLANES - s:SUBLANES - s + tm, :], 1.0)
        bp = jnp.where(keep, sb_ref[SUBLANES - s:SUBLANES - s + tm, :], 0.0)
        b = a * bp + b
        a = a * ap
    hseq_ref[...] = b

    gr = proj_ref[:, OFF_GR:OFF_GR + D_RNN]
    rnn_y_b = (b * jax.nn.gelu(gr, approximate=True)).astype(BF16)
    gate_pre = proj_ref[:, OFF_MG:OFF_MG + 2 * D_MODEL]
    x1_ref[...] = _mix_tail(x, g1, br_ret, rnn_y_b, gate_pre, bmg_ref[...], wbr_rnn_ref, wout_ref,
                            npost_ref[...])


def _mix_sample_call(x2d, mod_s, proj_s, o_s, zs, p):
    rows = x2d.shape[0]
    tm = SAMPLE_TM
    row_spec = lambda w: pl.BlockSpec((tm, w), lambda i: (i, 0))
    return pl.pallas_call(
        _mix_sample_kernel,
        grid=(rows // tm,),
        in_specs=[
            row_spec(D_MODEL), row_spec(3 * D_MODEL), row_spec(N_CAT), row_spec(D_V),
            row_spec(D_RNN),
            _const_spec((1, 2 * D_MODEL)),
            _const_spec((1, D_V)),
            _const_spec((D_V // 2, D_MODEL)),
            _const_spec((CONV_W, D_RNN)),
            _const_spec((1, D_RNN)),
            _const_spec((D_RNN // 2, 2 * RNN_BLOCK)),
            _const_spec((1, D_RNN)),
            _const_spec((1, D_RNN)),
            _const_spec((1, D_RNN)),
            _const_spec((D_RNN // 2, D_MODEL)),
            _const_spec((D_MODEL // 2, D_MODEL)),
            _const_spec((1, D_MODEL)),
        ],
        out_specs=[row_spec(D_MODEL), row_spec(D_RNN)],
        out_shape=[
            jax.ShapeDtypeStruct((rows, D_MODEL), F32),
            jax.ShapeDtypeStruct((rows, D_RNN), F32),
        ],
        scratch_shapes=[pltpu.VMEM((tm + SUBLANES, D_RNN), F32)] * 4,
        compiler_params=pltpu.CompilerParams(
            dimension_semantics=("arbitrary",),
            vmem_limit_bytes=56 * 1024 * 1024),
    )(x2d, mod_s, proj_s, o_s, zs, p['bmg'], p['gnw'], p['wbr_ret'], p['convw'],
      p['convb'], p['wrg'], p['bra'], p['brx'], p['lru'], p['wbr_rnn'], p['wout'], p['npost'])


def _rope_table(pos):
    half = DK // 2
    inv = ROPE_BASE ** (-jnp.arange(half, dtype=F32) / half)
    ang = pos.astype(F32)[:, None] * inv[None, :]
    cos, sin = jnp.cos(ang), jnp.sin(ang)
    ks = DK ** -0.5
    return jnp.concatenate([cos, sin, cos * ks, sin * ks], axis=1)


def _decay_tables(tpos, same_seq, chunk):
    lg = jnp.log(1.0 - 2.0 ** (-5.0 - jnp.arange(H_RET, dtype=F32)))
    idx = tpos.astype(F32)
    diff = idx[:, None] - idx[None, :]
    causal = (diff >= 0) & same_seq
    dec = jnp.where(causal[None], jnp.exp(jnp.where(causal, diff, 0.0)[None] * lg[:, None, None]), 0.0)
    qdec = jnp.exp((idx + 1.0)[None, :] * lg[:, None])
    kdec = jnp.exp((chunk - 1.0 - idx)[None, :] * lg[:, None])
    rep = lambda a: jnp.broadcast_to(a[:, :, None], a.shape + (LANES,))
    gpow = jnp.exp(chunk * lg)
    return dec, rep(qdec), rep(kdec), gpow


def kernel(x_prompt, x_sample, state_ret, state_rnn_h, state_rnn_conv, c_prompt, c_sample,
           w_ada, b_ada, norm_pre_mix, norm_post_mix, norm_pre_ffn, norm_post_ffn,
           w_in, ret_gn_w, w_br_ret, conv_w, conv_b, w_rg_a, b_rg_a, w_rg_x, b_rg_x,
           lru_param, w_br_rnn, w_mgate, b_mgate, w_out, w_ffn_in, w_ffn_out):
    depth = w_in.shape[0]
    assert depth == 1, "single layer step"
    nb, seq, _ = x_prompt.shape
    nsb, sseq, _ = x_sample.shape
    assert seq % PROMPT_TM == 0 and sseq * RET_SAMPLE_BB == SUBLANES and sseq == CONV_W
    l = 0
    row = lambda a: a[l][None, :]
    p = dict(
        npre=row(norm_pre_mix), npost=row(norm_post_mix), npre_ffn=row(norm_pre_ffn), npost_ffn=row(norm_post_ffn),
        win=_pack_rows(w_in[l], kb=D_MODEL // 8), wmg=_pack_rows(w_mgate[l]),
        bmg=row(b_mgate), gnw=row(ret_gn_w), wbr_ret=_pack_rows(w_br_ret[l]),
        convw=conv_w[l], convb=row(conv_b),
        wrg=_pack_rows(jnp.concatenate([w_rg_a[l], w_rg_x[l]], axis=2).reshape(D_RNN, 2 * RNN_BLOCK)),
        bra=row(b_rg_a), brx=row(b_rg_x), lru=row(lru_param),
        wbr_rnn=_pack_rows(w_br_rnn[l]), wout=_pack_rows(w_out[l]),
    )

    rows_s = nsb * sseq
    c_all = jnp.concatenate([jnp.repeat(c_sample, sseq, axis=0), c_prompt], axis=0)
    mod_all = _mod_call(c_all, w_ada[l], row(b_ada))
    mod_p = mod_all[rows_s:].reshape(nb, 1, 6 * D_MODEL)
    mod_s = mod_all

    tm = PROMPT_TM
    ng = tm // SUBLANES
    r = jnp.arange(tm)
    tpos = (r % SUBLANES) * ng + r // SUBLANES
    pos_p = (jnp.arange(seq // tm)[:, None] * tm + tpos[None, :]).reshape(seq).astype(jnp.int32)
    rope_p = _rope_table(pos_p)
    dec, qdec, kdec, gpow = _decay_tables(tpos, jnp.ones((tm, tm), bool), float(tm))
    interleave = lambda a: a.reshape(nb, seq // tm, SUBLANES, ng, D_MODEL).swapaxes(2, 3).reshape(nb, seq, D_MODEL)
    restore = lambda a: a.reshape(nb, seq // tm, ng, SUBLANES, D_MODEL).swapaxes(2, 3).reshape(nb, seq, D_MODEL)
    x1p, ret_p, hlast_p, conv_p, p['wffn_in'], p['wffn_out'] = _mix_prompt_call(
        interleave(x_prompt), mod_p, gpow, rope_p, dec, qdec, kdec, p, w_ffn_in[l], w_ffn_out[l])

    xs2d = x_sample.reshape(rows_s, D_MODEL)
    proj_s = _proj_sample_call(xs2d, mod_s, p)
    r8 = jnp.arange(SUBLANES)
    rope_s = _rope_table(PAST_LEN + (r8 % sseq).astype(jnp.int32))
    same = (r8[:, None] // sseq) == (r8[None, :] // sseq)
    dec8, qdec8, kdec8, gpow_s = _decay_tables(r8 % sseq, same, float(sseq))
    yp2d, o_s, ret_s = _ffn_ret_call(x1p.reshape(nb * seq, D_MODEL), mod_p, seq // FFN_TM, proj_s, state_ret[l],
                                     gpow_s, rope_s, dec8, qdec8, kdec8, p)
    yp = restore(yp2d.reshape(nb, seq, D_MODEL))
    zs = jnp.concatenate([state_rnn_conv[l], state_rnn_h[l][:, None, :]], axis=1).reshape(rows_s, D_RNN)
    x1s, hseq_s = _mix_sample_call(xs2d, mod_s, proj_s, o_s, zs, p)
    ys = _ffn_call(x1s, mod_s, FFN_TM, 1, p).reshape(nsb, sseq, D_MODEL)
    hlast_s = hseq_s.reshape(nsb, sseq, D_RNN)[:, sseq - 1]
    conv_s = proj_s[:, OFF_XR:OFF_XR + D_RNN].reshape(nsb, sseq, D_RNN)[:, sseq - (CONV_W - 1):]

    return (yp, ys, ret_p[None], ret_s[None], hlast_p.reshape(nb, D_RNN)[None], hlast_s[None],
            conv_p[None], conv_s[None])
```

```python
import functools

import jax
import jax.numpy as jnp
from jax import lax
from jax.experimental import pallas as pl
from jax.experimental.pallas import tpu as pltpu

F32 = jnp.float32
BF16 = jnp.bfloat16

D_MODEL = 1024
H_RET = 4
DK = D_MODEL // H_RET
DV = 2 * DK
D_QK = H_RET * DK
D_V = H_RET * DV
D_RNN = 1536
RNN_BLOCK = 128
N_RNN_BLOCKS = D_RNN // RNN_BLOCK
CONV_W = 4
LRU_C = 8.0
D_FF = 2816
ROPE_BASE = 10000.0
GN_EPS = 1e-5
RMS_EPS = 1e-6
PAST_LEN = 16384

OFF_Q = 0
OFF_K = OFF_Q + D_QK
OFF_V = OFF_K + D_QK
OFF_G = OFF_V + D_V
OFF_XR = OFF_G + D_V
OFF_GR = OFF_XR + D_RNN
OFF_MG = OFF_GR + D_RNN
N_CAT = OFF_MG + 2 * D_MODEL

SUBLANES = 8
LANES = 128
MXU_DIM = 256
VMEM_BYTES_V7X = 64 * 1024 * 1024

PROMPT_TM = 256
FFN_TM = 512
FF_CHUNK = MXU_DIM
SAMPLE_TM = 128
RET_SAMPLE_BB = 2


def _dot(a, b):
    return jnp.dot(a, b, preferred_element_type=F32)


def _dot_nt(a, b):
    return lax.dot_general(a, b, (((1,), (1,)), ((), ())), preferred_element_type=F32)


def _dot_tn(a, b):
    return lax.dot_general(a, b, (((0,), (0,)), ((), ())), preferred_element_type=F32)


def _wb(ref, k0=None, k1=None, c0=None, c1=None):
    rs = slice(None) if k0 is None else slice(k0 // 2, k1 // 2)
    cs = slice(None) if c0 is None else slice(c0, c1)
    return pltpu.bitcast(ref[rs, cs], BF16)


def _to_words(w):
    return pltpu.bitcast(w.astype(BF16), jnp.uint32)


def _pack_kernel(w_ref, o_ref):
    o_ref[...] = _to_words(w_ref[...])


def _pack_rows(w, kb=None):
    k, n = w.shape
    kb = k if kb is None else kb
    return pl.pallas_call(
        _pack_kernel,
        grid=(k // kb,),
        in_specs=[pl.BlockSpec((kb, n), lambda i: (i, 0))],
        out_specs=pl.BlockSpec((kb // 2, n), lambda i: (i, 0)),
        out_shape=jax.ShapeDtypeStruct((k // 2, n), jnp.uint32),
        compiler_params=pltpu.CompilerParams(
            dimension_semantics=("arbitrary",),
            vmem_limit_bytes=40 * 1024 * 1024),
    )(w)


def _rms(x, w):
    ms = jnp.mean(x * x, axis=-1, keepdims=True)
    return x * lax.rsqrt(ms + RMS_EPS) * w


def _sigmoid(x):
    return 0.5 * jnp.tanh(0.5 * x) + 0.5


def _silu(x):
    return x * _sigmoid(x)


def _rope(x, cos, sin):
    half = DK // 2
    x1, x2 = x[:, :half], x[:, half:]
    return jnp.concatenate([x1 * cos - x2 * sin, x1 * sin + x2 * cos], axis=1)


def _group_norm(o):
    mu = jnp.mean(o, axis=-1, keepdims=True)
    d = o - mu
    var = jnp.mean(d * d, axis=-1, keepdims=True)
    return d * lax.rsqrt(var + GN_EPS)


def _lru_gate_pre(xcb, wrg_ref):
    return [_dot(xcb[:, n * RNN_BLOCK:(n + 1) * RNN_BLOCK], _wb(wrg_ref, n * RNN_BLOCK, (n + 1) * RNN_BLOCK))
            for n in range(N_RNN_BLOCKS)]


def _lru_coeffs(xconv, pre, b_a, b_x, lru):
    ra = jnp.concatenate([p[:, :RNN_BLOCK] for p in pre], axis=1) + b_a
    ri = jnp.concatenate([p[:, RNN_BLOCK:] for p in pre], axis=1) + b_x
    r = _sigmoid(ra)
    i = _sigmoid(ri)
    z = -lru
    sp = jnp.maximum(z, 0.0) + jnp.log(1.0 + jnp.exp(-jnp.abs(z)))
    log_a = -LRU_C * r * sp
    a = jnp.exp(log_a)
    beta = jnp.sqrt(-jnp.tanh(log_a) * (a * a + 1.0))
    return a, beta * (i * xconv)


def _mix_tail(x, g1, br_ret, rnn_y_b, gate_pre, b_mg, wbr_rnn_ref, wout_ref, npost):
    br_rnn = _dot(rnn_y_b, _wb(wbr_rnn_ref))
    gates = _sigmoid(gate_pre + b_mg)
    ga, gb = gates[:, :D_MODEL], gates[:, D_MODEL:]
    mixed = _dot((ga * br_ret + gb * br_rnn).astype(BF16), _wb(wout_ref))
    return x + g1 * _rms(mixed, npost)


def _mod_kernel(c_ref, w_ref, b_ref, o_ref):
    a = _silu(c_ref[...]).astype(BF16)
    o_ref[...] = _dot(a, w_ref[...].astype(BF16)) + b_ref[...]


def _mod_call(c_all, w_ada_f32, b_ada):
    rows = c_all.shape[0]
    tn = 2 * D_MODEL
    return pl.pallas_call(
        _mod_kernel,
        grid=(6 * D_MODEL // tn,),
        in_specs=[
            pl.BlockSpec((rows, D_MODEL), lambda j: (0, 0)),
            pl.BlockSpec((D_MODEL, tn), lambda j: (0, j)),
            pl.BlockSpec((1, tn), lambda j: (0, j)),
        ],
        out_specs=pl.BlockSpec((rows, tn), lambda j: (0, j)),
        out_shape=jax.ShapeDtypeStruct((rows, 6 * D_MODEL), F32),
        compiler_params=pltpu.CompilerParams(
            dimension_semantics=("arbitrary",),
            vmem_limit_bytes=48 * 1024 * 1024),
    )(c_all, w_ada_f32, b_ada)


def _mix_prompt_kernel(gpow_ref, x_ref, mod_ref, npre_ref, wcat_ref, wmg_ref, bmg_ref, rope_ref,
                       dec_ref, qdec_ref, kdec_ref, gnw_ref, wbr_ret_ref, convw_ref, convb_ref,
                       wrg_ref, bra_ref, brx_ref, lru_ref, wbr_rnn_ref, wout_ref, npost_ref,
                       wf1_ref, wf2_ref,
                       x1_ref, s_ref, hlast_ref, convnew_ref, wf1b_ref, wf2b_ref,
                       xr_ref, prevg_ref, hc_ref):
    tm = PROMPT_TM
    ng = tm // SUBLANES
    halo = (CONV_W - 1) * SUBLANES
    t = pl.program_id(1)

    @pl.when(t == 0)
    def _():
        s_ref[...] = jnp.zeros_like(s_ref)
        prevg_ref[...] = jnp.zeros_like(prevg_ref)
        hc_ref[...] = jnp.zeros_like(hc_ref)

    sh1, sc1 = mod_ref[:, :D_MODEL], mod_ref[:, D_MODEL:2 * D_MODEL]
    hb = (_rms(x_ref[...], npre_ref[...]) * (1.0 + sc1) + sh1).astype(BF16)
    sub = lax.broadcasted_iota(jnp.int32, (SUBLANES, D_RNN), 0)
    half = DK // 2
    cos, sin = rope_ref[:, 0:half], rope_ref[:, half:2 * half]
    cosk, sink = rope_ref[:, 2 * half:3 * half], rope_ref[:, 3 * half:4 * half]
    st = {}

    def xr_proj():
        xr = _dot(hb, _wb(wcat_ref, c0=OFF_XR, c1=OFF_XR + D_RNN))
        xr_ref[halo:halo + tm, :] = xr
        for kk in range(1, CONV_W):
            r0 = (CONV_W - 1 - kk) * SUBLANES
            cur = xr[(ng - kk) * SUBLANES:(ng - kk + 1) * SUBLANES, :]
            prv = prevg_ref[r0:r0 + SUBLANES, :]
            xr_ref[r0:r0 + SUBLANES, :] = pltpu.roll(jnp.where(sub == SUBLANES - 1, prv, cur), 1, 0)
        prevg_ref[...] = xr[tm - halo:, :]
        for kk in range(1, CONV_W):
            r1 = (ng - kk) * SUBLANES + SUBLANES - 1
            convnew_ref[CONV_W - 1 - kk:CONV_W - kk, :] = xr[r1:r1 + 1, :]

    def lru_conv():
        cw = convw_ref[...]
        xconv = convb_ref[...]
        for j in range(CONV_W):
            r0 = halo - (CONV_W - 1 - j) * SUBLANES
            xconv = xconv + xr_ref[r0:r0 + tm, :] * cw[j:j + 1, :]
        st['xconv'] = xconv

    def lru_gate_proj():
        st['gpre'] = _lru_gate_pre(st['xconv'].astype(BF16), wrg_ref)

    def lru_coef():
        st['a'], st['b'] = _lru_coeffs(st.pop('xconv'), st.pop('gpre'), bra_ref[...], brx_ref[...], lru_ref[...])

    def lru_scan():
        a, b = st['a'], st['b']
        ca, cb = a[0:SUBLANES, :], b[0:SUBLANES, :]
        cas, cbs = [ca], [cb]
        for gi in range(1, ng):
            ag = a[gi * SUBLANES:(gi + 1) * SUBLANES, :]
            cb = ag * cb + b[gi * SUBLANES:(gi + 1) * SUBLANES, :]
            ca = ag * ca
            cas.append(ca)
            cbs.append(cb)
        cin = jnp.where(sub == 0, hc_ref[SUBLANES - 1:SUBLANES, :], 0.0)
        for s in range(SUBLANES - 1):
            cin = jnp.where(sub == s + 1, pltpu.roll(ca * cin + cb, 1, 0), cin)
        seg_end = ca * cin + cb
        hc_ref[...] = seg_end
        hlast_ref[...] = seg_end[SUBLANES - 1:SUBLANES, :]
        st['hseq'] = jnp.concatenate([cas[gi] * cin + cbs[gi] for gi in range(ng)], axis=0)

    def gr_proj():
        st['gr'] = _dot(hb, _wb(wcat_ref, c0=OFF_GR, c1=OFF_GR + D_RNN))

    def lru_y():
        st['rnn_y'] = (st.pop('hseq') * jax.nn.gelu(st.pop('gr'), approximate=True)).astype(BF16)

    def lru_out():
        st['br_rnn'] = _dot(st.pop('rnn_y'), _wb(wbr_rnn_ref))

    def gate_proj():
        st['gate_pre'] = _dot(hb, _wb(wmg_ref))

    def gate_act():
        st['gates'] = _sigmoid(st.pop('gate_pre') + bmg_ref[...])

    def head_proj(hh):
        q = _dot(hb, _wb(wcat_ref, c0=OFF_Q + hh * DK, c1=OFF_Q + (hh + 1) * DK))
        k = _dot(hb, _wb(wcat_ref, c0=OFF_K + hh * DK, c1=OFF_K + (hh + 1) * DK))
        vb = _dot(hb, _wb(wcat_ref, c0=OFF_V + hh * DV, c1=OFF_V + (hh + 1) * DV)).astype(BF16)
        g = _dot(hb, _wb(wcat_ref, c0=OFF_G + hh * DV, c1=OFF_G + (hh + 1) * DV))
        st['proj', hh] = (q, k, vb, g)

    def head_rope(hh):
        q, k, vb, g = st.pop(('proj', hh))
        kr = _rope(k, cosk, sink)
        kdec = kdec_ref[hh]
        kdb = (kr * jnp.concatenate([kdec, kdec], axis=1)).astype(BF16)
        st['rope', hh] = (_rope(q, cos, sin).astype(BF16), kr.astype(BF16), kdb, vb, g)

    def head_qk(hh):
        qb, kb, kdb, vb, g = st.pop(('rope', hh))
        scores = _dot_nt(qb, kb)
        cross = _dot(qb, s_ref[hh].astype(BF16))
        st['qk', hh] = (scores, cross, vb, g)
        st['kv', hh] = (kdb, vb)

    def head_state(hh):
        kdb, vb = st.pop(('kv', hh))
        s_ref[hh] = gpow_ref[hh] * s_ref[hh] + _dot_tn(kdb, vb)

    def head_decay(hh):
        scores, cross, vb, g = st.pop(('qk', hh))
        st['dec', hh] = ((scores * dec_ref[hh]).astype(BF16), cross, vb, g)

    def head_pv(hh):
        sb, cross, vb, g = st.pop(('dec', hh))
        st['pv', hh] = (_dot(sb, vb), cross, g)

    def head_norm(hh):
        intra, cross, g = st.pop(('pv', hh))
        qdec = qdec_ref[hh]
        on = _group_norm(intra + cross * jnp.concatenate([qdec] * (DV // LANES), axis=1))
        st['ry', hh] = (on * gnw_ref[:, hh * DV:(hh + 1) * DV] * _silu(g)).astype(BF16)

    def head_out(hh):
        part = _dot(st.pop(('ry', hh)), _wb(wbr_ret_ref, hh * DV, (hh + 1) * DV))
        st['br_ret'] = part if hh == 0 else st['br_ret'] + part

    heads = range(H_RET)
    order = (
        [xr_proj, (head_proj, 0), lru_conv, lru_gate_proj, (head_proj, 1), (head_proj, 2), lru_coef, (head_proj, 3)]
        + [(head_rope, h) for h in heads] + [gr_proj] + [(head_qk, h) for h in heads]
        + [gate_proj] + [(head_decay, h) for h in heads] + [lru_scan]
        + [(head_pv, h) for h in heads] + [(head_state, h) for h in heads] + [lru_y, lru_out]
        + [(head_norm, h) for h in heads] + [(head_out, h) for h in heads] + [gate_act]
    )
    for stage in order:
        if isinstance(stage, tuple):
            stage[0](stage[1])
        else:
            stage()

    gates = st['gates']
    ga, gb = gates[:, :D_MODEL], gates[:, D_MODEL:]
    mixed = _dot((ga * st['br_ret'] + gb * st['br_rnn']).astype(BF16), _wb(wout_ref))
    x1_ref[...] = x_ref[...] + mod_ref[:, 2 * D_MODEL:] * _rms(mixed, npost_ref[...])

    wf1b_ref[...] = _to_words(wf1_ref[...])
    wf2b_ref[...] = _to_words(wf2_ref[...])


def _const_spec(shape):
    nd = len(shape)
    return pl.BlockSpec(shape, lambda *_: (0,) * nd, pipeline_mode=pl.Buffered(1))


def _mix_prompt_call(x, mod3, gpow, rope_tab, dec, qdec, kdec, p, wf1, wf2):
    nb, seq, _ = x.shape
    tm = PROMPT_TM
    nt = seq // tm
    steps = nb * nt
    bf16_rows = 2 * SUBLANES
    r1 = wf1.shape[0] // steps
    assert r1 * steps == wf1.shape[0] and r1 % bf16_rows == 0
    rep2 = 1
    while wf2.shape[0] % (steps // rep2) or (wf2.shape[0] // (steps // rep2)) % bf16_rows:
        rep2 *= 2
    r2 = wf2.shape[0] // (steps // rep2)
    wf1_spec = pl.BlockSpec((r1, wf1.shape[1]), lambda b, t: (b * nt + t, 0))
    wf2_spec = pl.BlockSpec((r2, wf2.shape[1]), lambda b, t: ((b * nt + t) // rep2, 0))
    in_specs = [
        pl.BlockSpec(memory_space=pltpu.SMEM),
        pl.BlockSpec((None, tm, D_MODEL), lambda b, t: (b, t, 0)),
        pl.BlockSpec((None, 1, 3 * D_MODEL), lambda b, t: (b, 0, 0)),
        _const_spec((1, D_MODEL)),
        _const_spec((D_MODEL // 2, OFF_MG)),
        _const_spec((D_MODEL // 2, 2 * D_MODEL)),
        _const_spec((1, 2 * D_MODEL)),
        pl.BlockSpec((tm, 4 * (DK // 2)), lambda b, t: (t, 0)),
        _const_spec((H_RET, tm, tm)),
        _const_spec((H_RET, tm, LANES)),
        _const_spec((H_RET, tm, LANES)),
        _const_spec((1, D_V)),
        _const_spec((D_V // 2, D_MODEL)),
        _const_spec((CONV_W, D_RNN)),
        _const_spec((1, D_RNN)),
        _const_spec((D_RNN // 2, 2 * RNN_BLOCK)),
        _const_spec((1, D_RNN)),
        _const_spec((1, D_RNN)),
        _const_spec((1, D_RNN)),
        _const_spec((D_RNN // 2, D_MODEL)),
        _const_spec((D_MODEL // 2, D_MODEL)),
        _const_spec((1, D_MODEL)),
        wf1_spec,
        wf2_spec,
    ]
    out_specs = [
        pl.BlockSpec((None, tm, D_MODEL), lambda b, t: (b, t, 0)),
        pl.BlockSpec((None, H_RET, DK, DV), lambda b, t: (b, 0, 0, 0)),
        pl.BlockSpec((None, 1, D_RNN), lambda b, t: (b, 0, 0)),
        pl.BlockSpec((None, CONV_W - 1, D_RNN), lambda b, t: (b, 0, 0)),
        pl.BlockSpec((r1 // 2, wf1.shape[1]), wf1_spec.index_map),
        pl.BlockSpec((r2 // 2, wf2.shape[1]), wf2_spec.index_map),
    ]
    out_shape = [
        jax.ShapeDtypeStruct((nb, seq, D_MODEL), F32),
        jax.ShapeDtypeStruct((nb, H_RET, DK, DV), F32),
        jax.ShapeDtypeStruct((nb, 1, D_RNN), F32),
        jax.ShapeDtypeStruct((nb, CONV_W - 1, D_RNN), F32),
        jax.ShapeDtypeStruct((wf1.shape[0] // 2, wf1.shape[1]), jnp.uint32),
        jax.ShapeDtypeStruct((wf2.shape[0] // 2, wf2.shape[1]), jnp.uint32),
    ]
    halo = (CONV_W - 1) * SUBLANES
    scratch = [
        pltpu.VMEM((halo + tm, D_RNN), F32),
        pltpu.VMEM((halo, D_RNN), F32),
        pltpu.VMEM((SUBLANES, D_RNN), F32),
    ]
    return pl.pallas_call(
        _mix_prompt_kernel,
        grid=(nb, nt),
        in_specs=in_specs,
        out_specs=out_specs,
        out_shape=out_shape,
        scratch_shapes=scratch,
        compiler_params=pltpu.CompilerParams(
            dimension_semantics=("arbitrary", "arbitrary"),
            vmem_limit_bytes=VMEM_BYTES_V7X - 4 * 1024 * 1024),
    )(gpow, x, mod3, p['npre'], p['win'], p['wmg'], p['bmg'], rope_tab, dec, qdec, kdec, p['gnw'], p['wbr_ret'],
      p['convw'], p['convb'], p['wrg'], p['bra'], p['brx'], p['lru'], p['wbr_rnn'], p['wout'], p['npost'],
      wf1, wf2)


def _ffn_pre(x_ref, mod_ref, npre_ref):
    m = mod_ref[...]
    sh2, sc2 = m[:, :D_MODEL], m[:, D_MODEL:2 * D_MODEL]
    return (_rms(x_ref[...], npre_ref[...]) * (1.0 + sc2) + sh2).astype(BF16)


def _ffn_hidden(h2, acc, w1_ref, w2_ref, chunk_ids):
    for j in chunk_ids:
        c0 = j * FF_CHUNK
        fg = _dot(h2, _wb(w1_ref, c0=c0, c1=c0 + FF_CHUNK))
        fu = _dot(h2, _wb(w1_ref, c0=D_FF + c0, c1=D_FF + c0 + FF_CHUNK))
        part = _dot((_silu(fg) * fu).astype(BF16), _wb(w2_ref, c0, c0 + FF_CHUNK))
        acc = part if acc is None else acc + part
    return acc


def _ffn_post(x_ref, mod_ref, acc, npost_ref, o_ref):
    o_ref[...] = x_ref[...] + mod_ref[:, 2 * D_MODEL:] * _rms(acc, npost_ref[...])


def _ffn_kernel(x_ref, mod_ref, npre_ref, w1_ref, w2_ref, npost_ref, o_ref):
    h2 = _ffn_pre(x_ref, mod_ref, npre_ref)
    acc = _ffn_hidden(h2, None, w1_ref, w2_ref, range(D_FF // FF_CHUNK))
    _ffn_post(x_ref, mod_ref, acc, npost_ref, o_ref)


def _ffn_call(x2d, mod, mod_rows_per_tile, seq_tiles, p):
    rows = x2d.shape[0]
    tm = FFN_TM
    if mod_rows_per_tile == 1:
        mod_spec = pl.BlockSpec((None, 1, 3 * D_MODEL), lambda i: (i // seq_tiles, 0, 1))
    else:
        mod_spec = pl.BlockSpec((tm, 3 * D_MODEL), lambda i: (i, 1))
    return pl.pallas_call(
        _ffn_kernel,
        grid=(rows // tm,),
        in_specs=[
            pl.BlockSpec((tm, D_MODEL), lambda i: (i, 0)),
            mod_spec,
            _const_spec((1, D_MODEL)),
            _const_spec((D_MODEL // 2, 2 * D_FF)),
            _const_spec((D_FF // 2, D_MODEL)),
            _const_spec((1, D_MODEL)),
        ],
        out_specs=pl.BlockSpec((tm, D_MODEL), lambda i: (i, 0)),
        out_shape=jax.ShapeDtypeStruct((rows, D_MODEL), F32),
        compiler_params=pltpu.CompilerParams(
            dimension_semantics=("arbitrary",),
            vmem_limit_bytes=48 * 1024 * 1024),
    )(x2d, mod, p['npre_ffn'], p['wffn_in'], p['wffn_out'], p['npost_ffn'])


def _proj_sample_kernel(n_in_tiles, x_ref, mod_ref, npre_ref, win_ref, wmg_ref, o_ref):
    m = mod_ref[...]
    sh1, sc1 = m[:, :D_MODEL], m[:, D_MODEL:2 * D_MODEL]
    h = (_rms(x_ref[...], npre_ref[...]) * (1.0 + sc1) + sh1).astype(BF16)
    j = pl.program_id(0)

    @pl.when(j < n_in_tiles)
    def _():
        o_ref[...] = _dot(h, _wb(win_ref))

    @pl.when(j >= n_in_tiles)
    def _():
        o_ref[...] = _dot(h, _wb(wmg_ref))


def _proj_sample_call(x2d, mod_s, p):
    rows = x2d.shape[0]
    tn = D_MODEL
    n_in = OFF_MG // tn
    return pl.pallas_call(
        functools.partial(_proj_sample_kernel, n_in),
        grid=(N_CAT // tn,),
        in_specs=[
            pl.BlockSpec((rows, D_MODEL), lambda j: (0, 0)),
            pl.BlockSpec((rows, 3 * D_MODEL), lambda j: (0, 0)),
            pl.BlockSpec((1, D_MODEL), lambda j: (0, 0)),
            pl.BlockSpec((D_MODEL // 2, tn), lambda j: (0, jnp.minimum(j, n_in - 1))),
            pl.BlockSpec((D_MODEL // 2, tn), lambda j: (0, jnp.maximum(j - n_in, 0))),
        ],
        out_specs=pl.BlockSpec((rows, tn), lambda j: (0, j)),
        out_shape=jax.ShapeDtypeStruct((rows, N_CAT), F32),
        compiler_params=pltpu.CompilerParams(
            dimension_semantics=("arbitrary",),
            vmem_limit_bytes=48 * 1024 * 1024),
    )(x2d, mod_s, p['npre'], p['win'], p['wmg'])


def _ret_sample_kernel(gpow_ref, qkv_ref, s_ref, rope_ref, dec_ref, qdec_ref, kdec_ref, o_ref, snew_ref):
    half = DK // 2
    cos, sin = rope_ref[:, 0:half], rope_ref[:, half:2 * half]
    cosk, sink = rope_ref[:, 2 * half:3 * half], rope_ref[:, 3 * half:4 * half]
    nseq = RET_SAMPLE_BB
    tlen = SUBLANES // nseq
    row = lax.broadcasted_iota(jnp.int32, (SUBLANES, LANES), 0)
    for hh in range(H_RET):
        q = _rope(qkv_ref[:, OFF_Q + hh * DK:OFF_Q + (hh + 1) * DK], cos, sin)
        k = _rope(qkv_ref[:, OFF_K + hh * DK:OFF_K + (hh + 1) * DK], cosk, sink)
        v = qkv_ref[:, OFF_V + hh * DV:OFF_V + (hh + 1) * DV]
        kdec = kdec_ref[hh]
        kd = k * jnp.concatenate([kdec, kdec], axis=1)
        scores = _dot_nt(q, k) * dec_ref[hh]
        intra = _dot(scores, v)
        qb = q.astype(BF16)
        cross = jnp.zeros((SUBLANES, DV), F32)
        for bi in range(nseq):
            s_old = s_ref[bi, hh]
            in_seq = (row >= bi * tlen) & (row < (bi + 1) * tlen)
            cr = _dot(qb, s_old.astype(BF16))
            cross = jnp.where(jnp.concatenate([in_seq] * (DV // LANES), axis=1), cr, cross)
            kd_b = jnp.where(jnp.concatenate([in_seq] * (DK // LANES), axis=1), kd, 0.0)
            snew_ref[bi, hh] = gpow_ref[hh] * s_old + _dot_tn(kd_b, v)
        qdec = qdec_ref[hh]
        o_ref[:, hh * DV:(hh + 1) * DV] = intra + cross * jnp.concatenate([qdec] * (DV // LANES), axis=1)


def _ffn_ret_kernel(sub_steps, gpow_ref, x_ref, mod_ref, npre_ref, w1_ref, w2_ref, npost_ref,
                    qkv_ref, s_ref, rope_ref, dec_ref, qdec_ref, kdec_ref,
                    y_ref, o_ref, snew_ref, h2_ref, acc_ref):
    n_chunks = D_FF // FF_CHUNK
    bounds = [n_chunks * s // sub_steps for s in range(sub_steps + 1)]
    for s in range(sub_steps):
        @pl.when(pl.program_id(1) == s)
        def _(s=s):
            if s == 0:
                h2_ref[...] = _ffn_pre(x_ref, mod_ref, npre_ref)
            acc = _ffn_hidden(h2_ref[...], None if s == 0 else acc_ref[...], w1_ref, w2_ref,
                              range(bounds[s], bounds[s + 1]))
            if s == sub_steps - 1:
                _ffn_post(x_ref, mod_ref, acc, npost_ref, y_ref)
            else:
                acc_ref[...] = acc

    _ret_sample_kernel(gpow_ref, qkv_ref, s_ref, rope_ref, dec_ref, qdec_ref, kdec_ref, o_ref, snew_ref)


def _ffn_ret_call(x2d, mod_p, seq_tiles, proj_s, state, gpow, rope8, dec8, qdec8, kdec8, p):
    rows = x2d.shape[0]
    tm = FFN_TM
    nb = state.shape[0]
    bb = RET_SAMPLE_BB
    sub_steps = nb // bb // (rows // tm)
    assert sub_steps * (rows // tm) * bb == nb
    srow = lambda i, j: i * sub_steps + j
    return pl.pallas_call(
        functools.partial(_ffn_ret_kernel, sub_steps),
        grid=(rows // tm, sub_steps),
        in_specs=[
            pl.BlockSpec(memory_space=pltpu.SMEM),
            pl.BlockSpec((tm, D_MODEL), lambda i, j: (i, 0)),
            pl.BlockSpec((None, 1, 3 * D_MODEL), lambda i, j: (i // seq_tiles, 0, 1)),
            _const_spec((1, D_MODEL)),
            _const_spec((D_MODEL // 2, 2 * D_FF)),
            _const_spec((D_FF // 2, D_MODEL)),
            _const_spec((1, D_MODEL)),
            pl.BlockSpec((SUBLANES, OFF_G), lambda i, j: (srow(i, j), 0)),
            pl.BlockSpec((bb, H_RET, DK, DV), lambda i, j: (srow(i, j), 0, 0, 0)),
            _const_spec((SUBLANES, 4 * (DK // 2))),
            _const_spec((H_RET, SUBLANES, SUBLANES)),
            _const_spec((H_RET, SUBLANES, LANES)),
            _const_spec((H_RET, SUBLANES, LANES)),
        ],
        out_specs=[
            pl.BlockSpec((tm, D_MODEL), lambda i, j: (i, 0)),
            pl.BlockSpec((SUBLANES, D_V), lambda i, j: (srow(i, j), 0)),
            pl.BlockSpec((bb, H_RET, DK, DV), lambda i, j: (srow(i, j), 0, 0, 0)),
        ],
        out_shape=[
            jax.ShapeDtypeStruct((rows, D_MODEL), F32),
            jax.ShapeDtypeStruct((proj_s.shape[0], D_V), F32),
            jax.ShapeDtypeStruct(state.shape, F32),
        ],
        scratch_shapes=[pltpu.VMEM((tm, D_MODEL), BF16), pltpu.VMEM((tm, D_MODEL), F32)],
        compiler_params=pltpu.CompilerParams(
            dimension_semantics=("arbitrary", "arbitrary"),
            vmem_limit_bytes=56 * 1024 * 1024),
    )(gpow, x2d, mod_p, p['npre_ffn'], p['wffn_in'], p['wffn_out'], p['npost_ffn'],
      proj_s, state, rope8, dec8, qdec8, kdec8)


def _mix_sample_kernel(x_ref, mod_ref, proj_ref, o_ref, cs_ref, h0_ref,
                       bmg_ref, gnw_ref, wbr_ret_ref, convw_ref, convb_ref, wrg_ref, bra_ref, brx_ref,
                       lru_ref, wbr_rnn_ref, wout_ref, npost_ref,
                       x1_ref, hlast_ref, convnew_ref,
                       xr_ref, sa_ref, sb_ref, zsc_ref, slab_ref):
    tm = SAMPLE_TM
    tlen = 4
    nseq = tm // tlen
    nslab = D_RNN // LANES

    def lanes(sl):
        return slice(sl * LANES, (sl + 1) * LANES)

    def seq_rows(t):
        return pl.ds(t, nseq, stride=tlen)
    x = x_ref[...]
    g1 = mod_ref[:, 2 * D_MODEL:]

    br_ret = jnp.zeros((tm, D_MODEL), F32)
    for hh in range(H_RET):
        on = _group_norm(o_ref[:, hh * DV:(hh + 1) * DV])
        g = proj_ref[:, OFF_G + hh * DV:OFF_G + (hh + 1) * DV]
        ry = (on * gnw_ref[:, hh * DV:(hh + 1) * DV] * _silu(g)).astype(BF16)
        br_ret = br_ret + _dot(ry, _wb(wbr_ret_ref, hh * DV, (hh + 1) * DV))

    for sl in range(nslab):
        for j in range(CONV_W - 1):
            slab_ref[sl, seq_rows(j), :] = cs_ref[j, :, lanes(sl)]
        slab_ref[sl, seq_rows(CONV_W - 1), :] = h0_ref[:, lanes(sl)]
    for sl in range(nslab):
        zsc_ref[0:tm, lanes(sl)] = slab_ref[sl]
    zsc_ref[tm:tm + SUBLANES, :] = jnp.zeros((SUBLANES, D_RNN), F32)
    tpos = lax.broadcasted_iota(jnp.int32, (tm, D_RNN), 0) & (tlen - 1)
    xr = proj_ref[:, OFF_XR:OFF_XR + D_RNN]
    xr_ref[0:SUBLANES, :] = jnp.zeros((SUBLANES, D_RNN), F32)
    xr_ref[SUBLANES:SUBLANES + tm, :] = xr
    cw = convw_ref[...]
    xconv = convb_ref[...]
    for j in range(CONV_W - 1):
        sft = CONV_W - 1 - j
        shifted = jnp.where(tpos >= sft, xr_ref[SUBLANES - sft:SUBLANES - sft + tm, :], 0.0)
        carried = jnp.where(tpos < sft, zsc_ref[CONV_W - 1 - sft:CONV_W - 1 - sft + tm, :], 0.0)
        xconv = xconv + (shifted + carried) * cw[j:j + 1, :]
    xconv = xconv + xr * cw[CONV_W - 1:CONV_W, :]

    a, b = _lru_coeffs(xconv, _lru_gate_pre(xconv.astype(BF16), wrg_ref), bra_ref[...], brx_ref[...], lru_ref[...])
    b = jnp.where(tpos == 0, b + a * zsc_ref[CONV_W - 1:CONV_W - 1 + tm, :], b)
    sa_ref[0:SUBLANES, :] = jnp.zeros((SUBLANES, D_RNN), F32)
    sb_ref[0:SUBLANES, :] = jnp.zeros((SUBLANES, D_RNN), F32)
    for s in (1, 2):
        sa_ref[SUBLANES:SUBLANES + tm, :] = a
        sb_ref[SUBLANES:SUBLANES + tm, :] = b
        keep = tpos >= s
        ap = jnp.where(keep, sa_ref[SUBLANES - s:SUBLANES - s + tm, :], 1.0)
        bp = jnp.where(keep, sb_ref[SUBLANES - s:SUBLANES - s + tm, :], 0.0)
        b = a * bp + b
        a = a * ap
    for sl in range(nslab):
        slab_ref[sl] = b[:, lanes(sl)]
    for sl in range(nslab):
        hlast_ref[:, lanes(sl)] = slab_ref[sl, seq_rows(tlen - 1), :]
    for sl in range(nslab):
        slab_ref[sl] = xr[:, lanes(sl)]
    for sl in range(nslab):
        for j in range(CONV_W - 1):
            convnew_ref[j, :, lanes(sl)] = slab_ref[sl, seq_rows(tlen - (CONV_W - 1) + j), :]

    gr = proj_ref[:, OFF_GR:OFF_GR + D_RNN]
    rnn_y_b = (b * jax.nn.gelu(gr, approximate=True)).astype(BF16)
    gate_pre = proj_ref[:, OFF_MG:OFF_MG + 2 * D_MODEL]
    x1_ref[...] = _mix_tail(x, g1, br_ret, rnn_y_b, gate_pre, bmg_ref[...], wbr_rnn_ref, wout_ref,
                            npost_ref[...])


def _mix_sample_call(x2d, mod_s, proj_s, o_s, cs_t, h0, p):
    rows = x2d.shape[0]
    tm = SAMPLE_TM
    row_spec = lambda w: pl.BlockSpec((tm, w), lambda i: (i, 0))
    return pl.pallas_call(
        _mix_sample_kernel,
        grid=(rows // tm,),
        in_specs=[
            row_spec(D_MODEL), row_spec(3 * D_MODEL), row_spec(N_CAT), row_spec(D_V),
            pl.BlockSpec((CONV_W - 1, tm // 4, D_RNN), lambda i: (0, i, 0)),
            pl.BlockSpec((tm // 4, D_RNN), lambda i: (i, 0)),
            _const_spec((1, 2 * D_MODEL)),
            _const_spec((1, D_V)),
            _const_spec((D_V // 2, D_MODEL)),
            _const_spec((CONV_W, D_RNN)),
            _const_spec((1, D_RNN)),
            _const_spec((D_RNN // 2, 2 * RNN_BLOCK)),
            _const_spec((1, D_RNN)),
            _const_spec((1, D_RNN)),
            _const_spec((1, D_RNN)),
            _const_spec((D_RNN // 2, D_MODEL)),
            _const_spec((D_MODEL // 2, D_MODEL)),
            _const_spec((1, D_MODEL)),
        ],
        out_specs=[
            row_spec(D_MODEL),
            pl.BlockSpec((tm // 4, D_RNN), lambda i: (i, 0)),
            pl.BlockSpec((CONV_W - 1, tm // 4, D_RNN), lambda i: (0, i, 0)),
        ],
        out_shape=[
            jax.ShapeDtypeStruct((rows, D_MODEL), F32),
            jax.ShapeDtypeStruct((rows // 4, D_RNN), F32),
            jax.ShapeDtypeStruct((CONV_W - 1, rows // 4, D_RNN), F32),
        ],
        scratch_shapes=[pltpu.VMEM((tm + SUBLANES, D_RNN), F32)] * 4
        + [pltpu.VMEM((D_RNN // LANES, tm, LANES), F32)],
        compiler_params=pltpu.CompilerParams(
            dimension_semantics=("arbitrary",),
            vmem_limit_bytes=56 * 1024 * 1024),
    )(x2d, mod_s, proj_s, o_s, cs_t, h0, p['bmg'], p['gnw'], p['wbr_ret'], p['convw'],
      p['convb'], p['wrg'], p['bra'], p['brx'], p['lru'], p['wbr_rnn'], p['wout'], p['npost'])


def _rope_table(pos):
    half = DK // 2
    inv = ROPE_BASE ** (-jnp.arange(half, dtype=F32) / half)
    ang = pos.astype(F32)[:, None] * inv[None, :]
    cos, sin = jnp.cos(ang), jnp.sin(ang)
    ks = DK ** -0.5
    return jnp.concatenate([cos, sin, cos * ks, sin * ks], axis=1)


def _decay_tables(tpos, same_seq, chunk):
    lg = jnp.log(1.0 - 2.0 ** (-5.0 - jnp.arange(H_RET, dtype=F32)))
    idx = tpos.astype(F32)
    diff = idx[:, None] - idx[None, :]
    causal = (diff >= 0) & same_seq
    dec = jnp.where(causal[None], jnp.exp(jnp.where(causal, diff, 0.0)[None] * lg[:, None, None]), 0.0)
    qdec = jnp.exp((idx + 1.0)[None, :] * lg[:, None])
    kdec = jnp.exp((chunk - 1.0 - idx)[None, :] * lg[:, None])
    rep = lambda a: jnp.broadcast_to(a[:, :, None], a.shape + (LANES,))
    gpow = jnp.exp(chunk * lg)
    return dec, rep(qdec), rep(kdec), gpow


def kernel(x_prompt, x_sample, state_ret, state_rnn_h, state_rnn_conv, c_prompt, c_sample,
           w_ada, b_ada, norm_pre_mix, norm_post_mix, norm_pre_ffn, norm_post_ffn,
           w_in, ret_gn_w, w_br_ret, conv_w, conv_b, w_rg_a, b_rg_a, w_rg_x, b_rg_x,
           lru_param, w_br_rnn, w_mgate, b_mgate, w_out, w_ffn_in, w_ffn_out):
    depth = w_in.shape[0]
    assert depth == 1, "single layer step"
    nb, seq, _ = x_prompt.shape
    nsb, sseq, _ = x_sample.shape
    assert seq % PROMPT_TM == 0 and sseq * RET_SAMPLE_BB == SUBLANES and sseq == CONV_W
    l = 0
    row = lambda a: a[l][None, :]
    p = dict(
        npre=row(norm_pre_mix), npost=row(norm_post_mix), npre_ffn=row(norm_pre_ffn), npost_ffn=row(norm_post_ffn),
        win=_pack_rows(w_in[l], kb=D_MODEL // 8), wmg=_pack_rows(w_mgate[l]),
        bmg=row(b_mgate), gnw=row(ret_gn_w), wbr_ret=_pack_rows(w_br_ret[l]),
        convw=conv_w[l], convb=row(conv_b),
        wrg=_pack_rows(jnp.concatenate([w_rg_a[l], w_rg_x[l]], axis=2).reshape(D_RNN, 2 * RNN_BLOCK)),
        bra=row(b_rg_a), brx=row(b_rg_x), lru=row(lru_param),
        wbr_rnn=_pack_rows(w_br_rnn[l]), wout=_pack_rows(w_out[l]),
    )

    rows_s = nsb * sseq
    c_all = jnp.concatenate([jnp.repeat(c_sample, sseq, axis=0), c_prompt], axis=0)
    mod_all = _mod_call(c_all, w_ada[l], row(b_ada))
    mod_p = mod_all[rows_s:].reshape(nb, 1, 6 * D_MODEL)
    mod_s = mod_all

    tm = PROMPT_TM
    ng = tm // SUBLANES
    r = jnp.arange(tm)
    tpos = (r % SUBLANES) * ng + r // SUBLANES
    pos_p = (jnp.arange(seq // tm)[:, None] * tm + tpos[None, :]).reshape(seq).astype(jnp.int32)
    rope_p = _rope_table(pos_p)
    dec, qdec, kdec, gpow = _decay_tables(tpos, jnp.ones((tm, tm), bool), float(tm))
    interleave = lambda a: a.reshape(nb, seq // tm, SUBLANES, ng, D_MODEL).swapaxes(2, 3).reshape(nb, seq, D_MODEL)
    restore = lambda a: a.reshape(nb, seq // tm, ng, SUBLANES, D_MODEL).swapaxes(2, 3).reshape(nb, seq, D_MODEL)
    x1p, ret_p, hlast_p, conv_p, p['wffn_in'], p['wffn_out'] = _mix_prompt_call(
        interleave(x_prompt), mod_p, gpow, rope_p, dec, qdec, kdec, p, w_ffn_in[l], w_ffn_out[l])

    xs2d = x_sample.reshape(rows_s, D_MODEL)
    proj_s = _proj_sample_call(xs2d, mod_s, p)
    r8 = jnp.arange(SUBLANES)
    rope_s = _rope_table(PAST_LEN + (r8 % sseq).astype(jnp.int32))
    same = (r8[:, None] // sseq) == (r8[None, :] // sseq)
    dec8, qdec8, kdec8, gpow_s = _decay_tables(r8 % sseq, same, float(sseq))
    yp2d, o_s, ret_s = _ffn_ret_call(x1p.reshape(nb * seq, D_MODEL), mod_p, seq // FFN_TM, proj_s, state_ret[l],
                                     gpow_s, rope_s, dec8, qdec8, kdec8, p)
    yp = restore(yp2d.reshape(nb, seq, D_MODEL))
    cs_t = jnp.swapaxes(state_rnn_conv[l], 0, 1)
    x1s, hlast_s, conv_t = _mix_sample_call(xs2d, mod_s, proj_s, o_s, cs_t, state_rnn_h[l], p)
    ys = _ffn_call(x1s, mod_s, FFN_TM, 1, p).reshape(nsb, sseq, D_MODEL)
    conv_s = jnp.swapaxes(conv_t, 0, 1)

    return (yp, ys, ret_p[None], ret_s[None], hlast_p.reshape(nb, D_RNN)[None], hlast_s[None],
            conv_p[None], conv_s[None])
```

```python
import functools

import jax
import jax.numpy as jnp
from jax import lax
from jax.experimental import pallas as pl
from jax.experimental.pallas import tpu as pltpu

F32 = jnp.float32
BF16 = jnp.bfloat16

D_MODEL = 1024
H_RET = 4
DK = D_MODEL // H_RET
DV = 2 * DK
D_QK = H_RET * DK
D_V = H_RET * DV
D_RNN = 1536
RNN_BLOCK = 128
N_RNN_BLOCKS = D_RNN // RNN_BLOCK
CONV_W = 4
LRU_C = 8.0
D_FF = 2816
ROPE_BASE = 10000.0
GN_EPS = 1e-5
RMS_EPS = 1e-6
PAST_LEN = 16384

OFF_Q = 0
OFF_K = OFF_Q + D_QK
OFF_V = OFF_K + D_QK
OFF_G = OFF_V + D_V
OFF_XR = OFF_G + D_V
OFF_GR = OFF_XR + D_RNN
OFF_MG = OFF_GR + D_RNN
N_CAT = OFF_MG + 2 * D_MODEL

SUBLANES = 8
LANES = 128
MXU_DIM = 256
VMEM_BYTES_V7X = 64 * 1024 * 1024

PROMPT_TM = 256
FFN_TM = 512
FF_CHUNK = MXU_DIM
SAMPLE_TM = 128
RET_SAMPLE_BB = 2


def _dot(a, b):
    return jnp.dot(a, b, preferred_element_type=F32)


def _dot_nt(a, b):
    return lax.dot_general(a, b, (((1,), (1,)), ((), ())), preferred_element_type=F32)


def _dot_tn(a, b):
    return lax.dot_general(a, b, (((0,), (0,)), ((), ())), preferred_element_type=F32)


def _wb(ref, k0=None, k1=None, c0=None, c1=None):
    rs = slice(None) if k0 is None else slice(k0 // 2, k1 // 2)
    cs = slice(None) if c0 is None else slice(c0, c1)
    return pltpu.bitcast(ref[rs, cs], BF16)


def _to_words(w):
    return pltpu.bitcast(w.astype(BF16), jnp.uint32)


def _pack_kernel(w_ref, o_ref):
    o_ref[...] = _to_words(w_ref[...])


def _pack_rows(w, kb=None):
    k, n = w.shape
    kb = k if kb is None else kb
    return pl.pallas_call(
        _pack_kernel,
        grid=(k // kb,),
        in_specs=[pl.BlockSpec((kb, n), lambda i: (i, 0))],
        out_specs=pl.BlockSpec((kb // 2, n), lambda i: (i, 0)),
        out_shape=jax.ShapeDtypeStruct((k // 2, n), jnp.uint32),
        compiler_params=pltpu.CompilerParams(
            dimension_semantics=("arbitrary",),
            vmem_limit_bytes=40 * 1024 * 1024),
    )(w)


def _rms(x, w):
    ms = jnp.mean(x * x, axis=-1, keepdims=True)
    return x * lax.rsqrt(ms + RMS_EPS) * w


def _sigmoid(x):
    return 0.5 * jnp.tanh(0.5 * x) + 0.5


def _silu(x):
    return x * _sigmoid(x)


def _rope(x, cos, sin):
    half = DK // 2
    x1, x2 = x[:, :half], x[:, half:]
    return jnp.concatenate([x1 * cos - x2 * sin, x1 * sin + x2 * cos], axis=1)


def _group_norm(o):
    mu = jnp.mean(o, axis=-1, keepdims=True)
    d = o - mu
    var = jnp.mean(d * d, axis=-1, keepdims=True)
    return d * lax.rsqrt(var + GN_EPS)


def _lru_gate_pre(xcb, wrg_ref):
    return [_dot(xcb[:, n * RNN_BLOCK:(n + 1) * RNN_BLOCK], _wb(wrg_ref, n * RNN_BLOCK, (n + 1) * RNN_BLOCK))
            for n in range(N_RNN_BLOCKS)]


def _lru_coeffs(xconv, pre, b_a, b_x, lru):
    ra = jnp.concatenate([p[:, :RNN_BLOCK] for p in pre], axis=1) + b_a
    ri = jnp.concatenate([p[:, RNN_BLOCK:] for p in pre], axis=1) + b_x
    r = _sigmoid(ra)
    i = _sigmoid(ri)
    z = -lru
    sp = jnp.maximum(z, 0.0) + jnp.log(1.0 + jnp.exp(-jnp.abs(z)))
    log_a = -LRU_C * r * sp
    a = jnp.exp(log_a)
    beta = jnp.sqrt(-jnp.tanh(log_a) * (a * a + 1.0))
    return a, beta * (i * xconv)


def _mix_tail(x, g1, br_ret, rnn_y_b, gate_pre, b_mg, wbr_rnn_ref, wout_ref, npost):
    br_rnn = _dot(rnn_y_b, _wb(wbr_rnn_ref))
    gates = _sigmoid(gate_pre + b_mg)
    ga, gb = gates[:, :D_MODEL], gates[:, D_MODEL:]
    mixed = _dot((ga * br_ret + gb * br_rnn).astype(BF16), _wb(wout_ref))
    return x + g1 * _rms(mixed, npost)


def _mod_kernel(c_ref, w_ref, b_ref, o_ref):
    a = _silu(c_ref[...]).astype(BF16)
    o_ref[...] = _dot(a, w_ref[...].astype(BF16)) + b_ref[...]


def _mod_call(c_all, w_ada_f32, b_ada):
    rows = c_all.shape[0]
    tn = 2 * D_MODEL
    return pl.pallas_call(
        _mod_kernel,
        grid=(6 * D_MODEL // tn,),
        in_specs=[
            pl.BlockSpec((rows, D_MODEL), lambda j: (0, 0)),
            pl.BlockSpec((D_MODEL, tn), lambda j: (0, j)),
            pl.BlockSpec((1, tn), lambda j: (0, j)),
        ],
        out_specs=pl.BlockSpec((rows, tn), lambda j: (0, j)),
        out_shape=jax.ShapeDtypeStruct((rows, 6 * D_MODEL), F32),
        compiler_params=pltpu.CompilerParams(
            dimension_semantics=("arbitrary",),
            vmem_limit_bytes=48 * 1024 * 1024),
    )(c_all, w_ada_f32, b_ada)


def _mix_prompt_kernel(gpow_ref, x_ref, mod_ref, npre_ref, wcat_ref, wmg_ref, bmg_ref, rope_ref,
                       dec_ref, qdec_ref, kdec_ref, gnw_ref, wbr_ret_ref, convw_ref, convb_ref,
                       wrg_ref, bra_ref, brx_ref, lru_ref, wbr_rnn_ref, wout_ref, npost_ref,
                       wf1_ref, wf2_ref,
                       x1_ref, s_ref, hlast_ref, convnew_ref, wf1b_ref, wf2b_ref,
                       xr_ref, prevg_ref, hc_ref, perm_ref):
    tm = PROMPT_TM
    ng = tm // SUBLANES
    halo = (CONV_W - 1) * SUBLANES
    t = pl.program_id(1)

    @pl.when(t == 0)
    def _():
        s_ref[...] = jnp.zeros_like(s_ref)
        prevg_ref[...] = jnp.zeros_like(prevg_ref)
        hc_ref[...] = jnp.zeros_like(hc_ref)

    nslab = D_MODEL // LANES

    def lanes(sl):
        return slice(sl * LANES, (sl + 1) * LANES)

    def seg_rows(s):
        return pl.ds(s, ng, stride=SUBLANES)

    for sl in range(nslab):
        for s in range(SUBLANES):
            perm_ref[sl, seg_rows(s), :] = x_ref[s * ng:(s + 1) * ng, lanes(sl)]
    xp = jnp.concatenate([perm_ref[sl] for sl in range(nslab)], axis=1)
    sh1, sc1 = mod_ref[:, :D_MODEL], mod_ref[:, D_MODEL:2 * D_MODEL]
    hb = (_rms(xp, npre_ref[...]) * (1.0 + sc1) + sh1).astype(BF16)
    sub = lax.broadcasted_iota(jnp.int32, (SUBLANES, D_RNN), 0)
    half = DK // 2
    cos, sin = rope_ref[:, 0:half], rope_ref[:, half:2 * half]
    cosk, sink = rope_ref[:, 2 * half:3 * half], rope_ref[:, 3 * half:4 * half]
    st = {}

    def xr_proj():
        xr = _dot(hb, _wb(wcat_ref, c0=OFF_XR, c1=OFF_XR + D_RNN))
        xr_ref[halo:halo + tm, :] = xr
        for kk in range(1, CONV_W):
            r0 = (CONV_W - 1 - kk) * SUBLANES
            cur = xr[(ng - kk) * SUBLANES:(ng - kk + 1) * SUBLANES, :]
            prv = prevg_ref[r0:r0 + SUBLANES, :]
            xr_ref[r0:r0 + SUBLANES, :] = pltpu.roll(jnp.where(sub == SUBLANES - 1, prv, cur), 1, 0)
        prevg_ref[...] = xr[tm - halo:, :]
        for kk in range(1, CONV_W):
            r1 = (ng - kk) * SUBLANES + SUBLANES - 1
            convnew_ref[CONV_W - 1 - kk:CONV_W - kk, :] = xr[r1:r1 + 1, :]

    def lru_conv():
        cw = convw_ref[...]
        xconv = convb_ref[...]
        for j in range(CONV_W):
            r0 = halo - (CONV_W - 1 - j) * SUBLANES
            xconv = xconv + xr_ref[r0:r0 + tm, :] * cw[j:j + 1, :]
        st['xconv'] = xconv

    def lru_gate_proj():
        st['gpre'] = _lru_gate_pre(st['xconv'].astype(BF16), wrg_ref)

    def lru_coef():
        st['a'], st['b'] = _lru_coeffs(st.pop('xconv'), st.pop('gpre'), bra_ref[...], brx_ref[...], lru_ref[...])

    def lru_scan():
        a, b = st['a'], st['b']
        ca, cb = a[0:SUBLANES, :], b[0:SUBLANES, :]
        cas, cbs = [ca], [cb]
        for gi in range(1, ng):
            ag = a[gi * SUBLANES:(gi + 1) * SUBLANES, :]
            cb = ag * cb + b[gi * SUBLANES:(gi + 1) * SUBLANES, :]
            ca = ag * ca
            cas.append(ca)
            cbs.append(cb)
        cin = jnp.where(sub == 0, hc_ref[SUBLANES - 1:SUBLANES, :], 0.0)
        for s in range(SUBLANES - 1):
            cin = jnp.where(sub == s + 1, pltpu.roll(ca * cin + cb, 1, 0), cin)
        seg_end = ca * cin + cb
        hc_ref[...] = seg_end
        hlast_ref[...] = seg_end[SUBLANES - 1:SUBLANES, :]
        st['hseq'] = jnp.concatenate([cas[gi] * cin + cbs[gi] for gi in range(ng)], axis=0)

    def gr_proj():
        st['gr'] = _dot(hb, _wb(wcat_ref, c0=OFF_GR, c1=OFF_GR + D_RNN))

    def lru_y():
        st['rnn_y'] = (st.pop('hseq') * jax.nn.gelu(st.pop('gr'), approximate=True)).astype(BF16)

    def lru_out():
        st['br_rnn'] = _dot(st.pop('rnn_y'), _wb(wbr_rnn_ref))

    def gate_proj():
        st['gate_pre'] = _dot(hb, _wb(wmg_ref))

    def gate_act():
        st['gates'] = _sigmoid(st.pop('gate_pre') + bmg_ref[...])

    def head_proj(hh):
        q = _dot(hb, _wb(wcat_ref, c0=OFF_Q + hh * DK, c1=OFF_Q + (hh + 1) * DK))
        k = _dot(hb, _wb(wcat_ref, c0=OFF_K + hh * DK, c1=OFF_K + (hh + 1) * DK))
        vb = _dot(hb, _wb(wcat_ref, c0=OFF_V + hh * DV, c1=OFF_V + (hh + 1) * DV)).astype(BF16)
        g = _dot(hb, _wb(wcat_ref, c0=OFF_G + hh * DV, c1=OFF_G + (hh + 1) * DV))
        st['proj', hh] = (q, k, vb, g)

    def head_rope(hh):
        q, k, vb, g = st.pop(('proj', hh))
        kr = _rope(k, cosk, sink)
        kdec = kdec_ref[hh]
        kdb = (kr * jnp.concatenate([kdec, kdec], axis=1)).astype(BF16)
        st['rope', hh] = (_rope(q, cos, sin).astype(BF16), kr.astype(BF16), kdb, vb, g)

    def head_qk(hh):
        qb, kb, kdb, vb, g = st.pop(('rope', hh))
        scores = _dot_nt(qb, kb)
        cross = _dot(qb, s_ref[hh].astype(BF16))
        st['qk', hh] = (scores, cross, vb, g)
        st['kv', hh] = (kdb, vb)

    def head_state(hh):
        kdb, vb = st.pop(('kv', hh))
        s_ref[hh] = gpow_ref[hh] * s_ref[hh] + _dot_tn(kdb, vb)

    def head_decay(hh):
        scores, cross, vb, g = st.pop(('qk', hh))
        st['dec', hh] = ((scores * dec_ref[hh]).astype(BF16), cross, vb, g)

    def head_pv(hh):
        sb, cross, vb, g = st.pop(('dec', hh))
        st['pv', hh] = (_dot(sb, vb), cross, g)

    def head_norm(hh):
        intra, cross, g = st.pop(('pv', hh))
        qdec = qdec_ref[hh]
        on = _group_norm(intra + cross * jnp.concatenate([qdec] * (DV // LANES), axis=1))
        st['ry', hh] = (on * gnw_ref[:, hh * DV:(hh + 1) * DV] * _silu(g)).astype(BF16)

    def head_out(hh):
        part = _dot(st.pop(('ry', hh)), _wb(wbr_ret_ref, hh * DV, (hh + 1) * DV))
        st['br_ret'] = part if hh == 0 else st['br_ret'] + part

    heads = range(H_RET)
    order = (
        [xr_proj, (head_proj, 0), lru_conv, lru_gate_proj, (head_proj, 1), (head_proj, 2), lru_coef, (head_proj, 3)]
        + [(head_rope, h) for h in heads] + [gr_proj] + [(head_qk, h) for h in heads]
        + [gate_proj] + [(head_decay, h) for h in heads] + [lru_scan]
        + [(head_pv, h) for h in heads] + [(head_state, h) for h in heads] + [lru_y, lru_out]
        + [(head_norm, h) for h in heads] + [(head_out, h) for h in heads] + [gate_act]
    )
    for stage in order:
        if isinstance(stage, tuple):
            stage[0](stage[1])
        else:
            stage()

    gates = st['gates']
    ga, gb = gates[:, :D_MODEL], gates[:, D_MODEL:]
    mixed = _dot((ga * st['br_ret'] + gb * st['br_rnn']).astype(BF16), _wb(wout_ref))
    delta = mod_ref[:, 2 * D_MODEL:] * _rms(mixed, npost_ref[...])
    for sl in range(nslab):
        perm_ref[sl] = delta[:, lanes(sl)]
    for sl in range(nslab):
        for s in range(SUBLANES):
            rows = slice(s * ng, (s + 1) * ng)
            x1_ref[rows, lanes(sl)] = x_ref[rows, lanes(sl)] + perm_ref[sl, seg_rows(s), :]

    wf1b_ref[...] = _to_words(wf1_ref[...])
    wf2b_ref[...] = _to_words(wf2_ref[...])


def _const_spec(shape):
    nd = len(shape)
    return pl.BlockSpec(shape, lambda *_: (0,) * nd, pipeline_mode=pl.Buffered(1))


def _mix_prompt_call(x, mod3, gpow, rope_tab, dec, qdec, kdec, p, wf1, wf2):
    nb, seq, _ = x.shape
    tm = PROMPT_TM
    nt = seq // tm
    steps = nb * nt
    bf16_rows = 2 * SUBLANES
    r1 = wf1.shape[0] // steps
    assert r1 * steps == wf1.shape[0] and r1 % bf16_rows == 0
    rep2 = 1
    while wf2.shape[0] % (steps // rep2) or (wf2.shape[0] // (steps // rep2)) % bf16_rows:
        rep2 *= 2
    r2 = wf2.shape[0] // (steps // rep2)
    wf1_spec = pl.BlockSpec((r1, wf1.shape[1]), lambda b, t: (b * nt + t, 0))
    wf2_spec = pl.BlockSpec((r2, wf2.shape[1]), lambda b, t: ((b * nt + t) // rep2, 0))
    in_specs = [
        pl.BlockSpec(memory_space=pltpu.SMEM),
        pl.BlockSpec((None, tm, D_MODEL), lambda b, t: (b, t, 0)),
        pl.BlockSpec((None, 1, 3 * D_MODEL), lambda b, t: (b, 0, 0)),
        _const_spec((1, D_MODEL)),
        _const_spec((D_MODEL // 2, OFF_MG)),
        _const_spec((D_MODEL // 2, 2 * D_MODEL)),
        _const_spec((1, 2 * D_MODEL)),
        pl.BlockSpec((tm, 4 * (DK // 2)), lambda b, t: (t, 0)),
        _const_spec((H_RET, tm, tm)),
        _const_spec((H_RET, tm, LANES)),
        _const_spec((H_RET, tm, LANES)),
        _const_spec((1, D_V)),
        _const_spec((D_V // 2, D_MODEL)),
        _const_spec((CONV_W, D_RNN)),
        _const_spec((1, D_RNN)),
        _const_spec((D_RNN // 2, 2 * RNN_BLOCK)),
        _const_spec((1, D_RNN)),
        _const_spec((1, D_RNN)),
        _const_spec((1, D_RNN)),
        _const_spec((D_RNN // 2, D_MODEL)),
        _const_spec((D_MODEL // 2, D_MODEL)),
        _const_spec((1, D_MODEL)),
        wf1_spec,
        wf2_spec,
    ]
    out_specs = [
        pl.BlockSpec((None, tm, D_MODEL), lambda b, t: (b, t, 0)),
        pl.BlockSpec((None, H_RET, DK, DV), lambda b, t: (b, 0, 0, 0)),
        pl.BlockSpec((None, 1, D_RNN), lambda b, t: (b, 0, 0)),
        pl.BlockSpec((None, CONV_W - 1, D_RNN), lambda b, t: (b, 0, 0)),
        pl.BlockSpec((r1 // 2, wf1.shape[1]), wf1_spec.index_map),
        pl.BlockSpec((r2 // 2, wf2.shape[1]), wf2_spec.index_map),
    ]
    out_shape = [
        jax.ShapeDtypeStruct((nb, seq, D_MODEL), F32),
        jax.ShapeDtypeStruct((nb, H_RET, DK, DV), F32),
        jax.ShapeDtypeStruct((nb, 1, D_RNN), F32),
        jax.ShapeDtypeStruct((nb, CONV_W - 1, D_RNN), F32),
        jax.ShapeDtypeStruct((wf1.shape[0] // 2, wf1.shape[1]), jnp.uint32),
        jax.ShapeDtypeStruct((wf2.shape[0] // 2, wf2.shape[1]), jnp.uint32),
    ]
    halo = (CONV_W - 1) * SUBLANES
    scratch = [
        pltpu.VMEM((halo + tm, D_RNN), F32),
        pltpu.VMEM((halo, D_RNN), F32),
        pltpu.VMEM((SUBLANES, D_RNN), F32),
        pltpu.VMEM((D_MODEL // LANES, tm, LANES), F32),
    ]
    return pl.pallas_call(
        _mix_prompt_kernel,
        grid=(nb, nt),
        in_specs=in_specs,
        out_specs=out_specs,
        out_shape=out_shape,
        scratch_shapes=scratch,
        compiler_params=pltpu.CompilerParams(
            dimension_semantics=("arbitrary", "arbitrary"),
            vmem_limit_bytes=VMEM_BYTES_V7X - 4 * 1024 * 1024),
    )(gpow, x, mod3, p['npre'], p['win'], p['wmg'], p['bmg'], rope_tab, dec, qdec, kdec, p['gnw'], p['wbr_ret'],
      p['convw'], p['convb'], p['wrg'], p['bra'], p['brx'], p['lru'], p['wbr_rnn'], p['wout'], p['npost'],
      wf1, wf2)


def _ffn_pre(x_ref, mod_ref, npre_ref):
    m = mod_ref[...]
    sh2, sc2 = m[:, :D_MODEL], m[:, D_MODEL:2 * D_MODEL]
    return (_rms(x_ref[...], npre_ref[...]) * (1.0 + sc2) + sh2).astype(BF16)


def _ffn_hidden(h2, acc, w1_ref, w2_ref, chunk_ids):
    for j in chunk_ids:
        c0 = j * FF_CHUNK
        fg = _dot(h2, _wb(w1_ref, c0=c0, c1=c0 + FF_CHUNK))
        fu = _dot(h2, _wb(w1_ref, c0=D_FF + c0, c1=D_FF + c0 + FF_CHUNK))
        part = _dot((_silu(fg) * fu).astype(BF16), _wb(w2_ref, c0, c0 + FF_CHUNK))
        acc = part if acc is None else acc + part
    return acc


def _ffn_post(x_ref, mod_ref, acc, npost_ref, o_ref):
    o_ref[...] = x_ref[...] + mod_ref[:, 2 * D_MODEL:] * _rms(acc, npost_ref[...])


def _ffn_kernel(x_ref, mod_ref, npre_ref, w1_ref, w2_ref, npost_ref, o_ref):
    h2 = _ffn_pre(x_ref, mod_ref, npre_ref)
    acc = _ffn_hidden(h2, None, w1_ref, w2_ref, range(D_FF // FF_CHUNK))
    _ffn_post(x_ref, mod_ref, acc, npost_ref, o_ref)


def _ffn_call(x2d, mod, mod_rows_per_tile, seq_tiles, p):
    rows = x2d.shape[0]
    tm = FFN_TM
    if mod_rows_per_tile == 1:
        mod_spec = pl.BlockSpec((None, 1, 3 * D_MODEL), lambda i: (i // seq_tiles, 0, 1))
    else:
        mod_spec = pl.BlockSpec((tm, 3 * D_MODEL), lambda i: (i, 1))
    return pl.pallas_call(
        _ffn_kernel,
        grid=(rows // tm,),
        in_specs=[
            pl.BlockSpec((tm, D_MODEL), lambda i: (i, 0)),
            mod_spec,
            _const_spec((1, D_MODEL)),
            _const_spec((D_MODEL // 2, 2 * D_FF)),
            _const_spec((D_FF // 2, D_MODEL)),
            _const_spec((1, D_MODEL)),
        ],
        out_specs=pl.BlockSpec((tm, D_MODEL), lambda i: (i, 0)),
        out_shape=jax.ShapeDtypeStruct((rows, D_MODEL), F32),
        compiler_params=pltpu.CompilerParams(
            dimension_semantics=("arbitrary",),
            vmem_limit_bytes=48 * 1024 * 1024),
    )(x2d, mod, p['npre_ffn'], p['wffn_in'], p['wffn_out'], p['npost_ffn'])


def _proj_sample_kernel(n_in_tiles, x_ref, mod_ref, npre_ref, win_ref, wmg_ref, o_ref):
    m = mod_ref[...]
    sh1, sc1 = m[:, :D_MODEL], m[:, D_MODEL:2 * D_MODEL]
    h = (_rms(x_ref[...], npre_ref[...]) * (1.0 + sc1) + sh1).astype(BF16)
    j = pl.program_id(0)

    @pl.when(j < n_in_tiles)
    def _():
        o_ref[...] = _dot(h, _wb(win_ref))

    @pl.when(j >= n_in_tiles)
    def _():
        o_ref[...] = _dot(h, _wb(wmg_ref))


def _proj_sample_call(x2d, mod_s, p):
    rows = x2d.shape[0]
    tn = D_MODEL
    n_in = OFF_MG // tn
    return pl.pallas_call(
        functools.partial(_proj_sample_kernel, n_in),
        grid=(N_CAT // tn,),
        in_specs=[
            pl.BlockSpec((rows, D_MODEL), lambda j: (0, 0)),
            pl.BlockSpec((rows, 3 * D_MODEL), lambda j: (0, 0)),
            pl.BlockSpec((1, D_MODEL), lambda j: (0, 0)),
            pl.BlockSpec((D_MODEL // 2, tn), lambda j: (0, jnp.minimum(j, n_in - 1))),
            pl.BlockSpec((D_MODEL // 2, tn), lambda j: (0, jnp.maximum(j - n_in, 0))),
        ],
        out_specs=pl.BlockSpec((rows, tn), lambda j: (0, j)),
        out_shape=jax.ShapeDtypeStruct((rows, N_CAT), F32),
        compiler_params=pltpu.CompilerParams(
            dimension_semantics=("arbitrary",),
            vmem_limit_bytes=48 * 1024 * 1024),
    )(x2d, mod_s, p['npre'], p['win'], p['wmg'])


def _ret_sample_kernel(gpow_ref, qkv_ref, s_ref, rope_ref, dec_ref, qdec_ref, kdec_ref, o_ref, snew_ref):
    half = DK // 2
    cos, sin = rope_ref[:, 0:half], rope_ref[:, half:2 * half]
    cosk, sink = rope_ref[:, 2 * half:3 * half], rope_ref[:, 3 * half:4 * half]
    nseq = RET_SAMPLE_BB
    tlen = SUBLANES // nseq
    row = lax.broadcasted_iota(jnp.int32, (SUBLANES, LANES), 0)
    for hh in range(H_RET):
        q = _rope(qkv_ref[:, OFF_Q + hh * DK:OFF_Q + (hh + 1) * DK], cos, sin)
        k = _rope(qkv_ref[:, OFF_K + hh * DK:OFF_K + (hh + 1) * DK], cosk, sink)
        v = qkv_ref[:, OFF_V + hh * DV:OFF_V + (hh + 1) * DV]
        kdec = kdec_ref[hh]
        kd = k * jnp.concatenate([kdec, kdec], axis=1)
        scores = _dot_nt(q, k) * dec_ref[hh]
        intra = _dot(scores, v)
        qb = q.astype(BF16)
        cross = jnp.zeros((SUBLANES, DV), F32)
        for bi in range(nseq):
            s_old = s_ref[bi, hh]
            in_seq = (row >= bi * tlen) & (row < (bi + 1) * tlen)
            cr = _dot(qb, s_old.astype(BF16))
            cross = jnp.where(jnp.concatenate([in_seq] * (DV // LANES), axis=1), cr, cross)
            kd_b = jnp.where(jnp.concatenate([in_seq] * (DK // LANES), axis=1), kd, 0.0)
            snew_ref[bi, hh] = gpow_ref[hh] * s_old + _dot_tn(kd_b, v)
        qdec = qdec_ref[hh]
        o_ref[:, hh * DV:(hh + 1) * DV] = intra + cross * jnp.concatenate([qdec] * (DV // LANES), axis=1)


def _ffn_ret_kernel(sub_steps, gpow_ref, x_ref, mod_ref, npre_ref, w1_ref, w2_ref, npost_ref,
                    qkv_ref, s_ref, rope_ref, dec_ref, qdec_ref, kdec_ref,
                    y_ref, o_ref, snew_ref, h2_ref, acc_ref):
    n_chunks = D_FF // FF_CHUNK
    bounds = [n_chunks * s // sub_steps for s in range(sub_steps + 1)]
    for s in range(sub_steps):
        @pl.when(pl.program_id(1) == s)
        def _(s=s):
            if s == 0:
                h2_ref[...] = _ffn_pre(x_ref, mod_ref, npre_ref)
            acc = _ffn_hidden(h2_ref[...], None if s == 0 else acc_ref[...], w1_ref, w2_ref,
                              range(bounds[s], bounds[s + 1]))
            if s == sub_steps - 1:
                _ffn_post(x_ref, mod_ref, acc, npost_ref, y_ref)
            else:
                acc_ref[...] = acc

    _ret_sample_kernel(gpow_ref, qkv_ref, s_ref, rope_ref, dec_ref, qdec_ref, kdec_ref, o_ref, snew_ref)


def _ffn_ret_call(x2d, mod_p, seq_tiles, proj_s, state, gpow, rope8, dec8, qdec8, kdec8, p):
    rows = x2d.shape[0]
    tm = FFN_TM
    nb = state.shape[0]
    bb = RET_SAMPLE_BB
    sub_steps = nb // bb // (rows // tm)
    assert sub_steps * (rows // tm) * bb == nb
    srow = lambda i, j: i * sub_steps + j
    return pl.pallas_call(
        functools.partial(_ffn_ret_kernel, sub_steps),
        grid=(rows // tm, sub_steps),
        in_specs=[
            pl.BlockSpec(memory_space=pltpu.SMEM),
            pl.BlockSpec((tm, D_MODEL), lambda i, j: (i, 0)),
            pl.BlockSpec((None, 1, 3 * D_MODEL), lambda i, j: (i // seq_tiles, 0, 1)),
            _const_spec((1, D_MODEL)),
            _const_spec((D_MODEL // 2, 2 * D_FF)),
            _const_spec((D_FF // 2, D_MODEL)),
            _const_spec((1, D_MODEL)),
            pl.BlockSpec((SUBLANES, OFF_G), lambda i, j: (srow(i, j), 0)),
            pl.BlockSpec((bb, H_RET, DK, DV), lambda i, j: (srow(i, j), 0, 0, 0)),
            _const_spec((SUBLANES, 4 * (DK // 2))),
            _const_spec((H_RET, SUBLANES, SUBLANES)),
            _const_spec((H_RET, SUBLANES, LANES)),
            _const_spec((H_RET, SUBLANES, LANES)),
        ],
        out_specs=[
            pl.BlockSpec((tm, D_MODEL), lambda i, j: (i, 0)),
            pl.BlockSpec((SUBLANES, D_V), lambda i, j: (srow(i, j), 0)),
            pl.BlockSpec((bb, H_RET, DK, DV), lambda i, j: (srow(i, j), 0, 0, 0)),
        ],
        out_shape=[
            jax.ShapeDtypeStruct((rows, D_MODEL), F32),
            jax.ShapeDtypeStruct((proj_s.shape[0], D_V), F32),
            jax.ShapeDtypeStruct(state.shape, F32),
        ],
        scratch_shapes=[pltpu.VMEM((tm, D_MODEL), BF16), pltpu.VMEM((tm, D_MODEL), F32)],
        compiler_params=pltpu.CompilerParams(
            dimension_semantics=("arbitrary", "arbitrary"),
            vmem_limit_bytes=56 * 1024 * 1024),
    )(gpow, x2d, mod_p, p['npre_ffn'], p['wffn_in'], p['wffn_out'], p['npost_ffn'],
      proj_s, state, rope8, dec8, qdec8, kdec8)


def _mix_sample_kernel(x_ref, mod_ref, proj_ref, o_ref, cs_ref, h0_ref,
                       bmg_ref, gnw_ref, wbr_ret_ref, convw_ref, convb_ref, wrg_ref, bra_ref, brx_ref,
                       lru_ref, wbr_rnn_ref, wout_ref, npost_ref,
                       x1_ref, hlast_ref, convnew_ref,
                       xr_ref, sa_ref, sb_ref, zsc_ref, slab_ref):
    tm = SAMPLE_TM
    tlen = 4
    nseq = tm // tlen
    nslab = D_RNN // LANES

    def lanes(sl):
        return slice(sl * LANES, (sl + 1) * LANES)

    def seq_rows(t):
        return pl.ds(t, nseq, stride=tlen)
    x = x_ref[...]
    g1 = mod_ref[:, 2 * D_MODEL:]

    br_ret = jnp.zeros((tm, D_MODEL), F32)
    for hh in range(H_RET):
        on = _group_norm(o_ref[:, hh * DV:(hh + 1) * DV])
        g = proj_ref[:, OFF_G + hh * DV:OFF_G + (hh + 1) * DV]
        ry = (on * gnw_ref[:, hh * DV:(hh + 1) * DV] * _silu(g)).astype(BF16)
        br_ret = br_ret + _dot(ry, _wb(wbr_ret_ref, hh * DV, (hh + 1) * DV))

    for sl in range(nslab):
        for j in range(CONV_W - 1):
            slab_ref[sl, seq_rows(j), :] = cs_ref[j, :, lanes(sl)]
        slab_ref[sl, seq_rows(CONV_W - 1), :] = h0_ref[:, lanes(sl)]
    for sl in range(nslab):
        zsc_ref[0:tm, lanes(sl)] = slab_ref[sl]
    zsc_ref[tm:tm + SUBLANES, :] = jnp.zeros((SUBLANES, D_RNN), F32)
    tpos = lax.broadcasted_iota(jnp.int32, (tm, D_RNN), 0) & (tlen - 1)
    xr = proj_ref[:, OFF_XR:OFF_XR + D_RNN]
    xr_ref[0:SUBLANES, :] = jnp.zeros((SUBLANES, D_RNN), F32)
    xr_ref[SUBLANES:SUBLANES + tm, :] = xr
    cw = convw_ref[...]
    xconv = convb_ref[...]
    for j in range(CONV_W - 1):
        sft = CONV_W - 1 - j
        shifted = jnp.where(tpos >= sft, xr_ref[SUBLANES - sft:SUBLANES - sft + tm, :], 0.0)
        carried = jnp.where(tpos < sft, zsc_ref[CONV_W - 1 - sft:CONV_W - 1 - sft + tm, :], 0.0)
        xconv = xconv + (shifted + carried) * cw[j:j + 1, :]
    xconv = xconv + xr * cw[CONV_W - 1:CONV_W, :]

    a, b = _lru_coeffs(xconv, _lru_gate_pre(xconv.astype(BF16), wrg_ref), bra_ref[...], brx_ref[...], lru_ref[...])
    b = jnp.where(tpos == 0, b + a * zsc_ref[CONV_W - 1:CONV_W - 1 + tm, :], b)
    sa_ref[0:SUBLANES, :] = jnp.zeros((SUBLANES, D_RNN), F32)
    sb_ref[0:SUBLANES, :] = jnp.zeros((SUBLANES, D_RNN), F32)
    for s in (1, 2):
        sa_ref[SUBLANES:SUBLANES + tm, :] = a
        sb_ref[SUBLANES:SUBLANES + tm, :] = b
        keep = tpos >= s
        ap = jnp.where(keep, sa_ref[SUBLANES - s:SUBLANES - s + tm, :], 1.0)
        bp = jnp.where(keep, sb_ref[SUBLANES - s:SUBLANES - s + tm, :], 0.0)
        b = a * bp + b
        a = a * ap
    for sl in range(nslab):
        slab_ref[sl] = b[:, lanes(sl)]
    for sl in range(nslab):
        hlast_ref[:, lanes(sl)] = slab_ref[sl, seq_rows(tlen - 1), :]
    for sl in range(nslab):
        slab_ref[sl] = xr[:, lanes(sl)]
    for sl in range(nslab):
        for j in range(CONV_W - 1):
            convnew_ref[j, :, lanes(sl)] = slab_ref[sl, seq_rows(tlen - (CONV_W - 1) + j), :]

    gr = proj_ref[:, OFF_GR:OFF_GR + D_RNN]
    rnn_y_b = (b * jax.nn.gelu(gr, approximate=True)).astype(BF16)
    gate_pre = proj_ref[:, OFF_MG:OFF_MG + 2 * D_MODEL]
    x1_ref[...] = _mix_tail(x, g1, br_ret, rnn_y_b, gate_pre, bmg_ref[...], wbr_rnn_ref, wout_ref,
                            npost_ref[...])


def _mix_sample_call(x2d, mod_s, proj_s, o_s, cs_t, h0, p):
    rows = x2d.shape[0]
    tm = SAMPLE_TM
    row_spec = lambda w: pl.BlockSpec((tm, w), lambda i: (i, 0))
    return pl.pallas_call(
        _mix_sample_kernel,
        grid=(rows // tm,),
        in_specs=[
            row_spec(D_MODEL), row_spec(3 * D_MODEL), row_spec(N_CAT), row_spec(D_V),
            pl.BlockSpec((CONV_W - 1, tm // 4, D_RNN), lambda i: (0, i, 0)),
            pl.BlockSpec((tm // 4, D_RNN), lambda i: (i, 0)),
            _const_spec((1, 2 * D_MODEL)),
            _const_spec((1, D_V)),
            _const_spec((D_V // 2, D_MODEL)),
            _const_spec((CONV_W, D_RNN)),
            _const_spec((1, D_RNN)),
            _const_spec((D_RNN // 2, 2 * RNN_BLOCK)),
            _const_spec((1, D_RNN)),
            _const_spec((1, D_RNN)),
            _const_spec((1, D_RNN)),
            _const_spec((D_RNN // 2, D_MODEL)),
            _const_spec((D_MODEL // 2, D_MODEL)),
            _const_spec((1, D_MODEL)),
        ],
        out_specs=[
            row_spec(D_MODEL),
            pl.BlockSpec((tm // 4, D_RNN), lambda i: (i, 0)),
            pl.BlockSpec((CONV_W - 1, tm // 4, D_RNN), lambda i: (0, i, 0)),
        ],
        out_shape=[
            jax.ShapeDtypeStruct((rows, D_MODEL), F32),
            jax.ShapeDtypeStruct((rows // 4, D_RNN), F32),
            jax.ShapeDtypeStruct((CONV_W - 1, rows // 4, D_RNN), F32),
        ],
        scratch_shapes=[pltpu.VMEM((tm + SUBLANES, D_RNN), F32)] * 4
        + [pltpu.VMEM((D_RNN // LANES, tm, LANES), F32)],
        compiler_params=pltpu.CompilerParams(
            dimension_semantics=("arbitrary",),
            vmem_limit_bytes=56 * 1024 * 1024),
    )(x2d, mod_s, proj_s, o_s, cs_t, h0, p['bmg'], p['gnw'], p['wbr_ret'], p['convw'],
      p['convb'], p['wrg'], p['bra'], p['brx'], p['lru'], p['wbr_rnn'], p['wout'], p['npost'])


def _rope_table(pos):
    half = DK // 2
    inv = ROPE_BASE ** (-jnp.arange(half, dtype=F32) / half)
    ang = pos.astype(F32)[:, None] * inv[None, :]
    cos, sin = jnp.cos(ang), jnp.sin(ang)
    ks = DK ** -0.5
    return jnp.concatenate([cos, sin, cos * ks, sin * ks], axis=1)


def _decay_tables(tpos, same_seq, chunk):
    lg = jnp.log(1.0 - 2.0 ** (-5.0 - jnp.arange(H_RET, dtype=F32)))
    idx = tpos.astype(F32)
    diff = idx[:, None] - idx[None, :]
    causal = (diff >= 0) & same_seq
    dec = jnp.where(causal[None], jnp.exp(jnp.where(causal, diff, 0.0)[None] * lg[:, None, None]), 0.0)
    qdec = jnp.exp((idx + 1.0)[None, :] * lg[:, None])
    kdec = jnp.exp((chunk - 1.0 - idx)[None, :] * lg[:, None])
    rep = lambda a: jnp.broadcast_to(a[:, :, None], a.shape + (LANES,))
    gpow = jnp.exp(chunk * lg)
    return dec, rep(qdec), rep(kdec), gpow


def kernel(x_prompt, x_sample, state_ret, state_rnn_h, state_rnn_conv, c_prompt, c_sample,
           w_ada, b_ada, norm_pre_mix, norm_post_mix, norm_pre_ffn, norm_post_ffn,
           w_in, ret_gn_w, w_br_ret, conv_w, conv_b, w_rg_a, b_rg_a, w_rg_x, b_rg_x,
           lru_param, w_br_rnn, w_mgate, b_mgate, w_out, w_ffn_in, w_ffn_out):
    depth = w_in.shape[0]
    assert depth == 1, "single layer step"
    nb, seq, _ = x_prompt.shape
    nsb, sseq, _ = x_sample.shape
    assert seq % PROMPT_TM == 0 and sseq * RET_SAMPLE_BB == SUBLANES and sseq == CONV_W
    l = 0
    row = lambda a: a[l][None, :]
    p = dict(
        npre=row(norm_pre_mix), npost=row(norm_post_mix), npre_ffn=row(norm_pre_ffn), npost_ffn=row(norm_post_ffn),
        win=_pack_rows(w_in[l], kb=D_MODEL // 8), wmg=_pack_rows(w_mgate[l]),
        bmg=row(b_mgate), gnw=row(ret_gn_w), wbr_ret=_pack_rows(w_br_ret[l]),
        convw=conv_w[l], convb=row(conv_b),
        wrg=_pack_rows(jnp.concatenate([w_rg_a[l], w_rg_x[l]], axis=2).reshape(D_RNN, 2 * RNN_BLOCK)),
        bra=row(b_rg_a), brx=row(b_rg_x), lru=row(lru_param),
        wbr_rnn=_pack_rows(w_br_rnn[l]), wout=_pack_rows(w_out[l]),
    )

    rows_s = nsb * sseq
    c_all = jnp.concatenate([jnp.repeat(c_sample, sseq, axis=0), c_prompt], axis=0)
    mod_all = _mod_call(c_all, w_ada[l], row(b_ada))
    mod_p = mod_all[rows_s:].reshape(nb, 1, 6 * D_MODEL)
    mod_s = mod_all

    tm = PROMPT_TM
    ng = tm // SUBLANES
    r = jnp.arange(tm)
    tpos = (r % SUBLANES) * ng + r // SUBLANES
    pos_p = (jnp.arange(seq // tm)[:, None] * tm + tpos[None, :]).reshape(seq).astype(jnp.int32)
    rope_p = _rope_table(pos_p)
    dec, qdec, kdec, gpow = _decay_tables(tpos, jnp.ones((tm, tm), bool), float(tm))
    x1p, ret_p, hlast_p, conv_p, p['wffn_in'], p['wffn_out'] = _mix_prompt_call(
        x_prompt, mod_p, gpow, rope_p, dec, qdec, kdec, p, w_ffn_in[l], w_ffn_out[l])

    xs2d = x_sample.reshape(rows_s, D_MODEL)
    proj_s = _proj_sample_call(xs2d, mod_s, p)
    r8 = jnp.arange(SUBLANES)
    rope_s = _rope_table(PAST_LEN + (r8 % sseq).astype(jnp.int32))
    same = (r8[:, None] // sseq) == (r8[None, :] // sseq)
    dec8, qdec8, kdec8, gpow_s = _decay_tables(r8 % sseq, same, float(sseq))
    yp2d, o_s, ret_s = _ffn_ret_call(x1p.reshape(nb * seq, D_MODEL), mod_p, seq // FFN_TM, proj_s, state_ret[l],
                                     gpow_s, rope_s, dec8, qdec8, kdec8, p)
    yp = yp2d.reshape(nb, seq, D_MODEL)
    cs_t = jnp.swapaxes(state_rnn_conv[l], 0, 1)
    x1s, hlast_s, conv_t = _mix_sample_call(xs2d, mod_s, proj_s, o_s, cs_t, state_rnn_h[l], p)
    ys = _ffn_call(x1s, mod_s, FFN_TM, 1, p).reshape(nsb, sseq, D_MODEL)
    conv_s = jnp.swapaxes(conv_t, 0, 1)

    return (yp, ys, ret_p[None], ret_s[None], hlast_p.reshape(nb, D_RNN)[None], hlast_s[None],
            conv_p[None], conv_s[None])
```

```python
import functools

import jax
import jax.numpy as jnp
from jax import lax
from jax.experimental import pallas as pl
from jax.experimental.pallas import tpu as pltpu

F32 = jnp.float32
BF16 = jnp.bfloat16

D_MODEL = 1024
H_RET = 4
DK = D_MODEL // H_RET
DV = 2 * DK
D_QK = H_RET * DK
D_V = H_RET * DV
D_RNN = 1536
RNN_BLOCK = 128
N_RNN_BLOCKS = D_RNN // RNN_BLOCK
CONV_W = 4
LRU_C = 8.0
D_FF = 2816
ROPE_BASE = 10000.0
GN_EPS = 1e-5
RMS_EPS = 1e-6
PAST_LEN = 16384

OFF_Q = 0
OFF_K = OFF_Q + D_QK
OFF_V = OFF_K + D_QK
OFF_G = OFF_V + D_V
OFF_XR = OFF_G + D_V
OFF_GR = OFF_XR + D_RNN
OFF_MG = OFF_GR + D_RNN
N_CAT = OFF_MG + 2 * D_MODEL

SUBLANES = 8
LANES = 128
MXU_DIM = 256
VMEM_BYTES_V7X = 64 * 1024 * 1024

PROMPT_TM = 256
FFN_TM = 512
FF_CHUNK = MXU_DIM
SAMPLE_TM = 128
RET_SAMPLE_BB = 2


def _dot(a, b):
    return jnp.dot(a, b, preferred_element_type=F32)


def _dot_nt(a, b):
    return lax.dot_general(a, b, (((1,), (1,)), ((), ())), preferred_element_type=F32)


def _dot_tn(a, b):
    return lax.dot_general(a, b, (((0,), (0,)), ((), ())), preferred_element_type=F32)


def _wb(ref, k0=None, k1=None, c0=None, c1=None):
    rs = slice(None) if k0 is None else slice(k0 // 2, k1 // 2)
    cs = slice(None) if c0 is None else slice(c0, c1)
    return pltpu.bitcast(ref[rs, cs], BF16)


def _to_words(w):
    return pltpu.bitcast(w.astype(BF16), jnp.uint32)


def _pack_kernel(w_ref, o_ref):
    o_ref[...] = _to_words(w_ref[...])


def _pack_rows(w, kb=None):
    k, n = w.shape
    kb = k if kb is None else kb
    return pl.pallas_call(
        _pack_kernel,
        grid=(k // kb,),
        in_specs=[pl.BlockSpec((kb, n), lambda i: (i, 0))],
        out_specs=pl.BlockSpec((kb // 2, n), lambda i: (i, 0)),
        out_shape=jax.ShapeDtypeStruct((k // 2, n), jnp.uint32),
        compiler_params=pltpu.CompilerParams(
            dimension_semantics=("arbitrary",),
            vmem_limit_bytes=40 * 1024 * 1024),
    )(w)


def _rms(x, w):
    ms = jnp.mean(x * x, axis=-1, keepdims=True)
    return x * lax.rsqrt(ms + RMS_EPS) * w


def _sigmoid(x):
    return 0.5 * jnp.tanh(0.5 * x) + 0.5


def _silu(x):
    return x * _sigmoid(x)


def _rope(x, cos, sin):
    half = DK // 2
    x1, x2 = x[:, :half], x[:, half:]
    return jnp.concatenate([x1 * cos - x2 * sin, x1 * sin + x2 * cos], axis=1)


def _group_norm(o):
    mu = jnp.mean(o, axis=-1, keepdims=True)
    d = o - mu
    var = jnp.mean(d * d, axis=-1, keepdims=True)
    return d * lax.rsqrt(var + GN_EPS)


def _lru_gate_pre(xcb, wrg_ref):
    return [_dot(xcb[:, n * RNN_BLOCK:(n + 1) * RNN_BLOCK], _wb(wrg_ref, n * RNN_BLOCK, (n + 1) * RNN_BLOCK))
            for n in range(N_RNN_BLOCKS)]


def _lru_coeffs(xconv, pre, b_a, b_x, lru):
    ra = jnp.concatenate([p[:, :RNN_BLOCK] for p in pre], axis=1) + b_a
    ri = jnp.concatenate([p[:, RNN_BLOCK:] for p in pre], axis=1) + b_x
    r = _sigmoid(ra)
    i = _sigmoid(ri)
    z = -lru
    sp = jnp.maximum(z, 0.0) + jnp.log(1.0 + jnp.exp(-jnp.abs(z)))
    log_a = -LRU_C * r * sp
    a = jnp.exp(log_a)
    beta = jnp.sqrt(-jnp.tanh(log_a) * (a * a + 1.0))
    return a, beta * (i * xconv)


def _mix_tail(x, g1, br_ret, rnn_y_b, gate_pre, b_mg, wbr_rnn_ref, wout_ref, npost):
    br_rnn = _dot(rnn_y_b, _wb(wbr_rnn_ref))
    gates = _sigmoid(gate_pre + b_mg)
    ga, gb = gates[:, :D_MODEL], gates[:, D_MODEL:]
    mixed = _dot((ga * br_ret + gb * br_rnn).astype(BF16), _wb(wout_ref))
    return x + g1 * _rms(mixed, npost)


def _mod_kernel(c_ref, w_ref, b_ref, o_ref):
    a = _silu(c_ref[...]).astype(BF16)
    o_ref[...] = _dot(a, w_ref[...].astype(BF16)) + b_ref[...]


def _mod_call(c_all, w_ada_f32, b_ada):
    rows = c_all.shape[0]
    tn = 2 * D_MODEL
    return pl.pallas_call(
        _mod_kernel,
        grid=(6 * D_MODEL // tn,),
        in_specs=[
            pl.BlockSpec((rows, D_MODEL), lambda j: (0, 0)),
            pl.BlockSpec((D_MODEL, tn), lambda j: (0, j)),
            pl.BlockSpec((1, tn), lambda j: (0, j)),
        ],
        out_specs=pl.BlockSpec((rows, tn), lambda j: (0, j)),
        out_shape=jax.ShapeDtypeStruct((rows, 6 * D_MODEL), F32),
        compiler_params=pltpu.CompilerParams(
            dimension_semantics=("arbitrary",),
            vmem_limit_bytes=48 * 1024 * 1024),
    )(c_all, w_ada_f32, b_ada)


def _mix_prompt_kernel(wf_rep, gpow_ref, x_ref, mod_ref, npre_ref, wcat_ref, wmg_ref, bmg_ref, rope_ref,
                       dec_ref, qdec_ref, kdec_ref, gnw_ref, wbr_ret_ref, convw_ref, convb_ref,
                       wrg_ref, bra_ref, brx_ref, lru_ref, wbr_rnn_ref, wout_ref, npost_ref,
                       wf1_ref, wf2_ref,
                       x1_ref, s_ref, hlast_ref, convnew_ref, wf1b_ref, wf2b_ref,
                       xr_ref, prevg_ref, hc_ref, perm_ref):
    tm = PROMPT_TM
    ng = tm // SUBLANES
    halo = (CONV_W - 1) * SUBLANES
    t = pl.program_id(1)

    @pl.when(t == 0)
    def _():
        s_ref[...] = jnp.zeros_like(s_ref)
        prevg_ref[...] = jnp.zeros_like(prevg_ref)
        hc_ref[...] = jnp.zeros_like(hc_ref)

    nslab = D_MODEL // LANES

    def lanes(sl):
        return slice(sl * LANES, (sl + 1) * LANES)

    def seg_rows(s):
        return pl.ds(s, ng, stride=SUBLANES)

    for sl in range(nslab):
        for s in range(SUBLANES):
            perm_ref[sl, seg_rows(s), :] = x_ref[s * ng:(s + 1) * ng, lanes(sl)]
    xp = jnp.concatenate([perm_ref[sl] for sl in range(nslab)], axis=1)
    sh1, sc1 = mod_ref[:, :D_MODEL], mod_ref[:, D_MODEL:2 * D_MODEL]
    hb = (_rms(xp, npre_ref[...]) * (1.0 + sc1) + sh1).astype(BF16)
    sub = lax.broadcasted_iota(jnp.int32, (SUBLANES, D_RNN), 0)
    half = DK // 2
    cos, sin = rope_ref[:, 0:half], rope_ref[:, half:2 * half]
    cosk, sink = rope_ref[:, 2 * half:3 * half], rope_ref[:, 3 * half:4 * half]
    st = {}

    def xr_proj():
        xr = _dot(hb, _wb(wcat_ref, c0=OFF_XR, c1=OFF_XR + D_RNN))
        xr_ref[halo:halo + tm, :] = xr
        for kk in range(1, CONV_W):
            r0 = (CONV_W - 1 - kk) * SUBLANES
            cur = xr[(ng - kk) * SUBLANES:(ng - kk + 1) * SUBLANES, :]
            prv = prevg_ref[r0:r0 + SUBLANES, :]
            xr_ref[r0:r0 + SUBLANES, :] = pltpu.roll(jnp.where(sub == SUBLANES - 1, prv, cur), 1, 0)
        prevg_ref[...] = xr[tm - halo:, :]
        for kk in range(1, CONV_W):
            r1 = (ng - kk) * SUBLANES + SUBLANES - 1
            convnew_ref[CONV_W - 1 - kk:CONV_W - kk, :] = xr[r1:r1 + 1, :]

    def lru_conv():
        cw = convw_ref[...]
        xconv = convb_ref[...]
        for j in range(CONV_W):
            r0 = halo - (CONV_W - 1 - j) * SUBLANES
            xconv = xconv + xr_ref[r0:r0 + tm, :] * cw[j:j + 1, :]
        st['xconv'] = xconv

    def lru_gate_proj():
        st['gpre'] = _lru_gate_pre(st['xconv'].astype(BF16), wrg_ref)

    def lru_coef():
        st['a'], st['b'] = _lru_coeffs(st.pop('xconv'), st.pop('gpre'), bra_ref[...], brx_ref[...], lru_ref[...])

    def lru_scan():
        a, b = st['a'], st['b']
        ca, cb = a[0:SUBLANES, :], b[0:SUBLANES, :]
        cas, cbs = [ca], [cb]
        for gi in range(1, ng):
            ag = a[gi * SUBLANES:(gi + 1) * SUBLANES, :]
            cb = ag * cb + b[gi * SUBLANES:(gi + 1) * SUBLANES, :]
            ca = ag * ca
            cas.append(ca)
            cbs.append(cb)
        cin = jnp.where(sub == 0, hc_ref[SUBLANES - 1:SUBLANES, :], 0.0)
        for s in range(SUBLANES - 1):
            cin = jnp.where(sub == s + 1, pltpu.roll(ca * cin + cb, 1, 0), cin)
        seg_end = ca * cin + cb
        hc_ref[...] = seg_end
        hlast_ref[...] = seg_end[SUBLANES - 1:SUBLANES, :]
        st['hseq'] = jnp.concatenate([cas[gi] * cin + cbs[gi] for gi in range(ng)], axis=0)

    def gr_proj():
        st['gr'] = _dot(hb, _wb(wcat_ref, c0=OFF_GR, c1=OFF_GR + D_RNN))

    def lru_y():
        st['rnn_y'] = (st.pop('hseq') * jax.nn.gelu(st.pop('gr'), approximate=True)).astype(BF16)

    def lru_out():
        st['br_rnn'] = _dot(st.pop('rnn_y'), _wb(wbr_rnn_ref))

    def gate_proj():
        st['gate_pre'] = _dot(hb, _wb(wmg_ref))

    def gate_act():
        st['gates'] = _sigmoid(st.pop('gate_pre') + bmg_ref[...])

    def head_proj(hh):
        q = _dot(hb, _wb(wcat_ref, c0=OFF_Q + hh * DK, c1=OFF_Q + (hh + 1) * DK))
        k = _dot(hb, _wb(wcat_ref, c0=OFF_K + hh * DK, c1=OFF_K + (hh + 1) * DK))
        vb = _dot(hb, _wb(wcat_ref, c0=OFF_V + hh * DV, c1=OFF_V + (hh + 1) * DV)).astype(BF16)
        g = _dot(hb, _wb(wcat_ref, c0=OFF_G + hh * DV, c1=OFF_G + (hh + 1) * DV))
        st['proj', hh] = (q, k, vb, g)

    def head_rope(hh):
        q, k, vb, g = st.pop(('proj', hh))
        kr = _rope(k, cosk, sink)
        kdec = kdec_ref[hh]
        kdb = (kr * jnp.concatenate([kdec, kdec], axis=1)).astype(BF16)
        st['rope', hh] = (_rope(q, cos, sin).astype(BF16), kr.astype(BF16), kdb, vb, g)

    def head_qk(hh):
        qb, kb, kdb, vb, g = st.pop(('rope', hh))
        scores = _dot_nt(qb, kb)
        cross = _dot(qb, s_ref[hh].astype(BF16))
        st['qk', hh] = (scores, cross, vb, g)
        st['kv', hh] = (kdb, vb)

    def head_state(hh):
        kdb, vb = st.pop(('kv', hh))
        s_ref[hh] = gpow_ref[hh] * s_ref[hh] + _dot_tn(kdb, vb)

    def head_decay(hh):
        scores, cross, vb, g = st.pop(('qk', hh))
        st['dec', hh] = ((scores * dec_ref[hh]).astype(BF16), cross, vb, g)

    def head_pv(hh):
        sb, cross, vb, g = st.pop(('dec', hh))
        st['pv', hh] = (_dot(sb, vb), cross, g)

    def head_norm(hh):
        intra, cross, g = st.pop(('pv', hh))
        qdec = qdec_ref[hh]
        on = _group_norm(intra + cross * jnp.concatenate([qdec] * (DV // LANES), axis=1))
        st['ry', hh] = (on * gnw_ref[:, hh * DV:(hh + 1) * DV] * _silu(g)).astype(BF16)

    def head_out(hh):
        part = _dot(st.pop(('ry', hh)), _wb(wbr_ret_ref, hh * DV, (hh + 1) * DV))
        st['br_ret'] = part if hh == 0 else st['br_ret'] + part

    heads = range(H_RET)
    order = (
        [xr_proj, (head_proj, 0), lru_conv, lru_gate_proj, (head_proj, 1), (head_proj, 2), lru_coef, (head_proj, 3)]
        + [(head_rope, h) for h in heads] + [gr_proj] + [(head_qk, h) for h in heads]
        + [gate_proj] + [(head_decay, h) for h in heads] + [lru_scan]
        + [(head_pv, h) for h in heads] + [(head_state, h) for h in heads] + [lru_y, lru_out]
        + [(head_norm, h) for h in heads] + [(head_out, h) for h in heads] + [gate_act]
    )
    for stage in order:
        if isinstance(stage, tuple):
            stage[0](stage[1])
        else:
            stage()

    gates = st['gates']
    ga, gb = gates[:, :D_MODEL], gates[:, D_MODEL:]
    mixed = _dot((ga * st['br_ret'] + gb * st['br_rnn']).astype(BF16), _wb(wout_ref))
    delta = mod_ref[:, 2 * D_MODEL:] * _rms(mixed, npost_ref[...])
    for sl in range(nslab):
        perm_ref[sl] = delta[:, lanes(sl)]
    for sl in range(nslab):
        for s in range(SUBLANES):
            rows = slice(s * ng, (s + 1) * ng)
            x1_ref[rows, lanes(sl)] = x_ref[rows, lanes(sl)] + perm_ref[sl, seg_rows(s), :]

    step = pl.program_id(0) * pl.num_programs(1) + t

    @pl.when(step % wf_rep == 0)
    def _():
        wf1b_ref[...] = _to_words(wf1_ref[...])
        wf2b_ref[...] = _to_words(wf2_ref[...])


def _const_spec(shape):
    nd = len(shape)
    return pl.BlockSpec(shape, lambda *_: (0,) * nd, pipeline_mode=pl.Buffered(1))


def _mix_prompt_call(x, mod3, gpow, rope_tab, dec, qdec, kdec, p, wf1, wf2):
    nb, seq, _ = x.shape
    tm = PROMPT_TM
    nt = seq // tm
    steps = nb * nt
    bf16_rows = 2 * SUBLANES
    wf_rep = 1
    while any(w.shape[0] % (steps // wf_rep) or (w.shape[0] // (steps // wf_rep)) % bf16_rows for w in (wf1, wf2)):
        wf_rep *= 2
    r1, r2 = wf1.shape[0] // (steps // wf_rep), wf2.shape[0] // (steps // wf_rep)
    wf1_spec = pl.BlockSpec((r1, wf1.shape[1]), lambda b, t: ((b * nt + t) // wf_rep, 0))
    wf2_spec = pl.BlockSpec((r2, wf2.shape[1]), lambda b, t: ((b * nt + t) // wf_rep, 0))
    in_specs = [
        pl.BlockSpec(memory_space=pltpu.SMEM),
        pl.BlockSpec((None, tm, D_MODEL), lambda b, t: (b, t, 0)),
        pl.BlockSpec((None, 1, 3 * D_MODEL), lambda b, t: (b, 0, 0)),
        _const_spec((1, D_MODEL)),
        _const_spec((D_MODEL // 2, OFF_MG)),
        _const_spec((D_MODEL // 2, 2 * D_MODEL)),
        _const_spec((1, 2 * D_MODEL)),
        pl.BlockSpec((tm, 4 * (DK // 2)), lambda b, t: (t, 0)),
        _const_spec((H_RET, tm, tm)),
        _const_spec((H_RET, tm, LANES)),
        _const_spec((H_RET, tm, LANES)),
        _const_spec((1, D_V)),
        _const_spec((D_V // 2, D_MODEL)),
        _const_spec((CONV_W, D_RNN)),
        _const_spec((1, D_RNN)),
        _const_spec((D_RNN // 2, 2 * RNN_BLOCK)),
        _const_spec((1, D_RNN)),
        _const_spec((1, D_RNN)),
        _const_spec((1, D_RNN)),
        _const_spec((D_RNN // 2, D_MODEL)),
        _const_spec((D_MODEL // 2, D_MODEL)),
        _const_spec((1, D_MODEL)),
        wf1_spec,
        wf2_spec,
    ]
    out_specs = [
        pl.BlockSpec((None, tm, D_MODEL), lambda b, t: (b, t, 0)),
        pl.BlockSpec((None, H_RET, DK, DV), lambda b, t: (b, 0, 0, 0)),
        pl.BlockSpec((None, 1, D_RNN), lambda b, t: (b, 0, 0)),
        pl.BlockSpec((None, CONV_W - 1, D_RNN), lambda b, t: (b, 0, 0)),
        pl.BlockSpec((r1 // 2, wf1.shape[1]), wf1_spec.index_map),
        pl.BlockSpec((r2 // 2, wf2.shape[1]), wf2_spec.index_map),
    ]
    out_shape = [
        jax.ShapeDtypeStruct((nb, seq, D_MODEL), F32),
        jax.ShapeDtypeStruct((nb, H_RET, DK, DV), F32),
        jax.ShapeDtypeStruct((nb, 1, D_RNN), F32),
        jax.ShapeDtypeStruct((nb, CONV_W - 1, D_RNN), F32),
        jax.ShapeDtypeStruct((wf1.shape[0] // 2, wf1.shape[1]), jnp.uint32),
        jax.ShapeDtypeStruct((wf2.shape[0] // 2, wf2.shape[1]), jnp.uint32),
    ]
    halo = (CONV_W - 1) * SUBLANES
    scratch = [
        pltpu.VMEM((halo + tm, D_RNN), F32),
        pltpu.VMEM((halo, D_RNN), F32),
        pltpu.VMEM((SUBLANES, D_RNN), F32),
        pltpu.VMEM((D_MODEL // LANES, tm, LANES), F32),
    ]
    return pl.pallas_call(
        functools.partial(_mix_prompt_kernel, wf_rep),
        grid=(nb, nt),
        in_specs=in_specs,
        out_specs=out_specs,
        out_shape=out_shape,
        scratch_shapes=scratch,
        compiler_params=pltpu.CompilerParams(
            dimension_semantics=("arbitrary", "arbitrary"),
            vmem_limit_bytes=VMEM_BYTES_V7X - 4 * 1024 * 1024),
    )(gpow, x, mod3, p['npre'], p['win'], p['wmg'], p['bmg'], rope_tab, dec, qdec, kdec, p['gnw'], p['wbr_ret'],
      p['convw'], p['convb'], p['wrg'], p['bra'], p['brx'], p['lru'], p['wbr_rnn'], p['wout'], p['npost'],
      wf1, wf2)


def _ffn_pre(x_ref, mod_ref, npre_ref):
    m = mod_ref[...]
    sh2, sc2 = m[:, :D_MODEL], m[:, D_MODEL:2 * D_MODEL]
    return (_rms(x_ref[...], npre_ref[...]) * (1.0 + sc2) + sh2).astype(BF16)


def _ffn_hidden(h2, acc, w1_ref, w2_ref, chunk_ids):
    for j in chunk_ids:
        c0 = j * FF_CHUNK
        fg = _dot(h2, _wb(w1_ref, c0=c0, c1=c0 + FF_CHUNK))
        fu = _dot(h2, _wb(w1_ref, c0=D_FF + c0, c1=D_FF + c0 + FF_CHUNK))
        part = _dot((_silu(fg) * fu).astype(BF16), _wb(w2_ref, c0, c0 + FF_CHUNK))
        acc = part if acc is None else acc + part
    return acc


def _ffn_post(x_ref, mod_ref, acc, npost_ref, o_ref):
    o_ref[...] = x_ref[...] + mod_ref[:, 2 * D_MODEL:] * _rms(acc, npost_ref[...])


def _ffn_kernel(x_ref, mod_ref, npre_ref, w1_ref, w2_ref, npost_ref, o_ref):
    h2 = _ffn_pre(x_ref, mod_ref, npre_ref)
    acc = _ffn_hidden(h2, None, w1_ref, w2_ref, range(D_FF // FF_CHUNK))
    _ffn_post(x_ref, mod_ref, acc, npost_ref, o_ref)


def _ffn_call(x2d, mod, mod_rows_per_tile, seq_tiles, p):
    rows = x2d.shape[0]
    tm = FFN_TM
    if mod_rows_per_tile == 1:
        mod_spec = pl.BlockSpec((None, 1, 3 * D_MODEL), lambda i: (i // seq_tiles, 0, 1))
    else:
        mod_spec = pl.BlockSpec((tm, 3 * D_MODEL), lambda i: (i, 1))
    return pl.pallas_call(
        _ffn_kernel,
        grid=(rows // tm,),
        in_specs=[
            pl.BlockSpec((tm, D_MODEL), lambda i: (i, 0)),
            mod_spec,
            _const_spec((1, D_MODEL)),
            _const_spec((D_MODEL // 2, 2 * D_FF)),
            _const_spec((D_FF // 2, D_MODEL)),
            _const_spec((1, D_MODEL)),
        ],
        out_specs=pl.BlockSpec((tm, D_MODEL), lambda i: (i, 0)),
        out_shape=jax.ShapeDtypeStruct((rows, D_MODEL), F32),
        compiler_params=pltpu.CompilerParams(
            dimension_semantics=("arbitrary",),
            vmem_limit_bytes=48 * 1024 * 1024),
    )(x2d, mod, p['npre_ffn'], p['wffn_in'], p['wffn_out'], p['npost_ffn'])


def _proj_sample_kernel(n_in_tiles, x_ref, mod_ref, npre_ref, win_ref, wmg_ref, o_ref):
    m = mod_ref[...]
    sh1, sc1 = m[:, :D_MODEL], m[:, D_MODEL:2 * D_MODEL]
    h = (_rms(x_ref[...], npre_ref[...]) * (1.0 + sc1) + sh1).astype(BF16)
    j = pl.program_id(0)

    @pl.when(j < n_in_tiles)
    def _():
        o_ref[...] = _dot(h, _wb(win_ref))

    @pl.when(j >= n_in_tiles)
    def _():
        o_ref[...] = _dot(h, _wb(wmg_ref))


def _proj_sample_call(x2d, mod_s, p):
    rows = x2d.shape[0]
    tn = D_MODEL
    n_in = OFF_MG // tn
    return pl.pallas_call(
        functools.partial(_proj_sample_kernel, n_in),
        grid=(N_CAT // tn,),
        in_specs=[
            pl.BlockSpec((rows, D_MODEL), lambda j: (0, 0)),
            pl.BlockSpec((rows, 3 * D_MODEL), lambda j: (0, 0)),
            pl.BlockSpec((1, D_MODEL), lambda j: (0, 0)),
            pl.BlockSpec((D_MODEL // 2, tn), lambda j: (0, jnp.minimum(j, n_in - 1))),
            pl.BlockSpec((D_MODEL // 2, tn), lambda j: (0, jnp.maximum(j - n_in, 0))),
        ],
        out_specs=pl.BlockSpec((rows, tn), lambda j: (0, j)),
        out_shape=jax.ShapeDtypeStruct((rows, N_CAT), F32),
        compiler_params=pltpu.CompilerParams(
            dimension_semantics=("arbitrary",),
            vmem_limit_bytes=48 * 1024 * 1024),
    )(x2d, mod_s, p['npre'], p['win'], p['wmg'])


def _ret_sample_kernel(gpow_ref, qkv_ref, s_ref, rope_ref, dec_ref, qdec_ref, kdec_ref, o_ref, snew_ref):
    half = DK // 2
    cos, sin = rope_ref[:, 0:half], rope_ref[:, half:2 * half]
    cosk, sink = rope_ref[:, 2 * half:3 * half], rope_ref[:, 3 * half:4 * half]
    nseq = RET_SAMPLE_BB
    tlen = SUBLANES // nseq
    row = lax.broadcasted_iota(jnp.int32, (SUBLANES, LANES), 0)
    for hh in range(H_RET):
        q = _rope(qkv_ref[:, OFF_Q + hh * DK:OFF_Q + (hh + 1) * DK], cos, sin)
        k = _rope(qkv_ref[:, OFF_K + hh * DK:OFF_K + (hh + 1) * DK], cosk, sink)
        v = qkv_ref[:, OFF_V + hh * DV:OFF_V + (hh + 1) * DV]
        kdec = kdec_ref[hh]
        kd = k * jnp.concatenate([kdec, kdec], axis=1)
        scores = _dot_nt(q, k) * dec_ref[hh]
        intra = _dot(scores, v)
        qb = q.astype(BF16)
        cross = jnp.zeros((SUBLANES, DV), F32)
        for bi in range(nseq):
            s_old = s_ref[bi, hh]
            in_seq = (row >= bi * tlen) & (row < (bi + 1) * tlen)
            cr = _dot(qb, s_old.astype(BF16))
            cross = jnp.where(jnp.concatenate([in_seq] * (DV // LANES), axis=1), cr, cross)
            kd_b = jnp.where(jnp.concatenate([in_seq] * (DK // LANES), axis=1), kd, 0.0)
            snew_ref[bi, hh] = gpow_ref[hh] * s_old + _dot_tn(kd_b, v)
        qdec = qdec_ref[hh]
        o_ref[:, hh * DV:(hh + 1) * DV] = intra + cross * jnp.concatenate([qdec] * (DV // LANES), axis=1)


def _ffn_ret_kernel(sub_steps, gpow_ref, x_ref, mod_ref, npre_ref, w1_ref, w2_ref, npost_ref,
                    qkv_ref, s_ref, rope_ref, dec_ref, qdec_ref, kdec_ref,
                    y_ref, o_ref, snew_ref, h2_ref, acc_ref):
    n_chunks = D_FF // FF_CHUNK
    bounds = [n_chunks * s // sub_steps for s in range(sub_steps + 1)]
    for s in range(sub_steps):
        @pl.when(pl.program_id(1) == s)
        def _(s=s):
            if s == 0:
                h2_ref[...] = _ffn_pre(x_ref, mod_ref, npre_ref)
            acc = _ffn_hidden(h2_ref[...], None if s == 0 else acc_ref[...], w1_ref, w2_ref,
                              range(bounds[s], bounds[s + 1]))
            if s == sub_steps - 1:
                _ffn_post(x_ref, mod_ref, acc, npost_ref, y_ref)
            else:
                acc_ref[...] = acc

    _ret_sample_kernel(gpow_ref, qkv_ref, s_ref, rope_ref, dec_ref, qdec_ref, kdec_ref, o_ref, snew_ref)


def _ffn_ret_call(x2d, mod_p, seq_tiles, proj_s, state, gpow, rope8, dec8, qdec8, kdec8, p):
    rows = x2d.shape[0]
    tm = FFN_TM
    nb = state.shape[0]
    bb = RET_SAMPLE_BB
    sub_steps = nb // bb // (rows // tm)
    assert sub_steps * (rows // tm) * bb == nb
    srow = lambda i, j: i * sub_steps + j
    return pl.pallas_call(
        functools.partial(_ffn_ret_kernel, sub_steps),
        grid=(rows // tm, sub_steps),
        in_specs=[
            pl.BlockSpec(memory_space=pltpu.SMEM),
            pl.BlockSpec((tm, D_MODEL), lambda i, j: (i, 0)),
            pl.BlockSpec((None, 1, 3 * D_MODEL), lambda i, j: (i // seq_tiles, 0, 1)),
            _const_spec((1, D_MODEL)),
            _const_spec((D_MODEL // 2, 2 * D_FF)),
            _const_spec((D_FF // 2, D_MODEL)),
            _const_spec((1, D_MODEL)),
            pl.BlockSpec((SUBLANES, OFF_G), lambda i, j: (srow(i, j), 0)),
            pl.BlockSpec((bb, H_RET, DK, DV), lambda i, j: (srow(i, j), 0, 0, 0)),
            _const_spec((SUBLANES, 4 * (DK // 2))),
            _const_spec((H_RET, SUBLANES, SUBLANES)),
            _const_spec((H_RET, SUBLANES, LANES)),
            _const_spec((H_RET, SUBLANES, LANES)),
        ],
        out_specs=[
            pl.BlockSpec((tm, D_MODEL), lambda i, j: (i, 0)),
            pl.BlockSpec((SUBLANES, D_V), lambda i, j: (srow(i, j), 0)),
            pl.BlockSpec((bb, H_RET, DK, DV), lambda i, j: (srow(i, j), 0, 0, 0)),
        ],
        out_shape=[
            jax.ShapeDtypeStruct((rows, D_MODEL), F32),
            jax.ShapeDtypeStruct((proj_s.shape[0], D_V), F32),
            jax.ShapeDtypeStruct(state.shape, F32),
        ],
        scratch_shapes=[pltpu.VMEM((tm, D_MODEL), BF16), pltpu.VMEM((tm, D_MODEL), F32)],
        compiler_params=pltpu.CompilerParams(
            dimension_semantics=("arbitrary", "arbitrary"),
            vmem_limit_bytes=56 * 1024 * 1024),
    )(gpow, x2d, mod_p, p['npre_ffn'], p['wffn_in'], p['wffn_out'], p['npost_ffn'],
      proj_s, state, rope8, dec8, qdec8, kdec8)


def _mix_sample_kernel(x_ref, mod_ref, proj_ref, o_ref, cs_ref, h0_ref,
                       bmg_ref, gnw_ref, wbr_ret_ref, convw_ref, convb_ref, wrg_ref, bra_ref, brx_ref,
                       lru_ref, wbr_rnn_ref, wout_ref, npost_ref,
                       x1_ref, hlast_ref, convnew_ref,
                       xr_ref, sa_ref, sb_ref, zsc_ref, slab_ref):
    tm = SAMPLE_TM
    tlen = 4
    nseq = tm // tlen
    nslab = D_RNN // LANES

    def lanes(sl):
        return slice(sl * LANES, (sl + 1) * LANES)

    def seq_rows(t):
        return pl.ds(t, nseq, stride=tlen)
    x = x_ref[...]
    g1 = mod_ref[:, 2 * D_MODEL:]

    br_ret = jnp.zeros((tm, D_MODEL), F32)
    for hh in range(H_RET):
        on = _group_norm(o_ref[:, hh * DV:(hh + 1) * DV])
        g = proj_ref[:, OFF_G + hh * DV:OFF_G + (hh + 1) * DV]
        ry = (on * gnw_ref[:, hh * DV:(hh + 1) * DV] * _silu(g)).astype(BF16)
        br_ret = br_ret + _dot(ry, _wb(wbr_ret_ref, hh * DV, (hh + 1) * DV))

    for sl in range(nslab):
        for j in range(CONV_W - 1):
            slab_ref[sl, seq_rows(j), :] = cs_ref[j, :, lanes(sl)]
        slab_ref[sl, seq_rows(CONV_W - 1), :] = h0_ref[:, lanes(sl)]
    for sl in range(nslab):
        zsc_ref[0:tm, lanes(sl)] = slab_ref[sl]
    zsc_ref[tm:tm + SUBLANES, :] = jnp.zeros((SUBLANES, D_RNN), F32)
    tpos = lax.broadcasted_iota(jnp.int32, (tm, D_RNN), 0) & (tlen - 1)
    xr = proj_ref[:, OFF_XR:OFF_XR + D_RNN]
    xr_ref[0:SUBLANES, :] = jnp.zeros((SUBLANES, D_RNN), F32)
    xr_ref[SUBLANES:SUBLANES + tm, :] = xr
    cw = convw_ref[...]
    xconv = convb_ref[...]
    for j in range(CONV_W - 1):
        sft = CONV_W - 1 - j
        shifted = jnp.where(tpos >= sft, xr_ref[SUBLANES - sft:SUBLANES - sft + tm, :], 0.0)
        carried = jnp.where(tpos < sft, zsc_ref[CONV_W - 1 - sft:CONV_W - 1 - sft + tm, :], 0.0)
        xconv = xconv + (shifted + carried) * cw[j:j + 1, :]
    xconv = xconv + xr * cw[CONV_W - 1:CONV_W, :]

    a, b = _lru_coeffs(xconv, _lru_gate_pre(xconv.astype(BF16), wrg_ref), bra_ref[...], brx_ref[...], lru_ref[...])
    b = jnp.where(tpos == 0, b + a * zsc_ref[CONV_W - 1:CONV_W - 1 + tm, :], b)
    sa_ref[0:SUBLANES, :] = jnp.zeros((SUBLANES, D_RNN), F32)
    sb_ref[0:SUBLANES, :] = jnp.zeros((SUBLANES, D_RNN), F32)
    for s in (1, 2):
        sa_ref[SUBLANES:SUBLANES + tm, :] = a
        sb_ref[SUBLANES:SUBLANES + tm, :] = b
        keep = tpos >= s
        ap = jnp.where(keep, sa_ref[SUBLANES - s:SUBLANES - s + tm, :], 1.0)
        bp = jnp.where(keep, sb_ref[SUBLANES - s:SUBLANES - s + tm, :], 0.0)
        b = a * bp + b
        a = a * ap
    for sl in range(nslab):
        slab_ref[sl] = b[:, lanes(sl)]
    for sl in range(nslab):
        hlast_ref[:, lanes(sl)] = slab_ref[sl, seq_rows(tlen - 1), :]
    for sl in range(nslab):
        slab_ref[sl] = xr[:, lanes(sl)]
    for sl in range(nslab):
        for j in range(CONV_W - 1):
            convnew_ref[j, :, lanes(sl)] = slab_ref[sl, seq_rows(tlen - (CONV_W - 1) + j), :]

    gr = proj_ref[:, OFF_GR:OFF_GR + D_RNN]
    rnn_y_b = (b * jax.nn.gelu(gr, approximate=True)).astype(BF16)
    gate_pre = proj_ref[:, OFF_MG:OFF_MG + 2 * D_MODEL]
    x1_ref[...] = _mix_tail(x, g1, br_ret, rnn_y_b, gate_pre, bmg_ref[...], wbr_rnn_ref, wout_ref,
                            npost_ref[...])


def _mix_sample_call(x2d, mod_s, proj_s, o_s, cs_t, h0, p):
    rows = x2d.shape[0]
    tm = SAMPLE_TM
    row_spec = lambda w: pl.BlockSpec((tm, w), lambda i: (i, 0))
    return pl.pallas_call(
        _mix_sample_kernel,
        grid=(rows // tm,),
        in_specs=[
            row_spec(D_MODEL), row_spec(3 * D_MODEL), row_spec(N_CAT), row_spec(D_V),
            pl.BlockSpec((CONV_W - 1, tm // 4, D_RNN), lambda i: (0, i, 0)),
            pl.BlockSpec((tm // 4, D_RNN), lambda i: (i, 0)),
            _const_spec((1, 2 * D_MODEL)),
            _const_spec((1, D_V)),
            _const_spec((D_V // 2, D_MODEL)),
            _const_spec((CONV_W, D_RNN)),
            _const_spec((1, D_RNN)),
            _const_spec((D_RNN // 2, 2 * RNN_BLOCK)),
            _const_spec((1, D_RNN)),
            _const_spec((1, D_RNN)),
            _const_spec((1, D_RNN)),
            _const_spec((D_RNN // 2, D_MODEL)),
            _const_spec((D_MODEL // 2, D_MODEL)),
            _const_spec((1, D_MODEL)),
        ],
        out_specs=[
            row_spec(D_MODEL),
            pl.BlockSpec((tm // 4, D_RNN), lambda i: (i, 0)),
            pl.BlockSpec((CONV_W - 1, tm // 4, D_RNN), lambda i: (0, i, 0)),
        ],
        out_shape=[
            jax.ShapeDtypeStruct((rows, D_MODEL), F32),
            jax.ShapeDtypeStruct((rows // 4, D_RNN), F32),
            jax.ShapeDtypeStruct((CONV_W - 1, rows // 4, D_RNN), F32),
        ],
        scratch_shapes=[pltpu.VMEM((tm + SUBLANES, D_RNN), F32)] * 4
        + [pltpu.VMEM((D_RNN // LANES, tm, LANES), F32)],
        compiler_params=pltpu.CompilerParams(
            dimension_semantics=("arbitrary",),
            vmem_limit_bytes=56 * 1024 * 1024),
    )(x2d, mod_s, proj_s, o_s, cs_t, h0, p['bmg'], p['gnw'], p['wbr_ret'], p['convw'],
      p['convb'], p['wrg'], p['bra'], p['brx'], p['lru'], p['wbr_rnn'], p['wout'], p['npost'])


def _rope_table(pos):
    half = DK // 2
    inv = ROPE_BASE ** (-jnp.arange(half, dtype=F32) / half)
    ang = pos.astype(F32)[:, None] * inv[None, :]
    cos, sin = jnp.cos(ang), jnp.sin(ang)
    ks = DK ** -0.5
    return jnp.concatenate([cos, sin, cos * ks, sin * ks], axis=1)


def _decay_tables(tpos, same_seq, chunk):
    lg = jnp.log(1.0 - 2.0 ** (-5.0 - jnp.arange(H_RET, dtype=F32)))
    idx = tpos.astype(F32)
    diff = idx[:, None] - idx[None, :]
    causal = (diff >= 0) & same_seq
    dec = jnp.where(causal[None], jnp.exp(jnp.where(causal, diff, 0.0)[None] * lg[:, None, None]), 0.0)
    qdec = jnp.exp((idx + 1.0)[None, :] * lg[:, None])
    kdec = jnp.exp((chunk - 1.0 - idx)[None, :] * lg[:, None])
    rep = lambda a: jnp.broadcast_to(a[:, :, None], a.shape + (LANES,))
    gpow = jnp.exp(chunk * lg)
    return dec, rep(qdec), rep(kdec), gpow


def kernel(x_prompt, x_sample, state_ret, state_rnn_h, state_rnn_conv, c_prompt, c_sample,
           w_ada, b_ada, norm_pre_mix, norm_post_mix, norm_pre_ffn, norm_post_ffn,
           w_in, ret_gn_w, w_br_ret, conv_w, conv_b, w_rg_a, b_rg_a, w_rg_x, b_rg_x,
           lru_param, w_br_rnn, w_mgate, b_mgate, w_out, w_ffn_in, w_ffn_out):
    depth = w_in.shape[0]
    assert depth == 1, "single layer step"
    nb, seq, _ = x_prompt.shape
    nsb, sseq, _ = x_sample.shape
    assert seq % PROMPT_TM == 0 and sseq * RET_SAMPLE_BB == SUBLANES and sseq == CONV_W
    l = 0
    row = lambda a: a[l][None, :]
    p = dict(
        npre=row(norm_pre_mix), npost=row(norm_post_mix), npre_ffn=row(norm_pre_ffn), npost_ffn=row(norm_post_ffn),
        win=_pack_rows(w_in[l], kb=D_MODEL // 8), wmg=_pack_rows(w_mgate[l]),
        bmg=row(b_mgate), gnw=row(ret_gn_w), wbr_ret=_pack_rows(w_br_ret[l]),
        convw=conv_w[l], convb=row(conv_b),
        wrg=_pack_rows(jnp.concatenate([w_rg_a[l], w_rg_x[l]], axis=2).reshape(D_RNN, 2 * RNN_BLOCK)),
        bra=row(b_rg_a), brx=row(b_rg_x), lru=row(lru_param),
        wbr_rnn=_pack_rows(w_br_rnn[l]), wout=_pack_rows(w_out[l]),
    )

    rows_s = nsb * sseq
    c_all = jnp.concatenate([jnp.repeat(c_sample, sseq, axis=0), c_prompt], axis=0)
    mod_all = _mod_call(c_all, w_ada[l], row(b_ada))
    mod_p = mod_all[rows_s:].reshape(nb, 1, 6 * D_MODEL)
    mod_s = mod_all

    tm = PROMPT_TM
    ng = tm // SUBLANES
    r = jnp.arange(tm)
    tpos = (r % SUBLANES) * ng + r // SUBLANES
    pos_p = (jnp.arange(seq // tm)[:, None] * tm + tpos[None, :]).reshape(seq).astype(jnp.int32)
    rope_p = _rope_table(pos_p)
    dec, qdec, kdec, gpow = _decay_tables(tpos, jnp.ones((tm, tm), bool), float(tm))
    x1p, ret_p, hlast_p, conv_p, p['wffn_in'], p['wffn_out'] = _mix_prompt_call(
        x_prompt, mod_p, gpow, rope_p, dec, qdec, kdec, p, w_ffn_in[l], w_ffn_out[l])

    xs2d = x_sample.reshape(rows_s, D_MODEL)
    proj_s = _proj_sample_call(xs2d, mod_s, p)
    r8 = jnp.arange(SUBLANES)
    rope_s = _rope_table(PAST_LEN + (r8 % sseq).astype(jnp.int32))
    same = (r8[:, None] // sseq) == (r8[None, :] // sseq)
    dec8, qdec8, kdec8, gpow_s = _decay_tables(r8 % sseq, same, float(sseq))
    yp2d, o_s, ret_s = _ffn_ret_call(x1p.reshape(nb * seq, D_MODEL), mod_p, seq // FFN_TM, proj_s, state_ret[l],
                                     gpow_s, rope_s, dec8, qdec8, kdec8, p)
    yp = yp2d.reshape(nb, seq, D_MODEL)
    cs_t = jnp.swapaxes(state_rnn_conv[l], 0, 1)
    x1s, hlast_s, conv_t = _mix_sample_call(xs2d, mod_s, proj_s, o_s, cs_t, state_rnn_h[l], p)
    ys = _ffn_call(x1s, mod_s, FFN_TM, 1, p).reshape(nsb, sseq, D_MODEL)
    conv_s = jnp.swapaxes(conv_t, 0, 1)

    return (yp, ys, ret_p[None], ret_s[None], hlast_p.reshape(nb, D_RNN)[None], hlast_s[None],
            conv_p[None], conv_s[None])
```

```python
import functools

import jax
import jax.numpy as jnp
from jax import lax
from jax.experimental import pallas as pl
from jax.experimental.pallas import tpu as pltpu

F32 = jnp.float32
BF16 = jnp.bfloat16

D_MODEL = 1024
H_RET = 4
DK = D_MODEL // H_RET
DV = 2 * DK
D_QK = H_RET * DK
D_V = H_RET * DV
D_RNN = 1536
RNN_BLOCK = 128
N_RNN_BLOCKS = D_RNN // RNN_BLOCK
CONV_W = 4
LRU_C = 8.0
D_FF = 2816
ROPE_BASE = 10000.0
GN_EPS = 1e-5
RMS_EPS = 1e-6
PAST_LEN = 16384

OFF_Q = 0
OFF_K = OFF_Q + D_QK
OFF_V = OFF_K + D_QK
OFF_G = OFF_V + D_V
OFF_XR = OFF_G + D_V
OFF_GR = OFF_XR + D_RNN
OFF_MG = OFF_GR + D_RNN
N_CAT = OFF_MG + 2 * D_MODEL

SUBLANES = 8
LANES = 128
MXU_DIM = 256
VMEM_BYTES_V7X = 64 * 1024 * 1024

PROMPT_TM = 256
FFN_TM = 512
FF_CHUNK = MXU_DIM
SAMPLE_TM = 256
RET_SAMPLE_BB = 2


def _dot(a, b):
    return jnp.dot(a, b, preferred_element_type=F32)


def _dot_nt(a, b):
    return lax.dot_general(a, b, (((1,), (1,)), ((), ())), preferred_element_type=F32)


def _dot_tn(a, b):
    return lax.dot_general(a, b, (((0,), (0,)), ((), ())), preferred_element_type=F32)


def _wb(ref, k0=None, k1=None, c0=None, c1=None):
    rs = slice(None) if k0 is None else slice(k0 // 2, k1 // 2)
    cs = slice(None) if c0 is None else slice(c0, c1)
    return pltpu.bitcast(ref[rs, cs], BF16)


def _to_words(w):
    return pltpu.bitcast(w.astype(BF16), jnp.uint32)


def _pack_kernel(w_ref, o_ref):
    o_ref[...] = _to_words(w_ref[...])


def _pack_rows(w, kb=None):
    k, n = w.shape
    kb = k if kb is None else kb
    return pl.pallas_call(
        _pack_kernel,
        grid=(k // kb,),
        in_specs=[pl.BlockSpec((kb, n), lambda i: (i, 0))],
        out_specs=pl.BlockSpec((kb // 2, n), lambda i: (i, 0)),
        out_shape=jax.ShapeDtypeStruct((k // 2, n), jnp.uint32),
        compiler_params=pltpu.CompilerParams(
            dimension_semantics=("arbitrary",),
            vmem_limit_bytes=40 * 1024 * 1024),
    )(w)


def _rms(x, w):
    ms = jnp.mean(x * x, axis=-1, keepdims=True)
    return x * lax.rsqrt(ms + RMS_EPS) * w


def _sigmoid(x):
    return 0.5 * jnp.tanh(0.5 * x) + 0.5


def _silu(x):
    return x * _sigmoid(x)


def _rope(x, cos, sin):
    half = DK // 2
    x1, x2 = x[:, :half], x[:, half:]
    return jnp.concatenate([x1 * cos - x2 * sin, x1 * sin + x2 * cos], axis=1)


def _group_norm(o):
    mu = jnp.mean(o, axis=-1, keepdims=True)
    d = o - mu
    var = jnp.mean(d * d, axis=-1, keepdims=True)
    return d * lax.rsqrt(var + GN_EPS)


def _lru_gate_pre(xcb, wrg_ref):
    return [_dot(xcb[:, n * RNN_BLOCK:(n + 1) * RNN_BLOCK], _wb(wrg_ref, n * RNN_BLOCK, (n + 1) * RNN_BLOCK))
            for n in range(N_RNN_BLOCKS)]


def _lru_coeffs(xconv, pre, b_a, b_x, lru):
    ra = jnp.concatenate([p[:, :RNN_BLOCK] for p in pre], axis=1) + b_a
    ri = jnp.concatenate([p[:, RNN_BLOCK:] for p in pre], axis=1) + b_x
    r = _sigmoid(ra)
    i = _sigmoid(ri)
    z = -lru
    sp = jnp.maximum(z, 0.0) + jnp.log(1.0 + jnp.exp(-jnp.abs(z)))
    log_a = -LRU_C * r * sp
    a = jnp.exp(log_a)
    beta = jnp.sqrt(-jnp.tanh(log_a) * (a * a + 1.0))
    return a, beta * (i * xconv)


def _mix_tail(x, g1, br_ret, rnn_y_b, gate_pre, b_mg, wbr_rnn_ref, wout_ref, npost):
    br_rnn = _dot(rnn_y_b, _wb(wbr_rnn_ref))
    gates = _sigmoid(gate_pre + b_mg)
    ga, gb = gates[:, :D_MODEL], gates[:, D_MODEL:]
    mixed = _dot((ga * br_ret + gb * br_rnn).astype(BF16), _wb(wout_ref))
    return x + g1 * _rms(mixed, npost)


def _mod_kernel(c_ref, w_ref, b_ref, o_ref):
    a = _silu(c_ref[...]).astype(BF16)
    o_ref[...] = _dot(a, w_ref[...].astype(BF16)) + b_ref[...]


def _mod_call(c_all, w_ada_f32, b_ada):
    rows = c_all.shape[0]
    tn = 2 * D_MODEL
    return pl.pallas_call(
        _mod_kernel,
        grid=(6 * D_MODEL // tn,),
        in_specs=[
            pl.BlockSpec((rows, D_MODEL), lambda j: (0, 0)),
            pl.BlockSpec((D_MODEL, tn), lambda j: (0, j)),
            pl.BlockSpec((1, tn), lambda j: (0, j)),
        ],
        out_specs=pl.BlockSpec((rows, tn), lambda j: (0, j)),
        out_shape=jax.ShapeDtypeStruct((rows, 6 * D_MODEL), F32),
        compiler_params=pltpu.CompilerParams(
            dimension_semantics=("arbitrary",),
            vmem_limit_bytes=48 * 1024 * 1024),
    )(c_all, w_ada_f32, b_ada)


def _mix_prompt_kernel(wf_rep, gpow_ref, x_ref, mod_ref, npre_ref, wcat_ref, wmg_ref, bmg_ref, rope_ref,
                       dec_ref, qdec_ref, kdec_ref, gnw_ref, wbr_ret_ref, convw_ref, convb_ref,
                       wrg_ref, bra_ref, brx_ref, lru_ref, wbr_rnn_ref, wout_ref, npost_ref,
                       wf1_ref, wf2_ref,
                       x1_ref, s_ref, hlast_ref, convnew_ref, wf1b_ref, wf2b_ref,
                       xr_ref, prevg_ref, hc_ref, perm_ref):
    tm = PROMPT_TM
    ng = tm // SUBLANES
    halo = (CONV_W - 1) * SUBLANES
    t = pl.program_id(1)

    @pl.when(t == 0)
    def _():
        s_ref[...] = jnp.zeros_like(s_ref)
        prevg_ref[...] = jnp.zeros_like(prevg_ref)
        hc_ref[...] = jnp.zeros_like(hc_ref)

    nslab = D_MODEL // LANES

    def lanes(sl):
        return slice(sl * LANES, (sl + 1) * LANES)

    def seg_rows(s):
        return pl.ds(s, ng, stride=SUBLANES)

    for sl in range(nslab):
        for s in range(SUBLANES):
            perm_ref[sl, seg_rows(s), :] = x_ref[s * ng:(s + 1) * ng, lanes(sl)]
    xp = jnp.concatenate([perm_ref[sl] for sl in range(nslab)], axis=1)
    sh1, sc1 = mod_ref[:, :D_MODEL], mod_ref[:, D_MODEL:2 * D_MODEL]
    hb = (_rms(xp, npre_ref[...]) * (1.0 + sc1) + sh1).astype(BF16)
    sub = lax.broadcasted_iota(jnp.int32, (SUBLANES, D_RNN), 0)
    half = DK // 2
    cos, sin = rope_ref[:, 0:half], rope_ref[:, half:2 * half]
    cosk, sink = rope_ref[:, 2 * half:3 * half], rope_ref[:, 3 * half:4 * half]
    st = {}

    def xr_proj():
        xr = _dot(hb, _wb(wcat_ref, c0=OFF_XR, c1=OFF_XR + D_RNN))
        xr_ref[halo:halo + tm, :] = xr
        for kk in range(1, CONV_W):
            r0 = (CONV_W - 1 - kk) * SUBLANES
            cur = xr[(ng - kk) * SUBLANES:(ng - kk + 1) * SUBLANES, :]
            prv = prevg_ref[r0:r0 + SUBLANES, :]
            xr_ref[r0:r0 + SUBLANES, :] = pltpu.roll(jnp.where(sub == SUBLANES - 1, prv, cur), 1, 0)
        prevg_ref[...] = xr[tm - halo:, :]
        for kk in range(1, CONV_W):
            r1 = (ng - kk) * SUBLANES + SUBLANES - 1
            convnew_ref[CONV_W - 1 - kk:CONV_W - kk, :] = xr[r1:r1 + 1, :]

    def lru_conv():
        cw = convw_ref[...]
        xconv = convb_ref[...]
        for j in range(CONV_W):
            r0 = halo - (CONV_W - 1 - j) * SUBLANES
            xconv = xconv + xr_ref[r0:r0 + tm, :] * cw[j:j + 1, :]
        st['xconv'] = xconv

    def lru_gate_proj():
        st['gpre'] = _lru_gate_pre(st['xconv'].astype(BF16), wrg_ref)

    def lru_coef():
        st['a'], st['b'] = _lru_coeffs(st.pop('xconv'), st.pop('gpre'), bra_ref[...], brx_ref[...], lru_ref[...])

    def lru_scan():
        a, b = st['a'], st['b']
        ca, cb = a[0:SUBLANES, :], b[0:SUBLANES, :]
        cas, cbs = [ca], [cb]
        for gi in range(1, ng):
            ag = a[gi * SUBLANES:(gi + 1) * SUBLANES, :]
            cb = ag * cb + b[gi * SUBLANES:(gi + 1) * SUBLANES, :]
            ca = ag * ca
            cas.append(ca)
            cbs.append(cb)
        cin = jnp.where(sub == 0, hc_ref[SUBLANES - 1:SUBLANES, :], 0.0)
        for s in range(SUBLANES - 1):
            cin = jnp.where(sub == s + 1, pltpu.roll(ca * cin + cb, 1, 0), cin)
        seg_end = ca * cin + cb
        hc_ref[...] = seg_end
        hlast_ref[...] = seg_end[SUBLANES - 1:SUBLANES, :]
        st['hseq'] = jnp.concatenate([cas[gi] * cin + cbs[gi] for gi in range(ng)], axis=0)

    def gr_proj():
        st['gr'] = _dot(hb, _wb(wcat_ref, c0=OFF_GR, c1=OFF_GR + D_RNN))

    def lru_y():
        st['rnn_y'] = (st.pop('hseq') * jax.nn.gelu(st.pop('gr'), approximate=True)).astype(BF16)

    def lru_out():
        st['br_rnn'] = _dot(st.pop('rnn_y'), _wb(wbr_rnn_ref))

    def gate_proj():
        st['gate_pre'] = _dot(hb, _wb(wmg_ref))

    def gate_act():
        st['gates'] = _sigmoid(st.pop('gate_pre') + bmg_ref[...])

    def head_proj(hh):
        q = _dot(hb, _wb(wcat_ref, c0=OFF_Q + hh * DK, c1=OFF_Q + (hh + 1) * DK))
        k = _dot(hb, _wb(wcat_ref, c0=OFF_K + hh * DK, c1=OFF_K + (hh + 1) * DK))
        vb = _dot(hb, _wb(wcat_ref, c0=OFF_V + hh * DV, c1=OFF_V + (hh + 1) * DV)).astype(BF16)
        g = _dot(hb, _wb(wcat_ref, c0=OFF_G + hh * DV, c1=OFF_G + (hh + 1) * DV))
        st['proj', hh] = (q, k, vb, g)

    def head_rope(hh):
        q, k, vb, g = st.pop(('proj', hh))
        kr = _rope(k, cosk, sink)
        kdec = kdec_ref[hh]
        kdb = (kr * jnp.concatenate([kdec, kdec], axis=1)).astype(BF16)
        st['rope', hh] = (_rope(q, cos, sin).astype(BF16), kr.astype(BF16), kdb, vb, g)

    def head_qk(hh):
        qb, kb, kdb, vb, g = st.pop(('rope', hh))
        scores = _dot_nt(qb, kb)
        cross = _dot(qb, s_ref[hh].astype(BF16))
        st['qk', hh] = (scores, cross, vb, g)
        st['kv', hh] = (kdb, vb)

    def head_state(hh):
        kdb, vb = st.pop(('kv', hh))
        s_ref[hh] = gpow_ref[hh] * s_ref[hh] + _dot_tn(kdb, vb)

    def head_decay(hh):
        scores, cross, vb, g = st.pop(('qk', hh))
        st['dec', hh] = ((scores * dec_ref[hh]).astype(BF16), cross, vb, g)

    def head_pv(hh):
        sb, cross, vb, g = st.pop(('dec', hh))
        st['pv', hh] = (_dot(sb, vb), cross, g)

    def head_norm(hh):
        intra, cross, g = st.pop(('pv', hh))
        qdec = qdec_ref[hh]
        on = _group_norm(intra + cross * jnp.concatenate([qdec] * (DV // LANES), axis=1))
        st['ry', hh] = (on * gnw_ref[:, hh * DV:(hh + 1) * DV] * _silu(g)).astype(BF16)

    def head_out(hh):
        part = _dot(st.pop(('ry', hh)), _wb(wbr_ret_ref, hh * DV, (hh + 1) * DV))
        st['br_ret'] = part if hh == 0 else st['br_ret'] + part

    heads = range(H_RET)
    order = (
        [xr_proj, (head_proj, 0), lru_conv, lru_gate_proj, (head_proj, 1), (head_proj, 2), lru_coef, (head_proj, 3)]
        + [(head_rope, h) for h in heads] + [gr_proj] + [(head_qk, h) for h in heads]
        + [gate_proj] + [(head_decay, h) for h in heads] + [lru_scan]
        + [(head_pv, h) for h in heads] + [(head_state, h) for h in heads] + [lru_y, lru_out]
        + [(head_norm, h) for h in heads] + [(head_out, h) for h in heads] + [gate_act]
    )
    for stage in order:
        if isinstance(stage, tuple):
            stage[0](stage[1])
        else:
            stage()

    gates = st['gates']
    ga, gb = gates[:, :D_MODEL], gates[:, D_MODEL:]
    mixed = _dot((ga * st['br_ret'] + gb * st['br_rnn']).astype(BF16), _wb(wout_ref))
    delta = mod_ref[:, 2 * D_MODEL:] * _rms(mixed, npost_ref[...])
    for sl in range(nslab):
        perm_ref[sl] = delta[:, lanes(sl)]
    for sl in range(nslab):
        for s in range(SUBLANES):
            rows = slice(s * ng, (s + 1) * ng)
            x1_ref[rows, lanes(sl)] = x_ref[rows, lanes(sl)] + perm_ref[sl, seg_rows(s), :]

    step = pl.program_id(0) * pl.num_programs(1) + t

    @pl.when(step % wf_rep == 0)
    def _():
        wf1b_ref[...] = _to_words(wf1_ref[...])
        wf2b_ref[...] = _to_words(wf2_ref[...])


def _const_spec(shape):
    nd = len(shape)
    return pl.BlockSpec(shape, lambda *_: (0,) * nd, pipeline_mode=pl.Buffered(1))


def _mix_prompt_call(x, mod3, gpow, rope_tab, dec, qdec, kdec, p, wf1, wf2):
    nb, seq, _ = x.shape
    tm = PROMPT_TM
    nt = seq // tm
    steps = nb * nt
    bf16_rows = 2 * SUBLANES
    wf_rep = 1
    while any(w.shape[0] % (steps // wf_rep) or (w.shape[0] // (steps // wf_rep)) % bf16_rows for w in (wf1, wf2)):
        wf_rep *= 2
    r1, r2 = wf1.shape[0] // (steps // wf_rep), wf2.shape[0] // (steps // wf_rep)
    wf1_spec = pl.BlockSpec((r1, wf1.shape[1]), lambda b, t: ((b * nt + t) // wf_rep, 0))
    wf2_spec = pl.BlockSpec((r2, wf2.shape[1]), lambda b, t: ((b * nt + t) // wf_rep, 0))
    in_specs = [
        pl.BlockSpec(memory_space=pltpu.SMEM),
        pl.BlockSpec((None, tm, D_MODEL), lambda b, t: (b, t, 0)),
        pl.BlockSpec((None, 1, 3 * D_MODEL), lambda b, t: (b, 0, 0)),
        _const_spec((1, D_MODEL)),
        _const_spec((D_MODEL // 2, OFF_MG)),
        _const_spec((D_MODEL // 2, 2 * D_MODEL)),
        _const_spec((1, 2 * D_MODEL)),
        pl.BlockSpec((tm, 4 * (DK // 2)), lambda b, t: (t, 0)),
        _const_spec((H_RET, tm, tm)),
        _const_spec((H_RET, tm, LANES)),
        _const_spec((H_RET, tm, LANES)),
        _const_spec((1, D_V)),
        _const_spec((D_V // 2, D_MODEL)),
        _const_spec((CONV_W, D_RNN)),
        _const_spec((1, D_RNN)),
        _const_spec((D_RNN // 2, 2 * RNN_BLOCK)),
        _const_spec((1, D_RNN)),
        _const_spec((1, D_RNN)),
        _const_spec((1, D_RNN)),
        _const_spec((D_RNN // 2, D_MODEL)),
        _const_spec((D_MODEL // 2, D_MODEL)),
        _const_spec((1, D_MODEL)),
        wf1_spec,
        wf2_spec,
    ]
    out_specs = [
        pl.BlockSpec((None, tm, D_MODEL), lambda b, t: (b, t, 0)),
        pl.BlockSpec((None, H_RET, DK, DV), lambda b, t: (b, 0, 0, 0)),
        pl.BlockSpec((None, 1, D_RNN), lambda b, t: (b, 0, 0)),
        pl.BlockSpec((None, CONV_W - 1, D_RNN), lambda b, t: (b, 0, 0)),
        pl.BlockSpec((r1 // 2, wf1.shape[1]), wf1_spec.index_map),
        pl.BlockSpec((r2 // 2, wf2.shape[1]), wf2_spec.index_map),
    ]
    out_shape = [
        jax.ShapeDtypeStruct((nb, seq, D_MODEL), F32),
        jax.ShapeDtypeStruct((nb, H_RET, DK, DV), F32),
        jax.ShapeDtypeStruct((nb, 1, D_RNN), F32),
        jax.ShapeDtypeStruct((nb, CONV_W - 1, D_RNN), F32),
        jax.ShapeDtypeStruct((wf1.shape[0] // 2, wf1.shape[1]), jnp.uint32),
        jax.ShapeDtypeStruct((wf2.shape[0] // 2, wf2.shape[1]), jnp.uint32),
    ]
    halo = (CONV_W - 1) * SUBLANES
    scratch = [
        pltpu.VMEM((halo + tm, D_RNN), F32),
        pltpu.VMEM((halo, D_RNN), F32),
        pltpu.VMEM((SUBLANES, D_RNN), F32),
        pltpu.VMEM((D_MODEL // LANES, tm, LANES), F32),
    ]
    return pl.pallas_call(
        functools.partial(_mix_prompt_kernel, wf_rep),
        grid=(nb, nt),
        in_specs=in_specs,
        out_specs=out_specs,
        out_shape=out_shape,
        scratch_shapes=scratch,
        compiler_params=pltpu.CompilerParams(
            dimension_semantics=("arbitrary", "arbitrary"),
            vmem_limit_bytes=VMEM_BYTES_V7X - 4 * 1024 * 1024),
    )(gpow, x, mod3, p['npre'], p['win'], p['wmg'], p['bmg'], rope_tab, dec, qdec, kdec, p['gnw'], p['wbr_ret'],
      p['convw'], p['convb'], p['wrg'], p['bra'], p['brx'], p['lru'], p['wbr_rnn'], p['wout'], p['npost'],
      wf1, wf2)


def _ffn_pre(x_ref, mod_ref, npre_ref):
    m = mod_ref[...]
    sh2, sc2 = m[:, :D_MODEL], m[:, D_MODEL:2 * D_MODEL]
    return (_rms(x_ref[...], npre_ref[...]) * (1.0 + sc2) + sh2).astype(BF16)


def _ffn_hidden(h2, acc, w1_ref, w2_ref, chunk_ids):
    for j in chunk_ids:
        c0 = j * FF_CHUNK
        fg = _dot(h2, _wb(w1_ref, c0=c0, c1=c0 + FF_CHUNK))
        fu = _dot(h2, _wb(w1_ref, c0=D_FF + c0, c1=D_FF + c0 + FF_CHUNK))
        part = _dot((_silu(fg) * fu).astype(BF16), _wb(w2_ref, c0, c0 + FF_CHUNK))
        acc = part if acc is None else acc + part
    return acc


def _ffn_post(x_ref, mod_ref, acc, npost_ref, o_ref):
    o_ref[...] = x_ref[...] + mod_ref[:, 2 * D_MODEL:] * _rms(acc, npost_ref[...])


def _ffn_kernel(x_ref, mod_ref, npre_ref, w1_ref, w2_ref, npost_ref, o_ref):
    h2 = _ffn_pre(x_ref, mod_ref, npre_ref)
    acc = _ffn_hidden(h2, None, w1_ref, w2_ref, range(D_FF // FF_CHUNK))
    _ffn_post(x_ref, mod_ref, acc, npost_ref, o_ref)


def _ffn_call(x2d, mod, mod_rows_per_tile, seq_tiles, p):
    rows = x2d.shape[0]
    tm = FFN_TM
    if mod_rows_per_tile == 1:
        mod_spec = pl.BlockSpec((None, 1, 3 * D_MODEL), lambda i: (i // seq_tiles, 0, 1))
    else:
        mod_spec = pl.BlockSpec((tm, 3 * D_MODEL), lambda i: (i, 1))
    return pl.pallas_call(
        _ffn_kernel,
        grid=(rows // tm,),
        in_specs=[
            pl.BlockSpec((tm, D_MODEL), lambda i: (i, 0)),
            mod_spec,
            _const_spec((1, D_MODEL)),
            _const_spec((D_MODEL // 2, 2 * D_FF)),
            _const_spec((D_FF // 2, D_MODEL)),
            _const_spec((1, D_MODEL)),
        ],
        out_specs=pl.BlockSpec((tm, D_MODEL), lambda i: (i, 0)),
        out_shape=jax.ShapeDtypeStruct((rows, D_MODEL), F32),
        compiler_params=pltpu.CompilerParams(
            dimension_semantics=("arbitrary",),
            vmem_limit_bytes=48 * 1024 * 1024),
    )(x2d, mod, p['npre_ffn'], p['wffn_in'], p['wffn_out'], p['npost_ffn'])


def _proj_sample_kernel(n_in_tiles, x_ref, mod_ref, npre_ref, win_ref, wmg_ref, o_ref):
    m = mod_ref[...]
    sh1, sc1 = m[:, :D_MODEL], m[:, D_MODEL:2 * D_MODEL]
    h = (_rms(x_ref[...], npre_ref[...]) * (1.0 + sc1) + sh1).astype(BF16)
    j = pl.program_id(0)

    @pl.when(j < n_in_tiles)
    def _():
        o_ref[...] = _dot(h, _wb(win_ref))

    @pl.when(j >= n_in_tiles)
    def _():
        o_ref[...] = _dot(h, _wb(wmg_ref))


def _proj_sample_call(x2d, mod_s, p):
    rows = x2d.shape[0]
    tn = D_MODEL
    n_in = OFF_MG // tn
    return pl.pallas_call(
        functools.partial(_proj_sample_kernel, n_in),
        grid=(N_CAT // tn,),
        in_specs=[
            pl.BlockSpec((rows, D_MODEL), lambda j: (0, 0)),
            pl.BlockSpec((rows, 3 * D_MODEL), lambda j: (0, 0)),
            pl.BlockSpec((1, D_MODEL), lambda j: (0, 0)),
            pl.BlockSpec((D_MODEL // 2, tn), lambda j: (0, jnp.minimum(j, n_in - 1))),
            pl.BlockSpec((D_MODEL // 2, tn), lambda j: (0, jnp.maximum(j - n_in, 0))),
        ],
        out_specs=pl.BlockSpec((rows, tn), lambda j: (0, j)),
        out_shape=jax.ShapeDtypeStruct((rows, N_CAT), F32),
        compiler_params=pltpu.CompilerParams(
            dimension_semantics=("arbitrary",),
            vmem_limit_bytes=48 * 1024 * 1024),
    )(x2d, mod_s, p['npre'], p['win'], p['wmg'])


def _ret_sample_kernel(gpow_ref, qkv_ref, s_ref, rope_ref, dec_ref, qdec_ref, kdec_ref, o_ref, snew_ref):
    half = DK // 2
    cos, sin = rope_ref[:, 0:half], rope_ref[:, half:2 * half]
    cosk, sink = rope_ref[:, 2 * half:3 * half], rope_ref[:, 3 * half:4 * half]
    nseq = RET_SAMPLE_BB
    tlen = SUBLANES // nseq
    row = lax.broadcasted_iota(jnp.int32, (SUBLANES, LANES), 0)
    for hh in range(H_RET):
        q = _rope(qkv_ref[:, OFF_Q + hh * DK:OFF_Q + (hh + 1) * DK], cos, sin)
        k = _rope(qkv_ref[:, OFF_K + hh * DK:OFF_K + (hh + 1) * DK], cosk, sink)
        v = qkv_ref[:, OFF_V + hh * DV:OFF_V + (hh + 1) * DV]
        kdec = kdec_ref[hh]
        kd = k * jnp.concatenate([kdec, kdec], axis=1)
        scores = _dot_nt(q, k) * dec_ref[hh]
        intra = _dot(scores, v)
        qb = q.astype(BF16)
        cross = jnp.zeros((SUBLANES, DV), F32)
        for bi in range(nseq):
            s_old = s_ref[bi, hh]
            in_seq = (row >= bi * tlen) & (row < (bi + 1) * tlen)
            cr = _dot(qb, s_old.astype(BF16))
            cross = jnp.where(jnp.concatenate([in_seq] * (DV // LANES), axis=1), cr, cross)
            kd_b = jnp.where(jnp.concatenate([in_seq] * (DK // LANES), axis=1), kd, 0.0)
            snew_ref[bi, hh] = gpow_ref[hh] * s_old + _dot_tn(kd_b, v)
        qdec = qdec_ref[hh]
        o_ref[:, hh * DV:(hh + 1) * DV] = intra + cross * jnp.concatenate([qdec] * (DV // LANES), axis=1)


def _ffn_ret_kernel(sub_steps, gpow_ref, x_ref, mod_ref, npre_ref, w1_ref, w2_ref, npost_ref,
                    qkv_ref, s_ref, rope_ref, dec_ref, qdec_ref, kdec_ref,
                    y_ref, o_ref, snew_ref, h2_ref, acc_ref):
    n_chunks = D_FF // FF_CHUNK
    bounds = [n_chunks * s // sub_steps for s in range(sub_steps + 1)]
    for s in range(sub_steps):
        @pl.when(pl.program_id(1) == s)
        def _(s=s):
            if s == 0:
                h2_ref[...] = _ffn_pre(x_ref, mod_ref, npre_ref)
            acc = _ffn_hidden(h2_ref[...], None if s == 0 else acc_ref[...], w1_ref, w2_ref,
                              range(bounds[s], bounds[s + 1]))
            if s == sub_steps - 1:
                _ffn_post(x_ref, mod_ref, acc, npost_ref, y_ref)
            else:
                acc_ref[...] = acc

    _ret_sample_kernel(gpow_ref, qkv_ref, s_ref, rope_ref, dec_ref, qdec_ref, kdec_ref, o_ref, snew_ref)


def _ffn_ret_call(x2d, mod_p, seq_tiles, proj_s, state, gpow, rope8, dec8, qdec8, kdec8, p):
    rows = x2d.shape[0]
    tm = FFN_TM
    nb = state.shape[0]
    bb = RET_SAMPLE_BB
    sub_steps = nb // bb // (rows // tm)
    assert sub_steps * (rows // tm) * bb == nb
    srow = lambda i, j: i * sub_steps + j
    return pl.pallas_call(
        functools.partial(_ffn_ret_kernel, sub_steps),
        grid=(rows // tm, sub_steps),
        in_specs=[
            pl.BlockSpec(memory_space=pltpu.SMEM),
            pl.BlockSpec((tm, D_MODEL), lambda i, j: (i, 0)),
            pl.BlockSpec((None, 1, 3 * D_MODEL), lambda i, j: (i // seq_tiles, 0, 1)),
            _const_spec((1, D_MODEL)),
            _const_spec((D_MODEL // 2, 2 * D_FF)),
            _const_spec((D_FF // 2, D_MODEL)),
            _const_spec((1, D_MODEL)),
            pl.BlockSpec((SUBLANES, OFF_G), lambda i, j: (srow(i, j), 0)),
            pl.BlockSpec((bb, H_RET, DK, DV), lambda i, j: (srow(i, j), 0, 0, 0)),
            _const_spec((SUBLANES, 4 * (DK // 2))),
            _const_spec((H_RET, SUBLANES, SUBLANES)),
            _const_spec((H_RET, SUBLANES, LANES)),
            _const_spec((H_RET, SUBLANES, LANES)),
        ],
        out_specs=[
            pl.BlockSpec((tm, D_MODEL), lambda i, j: (i, 0)),
            pl.BlockSpec((SUBLANES, D_V), lambda i, j: (srow(i, j), 0)),
            pl.BlockSpec((bb, H_RET, DK, DV), lambda i, j: (srow(i, j), 0, 0, 0)),
        ],
        out_shape=[
            jax.ShapeDtypeStruct((rows, D_MODEL), F32),
            jax.ShapeDtypeStruct((proj_s.shape[0], D_V), F32),
            jax.ShapeDtypeStruct(state.shape, F32),
        ],
        scratch_shapes=[pltpu.VMEM((tm, D_MODEL), BF16), pltpu.VMEM((tm, D_MODEL), F32)],
        compiler_params=pltpu.CompilerParams(
            dimension_semantics=("arbitrary", "arbitrary"),
            vmem_limit_bytes=56 * 1024 * 1024),
    )(gpow, x2d, mod_p, p['npre_ffn'], p['wffn_in'], p['wffn_out'], p['npost_ffn'],
      proj_s, state, rope8, dec8, qdec8, kdec8)


def _mix_sample_kernel(x_ref, mod_ref, g_ref, xri_ref, gr_ref, mg0_ref, mg1_ref, o_ref, cs_ref, h0_ref,
                       bmg_ref, gnw_ref, wbr_ret_ref, convw_ref, convb_ref, wrg_ref, bra_ref, brx_ref,
                       lru_ref, wbr_rnn_ref, wout_ref, npost_ref,
                       x1_ref, hlast_ref, convnew_ref,
                       xr_ref, sa_ref, sb_ref, zsc_ref, slab_ref):
    tm = SAMPLE_TM
    tlen = 4
    nseq = tm // tlen
    nslab = D_RNN // LANES

    def lanes(sl):
        return slice(sl * LANES, (sl + 1) * LANES)

    def seq_rows(t):
        return pl.ds(t, nseq, stride=tlen)
    x = x_ref[...]
    g1 = mod_ref[:, 2 * D_MODEL:]

    br_ret = jnp.zeros((tm, D_MODEL), F32)
    for hh in range(H_RET):
        on = _group_norm(o_ref[:, hh * DV:(hh + 1) * DV])
        g = g_ref[:, hh * DV:(hh + 1) * DV]
        ry = (on * gnw_ref[:, hh * DV:(hh + 1) * DV] * _silu(g)).astype(BF16)
        br_ret = br_ret + _dot(ry, _wb(wbr_ret_ref, hh * DV, (hh + 1) * DV))

    for sl in range(nslab):
        for j in range(CONV_W - 1):
            slab_ref[sl, seq_rows(j), :] = cs_ref[j, :, lanes(sl)]
        slab_ref[sl, seq_rows(CONV_W - 1), :] = h0_ref[:, lanes(sl)]
    for sl in range(nslab):
        zsc_ref[0:tm, lanes(sl)] = slab_ref[sl]
    zsc_ref[tm:tm + SUBLANES, :] = jnp.zeros((SUBLANES, D_RNN), F32)
    tpos = lax.broadcasted_iota(jnp.int32, (tm, D_RNN), 0) & (tlen - 1)
    xr = xri_ref[...]
    xr_ref[0:SUBLANES, :] = jnp.zeros((SUBLANES, D_RNN), F32)
    xr_ref[SUBLANES:SUBLANES + tm, :] = xr
    cw = convw_ref[...]
    xconv = convb_ref[...]
    for j in range(CONV_W - 1):
        sft = CONV_W - 1 - j
        shifted = jnp.where(tpos >= sft, xr_ref[SUBLANES - sft:SUBLANES - sft + tm, :], 0.0)
        carried = jnp.where(tpos < sft, zsc_ref[CONV_W - 1 - sft:CONV_W - 1 - sft + tm, :], 0.0)
        xconv = xconv + (shifted + carried) * cw[j:j + 1, :]
    xconv = xconv + xr * cw[CONV_W - 1:CONV_W, :]

    a, b = _lru_coeffs(xconv, _lru_gate_pre(xconv.astype(BF16), wrg_ref), bra_ref[...], brx_ref[...], lru_ref[...])
    b = jnp.where(tpos == 0, b + a * zsc_ref[CONV_W - 1:CONV_W - 1 + tm, :], b)
    sa_ref[0:SUBLANES, :] = jnp.zeros((SUBLANES, D_RNN), F32)
    sb_ref[0:SUBLANES, :] = jnp.zeros((SUBLANES, D_RNN), F32)
    for s in (1, 2):
        sa_ref[SUBLANES:SUBLANES + tm, :] = a
        sb_ref[SUBLANES:SUBLANES + tm, :] = b
        keep = tpos >= s
        ap = jnp.where(keep, sa_ref[SUBLANES - s:SUBLANES - s + tm, :], 1.0)
        bp = jnp.where(keep, sb_ref[SUBLANES - s:SUBLANES - s + tm, :], 0.0)
        b = a * bp + b
        a = a * ap
    for sl in range(nslab):
        slab_ref[sl] = b[:, lanes(sl)]
    for sl in range(nslab):
        hlast_ref[:, lanes(sl)] = slab_ref[sl, seq_rows(tlen - 1), :]
    for sl in range(nslab):
        slab_ref[sl] = xr[:, lanes(sl)]
    for sl in range(nslab):
        for j in range(CONV_W - 1):
            convnew_ref[j, :, lanes(sl)] = slab_ref[sl, seq_rows(tlen - (CONV_W - 1) + j), :]

    gr = gr_ref[...]
    rnn_y_b = (b * jax.nn.gelu(gr, approximate=True)).astype(BF16)
    gate_pre = jnp.concatenate([mg0_ref[...], mg1_ref[...]], axis=1)
    x1_ref[...] = _mix_tail(x, g1, br_ret, rnn_y_b, gate_pre, bmg_ref[...], wbr_rnn_ref, wout_ref,
                            npost_ref[...])


def _mix_sample_call(x2d, mod_s, proj_s, o_s, cs_t, h0, p):
    rows = x2d.shape[0]
    tm = SAMPLE_TM
    row_spec = lambda w: pl.BlockSpec((tm, w), lambda i: (i, 0))
    col_spec = lambda w, c0: pl.BlockSpec((tm, w), lambda i: (i, c0 // w))
    return pl.pallas_call(
        _mix_sample_kernel,
        grid=(rows // tm,),
        in_specs=[
            row_spec(D_MODEL), row_spec(3 * D_MODEL),
            col_spec(D_V, OFF_G), col_spec(D_RNN, OFF_XR), col_spec(D_RNN, OFF_GR),
            col_spec(D_MODEL, OFF_MG), col_spec(D_MODEL, OFF_MG + D_MODEL),
            row_spec(D_V),
            pl.BlockSpec((CONV_W - 1, tm // 4, D_RNN), lambda i: (0, i, 0)),
            pl.BlockSpec((tm // 4, D_RNN), lambda i: (i, 0)),
            _const_spec((1, 2 * D_MODEL)),
            _const_spec((1, D_V)),
            _const_spec((D_V // 2, D_MODEL)),
            _const_spec((CONV_W, D_RNN)),
            _const_spec((1, D_RNN)),
            _const_spec((D_RNN // 2, 2 * RNN_BLOCK)),
            _const_spec((1, D_RNN)),
            _const_spec((1, D_RNN)),
            _const_spec((1, D_RNN)),
            _const_spec((D_RNN // 2, D_MODEL)),
            _const_spec((D_MODEL // 2, D_MODEL)),
            _const_spec((1, D_MODEL)),
        ],
        out_specs=[
            row_spec(D_MODEL),
            pl.BlockSpec((tm // 4, D_RNN), lambda i: (i, 0)),
            pl.BlockSpec((CONV_W - 1, tm // 4, D_RNN), lambda i: (0, i, 0)),
        ],
        out_shape=[
            jax.ShapeDtypeStruct((rows, D_MODEL), F32),
            jax.ShapeDtypeStruct((rows // 4, D_RNN), F32),
            jax.ShapeDtypeStruct((CONV_W - 1, rows // 4, D_RNN), F32),
        ],
        scratch_shapes=[pltpu.VMEM((tm + SUBLANES, D_RNN), F32)] * 4
        + [pltpu.VMEM((D_RNN // LANES, tm, LANES), F32)],
        compiler_params=pltpu.CompilerParams(
            dimension_semantics=("arbitrary",),
            vmem_limit_bytes=56 * 1024 * 1024),
    )(x2d, mod_s, proj_s, proj_s, proj_s, proj_s, proj_s, o_s, cs_t, h0, p['bmg'], p['gnw'], p['wbr_ret'], p['convw'],
      p['convb'], p['wrg'], p['bra'], p['brx'], p['lru'], p['wbr_rnn'], p['wout'], p['npost'])


def _rope_table(pos):
    half = DK // 2
    inv = ROPE_BASE ** (-jnp.arange(half, dtype=F32) / half)
    ang = pos.astype(F32)[:, None] * inv[None, :]
    cos, sin = jnp.cos(ang), jnp.sin(ang)
    ks = DK ** -0.5
    return jnp.concatenate([cos, sin, cos * ks, sin * ks], axis=1)


def _decay_tables(tpos, same_seq, chunk):
    lg = jnp.log(1.0 - 2.0 ** (-5.0 - jnp.arange(H_RET, dtype=F32)))
    idx = tpos.astype(F32)
    diff = idx[:, None] - idx[None, :]
    causal = (diff >= 0) & same_seq
    dec = jnp.where(causal[None], jnp.exp(jnp.where(causal, diff, 0.0)[None] * lg[:, None, None]), 0.0)
    qdec = jnp.exp((idx + 1.0)[None, :] * lg[:, None])
    kdec = jnp.exp((chunk - 1.0 - idx)[None, :] * lg[:, None])
    rep = lambda a: jnp.broadcast_to(a[:, :, None], a.shape + (LANES,))
    gpow = jnp.exp(chunk * lg)
    return dec, rep(qdec), rep(kdec), gpow


def kernel(x_prompt, x_sample, state_ret, state_rnn_h, state_rnn_conv, c_prompt, c_sample,
           w_ada, b_ada, norm_pre_mix, norm_post_mix, norm_pre_ffn, norm_post_ffn,
           w_in, ret_gn_w, w_br_ret, conv_w, conv_b, w_rg_a, b_rg_a, w_rg_x, b_rg_x,
           lru_param, w_br_rnn, w_mgate, b_mgate, w_out, w_ffn_in, w_ffn_out):
    depth = w_in.shape[0]
    assert depth == 1, "single layer step"
    nb, seq, _ = x_prompt.shape
    nsb, sseq, _ = x_sample.shape
    assert seq % PROMPT_TM == 0 and sseq * RET_SAMPLE_BB == SUBLANES and sseq == CONV_W
    l = 0
    row = lambda a: a[l][None, :]
    p = dict(
        npre=row(norm_pre_mix), npost=row(norm_post_mix), npre_ffn=row(norm_pre_ffn), npost_ffn=row(norm_post_ffn),
        win=_pack_rows(w_in[l], kb=D_MODEL // 8), wmg=_pack_rows(w_mgate[l]),
        bmg=row(b_mgate), gnw=row(ret_gn_w), wbr_ret=_pack_rows(w_br_ret[l]),
        convw=conv_w[l], convb=row(conv_b),
        wrg=_pack_rows(jnp.concatenate([w_rg_a[l], w_rg_x[l]], axis=2).reshape(D_RNN, 2 * RNN_BLOCK)),
        bra=row(b_rg_a), brx=row(b_rg_x), lru=row(lru_param),
        wbr_rnn=_pack_rows(w_br_rnn[l]), wout=_pack_rows(w_out[l]),
    )

    rows_s = nsb * sseq
    c_all = jnp.concatenate([jnp.repeat(c_sample, sseq, axis=0), c_prompt], axis=0)
    mod_all = _mod_call(c_all, w_ada[l], row(b_ada))
    mod_p = mod_all[rows_s:].reshape(nb, 1, 6 * D_MODEL)
    mod_s = mod_all

    tm = PROMPT_TM
    ng = tm // SUBLANES
    r = jnp.arange(tm)
    tpos = (r % SUBLANES) * ng + r // SUBLANES
    pos_p = (jnp.arange(seq // tm)[:, None] * tm + tpos[None, :]).reshape(seq).astype(jnp.int32)
    rope_p = _rope_table(pos_p)
    dec, qdec, kdec, gpow = _decay_tables(tpos, jnp.ones((tm, tm), bool), float(tm))
    x1p, ret_p, hlast_p, conv_p, p['wffn_in'], p['wffn_out'] = _mix_prompt_call(
        x_prompt, mod_p, gpow, rope_p, dec, qdec, kdec, p, w_ffn_in[l], w_ffn_out[l])

    xs2d = x_sample.reshape(rows_s, D_MODEL)
    proj_s = _proj_sample_call(xs2d, mod_s, p)
    r8 = jnp.arange(SUBLANES)
    rope_s = _rope_table(PAST_LEN + (r8 % sseq).astype(jnp.int32))
    same = (r8[:, None] // sseq) == (r8[None, :] // sseq)
    dec8, qdec8, kdec8, gpow_s = _decay_tables(r8 % sseq, same, float(sseq))
    yp2d, o_s, ret_s = _ffn_ret_call(x1p.reshape(nb * seq, D_MODEL), mod_p, seq // FFN_TM, proj_s, state_ret[l],
                                     gpow_s, rope_s, dec8, qdec8, kdec8, p)
    yp = yp2d.reshape(nb, seq, D_MODEL)
    cs_t = jnp.swapaxes(state_rnn_conv[l], 0, 1)
    x1s, hlast_s, conv_t = _mix_sample_call(xs2d, mod_s, proj_s, o_s, cs_t, state_rnn_h[l], p)
    ys = _ffn_call(x1s, mod_s, FFN_TM, 1, p).reshape(nsb, sseq, D_MODEL)
    conv_s = jnp.swapaxes(conv_t, 0, 1)

    return (yp, ys, ret_p[None], ret_s[None], hlast_p.reshape(nb, D_RNN)[None], hlast_s[None],
            conv_p[None], conv_s[None])
```

```python
import functools

import jax
import jax.numpy as jnp
from jax import lax
from jax.experimental import pallas as pl
from jax.experimental.pallas import tpu as pltpu

F32 = jnp.float32
BF16 = jnp.bfloat16

D_MODEL = 1024
H_RET = 4
DK = D_MODEL // H_RET
DV = 2 * DK
D_QK = H_RET * DK
D_V = H_RET * DV
D_RNN = 1536
RNN_BLOCK = 128
N_RNN_BLOCKS = D_RNN // RNN_BLOCK
CONV_W = 4
LRU_C = 8.0
D_FF = 2816
ROPE_BASE = 10000.0
GN_EPS = 1e-5
RMS_EPS = 1e-6
PAST_LEN = 16384

OFF_Q = 0
OFF_K = OFF_Q + D_QK
OFF_V = OFF_K + D_QK
OFF_G = OFF_V + D_V
OFF_XR = OFF_G + D_V
OFF_GR = OFF_XR + D_RNN
OFF_MG = OFF_GR + D_RNN
N_CAT = OFF_MG + 2 * D_MODEL

SUBLANES = 8
LANES = 128
MXU_DIM = 256
VMEM_BYTES_V7X = 64 * 1024 * 1024

PROMPT_TM = 256
FFN_TM = 512
FF_CHUNK = MXU_DIM
SAMPLE_TM = 128
RET_SAMPLE_BB = 2


def _dot(a, b):
    return jnp.dot(a, b, preferred_element_type=F32)


def _dot_nt(a, b):
    return lax.dot_general(a, b, (((1,), (1,)), ((), ())), preferred_element_type=F32)


def _dot_tn(a, b):
    return lax.dot_general(a, b, (((0,), (0,)), ((), ())), preferred_element_type=F32)


def _wb(ref, k0=None, k1=None, c0=None, c1=None):
    rs = slice(None) if k0 is None else slice(k0 // 2, k1 // 2)
    cs = slice(None) if c0 is None else slice(c0, c1)
    return pltpu.bitcast(ref[rs, cs], BF16)


def _to_words(w):
    return pltpu.bitcast(w.astype(BF16), jnp.uint32)


def _pack_kernel(w_ref, o_ref):
    o_ref[...] = _to_words(w_ref[...])


def _pack_rows(w, kb=None):
    k, n = w.shape
    kb = k if kb is None else kb
    return pl.pallas_call(
        _pack_kernel,
        grid=(k // kb,),
        in_specs=[pl.BlockSpec((kb, n), lambda i: (i, 0))],
        out_specs=pl.BlockSpec((kb // 2, n), lambda i: (i, 0)),
        out_shape=jax.ShapeDtypeStruct((k // 2, n), jnp.uint32),
        compiler_params=pltpu.CompilerParams(
            dimension_semantics=("arbitrary",),
            vmem_limit_bytes=40 * 1024 * 1024),
    )(w)


def _rms(x, w):
    ms = jnp.mean(x * x, axis=-1, keepdims=True)
    return x * lax.rsqrt(ms + RMS_EPS) * w


def _sigmoid(x):
    return 0.5 * jnp.tanh(0.5 * x) + 0.5


def _silu(x):
    return x * _sigmoid(x)


def _rope(x, cos, sin):
    half = DK // 2
    x1, x2 = x[:, :half], x[:, half:]
    return jnp.concatenate([x1 * cos - x2 * sin, x1 * sin + x2 * cos], axis=1)


def _group_norm(o):
    mu = jnp.mean(o, axis=-1, keepdims=True)
    d = o - mu
    var = jnp.mean(d * d, axis=-1, keepdims=True)
    return d * lax.rsqrt(var + GN_EPS)


def _lru_gate_pre(xcb, wrg_ref):
    return [_dot(xcb[:, n * RNN_BLOCK:(n + 1) * RNN_BLOCK], _wb(wrg_ref, n * RNN_BLOCK, (n + 1) * RNN_BLOCK))
            for n in range(N_RNN_BLOCKS)]


def _lru_coeffs(xconv, pre, b_a, b_x, lru):
    ra = jnp.concatenate([p[:, :RNN_BLOCK] for p in pre], axis=1) + b_a
    ri = jnp.concatenate([p[:, RNN_BLOCK:] for p in pre], axis=1) + b_x
    r = _sigmoid(ra)
    i = _sigmoid(ri)
    z = -lru
    sp = jnp.maximum(z, 0.0) + jnp.log(1.0 + jnp.exp(-jnp.abs(z)))
    log_a = -LRU_C * r * sp
    a = jnp.exp(log_a)
    beta = jnp.sqrt(-jnp.tanh(log_a) * (a * a + 1.0))
    return a, beta * (i * xconv)


def _mix_tail(x, g1, br_ret, rnn_y_b, gate_pre, b_mg, wbr_rnn_ref, wout_ref, npost):
    br_rnn = _dot(rnn_y_b, _wb(wbr_rnn_ref))
    gates = _sigmoid(gate_pre + b_mg)
    ga, gb = gates[:, :D_MODEL], gates[:, D_MODEL:]
    mixed = _dot((ga * br_ret + gb * br_rnn).astype(BF16), _wb(wout_ref))
    return x + g1 * _rms(mixed, npost)


def _mod_kernel(c_ref, w_ref, b_ref, o_ref):
    a = _silu(c_ref[...]).astype(BF16)
    o_ref[...] = _dot(a, w_ref[...].astype(BF16)) + b_ref[...]


def _mod_call(c_all, w_ada_f32, b_ada):
    rows = c_all.shape[0]
    tn = 2 * D_MODEL
    return pl.pallas_call(
        _mod_kernel,
        grid=(6 * D_MODEL // tn,),
        in_specs=[
            pl.BlockSpec((rows, D_MODEL), lambda j: (0, 0)),
            pl.BlockSpec((D_MODEL, tn), lambda j: (0, j)),
            pl.BlockSpec((1, tn), lambda j: (0, j)),
        ],
        out_specs=pl.BlockSpec((rows, tn), lambda j: (0, j)),
        out_shape=jax.ShapeDtypeStruct((rows, 6 * D_MODEL), F32),
        compiler_params=pltpu.CompilerParams(
            dimension_semantics=("arbitrary",),
            vmem_limit_bytes=48 * 1024 * 1024),
    )(c_all, w_ada_f32, b_ada)


def _mix_prompt_kernel(wf_rep, gpow_ref, x_ref, mod_ref, npre_ref, wcat_ref, wmg_ref, bmg_ref, rope_ref,
                       dec_ref, qdec_ref, kdec_ref, gnw_ref, wbr_ret_ref, convw_ref, convb_ref,
                       wrg_ref, bra_ref, brx_ref, lru_ref, wbr_rnn_ref, wout_ref, npost_ref,
                       wf1_ref, wf2_ref,
                       x1_ref, s_ref, hlast_ref, convnew_ref, wf1b_ref, wf2b_ref,
                       xr_ref, prevg_ref, hc_ref, perm_ref):
    tm = PROMPT_TM
    ng = tm // SUBLANES
    halo = (CONV_W - 1) * SUBLANES
    t = pl.program_id(1)

    @pl.when(t == 0)
    def _():
        s_ref[...] = jnp.zeros_like(s_ref)
        prevg_ref[...] = jnp.zeros_like(prevg_ref)
        hc_ref[...] = jnp.zeros_like(hc_ref)

    nslab = D_MODEL // LANES

    def lanes(sl):
        return slice(sl * LANES, (sl + 1) * LANES)

    def seg_rows(s):
        return pl.ds(s, ng, stride=SUBLANES)

    for sl in range(nslab):
        for s in range(SUBLANES):
            perm_ref[sl, seg_rows(s), :] = x_ref[s * ng:(s + 1) * ng, lanes(sl)]
    xp = jnp.concatenate([perm_ref[sl] for sl in range(nslab)], axis=1)
    sh1, sc1 = mod_ref[:, :D_MODEL], mod_ref[:, D_MODEL:2 * D_MODEL]
    hb = (_rms(xp, npre_ref[...]) * (1.0 + sc1) + sh1).astype(BF16)
    sub = lax.broadcasted_iota(jnp.int32, (SUBLANES, D_RNN), 0)
    half = DK // 2
    cos, sin = rope_ref[:, 0:half], rope_ref[:, half:2 * half]
    cosk, sink = rope_ref[:, 2 * half:3 * half], rope_ref[:, 3 * half:4 * half]
    st = {}

    def xr_proj():
        xr = _dot(hb, _wb(wcat_ref, c0=OFF_XR, c1=OFF_XR + D_RNN))
        xr_ref[halo:halo + tm, :] = xr
        for kk in range(1, CONV_W):
            r0 = (CONV_W - 1 - kk) * SUBLANES
            cur = xr[(ng - kk) * SUBLANES:(ng - kk + 1) * SUBLANES, :]
            prv = prevg_ref[r0:r0 + SUBLANES, :]
            xr_ref[r0:r0 + SUBLANES, :] = pltpu.roll(jnp.where(sub == SUBLANES - 1, prv, cur), 1, 0)
        prevg_ref[...] = xr[tm - halo:, :]
        for kk in range(1, CONV_W):
            r1 = (ng - kk) * SUBLANES + SUBLANES - 1
            convnew_ref[CONV_W - 1 - kk:CONV_W - kk, :] = xr[r1:r1 + 1, :]

    def lru_conv():
        cw = convw_ref[...]
        xconv = convb_ref[...]
        for j in range(CONV_W):
            r0 = halo - (CONV_W - 1 - j) * SUBLANES
            xconv = xconv + xr_ref[r0:r0 + tm, :] * cw[j:j + 1, :]
        st['xconv'] = xconv

    def lru_gate_proj():
        st['gpre'] = _lru_gate_pre(st['xconv'].astype(BF16), wrg_ref)

    def lru_coef():
        st['a'], st['b'] = _lru_coeffs(st.pop('xconv'), st.pop('gpre'), bra_ref[...], brx_ref[...], lru_ref[...])

    def lru_scan():
        a, b = st['a'], st['b']
        ca, cb = a[0:SUBLANES, :], b[0:SUBLANES, :]
        cas, cbs = [ca], [cb]
        for gi in range(1, ng):
            ag = a[gi * SUBLANES:(gi + 1) * SUBLANES, :]
            cb = ag * cb + b[gi * SUBLANES:(gi + 1) * SUBLANES, :]
            ca = ag * ca
            cas.append(ca)
            cbs.append(cb)
        cin = jnp.where(sub == 0, hc_ref[SUBLANES - 1:SUBLANES, :], 0.0)
        for s in range(SUBLANES - 1):
            cin = jnp.where(sub == s + 1, pltpu.roll(ca * cin + cb, 1, 0), cin)
        seg_end = ca * cin + cb
        hc_ref[...] = seg_end
        hlast_ref[...] = seg_end[SUBLANES - 1:SUBLANES, :]
        st['hseq'] = jnp.concatenate([cas[gi] * cin + cbs[gi] for gi in range(ng)], axis=0)

    def gr_proj():
        st['gr'] = _dot(hb, _wb(wcat_ref, c0=OFF_GR, c1=OFF_GR + D_RNN))

    def lru_y():
        st['rnn_y'] = (st.pop('hseq') * jax.nn.gelu(st.pop('gr'), approximate=True)).astype(BF16)

    def lru_out():
        st['br_rnn'] = _dot(st.pop('rnn_y'), _wb(wbr_rnn_ref))

    def gate_proj():
        st['gate_pre'] = _dot(hb, _wb(wmg_ref))

    def gate_act():
        st['gates'] = _sigmoid(st.pop('gate_pre') + bmg_ref[...])

    def head_proj(hh):
        q = _dot(hb, _wb(wcat_ref, c0=OFF_Q + hh * DK, c1=OFF_Q + (hh + 1) * DK))
        k = _dot(hb, _wb(wcat_ref, c0=OFF_K + hh * DK, c1=OFF_K + (hh + 1) * DK))
        vb = _dot(hb, _wb(wcat_ref, c0=OFF_V + hh * DV, c1=OFF_V + (hh + 1) * DV)).astype(BF16)
        g = _dot(hb, _wb(wcat_ref, c0=OFF_G + hh * DV, c1=OFF_G + (hh + 1) * DV))
        st['proj', hh] = (q, k, vb, g)

    def head_rope(hh):
        q, k, vb, g = st.pop(('proj', hh))
        kr = _rope(k, cosk, sink)
        kdec = kdec_ref[hh]
        kdb = (kr * jnp.concatenate([kdec, kdec], axis=1)).astype(BF16)
        st['rope', hh] = (_rope(q, cos, sin).astype(BF16), kr.astype(BF16), kdb, vb, g)

    def head_qk(hh):
        qb, kb, kdb, vb, g = st.pop(('rope', hh))
        scores = _dot_nt(qb, kb)
        cross = _dot(qb, s_ref[hh].astype(BF16))
        st['qk', hh] = (scores, cross, vb, g)
        st['kv', hh] = (kdb, vb)

    def head_state(hh):
        kdb, vb = st.pop(('kv', hh))
        s_ref[hh] = gpow_ref[hh] * s_ref[hh] + _dot_tn(kdb, vb)

    def head_decay(hh):
        scores, cross, vb, g = st.pop(('qk', hh))
        st['dec', hh] = ((scores * dec_ref[hh]).astype(BF16), cross, vb, g)

    def head_pv(hh):
        sb, cross, vb, g = st.pop(('dec', hh))
        st['pv', hh] = (_dot(sb, vb), cross, g)

    def head_norm(hh):
        intra, cross, g = st.pop(('pv', hh))
        qdec = qdec_ref[hh]
        on = _group_norm(intra + cross * jnp.concatenate([qdec] * (DV // LANES), axis=1))
        st['ry', hh] = (on * gnw_ref[:, hh * DV:(hh + 1) * DV] * _silu(g)).astype(BF16)

    def head_out(hh):
        part = _dot(st.pop(('ry', hh)), _wb(wbr_ret_ref, hh * DV, (hh + 1) * DV))
        st['br_ret'] = part if hh == 0 else st['br_ret'] + part

    heads = range(H_RET)
    order = (
        [xr_proj, (head_proj, 0), lru_conv, lru_gate_proj, (head_proj, 1), (head_proj, 2), lru_coef, (head_proj, 3)]
        + [(head_rope, h) for h in heads] + [gr_proj] + [(head_qk, h) for h in heads]
        + [gate_proj] + [(head_decay, h) for h in heads] + [lru_scan]
        + [(head_pv, h) for h in heads] + [lru_y, lru_out]
        + [st_ for h in heads for st_ in ((head_norm, h), (head_state, h))] + [(head_out, h) for h in heads]
        + [gate_act]
    )
    for stage in order:
        if isinstance(stage, tuple):
            stage[0](stage[1])
        else:
            stage()

    gates = st['gates']
    ga, gb = gates[:, :D_MODEL], gates[:, D_MODEL:]
    mixed = _dot((ga * st['br_ret'] + gb * st['br_rnn']).astype(BF16), _wb(wout_ref))
    delta = mod_ref[:, 2 * D_MODEL:] * _rms(mixed, npost_ref[...])
    for sl in range(nslab):
        perm_ref[sl] = delta[:, lanes(sl)]
    for sl in range(nslab):
        for s in range(SUBLANES):
            rows = slice(s * ng, (s + 1) * ng)
            x1_ref[rows, lanes(sl)] = x_ref[rows, lanes(sl)] + perm_ref[sl, seg_rows(s), :]

    step = pl.program_id(0) * pl.num_programs(1) + t

    @pl.when(step % wf_rep == 0)
    def _():
        wf1b_ref[...] = _to_words(wf1_ref[...])
        wf2b_ref[...] = _to_words(wf2_ref[...])


def _const_spec(shape):
    nd = len(shape)
    return pl.BlockSpec(shape, lambda *_: (0,) * nd, pipeline_mode=pl.Buffered(1))


def _mix_prompt_call(x, mod3, gpow, rope_tab, dec, qdec, kdec, p, wf1, wf2):
    nb, seq, _ = x.shape
    tm = PROMPT_TM
    nt = seq // tm
    steps = nb * nt
    bf16_rows = 2 * SUBLANES
    wf_rep = 1
    while any(w.shape[0] % (steps // wf_rep) or (w.shape[0] // (steps // wf_rep)) % bf16_rows for w in (wf1, wf2)):
        wf_rep *= 2
    r1, r2 = wf1.shape[0] // (steps // wf_rep), wf2.shape[0] // (steps // wf_rep)
    wf1_spec = pl.BlockSpec((r1, wf1.shape[1]), lambda b, t: ((b * nt + t) // wf_rep, 0))
    wf2_spec = pl.BlockSpec((r2, wf2.shape[1]), lambda b, t: ((b * nt + t) // wf_rep, 0))
    in_specs = [
        pl.BlockSpec(memory_space=pltpu.SMEM),
        pl.BlockSpec((None, tm, D_MODEL), lambda b, t: (b, t, 0)),
        pl.BlockSpec((None, 1, 3 * D_MODEL), lambda b, t: (b, 0, 0)),
        _const_spec((1, D_MODEL)),
        _const_spec((D_MODEL // 2, OFF_MG)),
        _const_spec((D_MODEL // 2, 2 * D_MODEL)),
        _const_spec((1, 2 * D_MODEL)),
        pl.BlockSpec((tm, 4 * (DK // 2)), lambda b, t: (t, 0)),
        _const_spec((H_RET, tm, tm)),
        _const_spec((H_RET, tm, LANES)),
        _const_spec((H_RET, tm, LANES)),
        _const_spec((1, D_V)),
        _const_spec((D_V // 2, D_MODEL)),
        _const_spec((CONV_W, D_RNN)),
        _const_spec((1, D_RNN)),
        _const_spec((D_RNN // 2, 2 * RNN_BLOCK)),
        _const_spec((1, D_RNN)),
        _const_spec((1, D_RNN)),
        _const_spec((1, D_RNN)),
        _const_spec((D_RNN // 2, D_MODEL)),
        _const_spec((D_MODEL // 2, D_MODEL)),
        _const_spec((1, D_MODEL)),
        wf1_spec,
        wf2_spec,
    ]
    out_specs = [
        pl.BlockSpec((None, tm, D_MODEL), lambda b, t: (b, t, 0)),
        pl.BlockSpec((None, H_RET, DK, DV), lambda b, t: (b, 0, 0, 0)),
        pl.BlockSpec((None, 1, D_RNN), lambda b, t: (b, 0, 0)),
        pl.BlockSpec((None, CONV_W - 1, D_RNN), lambda b, t: (b, 0, 0)),
        pl.BlockSpec((r1 // 2, wf1.shape[1]), wf1_spec.index_map),
        pl.BlockSpec((r2 // 2, wf2.shape[1]), wf2_spec.index_map),
    ]
    out_shape = [
        jax.ShapeDtypeStruct((nb, seq, D_MODEL), F32),
        jax.ShapeDtypeStruct((nb, H_RET, DK, DV), F32),
        jax.ShapeDtypeStruct((nb, 1, D_RNN), F32),
        jax.ShapeDtypeStruct((nb, CONV_W - 1, D_RNN), F32),
        jax.ShapeDtypeStruct((wf1.shape[0] // 2, wf1.shape[1]), jnp.uint32),
        jax.ShapeDtypeStruct((wf2.shape[0] // 2, wf2.shape[1]), jnp.uint32),
    ]
    halo = (CONV_W - 1) * SUBLANES
    scratch = [
        pltpu.VMEM((halo + tm, D_RNN), F32),
        pltpu.VMEM((halo, D_RNN), F32),
        pltpu.VMEM((SUBLANES, D_RNN), F32),
        pltpu.VMEM((D_MODEL // LANES, tm, LANES), F32),
    ]
    return pl.pallas_call(
        functools.partial(_mix_prompt_kernel, wf_rep),
        grid=(nb, nt),
        in_specs=in_specs,
        out_specs=out_specs,
        out_shape=out_shape,
        scratch_shapes=scratch,
        compiler_params=pltpu.CompilerParams(
            dimension_semantics=("arbitrary", "arbitrary"),
            vmem_limit_bytes=VMEM_BYTES_V7X - 4 * 1024 * 1024),
    )(gpow, x, mod3, p['npre'], p['win'], p['wmg'], p['bmg'], rope_tab, dec, qdec, kdec, p['gnw'], p['wbr_ret'],
      p['convw'], p['convb'], p['wrg'], p['bra'], p['brx'], p['lru'], p['wbr_rnn'], p['wout'], p['npost'],
      wf1, wf2)


def _ffn_pre(x_ref, mod_ref, npre_ref):
    m = mod_ref[...]
    sh2, sc2 = m[:, :D_MODEL], m[:, D_MODEL:2 * D_MODEL]
    return (_rms(x_ref[...], npre_ref[...]) * (1.0 + sc2) + sh2).astype(BF16)


def _ffn_hidden(h2, acc, w1_ref, w2_ref, chunk_ids):
    for j in chunk_ids:
        c0 = j * FF_CHUNK
        fg = _dot(h2, _wb(w1_ref, c0=c0, c1=c0 + FF_CHUNK))
        fu = _dot(h2, _wb(w1_ref, c0=D_FF + c0, c1=D_FF + c0 + FF_CHUNK))
        part = _dot((_silu(fg) * fu).astype(BF16), _wb(w2_ref, c0, c0 + FF_CHUNK))
        acc = part if acc is None else acc + part
    return acc


def _ffn_post(x_ref, mod_ref, acc, npost_ref, o_ref):
    o_ref[...] = x_ref[...] + mod_ref[:, 2 * D_MODEL:] * _rms(acc, npost_ref[...])


def _ffn_kernel(x_ref, mod_ref, npre_ref, w1_ref, w2_ref, npost_ref, o_ref):
    h2 = _ffn_pre(x_ref, mod_ref, npre_ref)
    acc = _ffn_hidden(h2, None, w1_ref, w2_ref, range(D_FF // FF_CHUNK))
    _ffn_post(x_ref, mod_ref, acc, npost_ref, o_ref)


def _ffn_call(x2d, mod, mod_rows_per_tile, seq_tiles, p):
    rows = x2d.shape[0]
    tm = FFN_TM
    if mod_rows_per_tile == 1:
        mod_spec = pl.BlockSpec((None, 1, 3 * D_MODEL), lambda i: (i // seq_tiles, 0, 1))
    else:
        mod_spec = pl.BlockSpec((tm, 3 * D_MODEL), lambda i: (i, 1))
    return pl.pallas_call(
        _ffn_kernel,
        grid=(rows // tm,),
        in_specs=[
            pl.BlockSpec((tm, D_MODEL), lambda i: (i, 0)),
            mod_spec,
            _const_spec((1, D_MODEL)),
            _const_spec((D_MODEL // 2, 2 * D_FF)),
            _const_spec((D_FF // 2, D_MODEL)),
            _const_spec((1, D_MODEL)),
        ],
        out_specs=pl.BlockSpec((tm, D_MODEL), lambda i: (i, 0)),
        out_shape=jax.ShapeDtypeStruct((rows, D_MODEL), F32),
        compiler_params=pltpu.CompilerParams(
            dimension_semantics=("arbitrary",),
            vmem_limit_bytes=48 * 1024 * 1024),
    )(x2d, mod, p['npre_ffn'], p['wffn_in'], p['wffn_out'], p['npost_ffn'])


def _proj_sample_kernel(n_in_tiles, x_ref, mod_ref, npre_ref, win_ref, wmg_ref, o_ref):
    m = mod_ref[...]
    sh1, sc1 = m[:, :D_MODEL], m[:, D_MODEL:2 * D_MODEL]
    h = (_rms(x_ref[...], npre_ref[...]) * (1.0 + sc1) + sh1).astype(BF16)
    j = pl.program_id(0)

    @pl.when(j < n_in_tiles)
    def _():
        o_ref[...] = _dot(h, _wb(win_ref))

    @pl.when(j >= n_in_tiles)
    def _():
        o_ref[...] = _dot(h, _wb(wmg_ref))


def _proj_sample_call(x2d, mod_s, p):
    rows = x2d.shape[0]
    tn = D_MODEL
    n_in = OFF_MG // tn
    return pl.pallas_call(
        functools.partial(_proj_sample_kernel, n_in),
        grid=(N_CAT // tn,),
        in_specs=[
            pl.BlockSpec((rows, D_MODEL), lambda j: (0, 0)),
            pl.BlockSpec((rows, 3 * D_MODEL), lambda j: (0, 0)),
            pl.BlockSpec((1, D_MODEL), lambda j: (0, 0)),
            pl.BlockSpec((D_MODEL // 2, tn), lambda j: (0, jnp.minimum(j, n_in - 1))),
            pl.BlockSpec((D_MODEL // 2, tn), lambda j: (0, jnp.maximum(j - n_in, 0))),
        ],
        out_specs=pl.BlockSpec((rows, tn), lambda j: (0, j)),
        out_shape=jax.ShapeDtypeStruct((rows, N_CAT), F32),
        compiler_params=pltpu.CompilerParams(
            dimension_semantics=("arbitrary",),
            vmem_limit_bytes=48 * 1024 * 1024),
    )(x2d, mod_s, p['npre'], p['win'], p['wmg'])


def _ret_sample_kernel(gpow_ref, qkv_ref, s_ref, rope_ref, dec_ref, qdec_ref, kdec_ref, o_ref, snew_ref):
    half = DK // 2
    cos, sin = rope_ref[:, 0:half], rope_ref[:, half:2 * half]
    cosk, sink = rope_ref[:, 2 * half:3 * half], rope_ref[:, 3 * half:4 * half]
    nseq = RET_SAMPLE_BB
    tlen = SUBLANES // nseq
    row = lax.broadcasted_iota(jnp.int32, (SUBLANES, LANES), 0)
    for hh in range(H_RET):
        q = _rope(qkv_ref[:, OFF_Q + hh * DK:OFF_Q + (hh + 1) * DK], cos, sin)
        k = _rope(qkv_ref[:, OFF_K + hh * DK:OFF_K + (hh + 1) * DK], cosk, sink)
        v = qkv_ref[:, OFF_V + hh * DV:OFF_V + (hh + 1) * DV]
        kdec = kdec_ref[hh]
        kd = k * jnp.concatenate([kdec, kdec], axis=1)
        scores = _dot_nt(q, k) * dec_ref[hh]
        intra = _dot(scores, v)
        qb = q.astype(BF16)
        cross = jnp.zeros((SUBLANES, DV), F32)
        for bi in range(nseq):
            s_old = s_ref[bi, hh]
            in_seq = (row >= bi * tlen) & (row < (bi + 1) * tlen)
            cr = _dot(qb, s_old.astype(BF16))
            cross = jnp.where(jnp.concatenate([in_seq] * (DV // LANES), axis=1), cr, cross)
            kd_b = jnp.where(jnp.concatenate([in_seq] * (DK // LANES), axis=1), kd, 0.0)
            snew_ref[bi, hh] = gpow_ref[hh] * s_old + _dot_tn(kd_b, v)
        qdec = qdec_ref[hh]
        o_ref[:, hh * DV:(hh + 1) * DV] = intra + cross * jnp.concatenate([qdec] * (DV // LANES), axis=1)


def _ffn_ret_kernel(sub_steps, gpow_ref, x_ref, mod_ref, npre_ref, w1_ref, w2_ref, npost_ref,
                    qkv_ref, s_ref, rope_ref, dec_ref, qdec_ref, kdec_ref,
                    y_ref, o_ref, snew_ref, h2_ref, acc_ref):
    n_chunks = D_FF // FF_CHUNK
    bounds = [n_chunks * s // sub_steps for s in range(sub_steps + 1)]
    for s in range(sub_steps):
        @pl.when(pl.program_id(1) == s)
        def _(s=s):
            if s == 0:
                h2_ref[...] = _ffn_pre(x_ref, mod_ref, npre_ref)
            acc = _ffn_hidden(h2_ref[...], None if s == 0 else acc_ref[...], w1_ref, w2_ref,
                              range(bounds[s], bounds[s + 1]))
            if s == sub_steps - 1:
                _ffn_post(x_ref, mod_ref, acc, npost_ref, y_ref)
            else:
                acc_ref[...] = acc

    _ret_sample_kernel(gpow_ref, qkv_ref, s_ref, rope_ref, dec_ref, qdec_ref, kdec_ref, o_ref, snew_ref)


def _ffn_ret_call(x2d, mod_p, seq_tiles, proj_s, state, gpow, rope8, dec8, qdec8, kdec8, p):
    rows = x2d.shape[0]
    tm = FFN_TM
    nb = state.shape[0]
    bb = RET_SAMPLE_BB
    sub_steps = nb // bb // (rows // tm)
    assert sub_steps * (rows // tm) * bb == nb
    srow = lambda i, j: i * sub_steps + j
    return pl.pallas_call(
        functools.partial(_ffn_ret_kernel, sub_steps),
        grid=(rows // tm, sub_steps),
        in_specs=[
            pl.BlockSpec(memory_space=pltpu.SMEM),
            pl.BlockSpec((tm, D_MODEL), lambda i, j: (i, 0)),
            pl.BlockSpec((None, 1, 3 * D_MODEL), lambda i, j: (i // seq_tiles, 0, 1)),
            _const_spec((1, D_MODEL)),
            _const_spec((D_MODEL // 2, 2 * D_FF)),
            _const_spec((D_FF // 2, D_MODEL)),
            _const_spec((1, D_MODEL)),
            pl.BlockSpec((SUBLANES, OFF_G), lambda i, j: (srow(i, j), 0)),
            pl.BlockSpec((bb, H_RET, DK, DV), lambda i, j: (srow(i, j), 0, 0, 0)),
            _const_spec((SUBLANES, 4 * (DK // 2))),
            _const_spec((H_RET, SUBLANES, SUBLANES)),
            _const_spec((H_RET, SUBLANES, LANES)),
            _const_spec((H_RET, SUBLANES, LANES)),
        ],
        out_specs=[
            pl.BlockSpec((tm, D_MODEL), lambda i, j: (i, 0)),
            pl.BlockSpec((SUBLANES, D_V), lambda i, j: (srow(i, j), 0)),
            pl.BlockSpec((bb, H_RET, DK, DV), lambda i, j: (srow(i, j), 0, 0, 0)),
        ],
        out_shape=[
            jax.ShapeDtypeStruct((rows, D_MODEL), F32),
            jax.ShapeDtypeStruct((proj_s.shape[0], D_V), F32),
            jax.ShapeDtypeStruct(state.shape, F32),
        ],
        scratch_shapes=[pltpu.VMEM((tm, D_MODEL), BF16), pltpu.VMEM((tm, D_MODEL), F32)],
        compiler_params=pltpu.CompilerParams(
            dimension_semantics=("arbitrary", "arbitrary"),
            vmem_limit_bytes=56 * 1024 * 1024),
    )(gpow, x2d, mod_p, p['npre_ffn'], p['wffn_in'], p['wffn_out'], p['npost_ffn'],
      proj_s, state, rope8, dec8, qdec8, kdec8)


def _mix_sample_kernel(x_ref, mod_ref, proj_ref, o_ref, cs_ref, h0_ref,
                       bmg_ref, gnw_ref, wbr_ret_ref, convw_ref, convb_ref, wrg_ref, bra_ref, brx_ref,
                       lru_ref, wbr_rnn_ref, wout_ref, npost_ref,
                       x1_ref, hlast_ref, convnew_ref,
                       xr_ref, sa_ref, sb_ref, zsc_ref, slab_ref):
    tm = SAMPLE_TM
    tlen = 4
    nseq = tm // tlen
    nslab = D_RNN // LANES

    def lanes(sl):
        return slice(sl * LANES, (sl + 1) * LANES)

    def seq_rows(t):
        return pl.ds(t, nseq, stride=tlen)
    x = x_ref[...]
    g1 = mod_ref[:, 2 * D_MODEL:]

    br_ret = jnp.zeros((tm, D_MODEL), F32)
    for hh in range(H_RET):
        on = _group_norm(o_ref[:, hh * DV:(hh + 1) * DV])
        g = proj_ref[:, OFF_G + hh * DV:OFF_G + (hh + 1) * DV]
        ry = (on * gnw_ref[:, hh * DV:(hh + 1) * DV] * _silu(g)).astype(BF16)
        br_ret = br_ret + _dot(ry, _wb(wbr_ret_ref, hh * DV, (hh + 1) * DV))

    for sl in range(nslab):
        for j in range(CONV_W - 1):
            slab_ref[sl, seq_rows(j), :] = cs_ref[j, :, lanes(sl)]
        slab_ref[sl, seq_rows(CONV_W - 1), :] = h0_ref[:, lanes(sl)]
    for sl in range(nslab):
        zsc_ref[0:tm, lanes(sl)] = slab_ref[sl]
    zsc_ref[tm:tm + SUBLANES, :] = jnp.zeros((SUBLANES, D_RNN), F32)
    tpos = lax.broadcasted_iota(jnp.int32, (tm, D_RNN), 0) & (tlen - 1)
    xr = proj_ref[:, OFF_XR:OFF_XR + D_RNN]
    xr_ref[0:SUBLANES, :] = jnp.zeros((SUBLANES, D_RNN), F32)
    xr_ref[SUBLANES:SUBLANES + tm, :] = xr
    cw = convw_ref[...]
    xconv = convb_ref[...]
    for j in range(CONV_W - 1):
        sft = CONV_W - 1 - j
        shifted = jnp.where(tpos >= sft, xr_ref[SUBLANES - sft:SUBLANES - sft + tm, :], 0.0)
        carried = jnp.where(tpos < sft, zsc_ref[CONV_W - 1 - sft:CONV_W - 1 - sft + tm, :], 0.0)
        xconv = xconv + (shifted + carried) * cw[j:j + 1, :]
    xconv = xconv + xr * cw[CONV_W - 1:CONV_W, :]

    a, b = _lru_coeffs(xconv, _lru_gate_pre(xconv.astype(BF16), wrg_ref), bra_ref[...], brx_ref[...], lru_ref[...])
    b = jnp.where(tpos == 0, b + a * zsc_ref[CONV_W - 1:CONV_W - 1 + tm, :], b)
    sa_ref[0:SUBLANES, :] = jnp.zeros((SUBLANES, D_RNN), F32)
    sb_ref[0:SUBLANES, :] = jnp.zeros((SUBLANES, D_RNN), F32)
    for s in (1, 2):
        sa_ref[SUBLANES:SUBLANES + tm, :] = a
        sb_ref[SUBLANES:SUBLANES + tm, :] = b
        keep = tpos >= s
        ap = jnp.where(keep, sa_ref[SUBLANES - s:SUBLANES - s + tm, :], 1.0)
        bp = jnp.where(keep, sb_ref[SUBLANES - s:SUBLANES - s + tm, :], 0.0)
        b = a * bp + b
        a = a * ap
    for sl in range(nslab):
        slab_ref[sl] = b[:, lanes(sl)]
    for sl in range(nslab):
        hlast_ref[:, lanes(sl)] = slab_ref[sl, seq_rows(tlen - 1), :]
    for sl in range(nslab):
        slab_ref[sl] = xr[:, lanes(sl)]
    for sl in range(nslab):
        for j in range(CONV_W - 1):
            convnew_ref[j, :, lanes(sl)] = slab_ref[sl, seq_rows(tlen - (CONV_W - 1) + j), :]

    gr = proj_ref[:, OFF_GR:OFF_GR + D_RNN]
    rnn_y_b = (b * jax.nn.gelu(gr, approximate=True)).astype(BF16)
    gate_pre = proj_ref[:, OFF_MG:OFF_MG + 2 * D_MODEL]
    x1_ref[...] = _mix_tail(x, g1, br_ret, rnn_y_b, gate_pre, bmg_ref[...], wbr_rnn_ref, wout_ref,
                            npost_ref[...])


def _mix_sample_call(x2d, mod_s, proj_s, o_s, cs_t, h0, p):
    rows = x2d.shape[0]
    tm = SAMPLE_TM
    row_spec = lambda w: pl.BlockSpec((tm, w), lambda i: (i, 0))
    return pl.pallas_call(
        _mix_sample_kernel,
        grid=(rows // tm,),
        in_specs=[
            row_spec(D_MODEL), row_spec(3 * D_MODEL), row_spec(N_CAT), row_spec(D_V),
            pl.BlockSpec((CONV_W - 1, tm // 4, D_RNN), lambda i: (0, i, 0)),
            pl.BlockSpec((tm // 4, D_RNN), lambda i: (i, 0)),
            _const_spec((1, 2 * D_MODEL)),
            _const_spec((1, D_V)),
            _const_spec((D_V // 2, D_MODEL)),
            _const_spec((CONV_W, D_RNN)),
            _const_spec((1, D_RNN)),
            _const_spec((D_RNN // 2, 2 * RNN_BLOCK)),
            _const_spec((1, D_RNN)),
            _const_spec((1, D_RNN)),
            _const_spec((1, D_RNN)),
            _const_spec((D_RNN // 2, D_MODEL)),
            _const_spec((D_MODEL // 2, D_MODEL)),
            _const_spec((1, D_MODEL)),
        ],
        out_specs=[
            row_spec(D_MODEL),
            pl.BlockSpec((tm // 4, D_RNN), lambda i: (i, 0)),
            pl.BlockSpec((CONV_W - 1, tm // 4, D_RNN), lambda i: (0, i, 0)),
        ],
        out_shape=[
            jax.ShapeDtypeStruct((rows, D_MODEL), F32),
            jax.ShapeDtypeStruct((rows // 4, D_RNN), F32),
            jax.ShapeDtypeStruct((CONV_W - 1, rows // 4, D_RNN), F32),
        ],
        scratch_shapes=[pltpu.VMEM((tm + SUBLANES, D_RNN), F32)] * 4
        + [pltpu.VMEM((D_RNN // LANES, tm, LANES), F32)],
        compiler_params=pltpu.CompilerParams(
            dimension_semantics=("arbitrary",),
            vmem_limit_bytes=56 * 1024 * 1024),
    )(x2d, mod_s, proj_s, o_s, cs_t, h0, p['bmg'], p['gnw'], p['wbr_ret'], p['convw'],
      p['convb'], p['wrg'], p['bra'], p['brx'], p['lru'], p['wbr_rnn'], p['wout'], p['npost'])


def _rope_table(pos):
    half = DK // 2
    inv = ROPE_BASE ** (-jnp.arange(half, dtype=F32) / half)
    ang = pos.astype(F32)[:, None] * inv[None, :]
    cos, sin = jnp.cos(ang), jnp.sin(ang)
    ks = DK ** -0.5
    return jnp.concatenate([cos, sin, cos * ks, sin * ks], axis=1)


def _decay_tables(tpos, same_seq, chunk):
    lg = jnp.log(1.0 - 2.0 ** (-5.0 - jnp.arange(H_RET, dtype=F32)))
    idx = tpos.astype(F32)
    diff = idx[:, None] - idx[None, :]
    causal = (diff >= 0) & same_seq
    dec = jnp.where(causal[None], jnp.exp(jnp.where(causal, diff, 0.0)[None] * lg[:, None, None]), 0.0)
    qdec = jnp.exp((idx + 1.0)[None, :] * lg[:, None])
    kdec = jnp.exp((chunk - 1.0 - idx)[None, :] * lg[:, None])
    rep = lambda a: jnp.broadcast_to(a[:, :, None], a.shape + (LANES,))
    gpow = jnp.exp(chunk * lg)
    return dec, rep(qdec), rep(kdec), gpow


def kernel(x_prompt, x_sample, state_ret, state_rnn_h, state_rnn_conv, c_prompt, c_sample,
           w_ada, b_ada, norm_pre_mix, norm_post_mix, norm_pre_ffn, norm_post_ffn,
           w_in, ret_gn_w, w_br_ret, conv_w, conv_b, w_rg_a, b_rg_a, w_rg_x, b_rg_x,
           lru_param, w_br_rnn, w_mgate, b_mgate, w_out, w_ffn_in, w_ffn_out):
    depth = w_in.shape[0]
    assert depth == 1, "single layer step"
    nb, seq, _ = x_prompt.shape
    nsb, sseq, _ = x_sample.shape
    assert seq % PROMPT_TM == 0 and sseq * RET_SAMPLE_BB == SUBLANES and sseq == CONV_W
    l = 0
    row = lambda a: a[l][None, :]
    p = dict(
        npre=row(norm_pre_mix), npost=row(norm_post_mix), npre_ffn=row(norm_pre_ffn), npost_ffn=row(norm_post_ffn),
        win=_pack_rows(w_in[l], kb=D_MODEL // 8), wmg=_pack_rows(w_mgate[l]),
        bmg=row(b_mgate), gnw=row(ret_gn_w), wbr_ret=_pack_rows(w_br_ret[l]),
        convw=conv_w[l], convb=row(conv_b),
        wrg=_pack_rows(jnp.concatenate([w_rg_a[l], w_rg_x[l]], axis=2).reshape(D_RNN, 2 * RNN_BLOCK)),
        bra=row(b_rg_a), brx=row(b_rg_x), lru=row(lru_param),
        wbr_rnn=_pack_rows(w_br_rnn[l]), wout=_pack_rows(w_out[l]),
    )

    rows_s = nsb * sseq
    c_all = jnp.concatenate([jnp.repeat(c_sample, sseq, axis=0), c_prompt], axis=0)
    mod_all = _mod_call(c_all, w_ada[l], row(b_ada))
    mod_p = mod_all[rows_s:].reshape(nb, 1, 6 * D_MODEL)
    mod_s = mod_all

    tm = PROMPT_TM
    ng = tm // SUBLANES
    r = jnp.arange(tm)
    tpos = (r % SUBLANES) * ng + r // SUBLANES
    pos_p = (jnp.arange(seq // tm)[:, None] * tm + tpos[None, :]).reshape(seq).astype(jnp.int32)
    rope_p = _rope_table(pos_p)
    dec, qdec, kdec, gpow = _decay_tables(tpos, jnp.ones((tm, tm), bool), float(tm))
    x1p, ret_p, hlast_p, conv_p, p['wffn_in'], p['wffn_out'] = _mix_prompt_call(
        x_prompt, mod_p, gpow, rope_p, dec, qdec, kdec, p, w_ffn_in[l], w_ffn_out[l])

    xs2d = x_sample.reshape(rows_s, D_MODEL)
    proj_s = _proj_sample_call(xs2d, mod_s, p)
    r8 = jnp.arange(SUBLANES)
    rope_s = _rope_table(PAST_LEN + (r8 % sseq).astype(jnp.int32))
    same = (r8[:, None] // sseq) == (r8[None, :] // sseq)
    dec8, qdec8, kdec8, gpow_s = _decay_tables(r8 % sseq, same, float(sseq))
    yp2d, o_s, ret_s = _ffn_ret_call(x1p.reshape(nb * seq, D_MODEL), mod_p, seq // FFN_TM, proj_s, state_ret[l],
                                     gpow_s, rope_s, dec8, qdec8, kdec8, p)
    yp = yp2d.reshape(nb, seq, D_MODEL)
    cs_t = jnp.swapaxes(state_rnn_conv[l], 0, 1)
    x1s, hlast_s, conv_t = _mix_sample_call(xs2d, mod_s, proj_s, o_s, cs_t, state_rnn_h[l], p)
    ys = _ffn_call(x1s, mod_s, FFN_TM, 1, p).reshape(nsb, sseq, D_MODEL)
    conv_s = jnp.swapaxes(conv_t, 0, 1)

    return (yp, ys, ret_p[None], ret_s[None], hlast_p.reshape(nb, D_RNN)[None], hlast_s[None],
            conv_p[None], conv_s[None])
```
